```python
import math
import jax, jax.numpy as jnp
from jax import lax
import numpy as np

D_MODEL = 2048
BATCH = 8
SEQ = 8192
DEPTH = 4

CHUNK = 64
Q_BLOCK = 128
EPS = 1e-6
NEG_INF = -1e30

FOX_HEADS = 8
FOX_HEAD_DIM = D_MODEL // 16
FOX_WIDTH = FOX_HEADS * FOX_HEAD_DIM
SSM_WIDTH = D_MODEL // 2
SSM_HEAD_DIM = 64
SSM_HEADS = SSM_WIDTH // SSM_HEAD_DIM
SSM_GROUPS = 2
SSM_STATE = 128
SSM_CONV = 4
SSM_CONV_DIM = SSM_WIDTH + 2 * SSM_GROUPS * SSM_STATE
SC_WIDTH = D_MODEL // 2
SC_CONV = 3
CF_WIDTH = D_MODEL // 2
CF_CONV = 31
N_BRANCH = 4
BRANCH_WIDTH = D_MODEL // 2

IN_SIZES = (
    FOX_WIDTH, FOX_WIDTH, FOX_WIDTH, FOX_HEADS, FOX_WIDTH,
    SSM_WIDTH, SSM_CONV_DIM, SSM_HEADS,
    SC_WIDTH, SC_WIDTH, SC_WIDTH, SC_WIDTH,
    2 * CF_WIDTH, CF_WIDTH,
)
N_IN = sum(IN_SIZES)

kernel_name = "hybrid_fox_ssd_shortconv_conformer_block"


def _split_columns(u, sizes):
    parts, off = [], 0
    for s in sizes:
        parts.append(u[..., off:off + s])
        off += s
    return parts


def rms_norm(x, w):
    xf = x.astype(jnp.float32)
    y = xf * lax.rsqrt(jnp.mean(xf * xf, axis=-1, keepdims=True) + EPS)
    return (y * w.astype(jnp.float32)).astype(x.dtype)


def layer_norm(x, w, b):
    xf = x.astype(jnp.float32)
    mu = jnp.mean(xf, axis=-1, keepdims=True)
    var = jnp.mean(jnp.square(xf - mu), axis=-1, keepdims=True)
    return ((xf - mu) * lax.rsqrt(var + EPS) * w + b).astype(x.dtype)


def causal_dwconv(x, w, b):
    k = w.shape[0]
    y = lax.conv_general_dilated(
        x, w[:, None, :].astype(x.dtype), window_strides=(1,), padding=((k - 1, 0),),
        dimension_numbers=("NWC", "WIO", "NWC"), feature_group_count=x.shape[-1])
    return y + b.astype(x.dtype)


def forgetting_attention(q, k, v, logf):
    bsz, s_len, n_h, dh = q.shape
    nb = s_len // Q_BLOCK
    scale = dh ** -0.5
    c = jnp.cumsum(logf, axis=1)
    qb = q.reshape(bsz, nb, Q_BLOCK, n_h, dh).transpose(1, 0, 3, 2, 4)
    cq = c.reshape(bsz, nb, Q_BLOCK, n_h).transpose(1, 0, 3, 2)
    kt = k.transpose(0, 2, 1, 3)
    vt = v.transpose(0, 2, 1, 3)
    ck = c.transpose(0, 2, 1)
    kpos = jnp.arange(s_len)

    def block(args):
        q_i, c_i, i = args
        s = jnp.einsum("bhqd,bhkd->bhqk", q_i, kt,
                       preferred_element_type=jnp.float32) * scale
        s = s + c_i[..., None] - ck[:, :, None, :]
        qpos = i * Q_BLOCK + jnp.arange(Q_BLOCK)
        s = jnp.where(kpos[None, :] <= qpos[:, None], s, NEG_INF)
        p = jax.nn.softmax(s, axis=-1)
        return jnp.einsum("bhqk,bhkd->bhqd", p.astype(vt.dtype), vt)

    o = lax.map(block, (qb, cq, jnp.arange(nb)))
    return o.transpose(1, 0, 3, 2, 4).reshape(bsz, s_len, n_h * dh)


def ssd(xh, dt, a, bm, cm, d_skip):
    bsz, s_len, n_h, p_dim = xh.shape
    g, n = bm.shape[2], bm.shape[3]
    r = n_h // g
    nc = s_len // CHUNK
    x = (xh * dt[..., None]).reshape(bsz, nc, CHUNK, g, r, p_dim)
    da = (dt * a).reshape(bsz, nc, CHUNK, g, r)
    bc = bm.reshape(bsz, nc, CHUNK, g, n)
    cc = cm.reshape(bsz, nc, CHUNK, g, n)
    cs = jnp.cumsum(da, axis=2)
    seg = cs[:, :, :, None] - cs[:, :, None, :]
    causal = jnp.tril(jnp.ones((CHUNK, CHUNK), dtype=bool))
    lmat = jnp.exp(jnp.where(causal[:, :, None, None], seg, -jnp.inf))
    cb = jnp.einsum("bclgn,bcsgn->bclsg", cc, bc)
    y_diag = jnp.einsum("bclsg,bclsgr,bcsgrp->bclgrp", cb, lmat, x)
    decay = jnp.exp(cs[:, :, -1:] - cs)
    states = jnp.einsum("bclgn,bclgr,bclgrp->bcgrpn", bc, decay, x)
    chunk_decay = jnp.exp(cs[:, :, -1])

    def step(h, inp):
        s_c, a_c = inp
        return h * a_c[..., None, None] + s_c, h

    h0 = jnp.zeros((bsz, g, r, p_dim, n), dtype=states.dtype)
    _, prev = lax.scan(step, h0, (jnp.moveaxis(states, 1, 0),
                                  jnp.moveaxis(chunk_decay, 1, 0).astype(states.dtype)))
    prev = jnp.moveaxis(prev, 0, 1)
    y_off = jnp.einsum("bclgn,bcgrpn,bclgr->bclgrp", cc, prev, jnp.exp(cs))
    y = (y_diag + y_off).reshape(bsz, s_len, n_h, p_dim) + xh * d_skip[:, None]
    return y.reshape(bsz, s_len, n_h * p_dim).astype(xh.dtype)


def hybrid_layer(x, norm_w, w_in, fg_bias, ssm_conv_w, ssm_conv_b, dt_bias, a_log, d_skip,
                 ssm_norm_w, sc_conv_w, sc_conv_b, cf_conv_w, cf_conv_b, cf_ln_w, cf_ln_b,
                 w_gate, b_gate, w_branch, w_out):
    bsz, s_len, _ = x.shape
    h = rms_norm(x, norm_w)
    u = h @ w_in
    (q, k, v, f_raw, g_a, z, xbc, dt_raw,
     sc_b, sc_c, sc_x, g_c, glu, g_d) = _split_columns(u, IN_SIZES)

    logf = jax.nn.log_sigmoid((f_raw + fg_bias).astype(jnp.float32))
    heads = lambda t: t.reshape(bsz, s_len, FOX_HEADS, FOX_HEAD_DIM)
    y_a = forgetting_attention(heads(q), heads(k), heads(v), logf) * jax.nn.silu(g_a)

    xbc = jax.nn.silu(causal_dwconv(xbc, ssm_conv_w, ssm_conv_b))
    xs, bm, cm = _split_columns(xbc, (SSM_WIDTH, SSM_GROUPS * SSM_STATE, SSM_GROUPS * SSM_STATE))
    dt = jax.nn.softplus((dt_raw + dt_bias).astype(jnp.float32))
    a = -jnp.exp(a_log.astype(jnp.float32))
    y = ssd(xs.reshape(bsz, s_len, SSM_HEADS, SSM_HEAD_DIM), dt, a,
            bm.reshape(bsz, s_len, SSM_GROUPS, SSM_STATE),
            cm.reshape(bsz, s_len, SSM_GROUPS, SSM_STATE), d_skip)
    y_b = rms_norm(y * jax.nn.silu(z), ssm_norm_w)

    y_c = sc_b * causal_dwconv(sc_c * sc_x, sc_conv_w, sc_conv_b) * jax.nn.silu(g_c)

    glu_a, glu_g = _split_columns(glu, (CF_WIDTH, CF_WIDTH))
    cf = causal_dwconv(glu_a * jax.nn.sigmoid(glu_g), cf_conv_w, cf_conv_b)
    y_d = jax.nn.silu(layer_norm(cf, cf_ln_w, cf_ln_b)) * jax.nn.silu(g_d)

    merged = None
    for i, y_i in enumerate((y_a, y_b, y_c, y_d)):
        gate = jax.nn.sigmoid(h @ w_gate[i] + b_gate[i])
        term = gate * (y_i @ w_branch[i])
        merged = term if merged is None else merged + term
    return x + merged @ w_out


def _fwd_setup_inputs(seed: int = 0) -> dict:
    key = jax.random.key(seed)
    ks = jax.random.split(key, 24)
    f32 = jnp.float32
    L = DEPTH

    def nrm(k, shape, scale):
        return jax.random.normal(k, shape, f32) * scale

    dt0 = jnp.exp(jax.random.uniform(ks[6], (L, SSM_HEADS), f32, math.log(1e-3), math.log(1e-1)))
    return {
        "x": nrm(ks[0], (BATCH, SEQ, D_MODEL), 1.0),
        "norm_w": 1.0 + nrm(ks[1], (L, D_MODEL), 0.02),
        "w_in": nrm(ks[2], (L, D_MODEL, N_IN), D_MODEL ** -0.5),
        "fg_bias": jax.random.uniform(ks[3], (L, FOX_HEADS), f32, 1.0, 6.0),
        "ssm_conv_w": nrm(ks[4], (L, SSM_CONV, SSM_CONV_DIM), SSM_CONV ** -0.5),
        "ssm_conv_b": nrm(ks[5], (L, SSM_CONV_DIM), 0.02),
        "dt_bias": dt0 + jnp.log(-jnp.expm1(-dt0)),
        "a_log": jnp.log(jax.random.uniform(ks[7], (L, SSM_HEADS), f32, 1.0, 16.0)),
        "d_skip": 1.0 + nrm(ks[8], (L, SSM_HEADS), 0.02),
        "ssm_norm_w": 1.0 + nrm(ks[9], (L, SSM_WIDTH), 0.02),
        "sc_conv_w": nrm(ks[10], (L, SC_CONV, SC_WIDTH), SC_CONV ** -0.5),
        "sc_conv_b": nrm(ks[11], (L, SC_WIDTH), 0.02),
        "cf_conv_w": nrm(ks[12], (L, CF_CONV, CF_WIDTH), CF_CONV ** -0.5),
        "cf_conv_b": nrm(ks[13], (L, CF_WIDTH), 0.02),
        "cf_ln_w": 1.0 + nrm(ks[14], (L, CF_WIDTH), 0.02),
        "cf_ln_b": nrm(ks[15], (L, CF_WIDTH), 0.02),
        "w_gate": nrm(ks[16], (L, N_BRANCH, D_MODEL, D_MODEL), D_MODEL ** -0.5),
        "b_gate": nrm(ks[17], (L, N_BRANCH, D_MODEL), 0.02),
        "w_branch": nrm(ks[18], (L, N_BRANCH, BRANCH_WIDTH, D_MODEL), BRANCH_WIDTH ** -0.5),
        "w_out": nrm(ks[19], (L, D_MODEL, D_MODEL), D_MODEL ** -0.5),
        "final_norm_w": 1.0 + nrm(ks[20], (D_MODEL,), 0.02),
    }


def _fwd_reference(x, norm_w, w_in, fg_bias, ssm_conv_w, ssm_conv_b, dt_bias, a_log, d_skip,
              ssm_norm_w, sc_conv_w, sc_conv_b, cf_conv_w, cf_conv_b, cf_ln_w, cf_ln_b,
              w_gate, b_gate, w_branch, w_out, final_norm_w):
    for l in range(DEPTH):
        x = hybrid_layer(x, norm_w[l], w_in[l], fg_bias[l], ssm_conv_w[l], ssm_conv_b[l],
                         dt_bias[l], a_log[l], d_skip[l], ssm_norm_w[l], sc_conv_w[l],
                         sc_conv_b[l], cf_conv_w[l], cf_conv_b[l], cf_ln_w[l], cf_ln_b[l],
                         w_gate[l], b_gate[l], w_branch[l], w_out[l])
    return rms_norm(x, final_norm_w)


import jax as _jax
import jax.numpy as _jnp

TWIN_FORMAT = 'train_step'
FWD_PARAMS = ['x', 'norm_w', 'w_in', 'fg_bias', 'ssm_conv_w', 'ssm_conv_b', 'dt_bias', 'a_log', 'd_skip', 'ssm_norm_w', 'sc_conv_w', 'sc_conv_b', 'cf_conv_w', 'cf_conv_b', 'cf_ln_w', 'cf_ln_b', 'w_gate', 'b_gate', 'w_branch', 'w_out', 'final_norm_w']
TWIN_WEIGHTS = ['norm_w', 'w_in', 'fg_bias', 'ssm_conv_w', 'ssm_conv_b', 'dt_bias', 'a_log', 'd_skip', 'ssm_norm_w', 'sc_conv_w', 'sc_conv_b', 'cf_conv_w', 'cf_conv_b', 'cf_ln_w', 'cf_ln_b', 'w_gate', 'b_gate', 'w_branch', 'w_out', 'final_norm_w']
TWIN_DIFF_INPUT = 'x'
TWIN_INPUTS = ['x', 'norm_w', 'w_in', 'fg_bias', 'ssm_conv_w', 'ssm_conv_b', 'dt_bias', 'a_log', 'd_skip', 'ssm_norm_w', 'sc_conv_w', 'sc_conv_b', 'cf_conv_w', 'cf_conv_b', 'cf_ln_w', 'cf_ln_b', 'w_gate', 'b_gate', 'w_branch', 'w_out', 'final_norm_w', 'loss_target', 'm_norm_w', 'm_w_in', 'm_fg_bias', 'm_ssm_conv_w', 'm_ssm_conv_b', 'm_dt_bias', 'm_a_log', 'm_d_skip', 'm_ssm_norm_w', 'm_sc_conv_w', 'm_sc_conv_b', 'm_cf_conv_w', 'm_cf_conv_b', 'm_cf_ln_w', 'm_cf_ln_b', 'm_w_gate', 'm_b_gate', 'm_w_branch', 'm_w_out', 'm_final_norm_w', 'v_norm_w', 'v_w_in', 'v_fg_bias', 'v_ssm_conv_w', 'v_ssm_conv_b', 'v_dt_bias', 'v_a_log', 'v_d_skip', 'v_ssm_norm_w', 'v_sc_conv_w', 'v_sc_conv_b', 'v_cf_conv_w', 'v_cf_conv_b', 'v_cf_ln_w', 'v_cf_ln_b', 'v_w_gate', 'v_b_gate', 'v_w_branch', 'v_w_out', 'v_final_norm_w']
TWIN_OUTPUTS = ['loss', 'grad_x', 'grad_norm_w', 'grad_w_in', 'grad_fg_bias', 'grad_ssm_conv_w', 'grad_ssm_conv_b', 'grad_dt_bias', 'grad_a_log', 'grad_d_skip', 'grad_ssm_norm_w', 'grad_sc_conv_w', 'grad_sc_conv_b', 'grad_cf_conv_w', 'grad_cf_conv_b', 'grad_cf_ln_w', 'grad_cf_ln_b', 'grad_w_gate', 'grad_b_gate', 'grad_w_branch', 'grad_w_out', 'grad_final_norm_w', 'delta_norm_w', 'delta_w_in', 'delta_fg_bias', 'delta_ssm_conv_w', 'delta_ssm_conv_b', 'delta_dt_bias', 'delta_a_log', 'delta_d_skip', 'delta_ssm_norm_w', 'delta_sc_conv_w', 'delta_sc_conv_b', 'delta_cf_conv_w', 'delta_cf_conv_b', 'delta_cf_ln_w', 'delta_cf_ln_b', 'delta_w_gate', 'delta_b_gate', 'delta_w_branch', 'delta_w_out', 'delta_final_norm_w', 'new_m_norm_w', 'new_m_w_in', 'new_m_fg_bias', 'new_m_ssm_conv_w', 'new_m_ssm_conv_b', 'new_m_dt_bias', 'new_m_a_log', 'new_m_d_skip', 'new_m_ssm_norm_w', 'new_m_sc_conv_w', 'new_m_sc_conv_b', 'new_m_cf_conv_w', 'new_m_cf_conv_b', 'new_m_cf_ln_w', 'new_m_cf_ln_b', 'new_m_w_gate', 'new_m_b_gate', 'new_m_w_branch', 'new_m_w_out', 'new_m_final_norm_w', 'new_v_norm_w', 'new_v_w_in', 'new_v_fg_bias', 'new_v_ssm_conv_w', 'new_v_ssm_conv_b', 'new_v_dt_bias', 'new_v_a_log', 'new_v_d_skip', 'new_v_ssm_norm_w', 'new_v_sc_conv_w', 'new_v_sc_conv_b', 'new_v_cf_conv_w', 'new_v_cf_conv_b', 'new_v_cf_ln_w', 'new_v_cf_ln_b', 'new_v_w_gate', 'new_v_b_gate', 'new_v_w_branch', 'new_v_w_out', 'new_v_final_norm_w']
TWIN_LEAF_KINDS = {'loss': 'loss', 'grad_x': 'grad_x', 'grad_norm_w': 'grad_w', 'grad_w_in': 'grad_w', 'grad_fg_bias': 'grad_w', 'grad_ssm_conv_w': 'grad_w', 'grad_ssm_conv_b': 'grad_w', 'grad_dt_bias': 'grad_w', 'grad_a_log': 'grad_w', 'grad_d_skip': 'grad_w', 'grad_ssm_norm_w': 'grad_w', 'grad_sc_conv_w': 'grad_w', 'grad_sc_conv_b': 'grad_w', 'grad_cf_conv_w': 'grad_w', 'grad_cf_conv_b': 'grad_w', 'grad_cf_ln_w': 'grad_w', 'grad_cf_ln_b': 'grad_w', 'grad_w_gate': 'grad_w', 'grad_b_gate': 'grad_w', 'grad_w_branch': 'grad_w', 'grad_w_out': 'grad_w', 'grad_final_norm_w': 'grad_w', 'delta_norm_w': 'delta_w', 'delta_w_in': 'delta_w', 'delta_fg_bias': 'delta_w', 'delta_ssm_conv_w': 'delta_w', 'delta_ssm_conv_b': 'delta_w', 'delta_dt_bias': 'delta_w', 'delta_a_log': 'delta_w', 'delta_d_skip': 'delta_w', 'delta_ssm_norm_w': 'delta_w', 'delta_sc_conv_w': 'delta_w', 'delta_sc_conv_b': 'delta_w', 'delta_cf_conv_w': 'delta_w', 'delta_cf_conv_b': 'delta_w', 'delta_cf_ln_w': 'delta_w', 'delta_cf_ln_b': 'delta_w', 'delta_w_gate': 'delta_w', 'delta_b_gate': 'delta_w', 'delta_w_branch': 'delta_w', 'delta_w_out': 'delta_w', 'delta_final_norm_w': 'delta_w', 'new_m_norm_w': 'new_m', 'new_m_w_in': 'new_m', 'new_m_fg_bias': 'new_m', 'new_m_ssm_conv_w': 'new_m', 'new_m_ssm_conv_b': 'new_m', 'new_m_dt_bias': 'new_m', 'new_m_a_log': 'new_m', 'new_m_d_skip': 'new_m', 'new_m_ssm_norm_w': 'new_m', 'new_m_sc_conv_w': 'new_m', 'new_m_sc_conv_b': 'new_m', 'new_m_cf_conv_w': 'new_m', 'new_m_cf_conv_b': 'new_m', 'new_m_cf_ln_w': 'new_m', 'new_m_cf_ln_b': 'new_m', 'new_m_w_gate': 'new_m', 'new_m_b_gate': 'new_m', 'new_m_w_branch': 'new_m', 'new_m_w_out': 'new_m', 'new_m_final_norm_w': 'new_m', 'new_v_norm_w': 'new_v', 'new_v_w_in': 'new_v', 'new_v_fg_bias': 'new_v', 'new_v_ssm_conv_w': 'new_v', 'new_v_ssm_conv_b': 'new_v', 'new_v_dt_bias': 'new_v', 'new_v_a_log': 'new_v', 'new_v_d_skip': 'new_v', 'new_v_ssm_norm_w': 'new_v', 'new_v_sc_conv_w': 'new_v', 'new_v_sc_conv_b': 'new_v', 'new_v_cf_conv_w': 'new_v', 'new_v_cf_conv_b': 'new_v', 'new_v_cf_ln_w': 'new_v', 'new_v_cf_ln_b': 'new_v', 'new_v_w_gate': 'new_v', 'new_v_b_gate': 'new_v', 'new_v_w_branch': 'new_v', 'new_v_w_out': 'new_v', 'new_v_final_norm_w': 'new_v'}


def _forward(args):
    return _fwd_reference(*[args[k] for k in FWD_PARAMS])


def _output_shape():
    def fwd():
        inp = _fwd_setup_inputs(0)
        return _fwd_reference(*[inp[k] for k in FWD_PARAMS])
    out = _jax.eval_shape(fwd)
    return out.shape, out.dtype

N_MICROBATCH = 1
ADAM_LR = 0.001
ADAM_B1 = 0.9
ADAM_B2 = 0.999
ADAM_EPS = 1e-08
ADAM_WD = 0.01
ADAM_STEP = 10
PER_EXAMPLE_BATCH_AXIS = {'x': 0, 'loss_target': 0}
SHARED_INPUTS = []
_WEIGHT_DTYPES = {'norm_w': _jnp.float32, 'w_in': _jnp.float32, 'fg_bias': _jnp.float32, 'ssm_conv_w': _jnp.float32, 'ssm_conv_b': _jnp.float32, 'dt_bias': _jnp.float32, 'a_log': _jnp.float32, 'd_skip': _jnp.float32, 'ssm_norm_w': _jnp.float32, 'sc_conv_w': _jnp.float32, 'sc_conv_b': _jnp.float32, 'cf_conv_w': _jnp.float32, 'cf_conv_b': _jnp.float32, 'cf_ln_w': _jnp.float32, 'cf_ln_b': _jnp.float32, 'w_gate': _jnp.float32, 'b_gate': _jnp.float32, 'w_branch': _jnp.float32, 'w_out': _jnp.float32, 'final_norm_w': _jnp.float32}
MOMENT_SCALE = {'norm_w': 1.345201e-01, 'w_in': 5.083464e-02, 'fg_bias': 1.416937e-01, 'ssm_conv_w': 7.964517e-02, 'ssm_conv_b': 1.061394e-01, 'dt_bias': 2.217053e-01, 'a_log': 2.942780e-01, 'd_skip': 4.323032e-01, 'ssm_norm_w': 9.117744e-02, 'sc_conv_w': 5.454305e-02, 'sc_conv_b': 5.496285e-02, 'cf_conv_w': 3.375361e-02, 'cf_conv_b': 7.266077e-02, 'cf_ln_w': 3.978957e-02, 'cf_ln_b': 3.359488e-02, 'w_gate': 1.555643e-02, 'b_gate': 1.578231e-02, 'w_branch': 3.985569e-02, 'w_out': 7.976383e-02, 'final_norm_w': 3.202349e+01}


def _to_microbatches(a, axis):
    t = _jnp.moveaxis(a, axis, 0)
    t = t.reshape((N_MICROBATCH, t.shape[0] // N_MICROBATCH) + t.shape[1:])
    return _jnp.moveaxis(t, 1, axis + 1)


def setup_inputs(seed: int = 0) -> dict:
    inp = _fwd_setup_inputs(seed)
    key = _jax.random.fold_in(_jax.random.key(seed), 7919)
    shape, _ = _output_shape()
    out = dict(inp)
    out["loss_target"] = _jax.random.normal(_jax.random.fold_in(key, 0), shape, _jnp.float32)
    for i, name in enumerate(TWIN_WEIGHTS):
        w = inp[name].astype(_jnp.float32)
        if MOMENT_SCALE is None:
            s = _jnp.sqrt(_jnp.mean(_jnp.square(w)) + 1e-30)
        else:
            s = MOMENT_SCALE[name]
        km, kv = _jax.random.split(_jax.random.fold_in(key, i + 1))
        out[name] = w
        out["m_" + name] = s * _jax.random.normal(km, w.shape, _jnp.float32)
        out["v_" + name] = (s * s) * _jax.random.uniform(kv, w.shape, _jnp.float32, 0.5, 1.5)
    if N_MICROBATCH > 1:
        for name, axis in PER_EXAMPLE_BATCH_AXIS.items():
            out[name] = _to_microbatches(out[name], axis)
    return {'x': out['x'], 'norm_w': out['norm_w'], 'w_in': out['w_in'], 'fg_bias': out['fg_bias'], 'ssm_conv_w': out['ssm_conv_w'], 'ssm_conv_b': out['ssm_conv_b'], 'dt_bias': out['dt_bias'], 'a_log': out['a_log'], 'd_skip': out['d_skip'], 'ssm_norm_w': out['ssm_norm_w'], 'sc_conv_w': out['sc_conv_w'], 'sc_conv_b': out['sc_conv_b'], 'cf_conv_w': out['cf_conv_w'], 'cf_conv_b': out['cf_conv_b'], 'cf_ln_w': out['cf_ln_w'], 'cf_ln_b': out['cf_ln_b'], 'w_gate': out['w_gate'], 'b_gate': out['b_gate'], 'w_branch': out['w_branch'], 'w_out': out['w_out'], 'final_norm_w': out['final_norm_w'], 'loss_target': out['loss_target'], 'm_norm_w': out['m_norm_w'], 'm_w_in': out['m_w_in'], 'm_fg_bias': out['m_fg_bias'], 'm_ssm_conv_w': out['m_ssm_conv_w'], 'm_ssm_conv_b': out['m_ssm_conv_b'], 'm_dt_bias': out['m_dt_bias'], 'm_a_log': out['m_a_log'], 'm_d_skip': out['m_d_skip'], 'm_ssm_norm_w': out['m_ssm_norm_w'], 'm_sc_conv_w': out['m_sc_conv_w'], 'm_sc_conv_b': out['m_sc_conv_b'], 'm_cf_conv_w': out['m_cf_conv_w'], 'm_cf_conv_b': out['m_cf_conv_b'], 'm_cf_ln_w': out['m_cf_ln_w'], 'm_cf_ln_b': out['m_cf_ln_b'], 'm_w_gate': out['m_w_gate'], 'm_b_gate': out['m_b_gate'], 'm_w_branch': out['m_w_branch'], 'm_w_out': out['m_w_out'], 'm_final_norm_w': out['m_final_norm_w'], 'v_norm_w': out['v_norm_w'], 'v_w_in': out['v_w_in'], 'v_fg_bias': out['v_fg_bias'], 'v_ssm_conv_w': out['v_ssm_conv_w'], 'v_ssm_conv_b': out['v_ssm_conv_b'], 'v_dt_bias': out['v_dt_bias'], 'v_a_log': out['v_a_log'], 'v_d_skip': out['v_d_skip'], 'v_ssm_norm_w': out['v_ssm_norm_w'], 'v_sc_conv_w': out['v_sc_conv_w'], 'v_sc_conv_b': out['v_sc_conv_b'], 'v_cf_conv_w': out['v_cf_conv_w'], 'v_cf_conv_b': out['v_cf_conv_b'], 'v_cf_ln_w': out['v_cf_ln_w'], 'v_cf_ln_b': out['v_cf_ln_b'], 'v_w_gate': out['v_w_gate'], 'v_b_gate': out['v_b_gate'], 'v_w_branch': out['v_w_branch'], 'v_w_out': out['v_w_out'], 'v_final_norm_w': out['v_final_norm_w']}


def _loss(weights, diff, rest, loss_target):
    with _jax.named_scope("forward"):
        args = {**rest, TWIN_DIFF_INPUT: diff, **{k: w.astype(_WEIGHT_DTYPES[k]) for k, w in weights.items()}}
        y = _forward(args)
    with _jax.named_scope("loss_head"):
        err = _jnp.square(y.astype(_jnp.float32) - loss_target)
        return 0.5 * _jnp.sum(_jnp.mean(err, axis=-1)) if err.ndim else 0.5 * err


def _adamw(w, g, m, v):
    m = ADAM_B1 * m + (1.0 - ADAM_B1) * g
    v = ADAM_B2 * v + (1.0 - ADAM_B2) * _jnp.square(g)
    m_hat = m / (1.0 - ADAM_B1 ** ADAM_STEP)
    v_hat = v / (1.0 - ADAM_B2 ** ADAM_STEP)
    delta = -ADAM_LR * (m_hat / (_jnp.sqrt(v_hat) + ADAM_EPS) + ADAM_WD * w)
    return delta, m, v


def reference(x, norm_w, w_in, fg_bias, ssm_conv_w, ssm_conv_b, dt_bias, a_log, d_skip, ssm_norm_w, sc_conv_w, sc_conv_b, cf_conv_w, cf_conv_b, cf_ln_w, cf_ln_b, w_gate, b_gate, w_branch, w_out, final_norm_w, loss_target, m_norm_w, m_w_in, m_fg_bias, m_ssm_conv_w, m_ssm_conv_b, m_dt_bias, m_a_log, m_d_skip, m_ssm_norm_w, m_sc_conv_w, m_sc_conv_b, m_cf_conv_w, m_cf_conv_b, m_cf_ln_w, m_cf_ln_b, m_w_gate, m_b_gate, m_w_branch, m_w_out, m_final_norm_w, v_norm_w, v_w_in, v_fg_bias, v_ssm_conv_w, v_ssm_conv_b, v_dt_bias, v_a_log, v_d_skip, v_ssm_norm_w, v_sc_conv_w, v_sc_conv_b, v_cf_conv_w, v_cf_conv_b, v_cf_ln_w, v_cf_ln_b, v_w_gate, v_b_gate, v_w_branch, v_w_out, v_final_norm_w):
    given = dict(x=x, norm_w=norm_w, w_in=w_in, fg_bias=fg_bias, ssm_conv_w=ssm_conv_w, ssm_conv_b=ssm_conv_b, dt_bias=dt_bias, a_log=a_log, d_skip=d_skip, ssm_norm_w=ssm_norm_w, sc_conv_w=sc_conv_w, sc_conv_b=sc_conv_b, cf_conv_w=cf_conv_w, cf_conv_b=cf_conv_b, cf_ln_w=cf_ln_w, cf_ln_b=cf_ln_b, w_gate=w_gate, b_gate=b_gate, w_branch=w_branch, w_out=w_out, final_norm_w=final_norm_w, loss_target=loss_target, m_norm_w=m_norm_w, m_w_in=m_w_in, m_fg_bias=m_fg_bias, m_ssm_conv_w=m_ssm_conv_w, m_ssm_conv_b=m_ssm_conv_b, m_dt_bias=m_dt_bias, m_a_log=m_a_log, m_d_skip=m_d_skip, m_ssm_norm_w=m_ssm_norm_w, m_sc_conv_w=m_sc_conv_w, m_sc_conv_b=m_sc_conv_b, m_cf_conv_w=m_cf_conv_w, m_cf_conv_b=m_cf_conv_b, m_cf_ln_w=m_cf_ln_w, m_cf_ln_b=m_cf_ln_b, m_w_gate=m_w_gate, m_b_gate=m_b_gate, m_w_branch=m_w_branch, m_w_out=m_w_out, m_final_norm_w=m_final_norm_w, v_norm_w=v_norm_w, v_w_in=v_w_in, v_fg_bias=v_fg_bias, v_ssm_conv_w=v_ssm_conv_w, v_ssm_conv_b=v_ssm_conv_b, v_dt_bias=v_dt_bias, v_a_log=v_a_log, v_d_skip=v_d_skip, v_ssm_norm_w=v_ssm_norm_w, v_sc_conv_w=v_sc_conv_w, v_sc_conv_b=v_sc_conv_b, v_cf_conv_w=v_cf_conv_w, v_cf_conv_b=v_cf_conv_b, v_cf_ln_w=v_cf_ln_w, v_cf_ln_b=v_cf_ln_b, v_w_gate=v_w_gate, v_b_gate=v_b_gate, v_w_branch=v_w_branch, v_w_out=v_w_out, v_final_norm_w=v_final_norm_w)
    weights = {n: given[n] for n in TWIN_WEIGHTS}
    shared = {n: given[n] for n in SHARED_INPUTS}
    per_example = {n: given[n] for n in ['x']}
    grad_fn = _jax.value_and_grad(_loss, argnums=(0, 1))

    def one_microbatch(ex, loss_target):
        ex = dict(ex)
        diff = ex.pop(TWIN_DIFF_INPUT)
        return grad_fn(weights, diff, {**shared, **ex}, loss_target)

    if N_MICROBATCH == 1:
        loss, (grad_w, grad_x) = one_microbatch(per_example, given["loss_target"])
    else:
        def body(carry, xs):
            loss_sum, grad_sum = carry
            l_k, (gw_k, gx_k) = one_microbatch(xs[0], xs[1])
            with _jax.named_scope("update"):
                return (loss_sum + l_k, _jax.tree.map(_jnp.add, grad_sum, gw_k)), gx_k

        init = (_jnp.zeros((), _jnp.float32), _jax.tree.map(_jnp.zeros_like, weights))
        (loss, grad_w), grad_x = _jax.lax.scan(body, init, (per_example, given["loss_target"]))
    with _jax.named_scope("update"):
        delta_w, new_m, new_v = {}, {}, {}
        for n in TWIN_WEIGHTS:
            delta_w[n], new_m[n], new_v[n] = _adamw(weights[n], grad_w[n], given["m_" + n], given["v_" + n])
    return (loss, grad_x, *[grad_w[n] for n in TWIN_WEIGHTS], *[delta_w[n] for n in TWIN_WEIGHTS],
            *[new_m[n] for n in TWIN_WEIGHTS], *[new_v[n] for n in TWIN_WEIGHTS])
```

```python
import functools
import math

import jax
import jax.numpy as jnp
from jax import lax
from jax.experimental import pallas as pl
from jax.experimental.pallas import tpu as pltpu

F32, BF16 = jnp.float32, jnp.bfloat16
MESH = pl.DeviceIdType.MESH

D_MODEL = 2048
BW = D_MODEL // 2
FOX_HEADS, FOX_HD = 8, 128
SSM_HEADS, SSM_P, SSM_N, SSM_G = 16, 64, 128, 2
SSM_CONV_DIM = BW + 2 * SSM_G * SSM_N
N_BRANCH = 4
EPS = 1e-6
NEG = -1e30
ORIG_SIZES = (BW, BW, BW, FOX_HEADS, BW, BW, SSM_CONV_DIM, SSM_HEADS, BW, BW, BW, BW, 2 * BW, BW)
ORIG_NAMES = ("q", "k", "v", "f", "ga", "z", "xbc", "dt", "scb", "scc", "scx", "gc", "glu", "gd")
N_IN = sum(ORIG_SIZES)
PAD_ORDER = ("q", "k", "v", "ga", "z", "scb", "scc", "scx", "gc", "glu", "gd", "xbc")
SMALL_W = 512
DT_LANE, F_LANE = 0, SSM_HEADS
OFF = {}
_o = 0
for _n in PAD_ORDER:
    OFF[_n] = _o
    _o += ORIG_SIZES[ORIG_NAMES.index(_n)]
OFF["small"] = _o
NP = _o + SMALL_W
LANES = 128
CONV_HALO = 32

VMEM_LIMIT = 56 * 1024 * 1024
MM_VMEM_BUDGET = 40 * 1024 * 1024

ADAM_LR, ADAM_B1, ADAM_B2, ADAM_EPS, ADAM_WD, ADAM_STEP = 0.001, 0.9, 0.999, 1e-08, 0.01, 10

TILES = dict(row=512, post=256, att=512, ssd=256, mm_m=1024, mm_n=1024, conv_c=256)


def _cp(sem=None):
    return pltpu.CompilerParams(dimension_semantics=sem, vmem_limit_bytes=VMEM_LIMIT)


def _tile(n, pref):
    t = 1 << (min(n, pref).bit_length() - 1)
    while n % t:
        t //= 2
    return t


def _sig(x):
    return 1.0 / (1.0 + jnp.exp(-x))


def _silu(x):
    return x * _sig(x)


def _dsilu(x):
    s = _sig(x)
    return s * (1.0 + x * (1.0 - s))


def _softplus(x):
    return jnp.maximum(x, 0.0) + jnp.log(1.0 + jnp.exp(-jnp.abs(x)))


def mm(a, b, *, name, out_dtype=F32, add=None, M=None, K=None, N=None, a_koff=0, b_koff=0, b_noff=0):
    M = M or a.shape[0]
    K = K or a.shape[1]
    N = N or b.shape[1]
    tm, tn = _tile(M, TILES["mm_m"]), _tile(N, TILES["mm_n"])
    sa, sb, so = a.dtype.itemsize, b.dtype.itemsize, jnp.dtype(out_dtype).itemsize

    def need(tk):
        return 2 * tm * tk * sa + 2 * tk * tn * sb + 2 * tm * tn * so + (8 * tm * tn if add is not None else 0) + 4 * tm * tn

    tk = K
    while need(tk) > MM_VMEM_BUDGET and tk % 256 == 0:
        tk //= 2
    assert K % tk == 0 and a_koff % tk == 0 and b_koff % tk == 0 and b_noff % tn == 0, (name, K, tk, tn)
    nk = K // tk
    ako, bko, bno = a_koff // tk, b_koff // tk, b_noff // tn

    def body(*refs):
        if add is not None:
            a_ref, b_ref, add_ref, o_ref = refs[:4]
        else:
            a_ref, b_ref, o_ref = refs[:3]
            add_ref = None
        prod = jnp.dot(a_ref[...].astype(BF16), b_ref[...].astype(BF16), preferred_element_type=F32)

        def finish(acc):
            if add_ref is not None:
                acc = acc + add_ref[...]
            o_ref[...] = acc.astype(out_dtype)

        if nk == 1:
            finish(prod)
        else:
            acc_ref = refs[-1]
            k = pl.program_id(2)

            @pl.when(k == 0)
            def _():
                acc_ref[...] = prod

            @pl.when(k > 0)
            def _():
                acc_ref[...] += prod

            @pl.when(k == nk - 1)
            def _():
                finish(acc_ref[...])

    in_specs = [pl.BlockSpec((tm, tk), lambda i, j, k: (i, k + ako)),
                pl.BlockSpec((tk, tn), lambda i, j, k: (k + bko, j + bno))]
    args = [a, b]
    if add is not None:
        in_specs.append(pl.BlockSpec((tm, tn), lambda i, j, k: (i, j)))
        args.append(add)
    return pl.pallas_call(
        body, name=name, grid=(M // tm, N // tn, nk),
        in_specs=in_specs, out_specs=pl.BlockSpec((tm, tn), lambda i, j, k: (i, j)),
        out_shape=jax.ShapeDtypeStruct((M, N), out_dtype),
        scratch_shapes=[pltpu.VMEM((tm, tn), F32)] if nk > 1 else [],
        compiler_params=_cp(("parallel", "parallel", "arbitrary")),
    )(*args)


def rms_fwd(x, w, *, name):
    S, D = x.shape
    ts = _tile(S, TILES["row"])

    def body(x_ref, w_ref, o_ref):
        xf = x_ref[...]
        r = lax.rsqrt(jnp.mean(xf * xf, axis=-1, keepdims=True) + EPS)
        o_ref[...] = (xf * r * w_ref[...]).astype(BF16)

    return pl.pallas_call(
        body, name=name, grid=(S // ts,),
        in_specs=[pl.BlockSpec((ts, D), lambda i: (i, 0)), pl.BlockSpec((1, D), lambda i: (0, 0))],
        out_specs=pl.BlockSpec((ts, D), lambda i: (i, 0)),
        out_shape=jax.ShapeDtypeStruct((S, D), BF16), compiler_params=_cp(("parallel",)),
    )(x, w)


def rms_bwd(dh, x, w, dres, *, name):
    S, D = x.shape
    ts = _tile(S, TILES["row"])

    def body(dh_ref, x_ref, w_ref, dres_ref, dx_ref, dw_ref):
        xf = x_ref[...]
        r = lax.rsqrt(jnp.mean(xf * xf, axis=-1, keepdims=True) + EPS)
        xh = xf * r
        g = dh_ref[...]
        dxh = g * w_ref[...]
        dx_ref[...] = dres_ref[...] + r * (dxh - xh * jnp.mean(dxh * xh, axis=-1, keepdims=True))
        part = jnp.sum(g * xh, axis=0, keepdims=True)

        @pl.when(pl.program_id(0) == 0)
        def _():
            dw_ref[...] = part

        @pl.when(pl.program_id(0) > 0)
        def _():
            dw_ref[...] += part

    row = pl.BlockSpec((ts, D), lambda i: (i, 0))
    vec = pl.BlockSpec((1, D), lambda i: (0, 0))
    return pl.pallas_call(
        body, name=name, grid=(S // ts,), in_specs=[row, row, vec, row], out_specs=[row, vec],
        out_shape=[jax.ShapeDtypeStruct((S, D), F32), jax.ShapeDtypeStruct((1, D), F32)],
        compiler_params=_cp(("arbitrary",)),
    )(dh, x, w, dres)


def loss_head(x, w, target, *, name):
    S, D = x.shape
    ts = _tile(S, TILES["row"])

    def body(x_ref, w_ref, t_ref, loss_ref, dx_ref, dw_ref):
        xf = x_ref[...]
        r = lax.rsqrt(jnp.mean(xf * xf, axis=-1, keepdims=True) + EPS)
        xh = xf * r
        err = xh * w_ref[...] - t_ref[...]
        sq = jnp.sum(jnp.sum(err * err, axis=0, keepdims=True), axis=1, keepdims=True)
        dy = err * (1.0 / D)
        dxh = dy * w_ref[...]
        dx_ref[...] = r * (dxh - xh * jnp.mean(dxh * xh, axis=-1, keepdims=True))
        part = jnp.sum(dy * xh, axis=0, keepdims=True)

        @pl.when(pl.program_id(0) == 0)
        def _():
            dw_ref[...] = part
            loss_ref[...] = jnp.broadcast_to(sq, (8, LANES))

        @pl.when(pl.program_id(0) > 0)
        def _():
            dw_ref[...] += part
            loss_ref[...] += jnp.broadcast_to(sq, (8, LANES))

    row = pl.BlockSpec((ts, D), lambda i: (i, 0))
    vec = pl.BlockSpec((1, D), lambda i: (0, 0))
    return pl.pallas_call(
        body, name=name, grid=(S // ts,), in_specs=[row, vec, row],
        out_specs=[pl.BlockSpec((8, LANES), lambda i: (0, 0)), row, vec],
        out_shape=[jax.ShapeDtypeStruct((8, LANES), F32), jax.ShapeDtypeStruct((S, D), F32),
                   jax.ShapeDtypeStruct((1, D), F32)],
        compiler_params=_cp(("arbitrary",)),
    )(x, w, target)


def _shift_scan(v, buf_ref, n, reverse=False):
    buf_ref[pl.ds(0, n), :] = jnp.zeros((n, LANES), F32)
    buf_ref[pl.ds(2 * n, n), :] = jnp.zeros((n, LANES), F32)
    s = 1
    while s < n:
        buf_ref[pl.ds(n, n), :] = v
        v = v + buf_ref[pl.ds(n + s if reverse else n - s, n), :]
        s *= 2
    return v


def small_fwd(u, bias, *, name):
    S = u.shape[0]
    ts = _tile(S, TILES["row"])
    cb = OFF["small"] // LANES

    def body(u_ref, b_ref, sp_ref, c_ref, buf_ref, carry_ref):
        x = u_ref[...] + b_ref[...]
        sp_ref[...] = _softplus(x)
        lf = jnp.minimum(x, 0.0) - jnp.log(1.0 + jnp.exp(-jnp.abs(x)))

        @pl.when(pl.program_id(0) == 0)
        def _():
            carry_ref[...] = jnp.zeros((8, LANES), F32)

        c = _shift_scan(lf, buf_ref, ts) + carry_ref[pl.ds(0, 1), :]
        c_ref[...] = c
        carry_ref[...] = jnp.broadcast_to(c_ref[pl.ds(ts - 1, 1), :], (8, LANES))

    blk = pl.BlockSpec((ts, LANES), lambda i: (i, 0))
    return pl.pallas_call(
        body, name=name, grid=(S // ts,),
        in_specs=[pl.BlockSpec((ts, LANES), lambda i: (i, cb)), pl.BlockSpec((1, LANES), lambda i: (0, 0))],
        out_specs=[blk, blk], out_shape=[jax.ShapeDtypeStruct((S, LANES), F32)] * 2,
        scratch_shapes=[pltpu.VMEM((3 * ts, LANES), F32), pltpu.VMEM((8, LANES), F32)],
        compiler_params=_cp(("arbitrary",)),
    )(u, bias)


def small_bwd(dc, dsp, u, bias, *, name):
    S = u.shape[0]
    ts = _tile(S, TILES["row"])
    cb = OFF["small"] // LANES
    nt = S // ts

    def body(dc_ref, dsp_ref, u_ref, b_ref, du_ref, db_ref, buf_ref, carry_ref):
        x = u_ref[...] + b_ref[...]

        @pl.when(pl.program_id(0) == 0)
        def _():
            carry_ref[...] = jnp.zeros((8, LANES), F32)

        dlf = _shift_scan(dc_ref[...], buf_ref, ts, reverse=True) + carry_ref[pl.ds(0, 1), :]
        buf_ref[pl.ds(0, ts), :] = dlf
        carry_ref[...] = jnp.broadcast_to(buf_ref[pl.ds(0, 1), :], (8, LANES))
        sg = _sig(x)
        dx = dlf * (1.0 - sg) + dsp_ref[...] * sg
        du_ref[...] = jnp.concatenate([dx, jnp.zeros((ts, SMALL_W - LANES), F32)], axis=1).astype(BF16)
        part = jnp.sum(dx, axis=0, keepdims=True)

        @pl.when(pl.program_id(0) == 0)
        def _():
            db_ref[...] = part

        @pl.when(pl.program_id(0) > 0)
        def _():
            db_ref[...] += part

    rev = pl.BlockSpec((ts, LANES), lambda i: (nt - 1 - i, 0))
    return pl.pallas_call(
        body, name=name, grid=(nt,),
        in_specs=[rev, rev, pl.BlockSpec((ts, LANES), lambda i: (nt - 1 - i, cb)), pl.BlockSpec((1, LANES), lambda i: (0, 0))],
        out_specs=[pl.BlockSpec((ts, SMALL_W), lambda i: (nt - 1 - i, 0)), pl.BlockSpec((1, LANES), lambda i: (0, 0))],
        out_shape=[jax.ShapeDtypeStruct((S, SMALL_W), BF16), jax.ShapeDtypeStruct((1, LANES), F32)],
        scratch_shapes=[pltpu.VMEM((3 * ts, LANES), F32), pltpu.VMEM((8, LANES), F32)],
        compiler_params=_cp(("arbitrary",)),
    )(dc, dsp, u, bias)


def _conv_in(mode, a, b):
    if mode == "plain":
        return a
    if mode == "mul":
        return a * b
    return a * _sig(b)


def conv_fwd(u, w, bias, *, mode, a_off, b_off, C, name):
    S = u.shape[0]
    Kc = w.shape[0]
    ts, cb, H = _tile(S, TILES["row"]), _tile(C, TILES["conv_c"]), CONV_HALO
    two = mode != "plain"
    rb = ts // H

    def body(*refs):
        if two:
            a_ref, ap_ref, b_ref, bp_ref, w_ref, bias_ref, o_ref, x_ref = refs
            cur, prev = _conv_in(mode, a_ref[...], b_ref[...]), _conv_in(mode, ap_ref[...], bp_ref[...])
        else:
            a_ref, ap_ref, w_ref, bias_ref, o_ref, x_ref = refs
            cur, prev = a_ref[...], ap_ref[...]
        x_ref[pl.ds(0, H), :] = jnp.where(pl.program_id(1) == 0, 0.0, prev)
        x_ref[pl.ds(H, ts), :] = cur
        acc = jnp.broadcast_to(bias_ref[...], (ts, cb))
        for j in range(Kc):
            acc = acc + w_ref[pl.ds(j, 1), :] * x_ref[pl.ds(H - (Kc - 1) + j, ts), :]
        o_ref[...] = acc

    def cur_spec(off):
        return pl.BlockSpec((ts, cb), lambda c, i: (i, c + off // cb))

    def prev_spec(off):
        return pl.BlockSpec((H, cb), lambda c, i: (jnp.maximum(i * rb - 1, 0), c + off // cb))

    in_specs, args = [cur_spec(a_off), prev_spec(a_off)], [u, u]
    if two:
        in_specs += [cur_spec(b_off), prev_spec(b_off)]
        args += [u, u]
    in_specs += [pl.BlockSpec((Kc, cb), lambda c, i: (0, c)), pl.BlockSpec((1, cb), lambda c, i: (0, c))]
    return pl.pallas_call(
        body, name=name, grid=(C // cb, S // ts), in_specs=in_specs,
        out_specs=pl.BlockSpec((ts, cb), lambda c, i: (i, c)), out_shape=jax.ShapeDtypeStruct((S, C), F32),
        scratch_shapes=[pltpu.VMEM((ts + H, cb), F32)], compiler_params=_cp(("parallel", "arbitrary")),
    )(*args, w, bias)


def conv_bwd(dy, u, w, *, mode, a_off, b_off, C, name):
    S = u.shape[0]
    Kc = w.shape[0]
    ts, cb, H = _tile(S, TILES["row"]), _tile(C, TILES["conv_c"]), CONV_HALO
    two = mode != "plain"
    rb, nt = ts // H, S // ts

    def body(*refs):
        if two:
            dy_ref, dyn_ref, a_ref, ap_ref, b_ref, bp_ref, w_ref, da_ref, db_ref, dw_ref, dbias_ref, x_ref, g_ref = refs
            a, b = a_ref[...], b_ref[...]
            cur, prev = _conv_in(mode, a, b), _conv_in(mode, ap_ref[...], bp_ref[...])
        else:
            dy_ref, dyn_ref, a_ref, ap_ref, w_ref, da_ref, dw_ref, dbias_ref, x_ref, g_ref = refs
            cur, prev = a_ref[...], ap_ref[...]
        i = pl.program_id(1)
        x_ref[pl.ds(0, H), :] = jnp.where(i == 0, 0.0, prev)
        x_ref[pl.ds(H, ts), :] = cur
        g = dy_ref[...]
        g_ref[pl.ds(0, ts), :] = g
        g_ref[pl.ds(ts, H), :] = jnp.where(i == nt - 1, 0.0, dyn_ref[...])

        @pl.when(i == 0)
        def _():
            dw_ref[...] = jnp.zeros((Kc, cb), F32)
            dbias_ref[...] = jnp.zeros((1, cb), F32)

        dbias_ref[...] += jnp.sum(g, axis=0, keepdims=True)
        dx = jnp.zeros((ts, cb), F32)
        for j in range(Kc):
            dx = dx + w_ref[pl.ds(j, 1), :] * g_ref[pl.ds(Kc - 1 - j, ts), :]
            dw_ref[pl.ds(j, 1), :] += jnp.sum(g * x_ref[pl.ds(H - (Kc - 1) + j, ts), :], axis=0, keepdims=True)
        if mode == "plain":
            da_ref[...] = dx.astype(BF16)
        elif mode == "mul":
            da_ref[...] = (dx * b).astype(BF16)
            db_ref[...] = (dx * a).astype(BF16)
        else:
            sg = _sig(b)
            da_ref[...] = (dx * sg).astype(BF16)
            db_ref[...] = (dx * a * sg * (1.0 - sg)).astype(BF16)

    def cur_spec(off):
        return pl.BlockSpec((ts, cb), lambda c, i: (i, c + off // cb))

    def prev_spec(off):
        return pl.BlockSpec((H, cb), lambda c, i: (jnp.maximum(i * rb - 1, 0), c + off // cb))

    out_blk = pl.BlockSpec((ts, cb), lambda c, i: (i, c))
    in_specs = [out_blk, pl.BlockSpec((H, cb), lambda c, i: (jnp.minimum((i + 1) * rb, S // H - 1), c)),
                cur_spec(a_off), prev_spec(a_off)]
    args = [dy, dy, u, u]
    if two:
        in_specs += [cur_spec(b_off), prev_spec(b_off)]
        args += [u, u]
    in_specs.append(pl.BlockSpec((Kc, cb), lambda c, i: (0, c)))
    n_d = 2 if two else 1
    return pl.pallas_call(
        body, name=name, grid=(C // cb, nt), in_specs=in_specs,
        out_specs=[out_blk] * n_d + [pl.BlockSpec((Kc, cb), lambda c, i: (0, c)), pl.BlockSpec((1, cb), lambda c, i: (0, c))],
        out_shape=[jax.ShapeDtypeStruct((S, C), BF16)] * n_d + [jax.ShapeDtypeStruct((Kc, C), F32), jax.ShapeDtypeStruct((1, C), F32)],
        scratch_shapes=[pltpu.VMEM((ts + H, cb), F32), pltpu.VMEM((ts + H, cb), F32)],
        compiler_params=_cp(("parallel", "arbitrary")),
    )(*args, w)


def _usec(ts, name):
    return pl.BlockSpec((ts, BW), lambda i, o=OFF[name] // BW: (i, o))


def _acc_rows(ref, part):
    @pl.when(pl.program_id(0) == 0)
    def _():
        ref[...] = part

    @pl.when(pl.program_id(0) > 0)
    def _():
        ref[...] += part


def post_fwd(u, y_ssd, cv_c, cv_d, nw, lnw, lnb, *, name):
    S = u.shape[0]
    ts = _tile(S, TILES["post"])

    def body(z_ref, scb_ref, gc_ref, gd_ref, ys_ref, cc_ref, cd_ref, nw_ref, lw_ref, lb_ref, yb_ref, yc_ref, yd_ref):
        t = ys_ref[...] * _silu(z_ref[...])
        r = lax.rsqrt(jnp.mean(t * t, axis=-1, keepdims=True) + EPS)
        yb_ref[...] = (t * r * nw_ref[...]).astype(BF16)
        yc_ref[...] = (scb_ref[...] * cc_ref[...] * _silu(gc_ref[...])).astype(BF16)
        cf = cd_ref[...]
        mu = jnp.mean(cf, axis=-1, keepdims=True)
        xc = cf - mu
        rl = lax.rsqrt(jnp.mean(xc * xc, axis=-1, keepdims=True) + EPS)
        yln = xc * rl * lw_ref[...] + lb_ref[...]
        yd_ref[...] = (_silu(yln) * _silu(gd_ref[...])).astype(BF16)

    row = pl.BlockSpec((ts, BW), lambda i: (i, 0))
    vec = pl.BlockSpec((1, BW), lambda i: (0, 0))
    return pl.pallas_call(
        body, name=name, grid=(S // ts,),
        in_specs=[_usec(ts, "z"), _usec(ts, "scb"), _usec(ts, "gc"), _usec(ts, "gd"), row, row, row, vec, vec, vec],
        out_specs=[row] * 3, out_shape=[jax.ShapeDtypeStruct((S, BW), BF16)] * 3, compiler_params=_cp(("parallel",)),
    )(u, u, u, u, y_ssd, cv_c, cv_d, nw, lnw, lnb)


def post_bwd(dy_a, dy_b, dy_c, dy_d, u, o, y_ssd, cv_c, cv_d, nw, lnw, lnb, *, name):
    S = u.shape[0]
    ts = _tile(S, TILES["post"])

    def body(dya_ref, dyb_ref, dyc_ref, dyd_ref, ga_ref, z_ref, scb_ref, gc_ref, gd_ref, o_ref, ys_ref, cc_ref, cd_ref,
             nw_ref, lw_ref, lb_ref,
             do_ref, dl_ref, dga_ref, dz_ref, dscb_ref, dgc_ref, dgd_ref, dys_ref, dcc_ref, dcd_ref, dnw_ref, dlw_ref, dlb_ref):
        ga, ov, dya = ga_ref[...], o_ref[...], dya_ref[...]
        dob = (dya * _silu(ga)).astype(BF16)
        do_ref[...] = dob
        dga_ref[...] = (dya * ov * _dsilu(ga)).astype(BF16)
        prod = dob.astype(F32) * ov
        lane = lax.broadcasted_iota(jnp.int32, (ts, LANES), 1)
        delta = jnp.zeros((ts, LANES), F32)
        for h in range(FOX_HEADS):
            col = jnp.sum(prod[:, h * FOX_HD:(h + 1) * FOX_HD], axis=1, keepdims=True)
            delta = jnp.where(lane == h, col, delta)
        dl_ref[...] = delta
        ys, z, dyb = ys_ref[...], z_ref[...], dyb_ref[...]
        sz = _silu(z)
        t = ys * sz
        r = lax.rsqrt(jnp.mean(t * t, axis=-1, keepdims=True) + EPS)
        th = t * r
        dth = dyb * nw_ref[...]
        dt_ = r * (dth - th * jnp.mean(dth * th, axis=-1, keepdims=True))
        dys_ref[...] = dt_ * sz
        dz_ref[...] = (dt_ * ys * _dsilu(z)).astype(BF16)
        _acc_rows(dnw_ref, jnp.sum(dyb * th, axis=0, keepdims=True))
        scb, cc, gc, dyc = scb_ref[...], cc_ref[...], gc_ref[...], dyc_ref[...]
        sg = _silu(gc)
        dscb_ref[...] = (dyc * cc * sg).astype(BF16)
        dcc_ref[...] = dyc * scb * sg
        dgc_ref[...] = (dyc * scb * cc * _dsilu(gc)).astype(BF16)
        cf, gd, dyd = cd_ref[...], gd_ref[...], dyd_ref[...]
        mu = jnp.mean(cf, axis=-1, keepdims=True)
        xc = cf - mu
        rl = lax.rsqrt(jnp.mean(xc * xc, axis=-1, keepdims=True) + EPS)
        xh = xc * rl
        yln = xh * lw_ref[...] + lb_ref[...]
        dyln = dyd * _silu(gd) * _dsilu(yln)
        dgd_ref[...] = (dyd * _silu(yln) * _dsilu(gd)).astype(BF16)
        _acc_rows(dlw_ref, jnp.sum(dyln * xh, axis=0, keepdims=True))
        _acc_rows(dlb_ref, jnp.sum(dyln, axis=0, keepdims=True))
        dxh = dyln * lw_ref[...]
        dcd_ref[...] = rl * (dxh - jnp.mean(dxh, axis=-1, keepdims=True) - xh * jnp.mean(dxh * xh, axis=-1, keepdims=True))

    row = pl.BlockSpec((ts, BW), lambda i: (i, 0))
    vec = pl.BlockSpec((1, BW), lambda i: (0, 0))
    sd = jax.ShapeDtypeStruct
    return pl.pallas_call(
        body, name=name, grid=(S // ts,),
        in_specs=[row] * 4 + [_usec(ts, n) for n in ("ga", "z", "scb", "gc", "gd")] + [row] * 4 + [vec] * 3,
        out_specs=[row, pl.BlockSpec((ts, LANES), lambda i: (i, 0))] + [row] * 8 + [vec] * 3,
        out_shape=[sd((S, BW), BF16), sd((S, LANES), F32)] + [sd((S, BW), BF16)] * 5 + [sd((S, BW), F32)] * 3 + [sd((1, BW), F32)] * 3,
        compiler_params=_cp(("arbitrary",)),
    )(dy_a, dy_b, dy_c, dy_d, u, u, u, u, u, o, y_ssd, cv_c, cv_d, nw, lnw, lnb)


def merge_fwd(h, ys, wg, bg, wb, *, name):
    S, D = h.shape
    tm, tn = _tile(S, 1024), _tile(D, 512)
    nb = D // tn

    def body(h_ref, y0, y1, y2, y3, wg_ref, bg_ref, wb_ref, m_ref, g_ref, p_ref, acc_ref):
        i = pl.program_id(2)
        g = jnp.dot(h_ref[...], wg_ref[...], preferred_element_type=F32) + bg_ref[...]
        gate = _sig(g)
        for b, y_ref in enumerate((y0, y1, y2, y3)):
            @pl.when(i == b)
            def _(y_ref=y_ref):
                p = jnp.dot(y_ref[...], wb_ref[...], preferred_element_type=F32)
                g_ref[...] = gate.astype(BF16)
                p_ref[...] = p.astype(BF16)
                if b == 0:
                    acc_ref[...] = gate * p
                else:
                    acc_ref[...] += gate * p

        @pl.when(i == N_BRANCH - 1)
        def _():
            m_ref[...] = acc_ref[...].astype(BF16)

    yspec = pl.BlockSpec((tm, BW), lambda m, n, i: (m, 0))
    return pl.pallas_call(
        body, name=name, grid=(S // tm, nb, N_BRANCH),
        in_specs=[pl.BlockSpec((tm, D), lambda m, n, i: (m, 0)), yspec, yspec, yspec, yspec,
                  pl.BlockSpec((None, D, tn), lambda m, n, i: (i, 0, n)),
                  pl.BlockSpec((None, 1, tn), lambda m, n, i: (i, 0, n)),
                  pl.BlockSpec((None, BW, tn), lambda m, n, i: (i, 0, n))],
        out_specs=[pl.BlockSpec((tm, tn), lambda m, n, i: (m, n)),
                   pl.BlockSpec((tm, tn), lambda m, n, i: (m, i * nb + n)),
                   pl.BlockSpec((tm, tn), lambda m, n, i: (m, i * nb + n))],
        out_shape=[jax.ShapeDtypeStruct((S, D), BF16), jax.ShapeDtypeStruct((S, N_BRANCH * D), BF16),
                   jax.ShapeDtypeStruct((S, N_BRANCH * D), BF16)],
        scratch_shapes=[pltpu.VMEM((tm, tn), F32)],
        compiler_params=_cp(("parallel", "parallel", "arbitrary")),
    )(h, *ys, wg, bg, wb)


def merge_bwd(dm, gates, proj, *, name):
    S, D = dm.shape
    ts, tn = _tile(S, TILES["row"]), _tile(D, 512)
    nb = D // tn

    def body(dm_ref, g_ref, p_ref, dp_ref, dg_ref, db_ref):
        d = dm_ref[...]
        g = g_ref[...].astype(F32)
        dp_ref[...] = (d * g).astype(BF16)
        dg = d * p_ref[...].astype(F32) * g * (1.0 - g)
        dg_ref[...] = dg.astype(BF16)
        part = jnp.sum(dg, axis=0, keepdims=True)

        @pl.when(pl.program_id(2) == 0)
        def _():
            db_ref[...] = part

        @pl.when(pl.program_id(2) > 0)
        def _():
            db_ref[...] += part

    wide = pl.BlockSpec((ts, tn), lambda b, n, i: (i, b * nb + n))
    return pl.pallas_call(
        body, name=name, grid=(N_BRANCH, nb, S // ts),
        in_specs=[pl.BlockSpec((ts, tn), lambda b, n, i: (i, n)), wide, wide],
        out_specs=[wide, wide, pl.BlockSpec((1, tn), lambda b, n, i: (0, b * nb + n))],
        out_shape=[jax.ShapeDtypeStruct((S, N_BRANCH * D), BF16)] * 2 + [jax.ShapeDtypeStruct((1, N_BRANCH * D), F32)],
        compiler_params=_cp(("parallel", "parallel", "arbitrary")),
    )(dm, gates, proj)


_NT = (((1,), (1,)), ((), ()))
_TN = (((0,), (0,)), ((), ()))


def attn_fwd(u, c_col, c_row, *, name):
    S = u.shape[0]
    T = _tile(S, TILES["att"])
    n = S // T
    qo, ko, vo, go = (OFF[k] // FOX_HD for k in ("q", "k", "v", "ga"))
    scale = FOX_HD ** -0.5

    def body(q_ref, k_ref, v_ref, cq_ref, ck_ref, ga_ref, o_ref, lse_ref, ya_ref, m_ref, l_ref, acc_ref):
        i, j = pl.program_id(1), pl.program_id(2)

        @pl.when(j == 0)
        def _():
            m_ref[...] = jnp.full((T, 1), NEG, F32)
            l_ref[...] = jnp.zeros((T, 1), F32)
            acc_ref[...] = jnp.zeros((T, FOX_HD), F32)

        @pl.when(j <= i)
        def _():
            s = lax.dot_general(q_ref[...].astype(BF16), k_ref[...].astype(BF16), _NT, preferred_element_type=F32) * scale
            s = s + cq_ref[...] - ck_ref[...]
            row = lax.broadcasted_iota(jnp.int32, (T, T), 0) + i * T
            col = lax.broadcasted_iota(jnp.int32, (T, T), 1) + j * T
            s = jnp.where(col <= row, s, NEG)
            m_old = m_ref[...]
            m_new = jnp.maximum(m_old, jnp.max(s, axis=1, keepdims=True))
            alpha = jnp.exp(m_old - m_new)
            p = jnp.exp(s - m_new)
            l_ref[...] = alpha * l_ref[...] + jnp.sum(p, axis=1, keepdims=True)
            p_hi = p.astype(BF16)
            p_lo = (p - p_hi.astype(F32)).astype(BF16)
            vb = v_ref[...].astype(BF16)
            pv = jnp.dot(p_hi, vb, preferred_element_type=F32) + jnp.dot(p_lo, vb, preferred_element_type=F32)
            acc_ref[...] = alpha * acc_ref[...] + pv
            m_ref[...] = m_new

        @pl.when(j == n - 1)
        def _():
            o = acc_ref[...] / l_ref[...]
            o_ref[...] = o
            lse_ref[...] = m_ref[...] + jnp.log(l_ref[...])
            ya_ref[...] = (o * _silu(ga_ref[...])).astype(BF16)

    def sec(off, kv):
        if kv:
            return pl.BlockSpec((T, FOX_HD), lambda h, i, j: (jnp.minimum(j, i), off + h))
        return pl.BlockSpec((T, FOX_HD), lambda h, i, j: (i, off + h))

    out = pl.BlockSpec((T, FOX_HD), lambda h, i, j: (i, h))
    colv = pl.BlockSpec((None, T, 1), lambda h, i, j: (h, i, 0))
    return pl.pallas_call(
        body, name=name, grid=(FOX_HEADS, n, n),
        in_specs=[sec(qo, False), sec(ko, True), sec(vo, True), colv,
                  pl.BlockSpec((None, 1, T), lambda h, i, j: (h, 0, jnp.minimum(j, i))), sec(go, False)],
        out_specs=[out, colv, out],
        out_shape=[jax.ShapeDtypeStruct((S, BW), F32), jax.ShapeDtypeStruct((FOX_HEADS, S, 1), F32),
                   jax.ShapeDtypeStruct((S, BW), BF16)],
        scratch_shapes=[pltpu.VMEM((T, 1), F32), pltpu.VMEM((T, 1), F32), pltpu.VMEM((T, FOX_HD), F32)],
        compiler_params=_cp(("parallel", "parallel", "arbitrary")),
    )(u, u, u, c_col, c_row, u)


def attn_bwd(u, do, c_col, c_row, lse_row, delta_row, *, name):
    S = u.shape[0]
    T = _tile(S, TILES["att"])
    n = S // T
    qo, ko, vo = (OFF[k] // FOX_HD for k in ("q", "k", "v"))
    scale = FOX_HD ** -0.5

    def body(q_ref, k_ref, v_ref, do_ref, cq_ref, ck_ref, lse_ref, dl_ref, dq_ref, dk_ref, dv_ref, dc_ref, dka_ref, dva_ref, dca_ref):
        j, i = pl.program_id(1), pl.program_id(2)

        @pl.when((j == 0) & (i == 0))
        def _():
            dq_ref[...] = jnp.zeros((S, FOX_HD), F32)

        @pl.when(i == 0)
        def _():
            dka_ref[...] = jnp.zeros((T, FOX_HD), F32)
            dva_ref[...] = jnp.zeros((T, FOX_HD), F32)
            dca_ref[...] = jnp.zeros((T, 1), F32)

        @pl.when(i >= j)
        def _():
            qb, kb, dob = q_ref[...].astype(BF16), k_ref[...].astype(BF16), do_ref[...]
            st = lax.dot_general(kb, qb, _NT, preferred_element_type=F32) * scale
            st = st + cq_ref[...] - ck_ref[...]
            kpos = lax.broadcasted_iota(jnp.int32, (T, T), 0) + j * T
            qpos = lax.broadcasted_iota(jnp.int32, (T, T), 1) + i * T
            pt = jnp.where(kpos <= qpos, jnp.exp(st - lse_ref[...]), 0.0)
            dva_ref[...] += jnp.dot(pt.astype(BF16), dob, preferred_element_type=F32)
            dpt = lax.dot_general(v_ref[...].astype(BF16), dob, _NT, preferred_element_type=F32)
            dst = pt * (dpt - dl_ref[...])
            dca_ref[...] -= jnp.sum(dst, axis=1, keepdims=True)
            dsb = (dst * scale).astype(BF16)
            dka_ref[...] += jnp.dot(dsb, qb, preferred_element_type=F32)
            rows = pl.ds(pl.multiple_of(i * T, T), T)
            dq_ref[rows, :] += lax.dot_general(dsb, kb, _TN, preferred_element_type=F32)

        @pl.when(i == n - 1)
        def _():
            dk_ref[...] = dka_ref[...].astype(BF16)
            dv_ref[...] = dva_ref[...].astype(BF16)
            dc_ref[...] = dca_ref[...]

    def qsec(off):
        return pl.BlockSpec((T, FOX_HD), lambda h, j, i: (jnp.maximum(i, j), off + h))

    def ksec(off):
        return pl.BlockSpec((T, FOX_HD), lambda h, j, i: (j, off + h))

    qrow = pl.BlockSpec((None, 1, T), lambda h, j, i: (h, 0, jnp.maximum(i, j)))
    kout = pl.BlockSpec((T, FOX_HD), lambda h, j, i: (j, h))
    kcol = pl.BlockSpec((None, T, 1), lambda h, j, i: (h, j, 0))
    return pl.pallas_call(
        body, name=name, grid=(FOX_HEADS, n, n),
        in_specs=[qsec(qo), ksec(ko), ksec(vo), pl.BlockSpec((T, FOX_HD), lambda h, j, i: (jnp.maximum(i, j), h)),
                  qrow, kcol, qrow, qrow],
        out_specs=[pl.BlockSpec((S, FOX_HD), lambda h, j, i: (0, h)), kout, kout, kcol],
        out_shape=[jax.ShapeDtypeStruct((S, BW), F32), jax.ShapeDtypeStruct((S, BW), BF16),
                   jax.ShapeDtypeStruct((S, BW), BF16), jax.ShapeDtypeStruct((FOX_HEADS, S, 1), F32)],
        scratch_shapes=[pltpu.VMEM((T, FOX_HD), F32), pltpu.VMEM((T, FOX_HD), F32), pltpu.VMEM((T, 1), F32)],
        compiler_params=_cp(("parallel", "arbitrary", "arbitrary")),
    )(u, u, u, do, c_row, c_col, lse_row, delta_row)


N_PAIR = SSM_HEADS // 2
PAIRS_PER_GROUP = N_PAIR // SSM_G


def _sel_t():
    r = lax.broadcasted_iota(jnp.int32, (LANES, BW), 0)
    c = lax.broadcasted_iota(jnp.int32, (LANES, BW), 1)
    return (lax.shift_right_logical(c, 6) == r).astype(BF16)


def _sel():
    r = lax.broadcasted_iota(jnp.int32, (BW, LANES), 0)
    c = lax.broadcasted_iota(jnp.int32, (BW, LANES), 1)
    return (lax.shift_right_logical(r, 6) == c).astype(BF16)


def _dot3(x, m):
    hi = x.astype(BF16)
    r1 = x - hi.astype(F32)
    mid = r1.astype(BF16)
    lo = (r1 - mid.astype(F32)).astype(BF16)
    d = functools.partial(jnp.dot, preferred_element_type=F32)
    return d(hi, m) + d(mid, m) + d(lo, m)


def _ssd_common(x_ref, sp_ref, al_ref, buf_ref, big_ref, cst_ref, LC):
    pre = x_ref[...]
    sg = _sig(pre)
    act = pre * sg
    dt = sp_ref[...]
    a = -jnp.exp(al_ref[...])
    cs = _shift_scan(dt * a, buf_ref, LC)
    sel_t = _sel_t()
    dtl = _dot3(dt, sel_t)
    csl = _dot3(cs, sel_t)
    big_ref[...] = csl
    csl_last = big_ref[pl.ds(LC - 1, 1), :]
    cst_ref[...] = cs.T
    return pre, sg, act, dt, a, cs, dtl, csl, csl_last


def ssd_fwd(xbc, sp, alog, dskip_l, *, name):
    S = xbc.shape[0]
    LC = _tile(S, TILES["ssd"])
    nc = S // LC

    def body(x_ref, sp_ref, al_ref, dk_ref, y_ref, hs_ref, st_ref, buf_ref, big_ref, cst_ref):
        @pl.when(pl.program_id(0) == 0)
        def _():
            st_ref[...] = jnp.zeros((N_PAIR, SSM_N, LANES), F32)

        pre, sg, act, dt, a, cs, dtl, csl, csl_last = _ssd_common(x_ref, sp_ref, al_ref, buf_ref, big_ref, cst_ref, LC)
        xs, bm, cm = act[:, :BW], act[:, BW:BW + SSM_G * SSM_N], act[:, BW + SSM_G * SSM_N:]
        e_all = jnp.exp(csl)
        dec = jnp.exp(csl_last - csl)
        ad = jnp.exp(csl_last)
        xd = xs * dtl
        tril = lax.broadcasted_iota(jnp.int32, (LC, LC), 0) >= lax.broadcasted_iota(jnp.int32, (LC, LC), 1)
        lane = lax.broadcasted_iota(jnp.int32, (LC, LANES), 1)
        for g in range(SSM_G):
            bgt = bm[:, g * SSM_N:(g + 1) * SSM_N].T.astype(BF16)
            cgb = cm[:, g * SSM_N:(g + 1) * SSM_N].astype(BF16)
            cb = jnp.dot(cgb, bgt, preferred_element_type=F32)
            for q in range(PAIRS_PER_GROUP):
                pp = g * PAIRS_PER_GROUP + q
                ln = slice(pp * LANES, (pp + 1) * LANES)
                xp = xd[:, ln]
                xpb = xp.astype(BF16)
                yh = []
                for hh in range(2):
                    row_b = jnp.broadcast_to(cst_ref[pl.ds(2 * pp + hh, 1), :], (LC, LC))
                    lmat = jnp.exp(jnp.where(tril, row_b.T - row_b, NEG))
                    yh.append(jnp.dot((cb * lmat).astype(BF16), xpb, preferred_element_type=F32))
                hin = st_ref[pp]
                hs_ref[pp] = hin
                yoff = jnp.dot(cgb, hin.astype(BF16), preferred_element_type=F32) * e_all[:, ln]
                y_ref[:, ln] = jnp.where(lane < SSM_P, yh[0], yh[1]) + yoff + xs[:, ln] * dk_ref[:, ln]
                st_ref[pp] = hin * ad[:, ln] + jnp.dot(bgt, (xp * dec[:, ln]).astype(BF16), preferred_element_type=F32)

    return pl.pallas_call(
        body, name=name, grid=(nc,),
        in_specs=[pl.BlockSpec((LC, SSM_CONV_DIM), lambda c: (c, 0)), pl.BlockSpec((LC, LANES), lambda c: (c, 0)),
                  pl.BlockSpec((1, LANES), lambda c: (0, 0)), pl.BlockSpec((1, BW), lambda c: (0, 0))],
        out_specs=[pl.BlockSpec((LC, BW), lambda c: (c, 0)), pl.BlockSpec((None, N_PAIR, SSM_N, LANES), lambda c: (c, 0, 0, 0))],
        out_shape=[jax.ShapeDtypeStruct((S, BW), F32), jax.ShapeDtypeStruct((nc, N_PAIR, SSM_N, LANES), F32)],
        scratch_shapes=[pltpu.VMEM((N_PAIR, SSM_N, LANES), F32), pltpu.VMEM((3 * LC, LANES), F32),
                        pltpu.VMEM((LC, BW), F32), pltpu.VMEM((LANES, LC), F32)],
        compiler_params=_cp(("arbitrary",)),
    )(xbc, sp, alog, dskip_l)


def ssd_bwd(dy, xbc, sp, alog, dskip_l, hs, *, name):
    S = xbc.shape[0]
    LC = _tile(S, TILES["ssd"])
    nc = S // LC
    GN = SSM_G * SSM_N

    def body(dy_ref, x_ref, sp_ref, al_ref, dk_ref, hs_ref, dx_ref, ddt_ref, da_ref, dd_ref,
             dh_ref, buf_ref, big_ref, cst_ref, gcs_ref, dxd_ref):
        @pl.when(pl.program_id(0) == 0)
        def _():
            dh_ref[...] = jnp.zeros((N_PAIR, SSM_N, LANES), F32)
            da_ref[...] = jnp.zeros((1, LANES), F32)
            dd_ref[...] = jnp.zeros((1, BW), F32)

        pre, sg, act, dt, a, cs, dtl, csl, csl_last = _ssd_common(x_ref, sp_ref, al_ref, buf_ref, big_ref, cst_ref, LC)
        dact = sg * (1.0 + pre * (1.0 - sg))
        xs, bm, cm = act[:, :BW], act[:, BW:BW + GN], act[:, BW + GN:]
        e_all = jnp.exp(csl)
        dec = jnp.exp(csl_last - csl)
        ad = jnp.exp(csl_last)
        xd = xs * dtl
        d_y = dy_ref[...]
        ri = lax.broadcasted_iota(jnp.int32, (LC, LC), 0)
        ci = lax.broadcasted_iota(jnp.int32, (LC, LC), 1)
        tril, triu = ri >= ci, ci >= ri
        lane = lax.broadcasted_iota(jnp.int32, (LC, LANES), 1)
        rowi = lax.broadcasted_iota(jnp.int32, (LC, LANES), 0)
        dot = functools.partial(jnp.dot, preferred_element_type=F32)
        dot_nt = functools.partial(lax.dot_general, dimension_numbers=_NT, preferred_element_type=F32)
        for g in range(SSM_G):
            gs = slice(g * SSM_N, (g + 1) * SSM_N)
            bg, cg = bm[:, gs], cm[:, gs]
            bgb, cgb = bg.astype(BF16), cg.astype(BF16)
            bgt, cgt = bg.T.astype(BF16), cg.T.astype(BF16)
            cb, cbt = dot(cgb, bgt), dot(bgb, cgt)
            dcb = jnp.zeros((LC, LC), F32)
            dcbt = jnp.zeros((LC, LC), F32)
            dcg = jnp.zeros((LC, SSM_N), F32)
            dbg = jnp.zeros((LC, SSM_N), F32)
            for q in range(PAIRS_PER_GROUP):
                pp = g * PAIRS_PER_GROUP + q
                ln = slice(pp * LANES, (pp + 1) * LANES)
                xp, dyp, ep, decp, adp = xd[:, ln], d_y[:, ln], e_all[:, ln], dec[:, ln], ad[:, ln]
                xpb, dypb = xp.astype(BF16), dyp.astype(BF16)
                hin, dho = hs_ref[pp], dh_ref[pp]
                hb, dhob = hin.astype(BF16), dho.astype(BF16)
                yoff = dot(cgb, hb) * ep
                dgb = (dyp * ep).astype(BF16)
                dh_ref[pp] = dho * adp + dot(cgt, dgb)
                dcg = dcg + dot_nt(dgb, hb)
                zf = xp * decp
                d_z = dot(bgb, dhob)
                dbg = dbg + dot_nt(zf.astype(BF16), dhob)
                dzz = d_z * zf
                last = jnp.sum(dzz, axis=0, keepdims=True) + jnp.sum(dho * hin, axis=0, keepdims=True) * adp
                gcs = dyp * yoff - dzz + jnp.where(rowi == LC - 1, last, 0.0)
                dxd = d_z * decp
                for hh in range(2):
                    row_b = jnp.broadcast_to(cst_ref[pl.ds(2 * pp + hh, 1), :], (LC, LC))
                    col_b = row_b.T
                    lmat = jnp.exp(jnp.where(tril, col_b - row_b, NEG))
                    lmat_t = jnp.exp(jnp.where(triu, row_b - col_b, NEG))
                    hm = (lane < SSM_P) if hh == 0 else (lane >= SSM_P)
                    dml = dot_nt(jnp.where(hm, dyp, 0.0).astype(BF16), xpb) * lmat
                    dmtl = dot_nt(jnp.where(hm, xp, 0.0).astype(BF16), dypb) * lmat_t
                    dcb = dcb + dml
                    dcbt = dcbt + dmtl
                    contrib = jnp.sum(dml * cb, axis=1, keepdims=True) - jnp.sum(dmtl * cbt, axis=1, keepdims=True)
                    gcs = gcs + jnp.where(lane == hh * SSM_P, contrib, 0.0)
                    dxd = dxd + jnp.where(hm, dot((cbt * lmat_t).astype(BF16), dypb), 0.0)
                gcs_ref[:, ln] = gcs
                dxd_ref[:, ln] = dxd
            dcg = dcg + dot(dcb.astype(BF16), bgb)
            dbg = dbg + dot(dcbt.astype(BF16), cgb)
            dx_ref[:, BW + g * SSM_N:BW + (g + 1) * SSM_N] = dbg * dact[:, BW + g * SSM_N:BW + (g + 1) * SSM_N]
            dx_ref[:, BW + GN + g * SSM_N:BW + GN + (g + 1) * SSM_N] = dcg * dact[:, BW + GN + g * SSM_N:BW + GN + (g + 1) * SSM_N]
        d_xd = dxd_ref[...]
        dx_ref[:, :BW] = (d_y * dk_ref[...] + d_xd * dtl) * dact[:, :BW]
        sel = _sel()
        dda = _shift_scan(_dot3(gcs_ref[...], sel), buf_ref, LC, reverse=True)
        ddt_ref[...] = _dot3(d_xd * xs, sel) + dda * a
        da_ref[...] += jnp.sum(dda * dt, axis=0, keepdims=True)
        dd_ref[...] += jnp.sum(d_y * xs, axis=0, keepdims=True)

    rev = lambda c: (nc - 1 - c, 0)
    return pl.pallas_call(
        body, name=name, grid=(nc,),
        in_specs=[pl.BlockSpec((LC, BW), rev), pl.BlockSpec((LC, SSM_CONV_DIM), rev), pl.BlockSpec((LC, LANES), rev),
                  pl.BlockSpec((1, LANES), lambda c: (0, 0)), pl.BlockSpec((1, BW), lambda c: (0, 0)),
                  pl.BlockSpec((None, N_PAIR, SSM_N, LANES), lambda c: (nc - 1 - c, 0, 0, 0))],
        out_specs=[pl.BlockSpec((LC, SSM_CONV_DIM), rev), pl.BlockSpec((LC, LANES), rev),
                   pl.BlockSpec((1, LANES), lambda c: (0, 0)), pl.BlockSpec((1, BW), lambda c: (0, 0))],
        out_shape=[jax.ShapeDtypeStruct((S, SSM_CONV_DIM), F32), jax.ShapeDtypeStruct((S, LANES), F32),
                   jax.ShapeDtypeStruct((1, LANES), F32), jax.ShapeDtypeStruct((1, BW), F32)],
        scratch_shapes=[pltpu.VMEM((N_PAIR, SSM_N, LANES), F32), pltpu.VMEM((3 * LC, LANES), F32),
                        pltpu.VMEM((LC, BW), F32), pltpu.VMEM((LANES, LC), F32),
                        pltpu.VMEM((LC, BW), F32), pltpu.VMEM((LC, BW), F32)],
        compiler_params=_cp(("arbitrary",)),
    )(dy, xbc, sp, alog, dskip_l, hs)


def addn(arrs, *, out_dtype, name):
    R, C = arrs[0].shape
    tr = _tile(R, max(8, (1 << 20) // C))

    def body(*refs):
        acc = refs[0][...].astype(F32)
        for r in refs[1:-1]:
            acc = acc + r[...].astype(F32)
        refs[-1][...] = acc.astype(out_dtype)

    blk = pl.BlockSpec((tr, C), lambda i: (i, 0))
    return pl.pallas_call(
        body, name=name, grid=(R // tr,), in_specs=[blk] * len(arrs), out_specs=blk,
        out_shape=jax.ShapeDtypeStruct((R, C), out_dtype), compiler_params=_cp(("parallel",)),
    )(*arrs)


def adamw(w, g, m, v, *, name):
    R, C = w.shape
    tr = _tile(R, max(8, (1 << 18) // C))
    c1, c2 = 1.0 / (1.0 - ADAM_B1 ** ADAM_STEP), 1.0 / (1.0 - ADAM_B2 ** ADAM_STEP)

    def body(w_ref, g_ref, m_ref, v_ref, d_ref, nm_ref, nv_ref):
        gv = g_ref[...]
        nm = ADAM_B1 * m_ref[...] + (1.0 - ADAM_B1) * gv
        nv = ADAM_B2 * v_ref[...] + (1.0 - ADAM_B2) * (gv * gv)
        nm_ref[...] = nm
        nv_ref[...] = nv
        d_ref[...] = -ADAM_LR * ((nm * c1) / (jnp.sqrt(nv * c2) + ADAM_EPS) + ADAM_WD * w_ref[...])

    blk = pl.BlockSpec((tr, C), lambda i: (i, 0))
    return pl.pallas_call(
        body, name=name, grid=(R // tr,), in_specs=[blk] * 4, out_specs=[blk] * 3,
        out_shape=[jax.ShapeDtypeStruct((R, C), F32)] * 3, compiler_params=_cp(("parallel",)),
    )(w, g, m, v)


_ANY = pl.BlockSpec(memory_space=pl.ANY)


def _place():
    return lax.axis_index("x"), lax.axis_index("y"), lax.axis_index("c")


def _rcopy(src, dst, sems_s, sems_r, k, to):
    return pltpu.make_async_remote_copy(src_ref=src, dst_ref=dst, send_sem=sems_s.at[k], recv_sem=sems_r.at[k],
                                        device_id=to, device_id_type=MESH)


def gather_chips(shard, *, name):
    _, R, C = shard.shape

    def body(in_ref, out_ref, ss, rs, ls):
        x, y, c = _place()
        chips = [(1 - x, y), (x, 1 - y), (1 - x, 1 - y)]
        me = 2 * x + y
        mine = pltpu.make_async_copy(in_ref, out_ref.at[me], ls)
        mine.start()
        first = [_rcopy(in_ref.at[c], out_ref.at[me, c], ss, rs, j, (cx, cy, c)) for j, (cx, cy) in enumerate(chips)]
        for cp in first:
            cp.start()
        passed = []
        for j, (cx, cy) in enumerate(chips):
            blk = out_ref.at[2 * cx + cy, c]
            _rcopy(blk, blk, ss, rs, j, (x, y, c)).wait_recv()
            cp = _rcopy(blk, blk, ss, rs, 3 + j, (x, y, 1 - c))
            cp.start()
            passed.append(cp)
        for j, (cx, cy) in enumerate(chips):
            blk = out_ref.at[2 * cx + cy, 1 - c]
            _rcopy(blk, blk, ss, rs, 3 + j, (x, y, c)).wait_recv()
        for cp in first + passed:
            cp.wait_send()
        mine.wait()

    return pl.pallas_call(
        body, name=name, in_specs=[_ANY], out_specs=_ANY, out_shape=jax.ShapeDtypeStruct((4, 2, R, C), shard.dtype),
        scratch_shapes=[pltpu.SemaphoreType.DMA((6,)), pltpu.SemaphoreType.DMA((6,)), pltpu.SemaphoreType.DMA],
    )(shard)


def gather_all(block, *, name):
    R, C = block.shape
    flips = [(fx, fy, fc) for fx in (0, 1) for fy in (0, 1) for fc in (0, 1)][1:]

    def body(in_ref, out_ref, ss, rs, ls):
        x, y, c = _place()
        me = 4 * x + 2 * y + c
        mine = pltpu.make_async_copy(in_ref, out_ref.at[me], ls)
        mine.start()
        sends = []
        for k, (fx, fy, fc) in enumerate(flips):
            px, py, pc = x ^ fx, y ^ fy, c ^ fc
            cp = _rcopy(in_ref, out_ref.at[me], ss, rs, k, (px, py, pc))
            cp.start()
            sends.append(cp)
        for k, (fx, fy, fc) in enumerate(flips):
            blk = out_ref.at[4 * (x ^ fx) + 2 * (y ^ fy) + (c ^ fc)]
            _rcopy(blk, blk, ss, rs, k, (x, y, c)).wait_recv()
        for cp in sends:
            cp.wait_send()
        mine.wait()

    return pl.pallas_call(
        body, name=name, in_specs=[_ANY], out_specs=_ANY, out_shape=jax.ShapeDtypeStruct((8, R, C), block.dtype),
        scratch_shapes=[pltpu.SemaphoreType.DMA((7,)), pltpu.SemaphoreType.DMA((7,)), pltpu.SemaphoreType.DMA],
    )(block)


def swap_sibling(send, *, name):
    N, R, C = send.shape

    def body(in_ref, out_ref, ss, rs):
        x, y, c = _place()
        cps = [_rcopy(in_ref.at[k], out_ref.at[k], ss, rs, k, (x, y, 1 - c)) for k in range(N)]
        for cp in cps:
            cp.start()
        for cp in cps:
            cp.wait_recv()
        for cp in cps:
            cp.wait_send()

    return pl.pallas_call(
        body, name=name, in_specs=[_ANY], out_specs=_ANY, out_shape=jax.ShapeDtypeStruct((N, R, C), send.dtype),
        scratch_shapes=[pltpu.SemaphoreType.DMA((N,)), pltpu.SemaphoreType.DMA((N,))],
    )(send)


def scatter_chips(parts, *, name):
    _, R, C = parts.shape

    def body(in_ref, out_ref, ss, rs):
        x, y, c = _place()
        chips = [(1 - x, y), (x, 1 - y), (1 - x, 1 - y)]
        cps = [_rcopy(in_ref.at[2 * cx + cy], out_ref.at[j], ss, rs, j, (cx, cy, c)) for j, (cx, cy) in enumerate(chips)]
        for cp in cps:
            cp.start()
        for cp in cps:
            cp.wait_recv()
        for cp in cps:
            cp.wait_send()

    return pl.pallas_call(
        body, name=name, in_specs=[_ANY], out_specs=_ANY, out_shape=jax.ShapeDtypeStruct((3, R, C), parts.dtype),
        scratch_shapes=[pltpu.SemaphoreType.DMA((3,)), pltpu.SemaphoreType.DMA((3,))],
    )(parts)


WEIGHTS = ("norm_w", "w_in", "fg_bias", "ssm_conv_w", "ssm_conv_b", "dt_bias", "a_log", "d_skip", "ssm_norm_w", "sc_conv_w",
           "sc_conv_b", "cf_conv_w", "cf_conv_b", "cf_ln_w", "cf_ln_b", "w_gate", "b_gate", "w_branch", "w_out", "final_norm_w")
BIG = ("w_in", "w_gate", "w_branch", "w_out")
SMALL = tuple(n for n in WEIGHTS if n not in BIG)
SMALL_SHARDED = ("ssm_conv_w", "sc_conv_w", "cf_conv_w", "b_gate")
N_CHIP = 4
PACK_C = 1024


def _pack(arrs, cols, row_mult):
    flat = jnp.concatenate([a.reshape(-1) for a in arrs])
    n = flat.shape[0]
    rows = -(-n // cols)
    rows = -(-rows // row_mult) * row_mult
    return jnp.pad(flat, (0, rows * cols - n)).reshape(rows, cols)


def _unpack(packed, shapes):
    flat = packed.reshape(-1)
    out, o = [], 0
    for s in shapes:
        n = math.prod(s)
        out.append(flat[o:o + n].reshape(s))
        o += n
    return out


def _orig_cols():
    cols = []
    for n, s in zip(ORIG_NAMES, ORIG_SIZES):
        if n == "dt":
            cols.append((OFF["small"] + DT_LANE, s))
        elif n == "f":
            cols.append((OFF["small"] + F_LANE, s))
        else:
            cols.append((OFF[n], s))
    return cols


def _to_padded(w_full):
    orig_off = dict(zip(ORIG_NAMES, [sum(ORIG_SIZES[:i]) for i in range(len(ORIG_SIZES))]))
    size = dict(zip(ORIG_NAMES, ORIG_SIZES))
    parts = [w_full[:, orig_off[n]:orig_off[n] + size[n]] for n in PAD_ORDER + ("dt", "f")]
    parts.append(jnp.zeros((w_full.shape[0], SMALL_W - size["dt"] - size["f"]), w_full.dtype))
    return jnp.concatenate(parts, axis=1)


def _from_padded(w_pad):
    return jnp.concatenate([w_pad[:, o:o + s] for o, s in _orig_cols()], axis=1)


def _lanes_row(parts, width=LANES):
    v = jnp.concatenate([p.reshape(-1) for p in parts])
    return jnp.pad(v, (0, width - v.shape[0])).reshape(1, width)


def kernel(x, norm_w, w_in, fg_bias, ssm_conv_w, ssm_conv_b, dt_bias, a_log, d_skip, ssm_norm_w, sc_conv_w, sc_conv_b, cf_conv_w, cf_conv_b, cf_ln_w, cf_ln_b, w_gate, b_gate, w_branch, w_out, final_norm_w, loss_target, m_norm_w, m_w_in, m_fg_bias, m_ssm_conv_w, m_ssm_conv_b, m_dt_bias, m_a_log, m_d_skip, m_ssm_norm_w, m_sc_conv_w, m_sc_conv_b, m_cf_conv_w, m_cf_conv_b, m_cf_ln_w, m_cf_ln_b, m_w_gate, m_b_gate, m_w_branch, m_w_out, m_final_norm_w, v_norm_w, v_w_in, v_fg_bias, v_ssm_conv_w, v_ssm_conv_b, v_dt_bias, v_a_log, v_d_skip, v_ssm_norm_w, v_sc_conv_w, v_sc_conv_b, v_cf_conv_w, v_cf_conv_b, v_cf_ln_w, v_cf_ln_b, v_w_gate, v_b_gate, v_w_branch, v_w_out, v_final_norm_w):
    wts = dict(norm_w=norm_w, w_in=w_in, fg_bias=fg_bias, ssm_conv_w=ssm_conv_w, ssm_conv_b=ssm_conv_b, dt_bias=dt_bias,
               a_log=a_log, d_skip=d_skip, ssm_norm_w=ssm_norm_w, sc_conv_w=sc_conv_w, sc_conv_b=sc_conv_b,
               cf_conv_w=cf_conv_w, cf_conv_b=cf_conv_b, cf_ln_w=cf_ln_w, cf_ln_b=cf_ln_b, w_gate=w_gate, b_gate=b_gate,
               w_branch=w_branch, w_out=w_out, final_norm_w=final_norm_w)
    mom = dict(norm_w=m_norm_w, w_in=m_w_in, fg_bias=m_fg_bias, ssm_conv_w=m_ssm_conv_w, ssm_conv_b=m_ssm_conv_b,
               dt_bias=m_dt_bias, a_log=m_a_log, d_skip=m_d_skip, ssm_norm_w=m_ssm_norm_w, sc_conv_w=m_sc_conv_w,
               sc_conv_b=m_sc_conv_b, cf_conv_w=m_cf_conv_w, cf_conv_b=m_cf_conv_b, cf_ln_w=m_cf_ln_w, cf_ln_b=m_cf_ln_b,
               w_gate=m_w_gate, b_gate=m_b_gate, w_branch=m_w_branch, w_out=m_w_out, final_norm_w=m_final_norm_w)
    vel = dict(norm_w=v_norm_w, w_in=v_w_in, fg_bias=v_fg_bias, ssm_conv_w=v_ssm_conv_w, ssm_conv_b=v_ssm_conv_b,
               dt_bias=v_dt_bias, a_log=v_a_log, d_skip=v_d_skip, ssm_norm_w=v_ssm_norm_w, sc_conv_w=v_sc_conv_w,
               sc_conv_b=v_sc_conv_b, cf_conv_w=v_cf_conv_w, cf_conv_b=v_cf_conv_b, cf_ln_w=v_cf_ln_w, cf_ln_b=v_cf_ln_b,
               w_gate=v_w_gate, b_gate=v_b_gate, w_branch=v_w_branch, w_out=v_w_out, final_norm_w=v_final_norm_w)
    L = norm_w.shape[0]
    S, D = x.shape[1], x.shape[2]
    assert D == D_MODEL and x.shape[0] == 1
    xi, yi, ci = _place()
    chip = 2 * xi + yi

    sh_shapes = [wts[n].shape for n in SMALL_SHARDED]
    got = gather_all(_pack([wts[n] for n in SMALL_SHARDED], LANES, 8), name="gather_small_w")
    per_chip = [_unpack(got[2 * k], sh_shapes) for k in range(N_CHIP)]
    full_small = {n: jnp.concatenate([per_chip[k][i] for k in range(N_CHIP)], axis=-1) for i, n in enumerate(SMALL_SHARDED)}

    big_shapes = [wts[n].shape[1:] for n in BIG]
    lw = []
    for l in range(L):
        packed = _pack([wts[n][l].astype(BF16) for n in BIG], PACK_C, 64)
        got = gather_chips(packed.reshape(2, -1, PACK_C), name="gather_w")
        pc = [_unpack(got[k], big_shapes) for k in range(N_CHIP)]
        w_in_f = jnp.concatenate([pc[k][0] for k in range(N_CHIP)], axis=1)
        wg = jnp.concatenate([pc[k][1] for k in range(N_CHIP)], axis=1)
        wb = jnp.concatenate([pc[k][2] for k in range(N_CHIP)], axis=2)
        wo = jnp.concatenate([pc[k][3] for k in range(N_CHIP)], axis=0)
        wp = _to_padded(w_in_f)
        lw.append(dict(wp=wp, wpt=wp.T, wg=wg, wgt=jnp.transpose(wg, (0, 2, 1)).reshape(N_BRANCH * D, D),
                       wb=wb, wbt=jnp.transpose(wb, (0, 2, 1)), wo=wo, wot=wo.T))

    saved = []
    xl = x[0]
    for l in range(L):
        w = lw[l]
        h = rms_fwd(xl, norm_w[l][None], name="rms_fwd")
        u = mm(h, w["wp"], name="mm_in")
        bias_small = _lanes_row([dt_bias[l], fg_bias[l]])
        sp, csum = small_fwd(u, bias_small, name="small_fwd")
        c8 = csum[:, F_LANE:F_LANE + FOX_HEADS].T
        c_col, c_row = c8[:, :, None], c8[:, None, :]
        o, lse, y_a = attn_fwd(u, c_col, c_row, name="attn_fwd")
        xbc = conv_fwd(u, full_small["ssm_conv_w"][l], ssm_conv_b[l][None], mode="plain", a_off=OFF["xbc"], b_off=0,
                       C=SSM_CONV_DIM, name="conv_ssm_fwd")
        alog_row = _lanes_row([a_log[l]])
        dskip_l = jnp.repeat(d_skip[l], SSM_P)[None]
        y_ssd, hs = ssd_fwd(xbc, sp, alog_row, dskip_l, name="ssd_fwd")
        cv_c = conv_fwd(u, full_small["sc_conv_w"][l], sc_conv_b[l][None], mode="mul", a_off=OFF["scc"], b_off=OFF["scx"],
                        C=BW, name="conv_sc_fwd")
        cv_d = conv_fwd(u, full_small["cf_conv_w"][l], cf_conv_b[l][None], mode="glu", a_off=OFF["glu"], b_off=OFF["glu"] + BW,
                        C=BW, name="conv_cf_fwd")
        y_b, y_c, y_d = post_fwd(u, y_ssd, cv_c, cv_d, ssm_norm_w[l][None], cf_ln_w[l][None], cf_ln_b[l][None], name="post_fwd")
        merged, gates, proj = merge_fwd(h, (y_a, y_b, y_c, y_d), w["wg"], full_small["b_gate"][l][:, None, :], w["wb"],
                                        name="merge_fwd")
        x_next = mm(merged, w["wo"], add=xl, name="mm_out")
        saved.append(dict(x=xl, h=h, u=u, bias_small=bias_small, sp=sp, c_col=c_col, c_row=c_row, o=o, lse=lse, xbc=xbc,
                          alog_row=alog_row, dskip_l=dskip_l, hs=hs, y_ssd=y_ssd, cv_c=cv_c, cv_d=cv_d,
                          ys=(y_a, y_b, y_c, y_d), merged=merged, gates=gates, proj=proj))
        xl = x_next

    sq, dx, d_final = loss_head(xl, final_norm_w[None], loss_target[0], name="loss_head")
    loss = lax.psum(sq[0, 0] * (0.5 / D), ("x", "y", "c"))

    small_g = {n: [None] * L for n in SMALL if n != "final_norm_w"}
    big_g = {n: [None] * L for n in BIG}
    for l in reversed(range(L)):
        w, sv = lw[l], saved[l]
        u, h = sv["u"], sv["h"]
        ht = h.T
        dm = mm(dx, w["wot"], name="mm_dmerged")
        d_wo = mm(sv["merged"].T, dx, out_dtype=BF16, name="mm_dwo")
        dp, dg, dbg = merge_bwd(dm, sv["gates"], sv["proj"], name="merge_bwd")
        dys = [mm(dp, w["wbt"][i], K=D, a_koff=i * D, name="mm_dy") for i in range(N_BRANCH)]
        d_wb = [mm(sv["ys"][i].T, dp, N=D, b_noff=i * D, out_dtype=BF16, name="mm_dwb") for i in range(N_BRANCH)]
        d_wg = mm(ht, dg, out_dtype=BF16, name="mm_dwg")
        (do, delta, dga, dz, dscb, dgc, dgd, dy_ssd, dcv_c, dcv_d, dnw, dlnw, dlnb) = post_bwd(
            dys[0], dys[1], dys[2], dys[3], u, sv["o"], sv["y_ssd"], sv["cv_c"], sv["cv_d"],
            ssm_norm_w[l][None], cf_ln_w[l][None], cf_ln_b[l][None], name="post_bwd")
        delta_row = delta[:, :FOX_HEADS].T[:, None, :]
        lse_row = jnp.transpose(sv["lse"], (0, 2, 1))
        dq, dk, dv, dc_col = attn_bwd(u, do, sv["c_col"], sv["c_row"], lse_row, delta_row, name="attn_bwd")
        dscc, dscx, d_scw, d_scb = conv_bwd(dcv_c, u, full_small["sc_conv_w"][l], mode="mul", a_off=OFF["scc"],
                                            b_off=OFF["scx"], C=BW, name="conv_sc_bwd")
        dglua, dglug, d_cfw, d_cfb = conv_bwd(dcv_d, u, full_small["cf_conv_w"][l], mode="glu", a_off=OFF["glu"],
                                              b_off=OFF["glu"] + BW, C=BW, name="conv_cf_bwd")
        dxbc_pre, ddt, d_a, d_dl = ssd_bwd(dy_ssd, sv["xbc"], sv["sp"], sv["alog_row"], sv["dskip_l"], sv["hs"], name="ssd_bwd")
        dxbc, d_ssmw, d_ssmb = conv_bwd(dxbc_pre, u, full_small["ssm_conv_w"][l], mode="plain", a_off=OFF["xbc"], b_off=0,
                                        C=SSM_CONV_DIM, name="conv_ssm_bwd")
        dc_full = jnp.pad(dc_col[:, :, 0].T, ((0, 0), (F_LANE, LANES - F_LANE - FOX_HEADS)))
        du_small, dbias_small = small_bwd(dc_full, ddt, u, sv["bias_small"], name="small_bwd")
        by_name = dict(q=dq.astype(BF16), k=dk, v=dv, ga=dga, z=dz, scb=dscb, scc=dscc, scx=dscx, gc=dgc, gd=dgd, xbc=dxbc)
        du = jnp.concatenate([jnp.concatenate([dglua, dglug], axis=1) if n == "glu" else by_name[n] for n in PAD_ORDER]
                             + [du_small], axis=1)
        dh = mm(du, w["wpt"], add=mm(dg, w["wgt"], name="mm_dh_gate"), name="mm_dh_in")
        d_wp = mm(ht, du, out_dtype=BF16, name="mm_dwp")
        dx, d_nw = rms_bwd(dh, sv["x"], norm_w[l][None], dx, name="rms_bwd")

        a_neg = -jnp.exp(a_log[l])
        sg = dict(norm_w=d_nw[0], fg_bias=dbias_small[0, F_LANE:F_LANE + FOX_HEADS], ssm_conv_w=d_ssmw, ssm_conv_b=d_ssmb[0],
                  dt_bias=dbias_small[0, DT_LANE:DT_LANE + SSM_HEADS], a_log=d_a[0, :SSM_HEADS] * a_neg,
                  d_skip=d_dl.reshape(SSM_HEADS, SSM_P).sum(-1), ssm_norm_w=dnw[0], sc_conv_w=d_scw, sc_conv_b=d_scb[0],
                  cf_conv_w=d_cfw, cf_conv_b=d_cfb[0], cf_ln_w=dlnw[0], cf_ln_b=dlnb[0], b_gate=dbg.reshape(N_BRANCH, D))
        for n in sg:
            small_g[n][l] = sg[n]

        d_win = _from_padded(d_wp)
        d_wg4 = jnp.transpose(d_wg.reshape(D, N_BRANCH, D), (1, 0, 2))
        d_wb4 = jnp.stack(d_wb)
        q_in, q_d = N_IN // N_CHIP, D // N_CHIP
        dest = [_pack([d_win[:, k * q_in:(k + 1) * q_in], d_wg4[:, k * q_d:(k + 1) * q_d, :],
                       d_wb4[:, :, k * q_d:(k + 1) * q_d], d_wo[k * q_d:(k + 1) * q_d, :]], PACK_C, 64) for k in range(N_CHIP)]
        R = dest[0].shape[0] // 2
        g_all = jnp.stack(dest).reshape(N_CHIP, 2, R, PACK_C)
        sib = swap_sibling(lax.dynamic_index_in_dim(g_all, 1 - ci, 1, keepdims=False), name="swap_partials")
        own = lax.dynamic_index_in_dim(g_all, ci, 1, keepdims=False)
        part = addn([own.reshape(N_CHIP * R, PACK_C), sib.reshape(N_CHIP * R, PACK_C)], out_dtype=BF16,
                    name="add_sibling").reshape(N_CHIP, R, PACK_C)
        rcv = scatter_chips(part, name="scatter_partials")
        half = addn([lax.dynamic_index_in_dim(part, chip, 0, keepdims=False), rcv[0], rcv[1], rcv[2]], out_dtype=F32,
                    name="add_chips")
        other = swap_sibling(half[None], name="swap_halves")[0]
        full = jnp.concatenate([jnp.where(ci == 0, half, other), jnp.where(ci == 0, other, half)], axis=0)
        for n, g in zip(BIG, _unpack(full, big_shapes)):
            big_g[n][l] = g

    names = [n for n in SMALL if n != "final_norm_w"]
    stacked = [jnp.stack(small_g[n]) for n in names] + [d_final[0]]
    shapes = [a.shape for a in stacked]
    got = gather_all(_pack(stacked, LANES, 8), name="gather_small_g")
    tot = addn([got[d] for d in range(8)], out_dtype=F32, name="add_small_g")
    grads = dict(zip(names + ["final_norm_w"], _unpack(tot, shapes)))
    for n in SMALL_SHARDED:
        sz = wts[n].shape[-1]
        grads[n] = lax.dynamic_slice_in_dim(grads[n], chip * sz, sz, axis=grads[n].ndim - 1)
    for n in BIG:
        grads[n] = jnp.stack(big_g[n])

    delta, new_m, new_v = {}, {}, {}
    for n in BIG:
        two_d = lambda a: a.reshape(-1, a.shape[-1])
        d, nm, nv = adamw(two_d(wts[n]), two_d(grads[n]), two_d(mom[n]), two_d(vel[n]), name="adamw_" + n)
        delta[n], new_m[n], new_v[n] = (t.reshape(wts[n].shape) for t in (d, nm, nv))
    small_shapes = [wts[n].shape for n in SMALL]
    pk = lambda src: _pack([src[n] for n in SMALL], LANES, 8)
    d, nm, nv = adamw(pk(wts), pk(grads), pk(mom), pk(vel), name="adamw_small")
    for tgt, src in ((delta, d), (new_m, nm), (new_v, nv)):
        for n, a in zip(SMALL, _unpack(src, small_shapes)):
            tgt[n] = a

    return (loss, dx[None], *[grads[n] for n in WEIGHTS], *[delta[n] for n in WEIGHTS],
            *[new_m[n] for n in WEIGHTS], *[new_v[n] for n in WEIGHTS])
```

```python
import functools
import math

import jax
import jax.numpy as jnp
from jax import lax
from jax.experimental import pallas as pl
from jax.experimental.pallas import tpu as pltpu

F32, BF16 = jnp.float32, jnp.bfloat16
MESH = pl.DeviceIdType.MESH

D_MODEL = 2048
BW = D_MODEL // 2
FOX_HEADS, FOX_HD = 8, 128
SSM_HEADS, SSM_P, SSM_N, SSM_G = 16, 64, 128, 2
SSM_CONV_DIM = BW + 2 * SSM_G * SSM_N
N_BRANCH = 4
EPS = 1e-6
NEG = -1e30
ORIG_SIZES = (BW, BW, BW, FOX_HEADS, BW, BW, SSM_CONV_DIM, SSM_HEADS, BW, BW, BW, BW, 2 * BW, BW)
ORIG_NAMES = ("q", "k", "v", "f", "ga", "z", "xbc", "dt", "scb", "scc", "scx", "gc", "glu", "gd")
N_IN = sum(ORIG_SIZES)
PAD_ORDER = ("q", "k", "v", "ga", "z", "scb", "scc", "scx", "gc", "glu", "gd", "xbc")
SMALL_W = 512
DT_LANE, F_LANE = 0, SSM_HEADS
OFF = {}
_o = 0
for _n in PAD_ORDER:
    OFF[_n] = _o
    _o += ORIG_SIZES[ORIG_NAMES.index(_n)]
OFF["small"] = _o
NP = _o + SMALL_W
LANES = 128
CONV_HALO = 32

VMEM_LIMIT = 56 * 1024 * 1024
MM_VMEM_BUDGET = 40 * 1024 * 1024

ADAM_LR, ADAM_B1, ADAM_B2, ADAM_EPS, ADAM_WD, ADAM_STEP = 0.001, 0.9, 0.999, 1e-08, 0.01, 10

TILES = dict(row=512, post=256, att=512, ssd=256, mm_m=1024, mm_n=1024, conv_c=256)


def _cp(sem=None):
    return pltpu.CompilerParams(dimension_semantics=sem, vmem_limit_bytes=VMEM_LIMIT)


def _tile(n, pref):
    t = 1 << (min(n, pref).bit_length() - 1)
    while n % t:
        t //= 2
    return t


def _sig(x):
    return 1.0 / (1.0 + jnp.exp(-x))


def _silu(x):
    return x * _sig(x)


def _dsilu(x):
    s = _sig(x)
    return s * (1.0 + x * (1.0 - s))


def _softplus(x):
    return jnp.maximum(x, 0.0) + jnp.log(1.0 + jnp.exp(-jnp.abs(x)))


def mm(a, b, *, name, out_dtype=F32, add=None, ta=False, tb=False, M=None, K=None, N=None,
       a_koff=0, a_moff=0, b_koff=0, b_noff=0):
    b3 = b.ndim == 3
    assert not b3 or (tb and b_koff == 0 and b_noff == 0)
    M = M or (a.shape[1] if ta else a.shape[0])
    K = K or (a.shape[0] if ta else a.shape[1])
    N = N or (b.shape[-2] if tb else b.shape[1])
    tm, tn = _tile(M, TILES["mm_m"]), _tile(N, TILES["mm_n"])
    sa, sb, so = a.dtype.itemsize, b.dtype.itemsize, jnp.dtype(out_dtype).itemsize

    def need(tk):
        return 2 * tm * tk * sa + 2 * tk * tn * sb + 2 * tm * tn * so + (8 * tm * tn if add is not None else 0) + 4 * tm * tn

    tk = b.shape[2] if b3 else K
    while need(tk) > MM_VMEM_BUDGET and tk % 256 == 0:
        tk //= 2
    assert K % tk == 0 and a_koff % tk == 0 and b_koff % tk == 0 and b_noff % tn == 0 and a_moff % tm == 0, (name, K, tk, tn)
    nk = K // tk
    ako, amo, bko, bno = a_koff // tk, a_moff // tm, b_koff // tk, b_noff // tn
    dims = (((0 if ta else 1,), (1 if tb else 0,)), ((), ()))

    def body(*refs):
        if add is not None:
            a_ref, b_ref, add_ref, o_ref = refs[:4]
        else:
            a_ref, b_ref, o_ref = refs[:3]
            add_ref = None
        prod = lax.dot_general(a_ref[...].astype(BF16), b_ref[...].astype(BF16), dims, preferred_element_type=F32)

        def finish(acc):
            if add_ref is not None:
                acc = acc + add_ref[...]
            o_ref[...] = acc.astype(out_dtype)

        if nk == 1:
            finish(prod)
        else:
            acc_ref = refs[-1]
            k = pl.program_id(2)

            @pl.when(k == 0)
            def _():
                acc_ref[...] = prod

            @pl.when(k > 0)
            def _():
                acc_ref[...] += prod

            @pl.when(k == nk - 1)
            def _():
                finish(acc_ref[...])

    if ta:
        a_spec = pl.BlockSpec((tk, tm), lambda i, j, k: (k + ako, i + amo))
    else:
        a_spec = pl.BlockSpec((tm, tk), lambda i, j, k: (i, k + ako))
    if b3:
        assert tk == b.shape[2], (name, tk)
        b_spec = pl.BlockSpec((None, tn, tk), lambda i, j, k: (k, j, 0))
    elif tb:
        b_spec = pl.BlockSpec((tn, tk), lambda i, j, k: (j + bno, k + bko))
    else:
        b_spec = pl.BlockSpec((tk, tn), lambda i, j, k: (k + bko, j + bno))
    in_specs = [a_spec, b_spec]
    args = [a, b]
    if add is not None:
        in_specs.append(pl.BlockSpec((tm, tn), lambda i, j, k: (i, j)))
        args.append(add)
    return pl.pallas_call(
        body, name=name, grid=(M // tm, N // tn, nk),
        in_specs=in_specs, out_specs=pl.BlockSpec((tm, tn), lambda i, j, k: (i, j)),
        out_shape=jax.ShapeDtypeStruct((M, N), out_dtype),
        scratch_shapes=[pltpu.VMEM((tm, tn), F32)] if nk > 1 else [],
        compiler_params=_cp(("parallel", "parallel", "arbitrary")),
    )(*args)


def rms_fwd(x, w, *, name):
    S, D = x.shape
    ts = _tile(S, TILES["row"])

    def body(x_ref, w_ref, o_ref):
        xf = x_ref[...]
        r = lax.rsqrt(jnp.mean(xf * xf, axis=-1, keepdims=True) + EPS)
        o_ref[...] = (xf * r * w_ref[...]).astype(BF16)

    return pl.pallas_call(
        body, name=name, grid=(S // ts,),
        in_specs=[pl.BlockSpec((ts, D), lambda i: (i, 0)), pl.BlockSpec((1, D), lambda i: (0, 0))],
        out_specs=pl.BlockSpec((ts, D), lambda i: (i, 0)),
        out_shape=jax.ShapeDtypeStruct((S, D), BF16), compiler_params=_cp(("parallel",)),
    )(x, w)


def rms_bwd(dh, x, w, dres, *, name):
    S, D = x.shape
    ts = _tile(S, TILES["row"])

    def body(dh_ref, x_ref, w_ref, dres_ref, dx_ref, dw_ref):
        xf = x_ref[...]
        r = lax.rsqrt(jnp.mean(xf * xf, axis=-1, keepdims=True) + EPS)
        xh = xf * r
        g = dh_ref[...]
        dxh = g * w_ref[...]
        dx_ref[...] = dres_ref[...] + r * (dxh - xh * jnp.mean(dxh * xh, axis=-1, keepdims=True))
        part = jnp.sum(g * xh, axis=0, keepdims=True)

        @pl.when(pl.program_id(0) == 0)
        def _():
            dw_ref[...] = part

        @pl.when(pl.program_id(0) > 0)
        def _():
            dw_ref[...] += part

    row = pl.BlockSpec((ts, D), lambda i: (i, 0))
    vec = pl.BlockSpec((1, D), lambda i: (0, 0))
    return pl.pallas_call(
        body, name=name, grid=(S // ts,), in_specs=[row, row, vec, row], out_specs=[row, vec],
        out_shape=[jax.ShapeDtypeStruct((S, D), F32), jax.ShapeDtypeStruct((1, D), F32)],
        compiler_params=_cp(("arbitrary",)),
    )(dh, x, w, dres)


def loss_head(x, w, target, *, name):
    S, D = x.shape
    ts = _tile(S, TILES["row"])

    def body(x_ref, w_ref, t_ref, loss_ref, dx_ref, dw_ref):
        xf = x_ref[...]
        r = lax.rsqrt(jnp.mean(xf * xf, axis=-1, keepdims=True) + EPS)
        xh = xf * r
        err = xh * w_ref[...] - t_ref[...]
        sq = jnp.sum(jnp.sum(err * err, axis=0, keepdims=True), axis=1, keepdims=True)
        dy = err * (1.0 / D)
        dxh = dy * w_ref[...]
        dx_ref[...] = r * (dxh - xh * jnp.mean(dxh * xh, axis=-1, keepdims=True))
        part = jnp.sum(dy * xh, axis=0, keepdims=True)

        @pl.when(pl.program_id(0) == 0)
        def _():
            dw_ref[...] = part
            loss_ref[...] = jnp.broadcast_to(sq, (8, LANES))

        @pl.when(pl.program_id(0) > 0)
        def _():
            dw_ref[...] += part
            loss_ref[...] += jnp.broadcast_to(sq, (8, LANES))

    row = pl.BlockSpec((ts, D), lambda i: (i, 0))
    vec = pl.BlockSpec((1, D), lambda i: (0, 0))
    return pl.pallas_call(
        body, name=name, grid=(S // ts,), in_specs=[row, vec, row],
        out_specs=[pl.BlockSpec((8, LANES), lambda i: (0, 0)), row, vec],
        out_shape=[jax.ShapeDtypeStruct((8, LANES), F32), jax.ShapeDtypeStruct((S, D), F32),
                   jax.ShapeDtypeStruct((1, D), F32)],
        compiler_params=_cp(("arbitrary",)),
    )(x, w, target)


def _shift_scan(v, buf_ref, n, reverse=False):
    buf_ref[pl.ds(0, n), :] = jnp.zeros((n, LANES), F32)
    buf_ref[pl.ds(2 * n, n), :] = jnp.zeros((n, LANES), F32)
    s = 1
    while s < n:
        buf_ref[pl.ds(n, n), :] = v
        v = v + buf_ref[pl.ds(n + s if reverse else n - s, n), :]
        s *= 2
    return v


def small_fwd(u, bias, *, name):
    S = u.shape[0]
    ts = _tile(S, TILES["row"])
    cb = OFF["small"] // LANES

    def body(u_ref, b_ref, sp_ref, c_ref, buf_ref, carry_ref):
        x = u_ref[...] + b_ref[...]
        sp_ref[...] = _softplus(x)
        lf = jnp.minimum(x, 0.0) - jnp.log(1.0 + jnp.exp(-jnp.abs(x)))

        @pl.when(pl.program_id(0) == 0)
        def _():
            carry_ref[...] = jnp.zeros((8, LANES), F32)

        c = _shift_scan(lf, buf_ref, ts) + carry_ref[pl.ds(0, 1), :]
        c_ref[...] = c
        carry_ref[...] = jnp.broadcast_to(c_ref[pl.ds(ts - 1, 1), :], (8, LANES))

    blk = pl.BlockSpec((ts, LANES), lambda i: (i, 0))
    return pl.pallas_call(
        body, name=name, grid=(S // ts,),
        in_specs=[pl.BlockSpec((ts, LANES), lambda i: (i, cb)), pl.BlockSpec((1, LANES), lambda i: (0, 0))],
        out_specs=[blk, blk], out_shape=[jax.ShapeDtypeStruct((S, LANES), F32)] * 2,
        scratch_shapes=[pltpu.VMEM((3 * ts, LANES), F32), pltpu.VMEM((8, LANES), F32)],
        compiler_params=_cp(("arbitrary",)),
    )(u, bias)


def small_bwd(dc, dsp, u, bias, *, name):
    S = u.shape[0]
    ts = _tile(S, TILES["row"])
    cb = OFF["small"] // LANES
    nt = S // ts

    def body(dc_ref, dsp_ref, u_ref, b_ref, du_ref, db_ref, buf_ref, carry_ref):
        x = u_ref[...] + b_ref[...]

        @pl.when(pl.program_id(0) == 0)
        def _():
            carry_ref[...] = jnp.zeros((8, LANES), F32)

        dlf = _shift_scan(dc_ref[...], buf_ref, ts, reverse=True) + carry_ref[pl.ds(0, 1), :]
        buf_ref[pl.ds(0, ts), :] = dlf
        carry_ref[...] = jnp.broadcast_to(buf_ref[pl.ds(0, 1), :], (8, LANES))
        sg = _sig(x)
        dx = dlf * (1.0 - sg) + dsp_ref[...] * sg
        du_ref[...] = jnp.concatenate([dx, jnp.zeros((ts, SMALL_W - LANES), F32)], axis=1).astype(BF16)
        part = jnp.sum(dx, axis=0, keepdims=True)

        @pl.when(pl.program_id(0) == 0)
        def _():
            db_ref[...] = part

        @pl.when(pl.program_id(0) > 0)
        def _():
            db_ref[...] += part

    rev = pl.BlockSpec((ts, LANES), lambda i: (nt - 1 - i, 0))
    return pl.pallas_call(
        body, name=name, grid=(nt,),
        in_specs=[rev, rev, pl.BlockSpec((ts, LANES), lambda i: (nt - 1 - i, cb)), pl.BlockSpec((1, LANES), lambda i: (0, 0))],
        out_specs=[pl.BlockSpec((ts, SMALL_W), lambda i: (nt - 1 - i, 0)), pl.BlockSpec((1, LANES), lambda i: (0, 0))],
        out_shape=[jax.ShapeDtypeStruct((S, SMALL_W), BF16), jax.ShapeDtypeStruct((1, LANES), F32)],
        scratch_shapes=[pltpu.VMEM((3 * ts, LANES), F32), pltpu.VMEM((8, LANES), F32)],
        compiler_params=_cp(("arbitrary",)),
    )(dc, dsp, u, bias)


def _conv_in(mode, a, b):
    if mode == "plain":
        return a
    if mode == "mul":
        return a * b
    return a * _sig(b)


def conv_fwd(u, w, bias, *, mode, a_off, b_off, C, name):
    S = u.shape[0]
    Kc = w.shape[0]
    ts, cb, H = _tile(S, TILES["row"]), _tile(C, TILES["conv_c"]), CONV_HALO
    two = mode != "plain"
    rb = ts // H

    def body(*refs):
        if two:
            a_ref, ap_ref, b_ref, bp_ref, w_ref, bias_ref, o_ref, x_ref = refs
            cur, prev = _conv_in(mode, a_ref[...], b_ref[...]), _conv_in(mode, ap_ref[...], bp_ref[...])
        else:
            a_ref, ap_ref, w_ref, bias_ref, o_ref, x_ref = refs
            cur, prev = a_ref[...], ap_ref[...]
        x_ref[pl.ds(0, H), :] = jnp.where(pl.program_id(1) == 0, 0.0, prev)
        x_ref[pl.ds(H, ts), :] = cur
        acc = jnp.broadcast_to(bias_ref[...], (ts, cb))
        for j in range(Kc):
            acc = acc + w_ref[pl.ds(j, 1), :] * x_ref[pl.ds(H - (Kc - 1) + j, ts), :]
        o_ref[...] = acc

    def cur_spec(off):
        return pl.BlockSpec((ts, cb), lambda c, i: (i, c + off // cb))

    def prev_spec(off):
        return pl.BlockSpec((H, cb), lambda c, i: (jnp.maximum(i * rb - 1, 0), c + off // cb))

    in_specs, args = [cur_spec(a_off), prev_spec(a_off)], [u, u]
    if two:
        in_specs += [cur_spec(b_off), prev_spec(b_off)]
        args += [u, u]
    in_specs += [pl.BlockSpec((Kc, cb), lambda c, i: (0, c)), pl.BlockSpec((1, cb), lambda c, i: (0, c))]
    return pl.pallas_call(
        body, name=name, grid=(C // cb, S // ts), in_specs=in_specs,
        out_specs=pl.BlockSpec((ts, cb), lambda c, i: (i, c)), out_shape=jax.ShapeDtypeStruct((S, C), F32),
        scratch_shapes=[pltpu.VMEM((ts + H, cb), F32)], compiler_params=_cp(("parallel", "arbitrary")),
    )(*args, w, bias)


def conv_bwd(dy, u, w, *, mode, a_off, b_off, C, name):
    S = u.shape[0]
    Kc = w.shape[0]
    ts, cb, H = _tile(S, TILES["row"]), _tile(C, TILES["conv_c"]), CONV_HALO
    two = mode != "plain"
    rb, nt = ts // H, S // ts

    def body(*refs):
        if two:
            dy_ref, dyn_ref, a_ref, ap_ref, b_ref, bp_ref, w_ref, da_ref, db_ref, dw_ref, dbias_ref, x_ref, g_ref = refs
            a, b = a_ref[...], b_ref[...]
            cur, prev = _conv_in(mode, a, b), _conv_in(mode, ap_ref[...], bp_ref[...])
        else:
            dy_ref, dyn_ref, a_ref, ap_ref, w_ref, da_ref, dw_ref, dbias_ref, x_ref, g_ref = refs
            cur, prev = a_ref[...], ap_ref[...]
        i = pl.program_id(1)
        x_ref[pl.ds(0, H), :] = jnp.where(i == 0, 0.0, prev)
        x_ref[pl.ds(H, ts), :] = cur
        g = dy_ref[...]
        g_ref[pl.ds(0, ts), :] = g
        g_ref[pl.ds(ts, H), :] = jnp.where(i == nt - 1, 0.0, dyn_ref[...])

        @pl.when(i == 0)
        def _():
            dw_ref[...] = jnp.zeros((Kc, cb), F32)
            dbias_ref[...] = jnp.zeros((1, cb), F32)

        dbias_ref[...] += jnp.sum(g, axis=0, keepdims=True)
        dx = jnp.zeros((ts, cb), F32)
        for j in range(Kc):
            dx = dx + w_ref[pl.ds(j, 1), :] * g_ref[pl.ds(Kc - 1 - j, ts), :]
            dw_ref[pl.ds(j, 1), :] += jnp.sum(g * x_ref[pl.ds(H - (Kc - 1) + j, ts), :], axis=0, keepdims=True)
        if mode == "plain":
            da_ref[...] = dx.astype(BF16)
        elif mode == "mul":
            da_ref[...] = (dx * b).astype(BF16)
            db_ref[...] = (dx * a).astype(BF16)
        else:
            sg = _sig(b)
            da_ref[...] = (dx * sg).astype(BF16)
            db_ref[...] = (dx * a * sg * (1.0 - sg)).astype(BF16)

    def cur_spec(off):
        return pl.BlockSpec((ts, cb), lambda c, i: (i, c + off // cb))

    def prev_spec(off):
        return pl.BlockSpec((H, cb), lambda c, i: (jnp.maximum(i * rb - 1, 0), c + off // cb))

    out_blk = pl.BlockSpec((ts, cb), lambda c, i: (i, c))
    in_specs = [out_blk, pl.BlockSpec((H, cb), lambda c, i: (jnp.minimum((i + 1) * rb, S // H - 1), c)),
                cur_spec(a_off), prev_spec(a_off)]
    args = [dy, dy, u, u]
    if two:
        in_specs += [cur_spec(b_off), prev_spec(b_off)]
        args += [u, u]
    in_specs.append(pl.BlockSpec((Kc, cb), lambda c, i: (0, c)))
    n_d = 2 if two else 1
    return pl.pallas_call(
        body, name=name, grid=(C // cb, nt), in_specs=in_specs,
        out_specs=[out_blk] * n_d + [pl.BlockSpec((Kc, cb), lambda c, i: (0, c)), pl.BlockSpec((1, cb), lambda c, i: (0, c))],
        out_shape=[jax.ShapeDtypeStruct((S, C), BF16)] * n_d + [jax.ShapeDtypeStruct((Kc, C), F32), jax.ShapeDtypeStruct((1, C), F32)],
        scratch_shapes=[pltpu.VMEM((ts + H, cb), F32), pltpu.VMEM((ts + H, cb), F32)],
        compiler_params=_cp(("parallel", "arbitrary")),
    )(*args, w)


def _usec(ts, name):
    return pl.BlockSpec((ts, BW), lambda i, o=OFF[name] // BW: (i, o))


def _acc_rows(ref, part):
    @pl.when(pl.program_id(0) == 0)
    def _():
        ref[...] = part

    @pl.when(pl.program_id(0) > 0)
    def _():
        ref[...] += part


def post_fwd(u, y_ssd, cv_c, cv_d, nw, lnw, lnb, *, name):
    S = u.shape[0]
    ts = _tile(S, TILES["post"])

    def body(z_ref, scb_ref, gc_ref, gd_ref, ys_ref, cc_ref, cd_ref, nw_ref, lw_ref, lb_ref, yb_ref, yc_ref, yd_ref):
        t = ys_ref[...] * _silu(z_ref[...])
        r = lax.rsqrt(jnp.mean(t * t, axis=-1, keepdims=True) + EPS)
        yb_ref[...] = (t * r * nw_ref[...]).astype(BF16)
        yc_ref[...] = (scb_ref[...] * cc_ref[...] * _silu(gc_ref[...])).astype(BF16)
        cf = cd_ref[...]
        mu = jnp.mean(cf, axis=-1, keepdims=True)
        xc = cf - mu
        rl = lax.rsqrt(jnp.mean(xc * xc, axis=-1, keepdims=True) + EPS)
        yln = xc * rl * lw_ref[...] + lb_ref[...]
        yd_ref[...] = (_silu(yln) * _silu(gd_ref[...])).astype(BF16)

    row = pl.BlockSpec((ts, BW), lambda i: (i, 0))
    vec = pl.BlockSpec((1, BW), lambda i: (0, 0))
    return pl.pallas_call(
        body, name=name, grid=(S // ts,),
        in_specs=[_usec(ts, "z"), _usec(ts, "scb"), _usec(ts, "gc"), _usec(ts, "gd"), row, row, row, vec, vec, vec],
        out_specs=[row] * 3, out_shape=[jax.ShapeDtypeStruct((S, BW), BF16)] * 3, compiler_params=_cp(("parallel",)),
    )(u, u, u, u, y_ssd, cv_c, cv_d, nw, lnw, lnb)


def post_bwd(dy_a, dy_b, dy_c, dy_d, u, o, y_ssd, cv_c, cv_d, nw, lnw, lnb, *, name):
    S = u.shape[0]
    ts = _tile(S, TILES["post"])

    def body(dya_ref, dyb_ref, dyc_ref, dyd_ref, ga_ref, z_ref, scb_ref, gc_ref, gd_ref, o_ref, ys_ref, cc_ref, cd_ref,
             nw_ref, lw_ref, lb_ref,
             do_ref, dl_ref, dga_ref, dz_ref, dscb_ref, dgc_ref, dgd_ref, dys_ref, dcc_ref, dcd_ref, dnw_ref, dlw_ref, dlb_ref):
        ga, ov, dya = ga_ref[...], o_ref[...], dya_ref[...]
        dob = (dya * _silu(ga)).astype(BF16)
        do_ref[...] = dob
        dga_ref[...] = (dya * ov * _dsilu(ga)).astype(BF16)
        prod = dob.astype(F32) * ov
        lane = lax.broadcasted_iota(jnp.int32, (ts, LANES), 1)
        delta = jnp.zeros((ts, LANES), F32)
        for h in range(FOX_HEADS):
            col = jnp.sum(prod[:, h * FOX_HD:(h + 1) * FOX_HD], axis=1, keepdims=True)
            delta = jnp.where(lane == h, col, delta)
        dl_ref[...] = delta
        ys, z, dyb = ys_ref[...], z_ref[...], dyb_ref[...]
        sz = _silu(z)
        t = ys * sz
        r = lax.rsqrt(jnp.mean(t * t, axis=-1, keepdims=True) + EPS)
        th = t * r
        dth = dyb * nw_ref[...]
        dt_ = r * (dth - th * jnp.mean(dth * th, axis=-1, keepdims=True))
        dys_ref[...] = dt_ * sz
        dz_ref[...] = (dt_ * ys * _dsilu(z)).astype(BF16)
        _acc_rows(dnw_ref, jnp.sum(dyb * th, axis=0, keepdims=True))
        scb, cc, gc, dyc = scb_ref[...], cc_ref[...], gc_ref[...], dyc_ref[...]
        sg = _silu(gc)
        dscb_ref[...] = (dyc * cc * sg).astype(BF16)
        dcc_ref[...] = dyc * scb * sg
        dgc_ref[...] = (dyc * scb * cc * _dsilu(gc)).astype(BF16)
        cf, gd, dyd = cd_ref[...], gd_ref[...], dyd_ref[...]
        mu = jnp.mean(cf, axis=-1, keepdims=True)
        xc = cf - mu
        rl = lax.rsqrt(jnp.mean(xc * xc, axis=-1, keepdims=True) + EPS)
        xh = xc * rl
        yln = xh * lw_ref[...] + lb_ref[...]
        dyln = dyd * _silu(gd) * _dsilu(yln)
        dgd_ref[...] = (dyd * _silu(yln) * _dsilu(gd)).astype(BF16)
        _acc_rows(dlw_ref, jnp.sum(dyln * xh, axis=0, keepdims=True))
        _acc_rows(dlb_ref, jnp.sum(dyln, axis=0, keepdims=True))
        dxh = dyln * lw_ref[...]
        dcd_ref[...] = rl * (dxh - jnp.mean(dxh, axis=-1, keepdims=True) - xh * jnp.mean(dxh * xh, axis=-1, keepdims=True))

    row = pl.BlockSpec((ts, BW), lambda i: (i, 0))
    vec = pl.BlockSpec((1, BW), lambda i: (0, 0))
    sd = jax.ShapeDtypeStruct
    return pl.pallas_call(
        body, name=name, grid=(S // ts,),
        in_specs=[row] * 4 + [_usec(ts, n) for n in ("ga", "z", "scb", "gc", "gd")] + [row] * 4 + [vec] * 3,
        out_specs=[row, pl.BlockSpec((ts, LANES), lambda i: (i, 0))] + [row] * 8 + [vec] * 3,
        out_shape=[sd((S, BW), BF16), sd((S, LANES), F32)] + [sd((S, BW), BF16)] * 5 + [sd((S, BW), F32)] * 3 + [sd((1, BW), F32)] * 3,
        compiler_params=_cp(("arbitrary",)),
    )(dy_a, dy_b, dy_c, dy_d, u, u, u, u, u, o, y_ssd, cv_c, cv_d, nw, lnw, lnb)


def merge_fwd(h, ys, wg, bg, wb, *, name):
    S, D = h.shape
    tm, tn = _tile(S, 1024), _tile(D, 512)
    nb = D // tn

    def body(h_ref, y0, y1, y2, y3, wg_ref, bg_ref, wb_ref, m_ref, g_ref, p_ref, acc_ref):
        i = pl.program_id(2)
        g = jnp.dot(h_ref[...], wg_ref[...], preferred_element_type=F32) + bg_ref[...]
        gate = _sig(g)
        for b, y_ref in enumerate((y0, y1, y2, y3)):
            @pl.when(i == b)
            def _(y_ref=y_ref):
                p = jnp.dot(y_ref[...], wb_ref[...], preferred_element_type=F32)
                g_ref[...] = gate.astype(BF16)
                p_ref[...] = p.astype(BF16)
                if b == 0:
                    acc_ref[...] = gate * p
                else:
                    acc_ref[...] += gate * p

        @pl.when(i == N_BRANCH - 1)
        def _():
            m_ref[...] = acc_ref[...].astype(BF16)

    yspec = pl.BlockSpec((tm, BW), lambda m, n, i: (m, 0))
    return pl.pallas_call(
        body, name=name, grid=(S // tm, nb, N_BRANCH),
        in_specs=[pl.BlockSpec((tm, D), lambda m, n, i: (m, 0)), yspec, yspec, yspec, yspec,
                  pl.BlockSpec((None, D, tn), lambda m, n, i: (i, 0, n)),
                  pl.BlockSpec((None, 1, tn), lambda m, n, i: (i, 0, n)),
                  pl.BlockSpec((None, BW, tn), lambda m, n, i: (i, 0, n))],
        out_specs=[pl.BlockSpec((tm, tn), lambda m, n, i: (m, n)),
                   pl.BlockSpec((tm, tn), lambda m, n, i: (m, i * nb + n)),
                   pl.BlockSpec((tm, tn), lambda m, n, i: (m, i * nb + n))],
        out_shape=[jax.ShapeDtypeStruct((S, D), BF16), jax.ShapeDtypeStruct((S, N_BRANCH * D), BF16),
                   jax.ShapeDtypeStruct((S, N_BRANCH * D), BF16)],
        scratch_shapes=[pltpu.VMEM((tm, tn), F32)],
        compiler_params=_cp(("parallel", "parallel", "arbitrary")),
    )(h, *ys, wg, bg, wb)


def merge_bwd(dm, gates, proj, *, name):
    S, D = dm.shape
    ts, tn = _tile(S, TILES["row"]), _tile(D, 512)
    nb = D // tn

    def body(dm_ref, g_ref, p_ref, dp_ref, dg_ref, db_ref):
        d = dm_ref[...]
        g = g_ref[...].astype(F32)
        dp_ref[...] = (d * g).astype(BF16)
        dg = d * p_ref[...].astype(F32) * g * (1.0 - g)
        dg_ref[...] = dg.astype(BF16)
        part = jnp.sum(dg, axis=0, keepdims=True)

        @pl.when(pl.program_id(2) == 0)
        def _():
            db_ref[...] = part

        @pl.when(pl.program_id(2) > 0)
        def _():
            db_ref[...] += part

    wide = pl.BlockSpec((ts, tn), lambda b, n, i: (i, b * nb + n))
    return pl.pallas_call(
        body, name=name, grid=(N_BRANCH, nb, S // ts),
        in_specs=[pl.BlockSpec((ts, tn), lambda b, n, i: (i, n)), wide, wide],
        out_specs=[wide, wide, pl.BlockSpec((1, tn), lambda b, n, i: (0, b * nb + n))],
        out_shape=[jax.ShapeDtypeStruct((S, N_BRANCH * D), BF16)] * 2 + [jax.ShapeDtypeStruct((1, N_BRANCH * D), F32)],
        compiler_params=_cp(("parallel", "parallel", "arbitrary")),
    )(dm, gates, proj)


_NT = (((1,), (1,)), ((), ()))
_TN = (((0,), (0,)), ((), ()))


def _tri_pairs(n, by_key):
    if by_key:
        pairs = [(i, j) for j in range(n) for i in range(j, n)]
    else:
        pairs = [(i, j) for i in range(n) for j in range(i + 1)]
    return (jnp.array([p[0] for p in pairs], jnp.int32), jnp.array([p[1] for p in pairs], jnp.int32))


def attn_fwd(u, c_row, *, name):
    S = u.shape[0]
    T = _tile(S, TILES["att"])
    n = S // T
    qo, ko, vo, go = (OFF[k] // FOX_HD for k in ("q", "k", "v", "ga"))
    scale = FOX_HD ** -0.5
    it, jt = _tri_pairs(n, by_key=False)

    def body(it_ref, jt_ref, q_ref, k_ref, v_ref, ck_ref, ga_ref, o_ref, lse_ref, ya_ref, m_ref, l_ref, acc_ref):
        i, j = it_ref[pl.program_id(1)], jt_ref[pl.program_id(1)]

        @pl.when(j == 0)
        def _():
            m_ref[...] = jnp.full((T, 1), NEG, F32)
            l_ref[...] = jnp.zeros((T, 1), F32)
            acc_ref[...] = jnp.zeros((T, FOX_HD), F32)

        def step(masked):
            qb = (q_ref[...] * scale).astype(BF16)
            s = lax.dot_general(qb, k_ref[...].astype(BF16), _NT, preferred_element_type=F32) - ck_ref[...]
            if masked:
                row = lax.broadcasted_iota(jnp.int32, (T, T), 0)
                col = lax.broadcasted_iota(jnp.int32, (T, T), 1)
                s = jnp.where(col <= row, s, NEG)
            m_old = m_ref[...]
            m_new = jnp.maximum(m_old, jnp.max(s, axis=1, keepdims=True))
            alpha = jnp.exp(m_old - m_new)
            p = jnp.exp(s - m_new)
            l_ref[...] = alpha * l_ref[...] + jnp.sum(p, axis=1, keepdims=True)
            p_hi = p.astype(BF16)
            p_lo = (p - p_hi.astype(F32)).astype(BF16)
            vb = v_ref[...].astype(BF16)
            pv = jnp.dot(p_hi, vb, preferred_element_type=F32) + jnp.dot(p_lo, vb, preferred_element_type=F32)
            acc_ref[...] = alpha * acc_ref[...] + pv
            m_ref[...] = m_new

        @pl.when(j < i)
        def _():
            step(False)

        @pl.when(j == i)
        def _():
            step(True)
            o = acc_ref[...] / l_ref[...]
            o_ref[...] = o
            lse_ref[...] = m_ref[...] + jnp.log(l_ref[...])
            ya_ref[...] = (o * _silu(ga_ref[...])).astype(BF16)

    def qsec(off):
        return pl.BlockSpec((T, FOX_HD), lambda h, p, it, jt: (it[p], off + h))

    def ksec(off):
        return pl.BlockSpec((T, FOX_HD), lambda h, p, it, jt: (jt[p], off + h))

    out = pl.BlockSpec((T, FOX_HD), lambda h, p, it, jt: (it[p], h))
    colv = pl.BlockSpec((None, T, 1), lambda h, p, it, jt: (h, it[p], 0))
    return pl.pallas_call(
        body, name=name,
        grid_spec=pltpu.PrefetchScalarGridSpec(
            num_scalar_prefetch=2, grid=(FOX_HEADS, n * (n + 1) // 2),
            in_specs=[qsec(qo), ksec(ko), ksec(vo), pl.BlockSpec((None, 1, T), lambda h, p, it, jt: (h, 0, jt[p])), qsec(go)],
            out_specs=[out, colv, out],
            scratch_shapes=[pltpu.VMEM((T, 1), F32), pltpu.VMEM((T, 1), F32), pltpu.VMEM((T, FOX_HD), F32)]),
        out_shape=[jax.ShapeDtypeStruct((S, BW), F32), jax.ShapeDtypeStruct((FOX_HEADS, S, 1), F32),
                   jax.ShapeDtypeStruct((S, BW), BF16)],
        compiler_params=_cp(("parallel", "arbitrary")),
    )(it, jt, u, u, u, c_row, u)


def attn_bwd(u, do, c_col, lse_row, delta_row, *, name):
    S = u.shape[0]
    T = _tile(S, TILES["att"])
    n = S // T
    qo, ko, vo = (OFF[k] // FOX_HD for k in ("q", "k", "v"))
    scale = FOX_HD ** -0.5
    it, jt = _tri_pairs(n, by_key=True)

    def body(it_ref, jt_ref, q_ref, k_ref, v_ref, do_ref, ck_ref, lse_ref, dl_ref, dq_ref, dk_ref, dv_ref, dc_ref,
             dka_ref, dva_ref, dca_ref):
        i, j = it_ref[pl.program_id(1)], jt_ref[pl.program_id(1)]

        @pl.when(pl.program_id(1) == 0)
        def _():
            dq_ref[...] = jnp.zeros((S, FOX_HD), F32)

        @pl.when(i == j)
        def _():
            dka_ref[...] = jnp.zeros((T, FOX_HD), F32)
            dva_ref[...] = jnp.zeros((T, FOX_HD), F32)
            dca_ref[...] = jnp.zeros((T, 1), F32)

        def step(masked):
            qb, dob = (q_ref[...] * scale).astype(BF16), do_ref[...]
            kf = k_ref[...]
            st = lax.dot_general(kf.astype(BF16), qb, _NT, preferred_element_type=F32) - ck_ref[...]
            pt = jnp.exp(st - lse_ref[...])
            if masked:
                kpos = lax.broadcasted_iota(jnp.int32, (T, T), 0)
                qpos = lax.broadcasted_iota(jnp.int32, (T, T), 1)
                pt = jnp.where(kpos <= qpos, pt, 0.0)
            dva_ref[...] += jnp.dot(pt.astype(BF16), dob, preferred_element_type=F32)
            dpt = lax.dot_general(v_ref[...].astype(BF16), dob, _NT, preferred_element_type=F32)
            dst = pt * (dpt - dl_ref[...])
            dca_ref[...] -= jnp.sum(dst, axis=1, keepdims=True)
            dsb = dst.astype(BF16)
            dka_ref[...] += jnp.dot(dsb, qb, preferred_element_type=F32)
            rows = pl.ds(pl.multiple_of(i * T, T), T)
            dq_ref[rows, :] += lax.dot_general(dsb, (kf * scale).astype(BF16), _TN, preferred_element_type=F32)

        @pl.when(i > j)
        def _():
            step(False)

        @pl.when(i == j)
        def _():
            step(True)

        @pl.when(i == n - 1)
        def _():
            dk_ref[...] = dka_ref[...].astype(BF16)
            dv_ref[...] = dva_ref[...].astype(BF16)
            dc_ref[...] = dca_ref[...]

    def qsec(off):
        return pl.BlockSpec((T, FOX_HD), lambda h, p, it, jt: (it[p], off + h))

    def ksec(off):
        return pl.BlockSpec((T, FOX_HD), lambda h, p, it, jt: (jt[p], off + h))

    qrow = pl.BlockSpec((None, 1, T), lambda h, p, it, jt: (h, 0, it[p]))
    kout = pl.BlockSpec((T, FOX_HD), lambda h, p, it, jt: (jt[p], h))
    kcol = pl.BlockSpec((None, T, 1), lambda h, p, it, jt: (h, jt[p], 0))
    return pl.pallas_call(
        body, name=name,
        grid_spec=pltpu.PrefetchScalarGridSpec(
            num_scalar_prefetch=2, grid=(FOX_HEADS, n * (n + 1) // 2),
            in_specs=[qsec(qo), ksec(ko), ksec(vo), pl.BlockSpec((T, FOX_HD), lambda h, p, it, jt: (it[p], h)),
                      kcol, qrow, qrow],
            out_specs=[pl.BlockSpec((S, FOX_HD), lambda h, p, it, jt: (0, h)), kout, kout, kcol],
            scratch_shapes=[pltpu.VMEM((T, FOX_HD), F32), pltpu.VMEM((T, FOX_HD), F32), pltpu.VMEM((T, 1), F32)]),
        out_shape=[jax.ShapeDtypeStruct((S, BW), F32), jax.ShapeDtypeStruct((S, BW), BF16),
                   jax.ShapeDtypeStruct((S, BW), BF16), jax.ShapeDtypeStruct((FOX_HEADS, S, 1), F32)],
        compiler_params=_cp(("parallel", "arbitrary")),
    )(it, jt, u, u, u, do, c_col, lse_row, delta_row)


N_PAIR = SSM_HEADS // 2
PAIRS_PER_GROUP = N_PAIR // SSM_G


def _sel_t():
    r = lax.broadcasted_iota(jnp.int32, (LANES, BW), 0)
    c = lax.broadcasted_iota(jnp.int32, (LANES, BW), 1)
    return (lax.shift_right_logical(c, 6) == r).astype(BF16)


def _sel():
    r = lax.broadcasted_iota(jnp.int32, (BW, LANES), 0)
    c = lax.broadcasted_iota(jnp.int32, (BW, LANES), 1)
    return (lax.shift_right_logical(r, 6) == c).astype(BF16)


def _dot3(x, m):
    hi = x.astype(BF16)
    r1 = x - hi.astype(F32)
    mid = r1.astype(BF16)
    lo = (r1 - mid.astype(F32)).astype(BF16)
    d = functools.partial(jnp.dot, preferred_element_type=F32)
    return d(hi, m) + d(mid, m) + d(lo, m)


def _ssd_common(x_ref, sp_ref, al_ref, buf_ref, big_ref, cst_ref, LC):
    pre = x_ref[...]
    sg = _sig(pre)
    act = pre * sg
    dt = sp_ref[...]
    a = -jnp.exp(al_ref[...])
    cs = _shift_scan(dt * a, buf_ref, LC)
    sel_t = _sel_t()
    dtl = _dot3(dt, sel_t)
    csl = _dot3(cs, sel_t)
    big_ref[...] = csl
    csl_last = big_ref[pl.ds(LC - 1, 1), :]
    cst_ref[...] = cs.T
    return pre, sg, act, dt, a, cs, dtl, csl, csl_last


def ssd_fwd(xbc, sp, alog, dskip_l, *, name):
    S = xbc.shape[0]
    LC = _tile(S, TILES["ssd"])
    nc = S // LC

    def body(x_ref, sp_ref, al_ref, dk_ref, y_ref, hs_ref, st_ref, buf_ref, big_ref, cst_ref):
        @pl.when(pl.program_id(0) == 0)
        def _():
            st_ref[...] = jnp.zeros((N_PAIR, SSM_N, LANES), F32)

        pre, sg, act, dt, a, cs, dtl, csl, csl_last = _ssd_common(x_ref, sp_ref, al_ref, buf_ref, big_ref, cst_ref, LC)
        xs, bm, cm = act[:, :BW], act[:, BW:BW + SSM_G * SSM_N], act[:, BW + SSM_G * SSM_N:]
        e_all = jnp.exp(csl)
        dec = jnp.exp(csl_last - csl)
        ad = jnp.exp(csl_last)
        xd = xs * dtl
        tril = lax.broadcasted_iota(jnp.int32, (LC, LC), 0) >= lax.broadcasted_iota(jnp.int32, (LC, LC), 1)
        lane = lax.broadcasted_iota(jnp.int32, (LC, LANES), 1)
        for g in range(SSM_G):
            bgt = bm[:, g * SSM_N:(g + 1) * SSM_N].T.astype(BF16)
            cgb = cm[:, g * SSM_N:(g + 1) * SSM_N].astype(BF16)
            cb = jnp.dot(cgb, bgt, preferred_element_type=F32)
            for q in range(PAIRS_PER_GROUP):
                pp = g * PAIRS_PER_GROUP + q
                ln = slice(pp * LANES, (pp + 1) * LANES)
                xp = xd[:, ln]
                xpb = xp.astype(BF16)
                yh = []
                for hh in range(2):
                    row_b = jnp.broadcast_to(cst_ref[pl.ds(2 * pp + hh, 1), :], (LC, LC))
                    lmat = jnp.exp(jnp.where(tril, row_b.T - row_b, NEG))
                    yh.append(jnp.dot((cb * lmat).astype(BF16), xpb, preferred_element_type=F32))
                hin = st_ref[pp]
                hs_ref[pp] = hin
                yoff = jnp.dot(cgb, hin.astype(BF16), preferred_element_type=F32) * e_all[:, ln]
                y_ref[:, ln] = jnp.where(lane < SSM_P, yh[0], yh[1]) + yoff + xs[:, ln] * dk_ref[:, ln]
                st_ref[pp] = hin * ad[:, ln] + jnp.dot(bgt, (xp * dec[:, ln]).astype(BF16), preferred_element_type=F32)

    return pl.pallas_call(
        body, name=name, grid=(nc,),
        in_specs=[pl.BlockSpec((LC, SSM_CONV_DIM), lambda c: (c, 0)), pl.BlockSpec((LC, LANES), lambda c: (c, 0)),
                  pl.BlockSpec((1, LANES), lambda c: (0, 0)), pl.BlockSpec((1, BW), lambda c: (0, 0))],
        out_specs=[pl.BlockSpec((LC, BW), lambda c: (c, 0)), pl.BlockSpec((None, N_PAIR, SSM_N, LANES), lambda c: (c, 0, 0, 0))],
        out_shape=[jax.ShapeDtypeStruct((S, BW), F32), jax.ShapeDtypeStruct((nc, N_PAIR, SSM_N, LANES), F32)],
        scratch_shapes=[pltpu.VMEM((N_PAIR, SSM_N, LANES), F32), pltpu.VMEM((3 * LC, LANES), F32),
                        pltpu.VMEM((LC, BW), F32), pltpu.VMEM((LANES, LC), F32)],
        compiler_params=_cp(("arbitrary",)),
    )(xbc, sp, alog, dskip_l)


def ssd_bwd(dy, xbc, sp, alog, dskip_l, hs, *, name):
    S = xbc.shape[0]
    LC = _tile(S, TILES["ssd"])
    nc = S // LC
    GN = SSM_G * SSM_N

    def body(dy_ref, x_ref, sp_ref, al_ref, dk_ref, hs_ref, dx_ref, ddt_ref, da_ref, dd_ref,
             dh_ref, buf_ref, big_ref, cst_ref, gcs_ref, dxd_ref):
        @pl.when(pl.program_id(0) == 0)
        def _():
            dh_ref[...] = jnp.zeros((N_PAIR, SSM_N, LANES), F32)
            da_ref[...] = jnp.zeros((1, LANES), F32)
            dd_ref[...] = jnp.zeros((1, BW), F32)

        pre, sg, act, dt, a, cs, dtl, csl, csl_last = _ssd_common(x_ref, sp_ref, al_ref, buf_ref, big_ref, cst_ref, LC)
        dact = sg * (1.0 + pre * (1.0 - sg))
        xs, bm, cm = act[:, :BW], act[:, BW:BW + GN], act[:, BW + GN:]
        e_all = jnp.exp(csl)
        dec = jnp.exp(csl_last - csl)
        ad = jnp.exp(csl_last)
        xd = xs * dtl
        d_y = dy_ref[...]
        ri = lax.broadcasted_iota(jnp.int32, (LC, LC), 0)
        ci = lax.broadcasted_iota(jnp.int32, (LC, LC), 1)
        tril, triu = ri >= ci, ci >= ri
        lane = lax.broadcasted_iota(jnp.int32, (LC, LANES), 1)
        rowi = lax.broadcasted_iota(jnp.int32, (LC, LANES), 0)
        dot = functools.partial(jnp.dot, preferred_element_type=F32)
        dot_nt = functools.partial(lax.dot_general, dimension_numbers=_NT, preferred_element_type=F32)
        for g in range(SSM_G):
            gs = slice(g * SSM_N, (g + 1) * SSM_N)
            bg, cg = bm[:, gs], cm[:, gs]
            bgb, cgb = bg.astype(BF16), cg.astype(BF16)
            bgt, cgt = bg.T.astype(BF16), cg.T.astype(BF16)
            cb, cbt = dot(cgb, bgt), dot(bgb, cgt)
            dcb = jnp.zeros((LC, LC), F32)
            dcbt = jnp.zeros((LC, LC), F32)
            dcg = jnp.zeros((LC, SSM_N), F32)
            dbg = jnp.zeros((LC, SSM_N), F32)
            for q in range(PAIRS_PER_GROUP):
                pp = g * PAIRS_PER_GROUP + q
                ln = slice(pp * LANES, (pp + 1) * LANES)
                xp, dyp, ep, decp, adp = xd[:, ln], d_y[:, ln], e_all[:, ln], dec[:, ln], ad[:, ln]
                xpb, dypb = xp.astype(BF16), dyp.astype(BF16)
                hin, dho = hs_ref[pp], dh_ref[pp]
                hb, dhob = hin.astype(BF16), dho.astype(BF16)
                yoff = dot(cgb, hb) * ep
                dgb = (dyp * ep).astype(BF16)
                dh_ref[pp] = dho * adp + dot(cgt, dgb)
                dcg = dcg + dot_nt(dgb, hb)
                zf = xp * decp
                d_z = dot(bgb, dhob)
                dbg = dbg + dot_nt(zf.astype(BF16), dhob)
                dzz = d_z * zf
                last = jnp.sum(dzz, axis=0, keepdims=True) + jnp.sum(dho * hin, axis=0, keepdims=True) * adp
                gcs = dyp * yoff - dzz + jnp.where(rowi == LC - 1, last, 0.0)
                dxd = d_z * decp
                for hh in range(2):
                    row_b = jnp.broadcast_to(cst_ref[pl.ds(2 * pp + hh, 1), :], (LC, LC))
                    col_b = row_b.T
                    lmat = jnp.exp(jnp.where(tril, col_b - row_b, NEG))
                    lmat_t = jnp.exp(jnp.where(triu, row_b - col_b, NEG))
                    hm = (lane < SSM_P) if hh == 0 else (lane >= SSM_P)
                    dml = dot_nt(jnp.where(hm, dyp, 0.0).astype(BF16), xpb) * lmat
                    dmtl = dot_nt(jnp.where(hm, xp, 0.0).astype(BF16), dypb) * lmat_t
                    dcb = dcb + dml
                    dcbt = dcbt + dmtl
                    contrib = jnp.sum(dml * cb, axis=1, keepdims=True) - jnp.sum(dmtl * cbt, axis=1, keepdims=True)
                    gcs = gcs + jnp.where(lane == hh * SSM_P, contrib, 0.0)
                    dxd = dxd + jnp.where(hm, dot((cbt * lmat_t).astype(BF16), dypb), 0.0)
                gcs_ref[:, ln] = gcs
                dxd_ref[:, ln] = dxd
            dcg = dcg + dot(dcb.astype(BF16), bgb)
            dbg = dbg + dot(dcbt.astype(BF16), cgb)
            dx_ref[:, BW + g * SSM_N:BW + (g + 1) * SSM_N] = dbg * dact[:, BW + g * SSM_N:BW + (g + 1) * SSM_N]
            dx_ref[:, BW + GN + g * SSM_N:BW + GN + (g + 1) * SSM_N] = dcg * dact[:, BW + GN + g * SSM_N:BW + GN + (g + 1) * SSM_N]
        d_xd = dxd_ref[...]
        dx_ref[:, :BW] = (d_y * dk_ref[...] + d_xd * dtl) * dact[:, :BW]
        sel = _sel()
        dda = _shift_scan(_dot3(gcs_ref[...], sel), buf_ref, LC, reverse=True)
        ddt_ref[...] = _dot3(d_xd * xs, sel) + dda * a
        da_ref[...] += jnp.sum(dda * dt, axis=0, keepdims=True)
        dd_ref[...] += jnp.sum(d_y * xs, axis=0, keepdims=True)

    rev = lambda c: (nc - 1 - c, 0)
    return pl.pallas_call(
        body, name=name, grid=(nc,),
        in_specs=[pl.BlockSpec((LC, BW), rev), pl.BlockSpec((LC, SSM_CONV_DIM), rev), pl.BlockSpec((LC, LANES), rev),
                  pl.BlockSpec((1, LANES), lambda c: (0, 0)), pl.BlockSpec((1, BW), lambda c: (0, 0)),
                  pl.BlockSpec((None, N_PAIR, SSM_N, LANES), lambda c: (nc - 1 - c, 0, 0, 0))],
        out_specs=[pl.BlockSpec((LC, SSM_CONV_DIM), rev), pl.BlockSpec((LC, LANES), rev),
                   pl.BlockSpec((1, LANES), lambda c: (0, 0)), pl.BlockSpec((1, BW), lambda c: (0, 0))],
        out_shape=[jax.ShapeDtypeStruct((S, SSM_CONV_DIM), F32), jax.ShapeDtypeStruct((S, LANES), F32),
                   jax.ShapeDtypeStruct((1, LANES), F32), jax.ShapeDtypeStruct((1, BW), F32)],
        scratch_shapes=[pltpu.VMEM((N_PAIR, SSM_N, LANES), F32), pltpu.VMEM((3 * LC, LANES), F32),
                        pltpu.VMEM((LC, BW), F32), pltpu.VMEM((LANES, LC), F32),
                        pltpu.VMEM((LC, BW), F32), pltpu.VMEM((LC, BW), F32)],
        compiler_params=_cp(("arbitrary",)),
    )(dy, xbc, sp, alog, dskip_l, hs)


def addn(arrs, *, out_dtype, name):
    R, C = arrs[0].shape
    tr = _tile(R, max(8, (1 << 20) // C))

    def body(*refs):
        acc = refs[0][...].astype(F32)
        for r in refs[1:-1]:
            acc = acc + r[...].astype(F32)
        refs[-1][...] = acc.astype(out_dtype)

    blk = pl.BlockSpec((tr, C), lambda i: (i, 0))
    return pl.pallas_call(
        body, name=name, grid=(R // tr,), in_specs=[blk] * len(arrs), out_specs=blk,
        out_shape=jax.ShapeDtypeStruct((R, C), out_dtype), compiler_params=_cp(("parallel",)),
    )(*arrs)


def adamw(w, g, m, v, *, name):
    R, C = w.shape
    tr = _tile(R, max(8, (1 << 18) // C))
    c1, c2 = 1.0 / (1.0 - ADAM_B1 ** ADAM_STEP), 1.0 / (1.0 - ADAM_B2 ** ADAM_STEP)

    def body(w_ref, g_ref, m_ref, v_ref, d_ref, nm_ref, nv_ref):
        gv = g_ref[...]
        nm = ADAM_B1 * m_ref[...] + (1.0 - ADAM_B1) * gv
        nv = ADAM_B2 * v_ref[...] + (1.0 - ADAM_B2) * (gv * gv)
        nm_ref[...] = nm
        nv_ref[...] = nv
        d_ref[...] = -ADAM_LR * ((nm * c1) / (jnp.sqrt(nv * c2) + ADAM_EPS) + ADAM_WD * w_ref[...])

    blk = pl.BlockSpec((tr, C), lambda i: (i, 0))
    return pl.pallas_call(
        body, name=name, grid=(R // tr,), in_specs=[blk] * 4, out_specs=[blk] * 3,
        out_shape=[jax.ShapeDtypeStruct((R, C), F32)] * 3, compiler_params=_cp(("parallel",)),
    )(w, g, m, v)


_ANY = pl.BlockSpec(memory_space=pl.ANY)


def _place():
    return lax.axis_index("x"), lax.axis_index("y"), lax.axis_index("c")


def _rcopy(src, dst, sems_s, sems_r, k, to):
    return pltpu.make_async_remote_copy(src_ref=src, dst_ref=dst, send_sem=sems_s.at[k], recv_sem=sems_r.at[k],
                                        device_id=to, device_id_type=MESH)


def gather_chips(shard, *, name):
    _, R, C = shard.shape

    def body(in_ref, out_ref, ss, rs, ls):
        x, y, c = _place()
        chips = [(1 - x, y), (x, 1 - y), (1 - x, 1 - y)]
        me = 2 * x + y
        mine = pltpu.make_async_copy(in_ref, out_ref.at[me], ls)
        mine.start()
        first = [_rcopy(in_ref.at[c], out_ref.at[me, c], ss, rs, j, (cx, cy, c)) for j, (cx, cy) in enumerate(chips)]
        for cp in first:
            cp.start()
        passed = []
        for j, (cx, cy) in enumerate(chips):
            blk = out_ref.at[2 * cx + cy, c]
            _rcopy(blk, blk, ss, rs, j, (x, y, c)).wait_recv()
            cp = _rcopy(blk, blk, ss, rs, 3 + j, (x, y, 1 - c))
            cp.start()
            passed.append(cp)
        for j, (cx, cy) in enumerate(chips):
            blk = out_ref.at[2 * cx + cy, 1 - c]
            _rcopy(blk, blk, ss, rs, 3 + j, (x, y, c)).wait_recv()
        for cp in first + passed:
            cp.wait_send()
        mine.wait()

    return pl.pallas_call(
        body, name=name, in_specs=[_ANY], out_specs=_ANY, out_shape=jax.ShapeDtypeStruct((4, 2, R, C), shard.dtype),
        scratch_shapes=[pltpu.SemaphoreType.DMA((6,)), pltpu.SemaphoreType.DMA((6,)), pltpu.SemaphoreType.DMA],
    )(shard)


def gather_all(block, *, name):
    R, C = block.shape
    flips = [(fx, fy, fc) for fx in (0, 1) for fy in (0, 1) for fc in (0, 1)][1:]

    def body(in_ref, out_ref, ss, rs, ls):
        x, y, c = _place()
        me = 4 * x + 2 * y + c
        mine = pltpu.make_async_copy(in_ref, out_ref.at[me], ls)
        mine.start()
        sends = []
        for k, (fx, fy, fc) in enumerate(flips):
            px, py, pc = x ^ fx, y ^ fy, c ^ fc
            cp = _rcopy(in_ref, out_ref.at[me], ss, rs, k, (px, py, pc))
            cp.start()
            sends.append(cp)
        for k, (fx, fy, fc) in enumerate(flips):
            blk = out_ref.at[4 * (x ^ fx) + 2 * (y ^ fy) + (c ^ fc)]
            _rcopy(blk, blk, ss, rs, k, (x, y, c)).wait_recv()
        for cp in sends:
            cp.wait_send()
        mine.wait()

    return pl.pallas_call(
        body, name=name, in_specs=[_ANY], out_specs=_ANY, out_shape=jax.ShapeDtypeStruct((8, R, C), block.dtype),
        scratch_shapes=[pltpu.SemaphoreType.DMA((7,)), pltpu.SemaphoreType.DMA((7,)), pltpu.SemaphoreType.DMA],
    )(block)


def swap_partials(g_all, *, name):
    N, _, R, C = g_all.shape

    def body(in_ref, out_ref, ss, rs):
        x, y, c = _place()
        cps = [_rcopy(in_ref.at[k, 1 - c], out_ref.at[k], ss, rs, k, (x, y, 1 - c)) for k in range(N)]
        for cp in cps:
            cp.start()
        for cp in cps:
            cp.wait_recv()
        for cp in cps:
            cp.wait_send()

    return pl.pallas_call(
        body, name=name, in_specs=[_ANY], out_specs=_ANY, out_shape=jax.ShapeDtypeStruct((N, R, C), g_all.dtype),
        scratch_shapes=[pltpu.SemaphoreType.DMA((N,)), pltpu.SemaphoreType.DMA((N,))],
    )(g_all)


def share_halves(half, *, name):
    R, C = half.shape

    def body(in_ref, out_ref, ss, rs, ls):
        x, y, c = _place()
        mine = pltpu.make_async_copy(in_ref, out_ref.at[c], ls)
        mine.start()
        cp = _rcopy(in_ref, out_ref.at[c], ss, rs, 0, (x, y, 1 - c))
        cp.start()
        _rcopy(in_ref, out_ref.at[1 - c], ss, rs, 0, (x, y, c)).wait_recv()
        cp.wait_send()
        mine.wait()

    return pl.pallas_call(
        body, name=name, in_specs=[_ANY], out_specs=_ANY, out_shape=jax.ShapeDtypeStruct((2, R, C), half.dtype),
        scratch_shapes=[pltpu.SemaphoreType.DMA((1,)), pltpu.SemaphoreType.DMA((1,)), pltpu.SemaphoreType.DMA],
    )(half)


def add_sibling(g_all, sib, c_arr, *, name):
    N, _, R, C = g_all.shape
    tr = _tile(R, max(8, (1 << 20) // C))

    def body(c_ref, a_ref, b_ref, o_ref):
        o_ref[...] = (a_ref[...].astype(F32) + b_ref[...].astype(F32)).astype(BF16)

    blk = pl.BlockSpec((None, tr, C), lambda k, r, c_ref: (k, r, 0))
    return pl.pallas_call(
        body, name=name,
        grid_spec=pltpu.PrefetchScalarGridSpec(
            num_scalar_prefetch=1, grid=(N, R // tr),
            in_specs=[pl.BlockSpec((None, None, tr, C), lambda k, r, c_ref: (k, c_ref[0], r, 0)), blk], out_specs=blk),
        out_shape=jax.ShapeDtypeStruct((N, R, C), BF16), compiler_params=_cp(("parallel", "parallel")),
    )(c_arr, g_all, sib)


def add_chips(part, rcv, chip_arr, *, name):
    _, R, C = part.shape
    tr = _tile(R, max(8, (1 << 20) // C))

    def body(chip_ref, p_ref, r0, r1, r2, o_ref):
        acc = p_ref[...].astype(F32)
        for r in (r0, r1, r2):
            acc = acc + r[...].astype(F32)
        o_ref[...] = acc

    def slot(j):
        return pl.BlockSpec((None, tr, C), lambda r, chip_ref, j=j: (j, r, 0))

    return pl.pallas_call(
        body, name=name,
        grid_spec=pltpu.PrefetchScalarGridSpec(
            num_scalar_prefetch=1, grid=(R // tr,),
            in_specs=[pl.BlockSpec((None, tr, C), lambda r, chip_ref: (chip_ref[0], r, 0)), slot(0), slot(1), slot(2)],
            out_specs=pl.BlockSpec((tr, C), lambda r, chip_ref: (r, 0))),
        out_shape=jax.ShapeDtypeStruct((R, C), F32), compiler_params=_cp(("parallel",)),
    )(chip_arr, part, rcv, rcv, rcv)


def scatter_chips(parts, *, name):
    _, R, C = parts.shape

    def body(in_ref, out_ref, ss, rs):
        x, y, c = _place()
        chips = [(1 - x, y), (x, 1 - y), (1 - x, 1 - y)]
        cps = [_rcopy(in_ref.at[2 * cx + cy], out_ref.at[j], ss, rs, j, (cx, cy, c)) for j, (cx, cy) in enumerate(chips)]
        for cp in cps:
            cp.start()
        for cp in cps:
            cp.wait_recv()
        for cp in cps:
            cp.wait_send()

    return pl.pallas_call(
        body, name=name, in_specs=[_ANY], out_specs=_ANY, out_shape=jax.ShapeDtypeStruct((3, R, C), parts.dtype),
        scratch_shapes=[pltpu.SemaphoreType.DMA((3,)), pltpu.SemaphoreType.DMA((3,))],
    )(parts)


WEIGHTS = ("norm_w", "w_in", "fg_bias", "ssm_conv_w", "ssm_conv_b", "dt_bias", "a_log", "d_skip", "ssm_norm_w", "sc_conv_w",
           "sc_conv_b", "cf_conv_w", "cf_conv_b", "cf_ln_w", "cf_ln_b", "w_gate", "b_gate", "w_branch", "w_out", "final_norm_w")
BIG = ("w_in", "w_gate", "w_branch", "w_out")
SMALL = tuple(n for n in WEIGHTS if n not in BIG)
SMALL_SHARDED = ("ssm_conv_w", "sc_conv_w", "cf_conv_w", "b_gate")
N_CHIP = 4
PACK_C = 1024


def _pack(arrs, cols, row_mult):
    flat = jnp.concatenate([a.reshape(-1) for a in arrs])
    n = flat.shape[0]
    rows = -(-n // cols)
    rows = -(-rows // row_mult) * row_mult
    return jnp.pad(flat, (0, rows * cols - n)).reshape(rows, cols)


def _unpack(packed, shapes):
    flat = packed.reshape(-1)
    out, o = [], 0
    for s in shapes:
        n = math.prod(s)
        out.append(flat[o:o + n].reshape(s))
        o += n
    return out


def _orig_cols():
    cols = []
    for n, s in zip(ORIG_NAMES, ORIG_SIZES):
        if n == "dt":
            cols.append((OFF["small"] + DT_LANE, s))
        elif n == "f":
            cols.append((OFF["small"] + F_LANE, s))
        else:
            cols.append((OFF[n], s))
    return cols


def _rows_to_padded(per_chip):
    q_in = N_IN // N_CHIP
    orig_off = dict(zip(ORIG_NAMES, [sum(ORIG_SIZES[:i]) for i in range(len(ORIG_SIZES))]))
    size = dict(zip(ORIG_NAMES, ORIG_SIZES))
    parts = []
    for n in PAD_ORDER + ("dt", "f"):
        lo, hi = orig_off[n], orig_off[n] + size[n]
        for k in range(N_CHIP):
            a, b = max(lo, k * q_in), min(hi, (k + 1) * q_in)
            if a < b:
                parts.append(per_chip[k][a - k * q_in:b - k * q_in])
    parts.append(jnp.zeros((SMALL_W - size["dt"] - size["f"], per_chip[0].shape[1]), per_chip[0].dtype))
    return jnp.concatenate(parts, axis=0)


def _rows_from_padded(w_pad, lo, hi):
    parts, o = [], 0
    for start, s in _orig_cols():
        a, b = max(lo, o), min(hi, o + s)
        if a < b:
            parts.append(w_pad[start + a - o:start + b - o])
        o += s
    return jnp.concatenate(parts, axis=0)


def _lanes_row(parts, width=LANES):
    v = jnp.concatenate([p.reshape(-1) for p in parts])
    return jnp.pad(v, (0, width - v.shape[0])).reshape(1, width)


def kernel(x, norm_w, w_in, fg_bias, ssm_conv_w, ssm_conv_b, dt_bias, a_log, d_skip, ssm_norm_w, sc_conv_w, sc_conv_b, cf_conv_w, cf_conv_b, cf_ln_w, cf_ln_b, w_gate, b_gate, w_branch, w_out, final_norm_w, loss_target, m_norm_w, m_w_in, m_fg_bias, m_ssm_conv_w, m_ssm_conv_b, m_dt_bias, m_a_log, m_d_skip, m_ssm_norm_w, m_sc_conv_w, m_sc_conv_b, m_cf_conv_w, m_cf_conv_b, m_cf_ln_w, m_cf_ln_b, m_w_gate, m_b_gate, m_w_branch, m_w_out, m_final_norm_w, v_norm_w, v_w_in, v_fg_bias, v_ssm_conv_w, v_ssm_conv_b, v_dt_bias, v_a_log, v_d_skip, v_ssm_norm_w, v_sc_conv_w, v_sc_conv_b, v_cf_conv_w, v_cf_conv_b, v_cf_ln_w, v_cf_ln_b, v_w_gate, v_b_gate, v_w_branch, v_w_out, v_final_norm_w):
    wts = dict(norm_w=norm_w, w_in=w_in, fg_bias=fg_bias, ssm_conv_w=ssm_conv_w, ssm_conv_b=ssm_conv_b, dt_bias=dt_bias,
               a_log=a_log, d_skip=d_skip, ssm_norm_w=ssm_norm_w, sc_conv_w=sc_conv_w, sc_conv_b=sc_conv_b,
               cf_conv_w=cf_conv_w, cf_conv_b=cf_conv_b, cf_ln_w=cf_ln_w, cf_ln_b=cf_ln_b, w_gate=w_gate, b_gate=b_gate,
               w_branch=w_branch, w_out=w_out, final_norm_w=final_norm_w)
    mom = dict(norm_w=m_norm_w, w_in=m_w_in, fg_bias=m_fg_bias, ssm_conv_w=m_ssm_conv_w, ssm_conv_b=m_ssm_conv_b,
               dt_bias=m_dt_bias, a_log=m_a_log, d_skip=m_d_skip, ssm_norm_w=m_ssm_norm_w, sc_conv_w=m_sc_conv_w,
               sc_conv_b=m_sc_conv_b, cf_conv_w=m_cf_conv_w, cf_conv_b=m_cf_conv_b, cf_ln_w=m_cf_ln_w, cf_ln_b=m_cf_ln_b,
               w_gate=m_w_gate, b_gate=m_b_gate, w_branch=m_w_branch, w_out=m_w_out, final_norm_w=m_final_norm_w)
    vel = dict(norm_w=v_norm_w, w_in=v_w_in, fg_bias=v_fg_bias, ssm_conv_w=v_ssm_conv_w, ssm_conv_b=v_ssm_conv_b,
               dt_bias=v_dt_bias, a_log=v_a_log, d_skip=v_d_skip, ssm_norm_w=v_ssm_norm_w, sc_conv_w=v_sc_conv_w,
               sc_conv_b=v_sc_conv_b, cf_conv_w=v_cf_conv_w, cf_conv_b=v_cf_conv_b, cf_ln_w=v_cf_ln_w, cf_ln_b=v_cf_ln_b,
               w_gate=v_w_gate, b_gate=v_b_gate, w_branch=v_w_branch, w_out=v_w_out, final_norm_w=v_final_norm_w)
    L = norm_w.shape[0]
    S, D = x.shape[1], x.shape[2]
    assert D == D_MODEL and x.shape[0] == 1
    xi, yi, ci = _place()
    chip = 2 * xi + yi

    sh_shapes = [wts[n].shape for n in SMALL_SHARDED]
    got = gather_all(_pack([wts[n] for n in SMALL_SHARDED], LANES, 8), name="gather_small_w")
    per_chip = [_unpack(got[2 * k], sh_shapes) for k in range(N_CHIP)]
    full_small = {n: jnp.concatenate([per_chip[k][i] for k in range(N_CHIP)], axis=-1) for i, n in enumerate(SMALL_SHARDED)}

    q_in, q_d = N_IN // N_CHIP, D // N_CHIP
    sent_shapes = [(q_in, D), (N_BRANCH, q_d, D), (N_BRANCH, BW, q_d), (q_d, D)]
    lw = []
    for l in range(L):
        packed = _pack([w_in[l].T.astype(BF16), w_gate[l].astype(BF16), w_branch[l].astype(BF16), w_out[l].astype(BF16)],
                       PACK_C, 64)
        got = gather_chips(packed.reshape(2, -1, PACK_C), name="gather_w")
        pc = [_unpack(got[k], sent_shapes) for k in range(N_CHIP)]
        lw.append(dict(wpt=_rows_to_padded([pc[k][0] for k in range(N_CHIP)]),
                       wg=jnp.concatenate([pc[k][1] for k in range(N_CHIP)], axis=1),
                       wb=jnp.concatenate([pc[k][2] for k in range(N_CHIP)], axis=2),
                       wo=jnp.concatenate([pc[k][3] for k in range(N_CHIP)], axis=0)))

    saved = []
    xl = x[0]
    for l in range(L):
        w = lw[l]
        h = rms_fwd(xl, norm_w[l][None], name="rms_fwd")
        u = mm(h, w["wpt"], tb=True, name="mm_in")
        bias_small = _lanes_row([dt_bias[l], fg_bias[l]])
        sp, csum = small_fwd(u, bias_small, name="small_fwd")
        c8 = csum[:, F_LANE:F_LANE + FOX_HEADS].T
        c_col, c_row = c8[:, :, None], c8[:, None, :]
        o, lse, y_a = attn_fwd(u, c_row, name="attn_fwd")
        xbc = conv_fwd(u, full_small["ssm_conv_w"][l], ssm_conv_b[l][None], mode="plain", a_off=OFF["xbc"], b_off=0,
                       C=SSM_CONV_DIM, name="conv_ssm_fwd")
        alog_row = _lanes_row([a_log[l]])
        dskip_l = jnp.repeat(d_skip[l], SSM_P)[None]
        y_ssd, hs = ssd_fwd(xbc, sp, alog_row, dskip_l, name="ssd_fwd")
        cv_c = conv_fwd(u, full_small["sc_conv_w"][l], sc_conv_b[l][None], mode="mul", a_off=OFF["scc"], b_off=OFF["scx"],
                        C=BW, name="conv_sc_fwd")
        cv_d = conv_fwd(u, full_small["cf_conv_w"][l], cf_conv_b[l][None], mode="glu", a_off=OFF["glu"], b_off=OFF["glu"] + BW,
                        C=BW, name="conv_cf_fwd")
        y_b, y_c, y_d = post_fwd(u, y_ssd, cv_c, cv_d, ssm_norm_w[l][None], cf_ln_w[l][None], cf_ln_b[l][None], name="post_fwd")
        merged, gates, proj = merge_fwd(h, (y_a, y_b, y_c, y_d), w["wg"], full_small["b_gate"][l][:, None, :], w["wb"],
                                        name="merge_fwd")
        x_next = mm(merged, w["wo"], add=xl, name="mm_out")
        saved.append(dict(x=xl, h=h, u=u, bias_small=bias_small, sp=sp, c_col=c_col, c_row=c_row, o=o, lse=lse, xbc=xbc,
                          alog_row=alog_row, dskip_l=dskip_l, hs=hs, y_ssd=y_ssd, cv_c=cv_c, cv_d=cv_d,
                          ys=(y_a, y_b, y_c, y_d), merged=merged, gates=gates, proj=proj))
        xl = x_next

    sq, dx, d_final = loss_head(xl, final_norm_w[None], loss_target[0], name="loss_head")
    loss = lax.psum(sq[0, 0] * (0.5 / D), ("x", "y", "c"))

    small_g = {n: [None] * L for n in SMALL if n != "final_norm_w"}
    big_g = {n: [None] * L for n in BIG}
    for l in reversed(range(L)):
        w, sv = lw[l], saved[l]
        u, h = sv["u"], sv["h"]
        dm = mm(dx, w["wo"], tb=True, name="mm_dmerged")
        d_wo = mm(sv["merged"], dx, ta=True, out_dtype=BF16, name="mm_dwo")
        dp, dg, dbg = merge_bwd(dm, sv["gates"], sv["proj"], name="merge_bwd")
        dys = [mm(dp, w["wb"][i], tb=True, K=D, a_koff=i * D, name="mm_dy") for i in range(N_BRANCH)]
        d_wbt = [mm(dp, sv["ys"][i], ta=True, M=D, a_moff=i * D, out_dtype=BF16, name="mm_dwb")
                 for i in range(N_BRANCH)]
        d_wg = mm(h, dg, ta=True, out_dtype=BF16, name="mm_dwg")
        (do, delta, dga, dz, dscb, dgc, dgd, dy_ssd, dcv_c, dcv_d, dnw, dlnw, dlnb) = post_bwd(
            dys[0], dys[1], dys[2], dys[3], u, sv["o"], sv["y_ssd"], sv["cv_c"], sv["cv_d"],
            ssm_norm_w[l][None], cf_ln_w[l][None], cf_ln_b[l][None], name="post_bwd")
        delta_row = delta[:, :FOX_HEADS].T[:, None, :]
        lse_row = jnp.transpose(sv["lse"], (0, 2, 1))
        dq, dk, dv, dc_col = attn_bwd(u, do, sv["c_col"], lse_row, delta_row, name="attn_bwd")
        dscc, dscx, d_scw, d_scb = conv_bwd(dcv_c, u, full_small["sc_conv_w"][l], mode="mul", a_off=OFF["scc"],
                                            b_off=OFF["scx"], C=BW, name="conv_sc_bwd")
        dglua, dglug, d_cfw, d_cfb = conv_bwd(dcv_d, u, full_small["cf_conv_w"][l], mode="glu", a_off=OFF["glu"],
                                              b_off=OFF["glu"] + BW, C=BW, name="conv_cf_bwd")
        dxbc_pre, ddt, d_a, d_dl = ssd_bwd(dy_ssd, sv["xbc"], sv["sp"], sv["alog_row"], sv["dskip_l"], sv["hs"], name="ssd_bwd")
        dxbc, d_ssmw, d_ssmb = conv_bwd(dxbc_pre, u, full_small["ssm_conv_w"][l], mode="plain", a_off=OFF["xbc"], b_off=0,
                                        C=SSM_CONV_DIM, name="conv_ssm_bwd")
        dc_full = jnp.pad(dc_col[:, :, 0].T, ((0, 0), (F_LANE, LANES - F_LANE - FOX_HEADS)))
        du_small, dbias_small = small_bwd(dc_full, ddt, u, sv["bias_small"], name="small_bwd")
        by_name = dict(q=dq.astype(BF16), k=dk, v=dv, ga=dga, z=dz, scb=dscb, scc=dscc, scx=dscx, gc=dgc, gd=dgd, xbc=dxbc)
        du = jnp.concatenate([jnp.concatenate([dglua, dglug], axis=1) if n == "glu" else by_name[n] for n in PAD_ORDER]
                             + [du_small], axis=1)
        dh = mm(du, w["wpt"], add=mm(dg, w["wg"], tb=True, name="mm_dh_gate"), name="mm_dh_in")
        d_wpt = mm(du, h, ta=True, out_dtype=BF16, name="mm_dwp")
        dx, d_nw = rms_bwd(dh, sv["x"], norm_w[l][None], dx, name="rms_bwd")

        a_neg = -jnp.exp(a_log[l])
        sg = dict(norm_w=d_nw[0], fg_bias=dbias_small[0, F_LANE:F_LANE + FOX_HEADS], ssm_conv_w=d_ssmw, ssm_conv_b=d_ssmb[0],
                  dt_bias=dbias_small[0, DT_LANE:DT_LANE + SSM_HEADS], a_log=d_a[0, :SSM_HEADS] * a_neg,
                  d_skip=d_dl.reshape(SSM_HEADS, SSM_P).sum(-1), ssm_norm_w=dnw[0], sc_conv_w=d_scw, sc_conv_b=d_scb[0],
                  cf_conv_w=d_cfw, cf_conv_b=d_cfb[0], cf_ln_w=dlnw[0], cf_ln_b=dlnb[0], b_gate=dbg.reshape(N_BRANCH, D))
        for n in sg:
            small_g[n][l] = sg[n]

        dest = [_pack([_rows_from_padded(d_wpt, k * q_in, (k + 1) * q_in), d_wg[k * q_d:(k + 1) * q_d]]
                      + [t[k * q_d:(k + 1) * q_d] for t in d_wbt] + [d_wo[k * q_d:(k + 1) * q_d]], PACK_C, 64)
                for k in range(N_CHIP)]
        R = dest[0].shape[0] // 2
        g_all = jnp.stack(dest).reshape(N_CHIP, 2, R, PACK_C)
        sib = swap_partials(g_all, name="swap_partials")
        part = add_sibling(g_all, sib, ci.reshape(1).astype(jnp.int32), name="add_sibling")
        rcv = scatter_chips(part, name="scatter_partials")
        half = add_chips(part, rcv, chip.reshape(1).astype(jnp.int32), name="add_chips")
        full = share_halves(half, name="share_halves")
        g_in, g_wg, g_wb, g_wo = _unpack(full, [(q_in, D), (q_d, N_BRANCH * D), (N_BRANCH, q_d, BW), (q_d, D)])
        big_g["w_in"][l] = g_in.T
        big_g["w_gate"][l] = jnp.transpose(g_wg.reshape(q_d, N_BRANCH, D), (1, 0, 2))
        big_g["w_branch"][l] = jnp.transpose(g_wb, (0, 2, 1))
        big_g["w_out"][l] = g_wo

    names = [n for n in SMALL if n != "final_norm_w"]
    stacked = [jnp.stack(small_g[n]) for n in names] + [d_final[0]]
    shapes = [a.shape for a in stacked]
    got = gather_all(_pack(stacked, LANES, 8), name="gather_small_g")
    tot = addn([got[d] for d in range(8)], out_dtype=F32, name="add_small_g")
    grads = dict(zip(names + ["final_norm_w"], _unpack(tot, shapes)))
    for n in SMALL_SHARDED:
        sz = wts[n].shape[-1]
        grads[n] = lax.dynamic_slice_in_dim(grads[n], chip * sz, sz, axis=grads[n].ndim - 1)
    for n in BIG:
        grads[n] = jnp.stack(big_g[n])

    delta, new_m, new_v = {}, {}, {}
    for n in BIG:
        two_d = lambda a: a.reshape(-1, a.shape[-1])
        d, nm, nv = adamw(two_d(wts[n]), two_d(grads[n]), two_d(mom[n]), two_d(vel[n]), name="adamw_" + n)
        delta[n], new_m[n], new_v[n] = (t.reshape(wts[n].shape) for t in (d, nm, nv))
    small_shapes = [wts[n].shape for n in SMALL]
    pk = lambda src: _pack([src[n] for n in SMALL], LANES, 8)
    d, nm, nv = adamw(pk(wts), pk(grads), pk(mom), pk(vel), name="adamw_small")
    for tgt, src in ((delta, d), (new_m, nm), (new_v, nv)):
        for n, a in zip(SMALL, _unpack(src, small_shapes)):
            tgt[n] = a

    return (loss, dx[None], *[grads[n] for n in WEIGHTS], *[delta[n] for n in WEIGHTS],
            *[new_m[n] for n in WEIGHTS], *[new_v[n] for n in WEIGHTS])
```

```python
import functools
import math

import jax
import jax.numpy as jnp
from jax import lax
from jax.experimental import pallas as pl
from jax.experimental.pallas import tpu as pltpu

F32, BF16 = jnp.float32, jnp.bfloat16
MESH = pl.DeviceIdType.MESH

D_MODEL = 2048
BW = D_MODEL // 2
FOX_HEADS, FOX_HD = 8, 128
SSM_HEADS, SSM_P, SSM_N, SSM_G = 16, 64, 128, 2
SSM_CONV_DIM = BW + 2 * SSM_G * SSM_N
N_BRANCH = 4
EPS = 1e-6
NEG = -1e30
ORIG_SIZES = (BW, BW, BW, FOX_HEADS, BW, BW, SSM_CONV_DIM, SSM_HEADS, BW, BW, BW, BW, 2 * BW, BW)
ORIG_NAMES = ("q", "k", "v", "f", "ga", "z", "xbc", "dt", "scb", "scc", "scx", "gc", "glu", "gd")
N_IN = sum(ORIG_SIZES)
PAD_ORDER = ("q", "k", "v", "ga", "z", "scb", "scc", "scx", "gc", "glu", "gd", "xbc")
SMALL_W = 512
DT_LANE, F_LANE = 0, SSM_HEADS
OFF = {}
_o = 0
for _n in PAD_ORDER:
    OFF[_n] = _o
    _o += ORIG_SIZES[ORIG_NAMES.index(_n)]
OFF["small"] = _o
NP = _o + SMALL_W
LANES = 128
CONV_HALO = 32

VMEM_LIMIT = 56 * 1024 * 1024
MM_VMEM_BUDGET = 40 * 1024 * 1024

ADAM_LR, ADAM_B1, ADAM_B2, ADAM_EPS, ADAM_WD, ADAM_STEP = 0.001, 0.9, 0.999, 1e-08, 0.01, 10

TILES = dict(row=512, post=256, att=512, ssd=256, mm_m=1024, mm_n=1024, conv_c=256)


def _cp(sem=None):
    return pltpu.CompilerParams(dimension_semantics=sem, vmem_limit_bytes=VMEM_LIMIT)


def _tile(n, pref):
    t = 1 << (min(n, pref).bit_length() - 1)
    while n % t:
        t //= 2
    return t


def _sig(x):
    return 1.0 / (1.0 + jnp.exp(-x))


def _silu(x):
    return x * _sig(x)


def _dsilu(x):
    s = _sig(x)
    return s * (1.0 + x * (1.0 - s))


def _softplus(x):
    return jnp.maximum(x, 0.0) + jnp.log(1.0 + jnp.exp(-jnp.abs(x)))


def mm(a, b, *, name, out_dtype=F32, add=None, ta=False, tb=False, M=None, K=None, N=None,
       a_koff=0, a_moff=0, b_koff=0, b_noff=0):
    b3 = b.ndim == 3
    assert not b3 or (tb and b_koff == 0 and b_noff == 0)
    M = M or (a.shape[1] if ta else a.shape[0])
    K = K or (a.shape[0] if ta else a.shape[1])
    N = N or (b.shape[-2] if tb else b.shape[1])
    tm, tn = _tile(M, TILES["mm_m"]), _tile(N, TILES["mm_n"])
    sa, sb, so = a.dtype.itemsize, b.dtype.itemsize, jnp.dtype(out_dtype).itemsize

    def need(tk):
        return 2 * tm * tk * sa + 2 * tk * tn * sb + 2 * tm * tn * so + (8 * tm * tn if add is not None else 0) + 4 * tm * tn

    tk = b.shape[2] if b3 else K
    while need(tk) > MM_VMEM_BUDGET and tk % 256 == 0:
        tk //= 2
    assert K % tk == 0 and a_koff % tk == 0 and b_koff % tk == 0 and b_noff % tn == 0 and a_moff % tm == 0, (name, K, tk, tn)
    nk = K // tk
    ako, amo, bko, bno = a_koff // tk, a_moff // tm, b_koff // tk, b_noff // tn
    dims = (((0 if ta else 1,), (1 if tb else 0,)), ((), ()))

    def body(*refs):
        if add is not None:
            a_ref, b_ref, add_ref, o_ref = refs[:4]
        else:
            a_ref, b_ref, o_ref = refs[:3]
            add_ref = None
        prod = lax.dot_general(a_ref[...].astype(BF16), b_ref[...].astype(BF16), dims, preferred_element_type=F32)

        def finish(acc):
            if add_ref is not None:
                acc = acc + add_ref[...]
            o_ref[...] = acc.astype(out_dtype)

        if nk == 1:
            finish(prod)
        else:
            acc_ref = refs[-1]
            k = pl.program_id(2)

            @pl.when(k == 0)
            def _():
                acc_ref[...] = prod

            @pl.when(k > 0)
            def _():
                acc_ref[...] += prod

            @pl.when(k == nk - 1)
            def _():
                finish(acc_ref[...])

    if ta:
        a_spec = pl.BlockSpec((tk, tm), lambda i, j, k: (k + ako, i + amo))
    else:
        a_spec = pl.BlockSpec((tm, tk), lambda i, j, k: (i, k + ako))
    if b3:
        assert tk == b.shape[2], (name, tk)
        b_spec = pl.BlockSpec((None, tn, tk), lambda i, j, k: (k, j, 0))
    elif tb:
        b_spec = pl.BlockSpec((tn, tk), lambda i, j, k: (j + bno, k + bko))
    else:
        b_spec = pl.BlockSpec((tk, tn), lambda i, j, k: (k + bko, j + bno))
    in_specs = [a_spec, b_spec]
    args = [a, b]
    if add is not None:
        in_specs.append(pl.BlockSpec((tm, tn), lambda i, j, k: (i, j)))
        args.append(add)
    return pl.pallas_call(
        body, name=name, grid=(M // tm, N // tn, nk),
        in_specs=in_specs, out_specs=pl.BlockSpec((tm, tn), lambda i, j, k: (i, j)),
        out_shape=jax.ShapeDtypeStruct((M, N), out_dtype),
        scratch_shapes=[pltpu.VMEM((tm, tn), F32)] if nk > 1 else [],
        compiler_params=_cp(("parallel", "parallel", "arbitrary")),
    )(*args)


def rms_fwd(x, w, *, name):
    S, D = x.shape
    ts = _tile(S, TILES["row"])

    def body(x_ref, w_ref, o_ref):
        xf = x_ref[...]
        r = lax.rsqrt(jnp.mean(xf * xf, axis=-1, keepdims=True) + EPS)
        o_ref[...] = (xf * r * w_ref[...]).astype(BF16)

    return pl.pallas_call(
        body, name=name, grid=(S // ts,),
        in_specs=[pl.BlockSpec((ts, D), lambda i: (i, 0)), pl.BlockSpec((1, D), lambda i: (0, 0))],
        out_specs=pl.BlockSpec((ts, D), lambda i: (i, 0)),
        out_shape=jax.ShapeDtypeStruct((S, D), BF16), compiler_params=_cp(("parallel",)),
    )(x, w)


def rms_bwd(dh, x, w, dres, *, name):
    S, D = x.shape
    ts = _tile(S, TILES["row"])

    def body(dh_ref, x_ref, w_ref, dres_ref, dx_ref, dw_ref):
        xf = x_ref[...]
        r = lax.rsqrt(jnp.mean(xf * xf, axis=-1, keepdims=True) + EPS)
        xh = xf * r
        g = dh_ref[...]
        dxh = g * w_ref[...]
        dx_ref[...] = dres_ref[...] + r * (dxh - xh * jnp.mean(dxh * xh, axis=-1, keepdims=True))
        part = jnp.sum(g * xh, axis=0, keepdims=True)

        @pl.when(pl.program_id(0) == 0)
        def _():
            dw_ref[...] = part

        @pl.when(pl.program_id(0) > 0)
        def _():
            dw_ref[...] += part

    row = pl.BlockSpec((ts, D), lambda i: (i, 0))
    vec = pl.BlockSpec((1, D), lambda i: (0, 0))
    return pl.pallas_call(
        body, name=name, grid=(S // ts,), in_specs=[row, row, vec, row], out_specs=[row, vec],
        out_shape=[jax.ShapeDtypeStruct((S, D), F32), jax.ShapeDtypeStruct((1, D), F32)],
        compiler_params=_cp(("arbitrary",)),
    )(dh, x, w, dres)


def loss_head(x, w, target, *, name):
    S, D = x.shape
    ts = _tile(S, TILES["row"])

    def body(x_ref, w_ref, t_ref, loss_ref, dx_ref, dw_ref):
        xf = x_ref[...]
        r = lax.rsqrt(jnp.mean(xf * xf, axis=-1, keepdims=True) + EPS)
        xh = xf * r
        err = xh * w_ref[...] - t_ref[...]
        sq = jnp.sum(jnp.sum(err * err, axis=0, keepdims=True), axis=1, keepdims=True)
        dy = err * (1.0 / D)
        dxh = dy * w_ref[...]
        dx_ref[...] = r * (dxh - xh * jnp.mean(dxh * xh, axis=-1, keepdims=True))
        part = jnp.sum(dy * xh, axis=0, keepdims=True)

        @pl.when(pl.program_id(0) == 0)
        def _():
            dw_ref[...] = part
            loss_ref[...] = jnp.broadcast_to(sq, (8, LANES))

        @pl.when(pl.program_id(0) > 0)
        def _():
            dw_ref[...] += part
            loss_ref[...] += jnp.broadcast_to(sq, (8, LANES))

    row = pl.BlockSpec((ts, D), lambda i: (i, 0))
    vec = pl.BlockSpec((1, D), lambda i: (0, 0))
    return pl.pallas_call(
        body, name=name, grid=(S // ts,), in_specs=[row, vec, row],
        out_specs=[pl.BlockSpec((8, LANES), lambda i: (0, 0)), row, vec],
        out_shape=[jax.ShapeDtypeStruct((8, LANES), F32), jax.ShapeDtypeStruct((S, D), F32),
                   jax.ShapeDtypeStruct((1, D), F32)],
        compiler_params=_cp(("arbitrary",)),
    )(x, w, target)


def _shift_scan(v, buf_ref, n, reverse=False):
    buf_ref[pl.ds(0, n), :] = jnp.zeros((n, LANES), F32)
    buf_ref[pl.ds(2 * n, n), :] = jnp.zeros((n, LANES), F32)
    s = 1
    while s < n:
        buf_ref[pl.ds(n, n), :] = v
        v = v + buf_ref[pl.ds(n + s if reverse else n - s, n), :]
        s *= 2
    return v


def small_fwd(u, bias, *, name):
    S = u.shape[0]
    ts = _tile(S, TILES["row"])
    cb = OFF["small"] // LANES

    def body(u_ref, b_ref, sp_ref, c_ref, buf_ref, carry_ref):
        x = u_ref[...] + b_ref[...]
        sp_ref[...] = _softplus(x)
        lf = jnp.minimum(x, 0.0) - jnp.log(1.0 + jnp.exp(-jnp.abs(x)))

        @pl.when(pl.program_id(0) == 0)
        def _():
            carry_ref[...] = jnp.zeros((8, LANES), F32)

        c = _shift_scan(lf, buf_ref, ts) + carry_ref[pl.ds(0, 1), :]
        c_ref[...] = c
        carry_ref[...] = jnp.broadcast_to(c_ref[pl.ds(ts - 1, 1), :], (8, LANES))

    blk = pl.BlockSpec((ts, LANES), lambda i: (i, 0))
    return pl.pallas_call(
        body, name=name, grid=(S // ts,),
        in_specs=[pl.BlockSpec((ts, LANES), lambda i: (i, cb)), pl.BlockSpec((1, LANES), lambda i: (0, 0))],
        out_specs=[blk, blk], out_shape=[jax.ShapeDtypeStruct((S, LANES), F32)] * 2,
        scratch_shapes=[pltpu.VMEM((3 * ts, LANES), F32), pltpu.VMEM((8, LANES), F32)],
        compiler_params=_cp(("arbitrary",)),
    )(u, bias)


def small_bwd(dc, dsp, u, bias, *, name):
    S = u.shape[0]
    ts = _tile(S, TILES["row"])
    cb = OFF["small"] // LANES
    nt = S // ts

    def body(dc_ref, dsp_ref, u_ref, b_ref, du_ref, db_ref, buf_ref, carry_ref):
        x = u_ref[...] + b_ref[...]

        @pl.when(pl.program_id(0) == 0)
        def _():
            carry_ref[...] = jnp.zeros((8, LANES), F32)

        dlf = _shift_scan(dc_ref[...], buf_ref, ts, reverse=True) + carry_ref[pl.ds(0, 1), :]
        buf_ref[pl.ds(0, ts), :] = dlf
        carry_ref[...] = jnp.broadcast_to(buf_ref[pl.ds(0, 1), :], (8, LANES))
        sg = _sig(x)
        dx = dlf * (1.0 - sg) + dsp_ref[...] * sg
        du_ref[...] = jnp.concatenate([dx, jnp.zeros((ts, SMALL_W - LANES), F32)], axis=1).astype(BF16)
        part = jnp.sum(dx, axis=0, keepdims=True)

        @pl.when(pl.program_id(0) == 0)
        def _():
            db_ref[...] = part

        @pl.when(pl.program_id(0) > 0)
        def _():
            db_ref[...] += part

    rev = pl.BlockSpec((ts, LANES), lambda i: (nt - 1 - i, 0))
    return pl.pallas_call(
        body, name=name, grid=(nt,),
        in_specs=[rev, rev, pl.BlockSpec((ts, LANES), lambda i: (nt - 1 - i, cb)), pl.BlockSpec((1, LANES), lambda i: (0, 0))],
        out_specs=[pl.BlockSpec((ts, SMALL_W), lambda i: (nt - 1 - i, 0)), pl.BlockSpec((1, LANES), lambda i: (0, 0))],
        out_shape=[jax.ShapeDtypeStruct((S, SMALL_W), BF16), jax.ShapeDtypeStruct((1, LANES), F32)],
        scratch_shapes=[pltpu.VMEM((3 * ts, LANES), F32), pltpu.VMEM((8, LANES), F32)],
        compiler_params=_cp(("arbitrary",)),
    )(dc, dsp, u, bias)


def _conv_in(mode, a, b):
    if mode == "plain":
        return a
    if mode == "mul":
        return a * b
    return a * _sig(b)


def conv_fwd(u, w, bias, *, mode, a_off, b_off, C, name):
    S = u.shape[0]
    Kc = w.shape[0]
    ts, cb, H = _tile(S, TILES["row"]), _tile(C, TILES["conv_c"]), CONV_HALO
    two = mode != "plain"
    rb = ts // H

    def body(*refs):
        if two:
            a_ref, ap_ref, b_ref, bp_ref, w_ref, bias_ref, o_ref, x_ref = refs
            cur, prev = _conv_in(mode, a_ref[...], b_ref[...]), _conv_in(mode, ap_ref[...], bp_ref[...])
        else:
            a_ref, ap_ref, w_ref, bias_ref, o_ref, x_ref = refs
            cur, prev = a_ref[...], ap_ref[...]
        x_ref[pl.ds(0, H), :] = jnp.where(pl.program_id(1) == 0, 0.0, prev)
        x_ref[pl.ds(H, ts), :] = cur
        acc = jnp.broadcast_to(bias_ref[...], (ts, cb))
        for j in range(Kc):
            acc = acc + w_ref[pl.ds(j, 1), :] * x_ref[pl.ds(H - (Kc - 1) + j, ts), :]
        o_ref[...] = acc

    def cur_spec(off):
        return pl.BlockSpec((ts, cb), lambda c, i: (i, c + off // cb))

    def prev_spec(off):
        return pl.BlockSpec((H, cb), lambda c, i: (jnp.maximum(i * rb - 1, 0), c + off // cb))

    in_specs, args = [cur_spec(a_off), prev_spec(a_off)], [u, u]
    if two:
        in_specs += [cur_spec(b_off), prev_spec(b_off)]
        args += [u, u]
    in_specs += [pl.BlockSpec((Kc, cb), lambda c, i: (0, c)), pl.BlockSpec((1, cb), lambda c, i: (0, c))]
    return pl.pallas_call(
        body, name=name, grid=(C // cb, S // ts), in_specs=in_specs,
        out_specs=pl.BlockSpec((ts, cb), lambda c, i: (i, c)), out_shape=jax.ShapeDtypeStruct((S, C), F32),
        scratch_shapes=[pltpu.VMEM((ts + H, cb), F32)], compiler_params=_cp(("parallel", "arbitrary")),
    )(*args, w, bias)


def conv_bwd(dy, u, w, *, mode, a_off, b_off, C, name):
    S = u.shape[0]
    Kc = w.shape[0]
    ts, cb, H = _tile(S, TILES["row"]), _tile(C, TILES["conv_c"]), CONV_HALO
    two = mode != "plain"
    rb, nt = ts // H, S // ts

    def body(*refs):
        if two:
            dy_ref, dyn_ref, a_ref, ap_ref, b_ref, bp_ref, w_ref, da_ref, db_ref, dw_ref, dbias_ref, x_ref, g_ref = refs
            a, b = a_ref[...], b_ref[...]
            cur, prev = _conv_in(mode, a, b), _conv_in(mode, ap_ref[...], bp_ref[...])
        else:
            dy_ref, dyn_ref, a_ref, ap_ref, w_ref, da_ref, dw_ref, dbias_ref, x_ref, g_ref = refs
            cur, prev = a_ref[...], ap_ref[...]
        i = pl.program_id(1)
        x_ref[pl.ds(0, H), :] = jnp.where(i == 0, 0.0, prev)
        x_ref[pl.ds(H, ts), :] = cur
        g = dy_ref[...]
        g_ref[pl.ds(0, ts), :] = g
        g_ref[pl.ds(ts, H), :] = jnp.where(i == nt - 1, 0.0, dyn_ref[...])

        @pl.when(i == 0)
        def _():
            dw_ref[...] = jnp.zeros((Kc, cb), F32)
            dbias_ref[...] = jnp.zeros((1, cb), F32)

        dbias_ref[...] += jnp.sum(g, axis=0, keepdims=True)
        dx = jnp.zeros((ts, cb), F32)
        for j in range(Kc):
            dx = dx + w_ref[pl.ds(j, 1), :] * g_ref[pl.ds(Kc - 1 - j, ts), :]
            dw_ref[pl.ds(j, 1), :] += jnp.sum(g * x_ref[pl.ds(H - (Kc - 1) + j, ts), :], axis=0, keepdims=True)
        if mode == "plain":
            da_ref[...] = dx.astype(BF16)
        elif mode == "mul":
            da_ref[...] = (dx * b).astype(BF16)
            db_ref[...] = (dx * a).astype(BF16)
        else:
            sg = _sig(b)
            da_ref[...] = (dx * sg).astype(BF16)
            db_ref[...] = (dx * a * sg * (1.0 - sg)).astype(BF16)

    def cur_spec(off):
        return pl.BlockSpec((ts, cb), lambda c, i: (i, c + off // cb))

    def prev_spec(off):
        return pl.BlockSpec((H, cb), lambda c, i: (jnp.maximum(i * rb - 1, 0), c + off // cb))

    out_blk = pl.BlockSpec((ts, cb), lambda c, i: (i, c))
    in_specs = [out_blk, pl.BlockSpec((H, cb), lambda c, i: (jnp.minimum((i + 1) * rb, S // H - 1), c)),
                cur_spec(a_off), prev_spec(a_off)]
    args = [dy, dy, u, u]
    if two:
        in_specs += [cur_spec(b_off), prev_spec(b_off)]
        args += [u, u]
    in_specs.append(pl.BlockSpec((Kc, cb), lambda c, i: (0, c)))
    n_d = 2 if two else 1
    return pl.pallas_call(
        body, name=name, grid=(C // cb, nt), in_specs=in_specs,
        out_specs=[out_blk] * n_d + [pl.BlockSpec((Kc, cb), lambda c, i: (0, c)), pl.BlockSpec((1, cb), lambda c, i: (0, c))],
        out_shape=[jax.ShapeDtypeStruct((S, C), BF16)] * n_d + [jax.ShapeDtypeStruct((Kc, C), F32), jax.ShapeDtypeStruct((1, C), F32)],
        scratch_shapes=[pltpu.VMEM((ts + H, cb), F32), pltpu.VMEM((ts + H, cb), F32)],
        compiler_params=_cp(("parallel", "arbitrary")),
    )(*args, w)


def _usec(ts, name):
    return pl.BlockSpec((ts, BW), lambda i, o=OFF[name] // BW: (i, o))


def _acc_rows(ref, part):
    @pl.when(pl.program_id(0) == 0)
    def _():
        ref[...] = part

    @pl.when(pl.program_id(0) > 0)
    def _():
        ref[...] += part


def post_fwd(u, y_ssd, cv_c, cv_d, nw, lnw, lnb, *, name):
    S = u.shape[0]
    ts = _tile(S, TILES["post"])

    def body(z_ref, scb_ref, gc_ref, gd_ref, ys_ref, cc_ref, cd_ref, nw_ref, lw_ref, lb_ref, yb_ref, yc_ref, yd_ref):
        t = ys_ref[...] * _silu(z_ref[...])
        r = lax.rsqrt(jnp.mean(t * t, axis=-1, keepdims=True) + EPS)
        yb_ref[...] = (t * r * nw_ref[...]).astype(BF16)
        yc_ref[...] = (scb_ref[...] * cc_ref[...] * _silu(gc_ref[...])).astype(BF16)
        cf = cd_ref[...]
        mu = jnp.mean(cf, axis=-1, keepdims=True)
        xc = cf - mu
        rl = lax.rsqrt(jnp.mean(xc * xc, axis=-1, keepdims=True) + EPS)
        yln = xc * rl * lw_ref[...] + lb_ref[...]
        yd_ref[...] = (_silu(yln) * _silu(gd_ref[...])).astype(BF16)

    row = pl.BlockSpec((ts, BW), lambda i: (i, 0))
    vec = pl.BlockSpec((1, BW), lambda i: (0, 0))
    return pl.pallas_call(
        body, name=name, grid=(S // ts,),
        in_specs=[_usec(ts, "z"), _usec(ts, "scb"), _usec(ts, "gc"), _usec(ts, "gd"), row, row, row, vec, vec, vec],
        out_specs=[row] * 3, out_shape=[jax.ShapeDtypeStruct((S, BW), BF16)] * 3, compiler_params=_cp(("parallel",)),
    )(u, u, u, u, y_ssd, cv_c, cv_d, nw, lnw, lnb)


def post_bwd(dy_a, dy_b, dy_c, dy_d, u, o, y_ssd, cv_c, cv_d, nw, lnw, lnb, *, name):
    S = u.shape[0]
    ts = _tile(S, TILES["post"])

    def body(dya_ref, dyb_ref, dyc_ref, dyd_ref, ga_ref, z_ref, scb_ref, gc_ref, gd_ref, o_ref, ys_ref, cc_ref, cd_ref,
             nw_ref, lw_ref, lb_ref,
             do_ref, dl_ref, dga_ref, dz_ref, dscb_ref, dgc_ref, dgd_ref, dys_ref, dcc_ref, dcd_ref, dnw_ref, dlw_ref, dlb_ref):
        ga, ov, dya = ga_ref[...], o_ref[...], dya_ref[...]
        dob = (dya * _silu(ga)).astype(BF16)
        do_ref[...] = dob
        dga_ref[...] = (dya * ov * _dsilu(ga)).astype(BF16)
        prod = dob.astype(F32) * ov
        lane = lax.broadcasted_iota(jnp.int32, (ts, LANES), 1)
        delta = jnp.zeros((ts, LANES), F32)
        for h in range(FOX_HEADS):
            col = jnp.sum(prod[:, h * FOX_HD:(h + 1) * FOX_HD], axis=1, keepdims=True)
            delta = jnp.where(lane == h, col, delta)
        dl_ref[...] = delta
        ys, z, dyb = ys_ref[...], z_ref[...], dyb_ref[...]
        sz = _silu(z)
        t = ys * sz
        r = lax.rsqrt(jnp.mean(t * t, axis=-1, keepdims=True) + EPS)
        th = t * r
        dth = dyb * nw_ref[...]
        dt_ = r * (dth - th * jnp.mean(dth * th, axis=-1, keepdims=True))
        dys_ref[...] = dt_ * sz
        dz_ref[...] = (dt_ * ys * _dsilu(z)).astype(BF16)
        _acc_rows(dnw_ref, jnp.sum(dyb * th, axis=0, keepdims=True))
        scb, cc, gc, dyc = scb_ref[...], cc_ref[...], gc_ref[...], dyc_ref[...]
        sg = _silu(gc)
        dscb_ref[...] = (dyc * cc * sg).astype(BF16)
        dcc_ref[...] = dyc * scb * sg
        dgc_ref[...] = (dyc * scb * cc * _dsilu(gc)).astype(BF16)
        cf, gd, dyd = cd_ref[...], gd_ref[...], dyd_ref[...]
        mu = jnp.mean(cf, axis=-1, keepdims=True)
        xc = cf - mu
        rl = lax.rsqrt(jnp.mean(xc * xc, axis=-1, keepdims=True) + EPS)
        xh = xc * rl
        yln = xh * lw_ref[...] + lb_ref[...]
        dyln = dyd * _silu(gd) * _dsilu(yln)
        dgd_ref[...] = (dyd * _silu(yln) * _dsilu(gd)).astype(BF16)
        _acc_rows(dlw_ref, jnp.sum(dyln * xh, axis=0, keepdims=True))
        _acc_rows(dlb_ref, jnp.sum(dyln, axis=0, keepdims=True))
        dxh = dyln * lw_ref[...]
        dcd_ref[...] = rl * (dxh - jnp.mean(dxh, axis=-1, keepdims=True) - xh * jnp.mean(dxh * xh, axis=-1, keepdims=True))

    row = pl.BlockSpec((ts, BW), lambda i: (i, 0))
    vec = pl.BlockSpec((1, BW), lambda i: (0, 0))
    sd = jax.ShapeDtypeStruct
    return pl.pallas_call(
        body, name=name, grid=(S // ts,),
        in_specs=[row] * 4 + [_usec(ts, n) for n in ("ga", "z", "scb", "gc", "gd")] + [row] * 4 + [vec] * 3,
        out_specs=[row, pl.BlockSpec((ts, LANES), lambda i: (i, 0))] + [row] * 8 + [vec] * 3,
        out_shape=[sd((S, BW), BF16), sd((S, LANES), F32)] + [sd((S, BW), BF16)] * 5 + [sd((S, BW), F32)] * 3 + [sd((1, BW), F32)] * 3,
        compiler_params=_cp(("arbitrary",)),
    )(dy_a, dy_b, dy_c, dy_d, u, u, u, u, u, o, y_ssd, cv_c, cv_d, nw, lnw, lnb)


def merge_fwd(h, ys, wg, bg, wb, *, name):
    S, D = h.shape
    tm, tn = _tile(S, 1024), _tile(D, 512)
    nb = D // tn

    def body(h_ref, y0, y1, y2, y3, wg_ref, bg_ref, wb_ref, m_ref, g_ref, p_ref, acc_ref):
        i = pl.program_id(2)
        g = jnp.dot(h_ref[...], wg_ref[...], preferred_element_type=F32) + bg_ref[...]
        gate = _sig(g)
        for b, y_ref in enumerate((y0, y1, y2, y3)):
            @pl.when(i == b)
            def _(y_ref=y_ref):
                p = jnp.dot(y_ref[...], wb_ref[...], preferred_element_type=F32)
                g_ref[...] = gate.astype(BF16)
                p_ref[...] = p.astype(BF16)
                if b == 0:
                    acc_ref[...] = gate * p
                else:
                    acc_ref[...] += gate * p

        @pl.when(i == N_BRANCH - 1)
        def _():
            m_ref[...] = acc_ref[...].astype(BF16)

    yspec = pl.BlockSpec((tm, BW), lambda m, n, i: (m, 0))
    return pl.pallas_call(
        body, name=name, grid=(S // tm, nb, N_BRANCH),
        in_specs=[pl.BlockSpec((tm, D), lambda m, n, i: (m, 0)), yspec, yspec, yspec, yspec,
                  pl.BlockSpec((None, D, tn), lambda m, n, i: (i, 0, n)),
                  pl.BlockSpec((None, 1, tn), lambda m, n, i: (i, 0, n)),
                  pl.BlockSpec((None, BW, tn), lambda m, n, i: (i, 0, n))],
        out_specs=[pl.BlockSpec((tm, tn), lambda m, n, i: (m, n)),
                   pl.BlockSpec((tm, tn), lambda m, n, i: (m, i * nb + n)),
                   pl.BlockSpec((tm, tn), lambda m, n, i: (m, i * nb + n))],
        out_shape=[jax.ShapeDtypeStruct((S, D), BF16), jax.ShapeDtypeStruct((S, N_BRANCH * D), BF16),
                   jax.ShapeDtypeStruct((S, N_BRANCH * D), BF16)],
        scratch_shapes=[pltpu.VMEM((tm, tn), F32)],
        compiler_params=_cp(("parallel", "parallel", "arbitrary")),
    )(h, *ys, wg, bg, wb)


def merge_bwd(dm, gates, proj, *, name):
    S, D = dm.shape
    ts, tn = _tile(S, TILES["row"]), _tile(D, 512)
    nb = D // tn

    def body(dm_ref, g_ref, p_ref, dp_ref, dg_ref, db_ref):
        d = dm_ref[...]
        g = g_ref[...].astype(F32)
        dp_ref[...] = (d * g).astype(BF16)
        dg = d * p_ref[...].astype(F32) * g * (1.0 - g)
        dg_ref[...] = dg.astype(BF16)
        part = jnp.sum(dg, axis=0, keepdims=True)

        @pl.when(pl.program_id(2) == 0)
        def _():
            db_ref[...] = part

        @pl.when(pl.program_id(2) > 0)
        def _():
            db_ref[...] += part

    wide = pl.BlockSpec((ts, tn), lambda b, n, i: (i, b * nb + n))
    return pl.pallas_call(
        body, name=name, grid=(N_BRANCH, nb, S // ts),
        in_specs=[pl.BlockSpec((ts, tn), lambda b, n, i: (i, n)), wide, wide],
        out_specs=[wide, wide, pl.BlockSpec((1, tn), lambda b, n, i: (0, b * nb + n))],
        out_shape=[jax.ShapeDtypeStruct((S, N_BRANCH * D), BF16)] * 2 + [jax.ShapeDtypeStruct((1, N_BRANCH * D), F32)],
        compiler_params=_cp(("parallel", "parallel", "arbitrary")),
    )(dm, gates, proj)


_NT = (((1,), (1,)), ((), ()))
_TN = (((0,), (0,)), ((), ()))


def _tri_pairs(n, by_key):
    if by_key:
        pairs = [(i, j) for j in range(n) for i in range(j, n)]
    else:
        pairs = [(i, j) for i in range(n) for j in range(i + 1)]
    return (jnp.array([p[0] for p in pairs], jnp.int32), jnp.array([p[1] for p in pairs], jnp.int32))


def attn_fwd(u, c_row, *, name):
    S = u.shape[0]
    T = _tile(S, TILES["att"])
    n = S // T
    qo, ko, vo, go = (OFF[k] // FOX_HD for k in ("q", "k", "v", "ga"))
    scale = FOX_HD ** -0.5
    it, jt = _tri_pairs(n, by_key=False)

    def body(it_ref, jt_ref, q_ref, k_ref, v_ref, ck_ref, ga_ref, o_ref, lse_ref, ya_ref, m_ref, l_ref, acc_ref):
        i, j = it_ref[pl.program_id(1)], jt_ref[pl.program_id(1)]

        @pl.when(j == 0)
        def _():
            m_ref[...] = jnp.full((T, 1), NEG, F32)
            l_ref[...] = jnp.zeros((T, 1), F32)
            acc_ref[...] = jnp.zeros((T, FOX_HD), F32)

        def step(masked):
            qb = (q_ref[...] * scale).astype(BF16)
            s = lax.dot_general(qb, k_ref[...].astype(BF16), _NT, preferred_element_type=F32) - ck_ref[...]
            if masked:
                row = lax.broadcasted_iota(jnp.int32, (T, T), 0)
                col = lax.broadcasted_iota(jnp.int32, (T, T), 1)
                s = jnp.where(col <= row, s, NEG)
            m_old = m_ref[...]
            m_new = jnp.maximum(m_old, jnp.max(s, axis=1, keepdims=True))
            alpha = jnp.exp(m_old - m_new)
            p = jnp.exp(s - m_new)
            l_ref[...] = alpha * l_ref[...] + jnp.sum(p, axis=1, keepdims=True)
            p_hi = p.astype(BF16)
            p_lo = (p - p_hi.astype(F32)).astype(BF16)
            vb = v_ref[...].astype(BF16)
            pv = jnp.dot(p_hi, vb, preferred_element_type=F32) + jnp.dot(p_lo, vb, preferred_element_type=F32)
            acc_ref[...] = alpha * acc_ref[...] + pv
            m_ref[...] = m_new

        @pl.when(j < i)
        def _():
            step(False)

        @pl.when(j == i)
        def _():
            step(True)
            o = acc_ref[...] / l_ref[...]
            o_ref[...] = o
            lse_ref[...] = m_ref[...] + jnp.log(l_ref[...])
            ya_ref[...] = (o * _silu(ga_ref[...])).astype(BF16)

    def qsec(off):
        return pl.BlockSpec((T, FOX_HD), lambda h, p, it, jt: (it[p], off + h))

    def ksec(off):
        return pl.BlockSpec((T, FOX_HD), lambda h, p, it, jt: (jt[p], off + h))

    out = pl.BlockSpec((T, FOX_HD), lambda h, p, it, jt: (it[p], h))
    colv = pl.BlockSpec((None, T, 1), lambda h, p, it, jt: (h, it[p], 0))
    return pl.pallas_call(
        body, name=name,
        grid_spec=pltpu.PrefetchScalarGridSpec(
            num_scalar_prefetch=2, grid=(FOX_HEADS, n * (n + 1) // 2),
            in_specs=[qsec(qo), ksec(ko), ksec(vo), pl.BlockSpec((None, 1, T), lambda h, p, it, jt: (h, 0, jt[p])), qsec(go)],
            out_specs=[out, colv, out],
            scratch_shapes=[pltpu.VMEM((T, 1), F32), pltpu.VMEM((T, 1), F32), pltpu.VMEM((T, FOX_HD), F32)]),
        out_shape=[jax.ShapeDtypeStruct((S, BW), F32), jax.ShapeDtypeStruct((FOX_HEADS, S, 1), F32),
                   jax.ShapeDtypeStruct((S, BW), BF16)],
        compiler_params=_cp(("parallel", "arbitrary")),
    )(it, jt, u, u, u, c_row, u)


def attn_bwd(u, do, c_col, lse_row, delta_row, *, name):
    S = u.shape[0]
    T = _tile(S, TILES["att"])
    n = S // T
    qo, ko, vo = (OFF[k] // FOX_HD for k in ("q", "k", "v"))
    scale = FOX_HD ** -0.5
    it, jt = _tri_pairs(n, by_key=True)

    def body(it_ref, jt_ref, q_ref, k_ref, v_ref, do_ref, ck_ref, lse_ref, dl_ref, dq_ref, dk_ref, dv_ref, dc_ref,
             dka_ref, dva_ref, dca_ref):
        i, j = it_ref[pl.program_id(1)], jt_ref[pl.program_id(1)]

        @pl.when(pl.program_id(1) == 0)
        def _():
            dq_ref[...] = jnp.zeros((S, FOX_HD), F32)

        @pl.when(i == j)
        def _():
            dka_ref[...] = jnp.zeros((T, FOX_HD), F32)
            dva_ref[...] = jnp.zeros((T, FOX_HD), F32)
            dca_ref[...] = jnp.zeros((T, 1), F32)

        def step(masked):
            qb, dob = (q_ref[...] * scale).astype(BF16), do_ref[...]
            kf = k_ref[...]
            st = lax.dot_general(kf.astype(BF16), qb, _NT, preferred_element_type=F32) - ck_ref[...]
            pt = jnp.exp(st - lse_ref[...])
            if masked:
                kpos = lax.broadcasted_iota(jnp.int32, (T, T), 0)
                qpos = lax.broadcasted_iota(jnp.int32, (T, T), 1)
                pt = jnp.where(kpos <= qpos, pt, 0.0)
            dva_ref[...] += jnp.dot(pt.astype(BF16), dob, preferred_element_type=F32)
            dpt = lax.dot_general(v_ref[...].astype(BF16), dob, _NT, preferred_element_type=F32)
            dst = pt * (dpt - dl_ref[...])
            dca_ref[...] -= jnp.sum(dst, axis=1, keepdims=True)
            dsb = dst.astype(BF16)
            dka_ref[...] += jnp.dot(dsb, qb, preferred_element_type=F32)
            rows = pl.ds(pl.multiple_of(i * T, T), T)
            dq_ref[rows, :] += lax.dot_general(dsb, (kf * scale).astype(BF16), _TN, preferred_element_type=F32)

        @pl.when(i > j)
        def _():
            step(False)

        @pl.when(i == j)
        def _():
            step(True)

        @pl.when(i == n - 1)
        def _():
            dk_ref[...] = dka_ref[...].astype(BF16)
            dv_ref[...] = dva_ref[...].astype(BF16)
            dc_ref[...] = dca_ref[...]

    def qsec(off):
        return pl.BlockSpec((T, FOX_HD), lambda h, p, it, jt: (it[p], off + h))

    def ksec(off):
        return pl.BlockSpec((T, FOX_HD), lambda h, p, it, jt: (jt[p], off + h))

    qrow = pl.BlockSpec((None, 1, T), lambda h, p, it, jt: (h, 0, it[p]))
    kout = pl.BlockSpec((T, FOX_HD), lambda h, p, it, jt: (jt[p], h))
    kcol = pl.BlockSpec((None, T, 1), lambda h, p, it, jt: (h, jt[p], 0))
    return pl.pallas_call(
        body, name=name,
        grid_spec=pltpu.PrefetchScalarGridSpec(
            num_scalar_prefetch=2, grid=(FOX_HEADS, n * (n + 1) // 2),
            in_specs=[qsec(qo), ksec(ko), ksec(vo), pl.BlockSpec((T, FOX_HD), lambda h, p, it, jt: (it[p], h)),
                      kcol, qrow, qrow],
            out_specs=[pl.BlockSpec((S, FOX_HD), lambda h, p, it, jt: (0, h)), kout, kout, kcol],
            scratch_shapes=[pltpu.VMEM((T, FOX_HD), F32), pltpu.VMEM((T, FOX_HD), F32), pltpu.VMEM((T, 1), F32)]),
        out_shape=[jax.ShapeDtypeStruct((S, BW), F32), jax.ShapeDtypeStruct((S, BW), BF16),
                   jax.ShapeDtypeStruct((S, BW), BF16), jax.ShapeDtypeStruct((FOX_HEADS, S, 1), F32)],
        compiler_params=_cp(("parallel", "arbitrary")),
    )(it, jt, u, u, u, do, c_col, lse_row, delta_row)


N_PAIR = SSM_HEADS // 2
PAIRS_PER_GROUP = N_PAIR // SSM_G


def _sel_t():
    r = lax.broadcasted_iota(jnp.int32, (LANES, BW), 0)
    c = lax.broadcasted_iota(jnp.int32, (LANES, BW), 1)
    return (lax.shift_right_logical(c, 6) == r).astype(BF16)


def _sel():
    r = lax.broadcasted_iota(jnp.int32, (BW, LANES), 0)
    c = lax.broadcasted_iota(jnp.int32, (BW, LANES), 1)
    return (lax.shift_right_logical(r, 6) == c).astype(BF16)


def _dot3(x, m):
    hi = x.astype(BF16)
    r1 = x - hi.astype(F32)
    mid = r1.astype(BF16)
    lo = (r1 - mid.astype(F32)).astype(BF16)
    d = functools.partial(jnp.dot, preferred_element_type=F32)
    return d(hi, m) + d(mid, m) + d(lo, m)


def _ssd_common(x_ref, sp_ref, al_ref, buf_ref, big_ref, cst_ref, LC):
    pre = x_ref[...]
    sg = _sig(pre)
    act = pre * sg
    dt = sp_ref[...]
    a = -jnp.exp(al_ref[...])
    cs = _shift_scan(dt * a, buf_ref, LC)
    sel_t = _sel_t()
    dtl = _dot3(dt, sel_t)
    csl = _dot3(cs, sel_t)
    big_ref[...] = csl
    csl_last = big_ref[pl.ds(LC - 1, 1), :]
    cst_ref[...] = cs.T
    return pre, sg, act, dt, a, cs, dtl, csl, csl_last


def ssd_fwd(xbc, sp, alog, dskip_l, *, name):
    S = xbc.shape[0]
    LC = _tile(S, TILES["ssd"])
    nc = S // LC

    def body(x_ref, sp_ref, al_ref, dk_ref, y_ref, hs_ref, st_ref, buf_ref, big_ref, cst_ref):
        @pl.when(pl.program_id(0) == 0)
        def _():
            st_ref[...] = jnp.zeros((N_PAIR, SSM_N, LANES), F32)

        pre, sg, act, dt, a, cs, dtl, csl, csl_last = _ssd_common(x_ref, sp_ref, al_ref, buf_ref, big_ref, cst_ref, LC)
        xs, bm, cm = act[:, :BW], act[:, BW:BW + SSM_G * SSM_N], act[:, BW + SSM_G * SSM_N:]
        e_all = jnp.exp(csl)
        dec = jnp.exp(csl_last - csl)
        ad = jnp.exp(csl_last)
        xd = xs * dtl
        tril = lax.broadcasted_iota(jnp.int32, (LC, LC), 0) >= lax.broadcasted_iota(jnp.int32, (LC, LC), 1)
        lane = lax.broadcasted_iota(jnp.int32, (LC, LANES), 1)
        for g in range(SSM_G):
            bgt = bm[:, g * SSM_N:(g + 1) * SSM_N].T.astype(BF16)
            cgb = cm[:, g * SSM_N:(g + 1) * SSM_N].astype(BF16)
            cb = jnp.dot(cgb, bgt, preferred_element_type=F32)
            for q in range(PAIRS_PER_GROUP):
                pp = g * PAIRS_PER_GROUP + q
                ln = slice(pp * LANES, (pp + 1) * LANES)
                xp = xd[:, ln]
                xpb = xp.astype(BF16)
                yh = []
                for hh in range(2):
                    row_b = jnp.broadcast_to(cst_ref[pl.ds(2 * pp + hh, 1), :], (LC, LC))
                    lmat = jnp.exp(jnp.where(tril, row_b.T - row_b, NEG))
                    yh.append(jnp.dot((cb * lmat).astype(BF16), xpb, preferred_element_type=F32))
                hin = st_ref[pp]
                hs_ref[pp] = hin
                yoff = jnp.dot(cgb, hin.astype(BF16), preferred_element_type=F32) * e_all[:, ln]
                y_ref[:, ln] = jnp.where(lane < SSM_P, yh[0], yh[1]) + yoff + xs[:, ln] * dk_ref[:, ln]
                st_ref[pp] = hin * ad[:, ln] + jnp.dot(bgt, (xp * dec[:, ln]).astype(BF16), preferred_element_type=F32)

    return pl.pallas_call(
        body, name=name, grid=(nc,),
        in_specs=[pl.BlockSpec((LC, SSM_CONV_DIM), lambda c: (c, 0)), pl.BlockSpec((LC, LANES), lambda c: (c, 0)),
                  pl.BlockSpec((1, LANES), lambda c: (0, 0)), pl.BlockSpec((1, BW), lambda c: (0, 0))],
        out_specs=[pl.BlockSpec((LC, BW), lambda c: (c, 0)), pl.BlockSpec((None, N_PAIR, SSM_N, LANES), lambda c: (c, 0, 0, 0))],
        out_shape=[jax.ShapeDtypeStruct((S, BW), F32), jax.ShapeDtypeStruct((nc, N_PAIR, SSM_N, LANES), F32)],
        scratch_shapes=[pltpu.VMEM((N_PAIR, SSM_N, LANES), F32), pltpu.VMEM((3 * LC, LANES), F32),
                        pltpu.VMEM((LC, BW), F32), pltpu.VMEM((LANES, LC), F32)],
        compiler_params=_cp(("arbitrary",)),
    )(xbc, sp, alog, dskip_l)


def ssd_bwd(dy, xbc, sp, alog, dskip_l, hs, *, name):
    S = xbc.shape[0]
    LC = _tile(S, TILES["ssd"])
    nc = S // LC
    GN = SSM_G * SSM_N

    def body(dy_ref, x_ref, sp_ref, al_ref, dk_ref, hs_ref, dx_ref, ddt_ref, da_ref, dd_ref,
             dh_ref, buf_ref, big_ref, cst_ref, gcs_ref, dxd_ref):
        @pl.when(pl.program_id(0) == 0)
        def _():
            dh_ref[...] = jnp.zeros((N_PAIR, SSM_N, LANES), F32)
            da_ref[...] = jnp.zeros((1, LANES), F32)
            dd_ref[...] = jnp.zeros((1, BW), F32)

        pre, sg, act, dt, a, cs, dtl, csl, csl_last = _ssd_common(x_ref, sp_ref, al_ref, buf_ref, big_ref, cst_ref, LC)
        dact = sg * (1.0 + pre * (1.0 - sg))
        xs, bm, cm = act[:, :BW], act[:, BW:BW + GN], act[:, BW + GN:]
        e_all = jnp.exp(csl)
        dec = jnp.exp(csl_last - csl)
        ad = jnp.exp(csl_last)
        xd = xs * dtl
        d_y = dy_ref[...]
        ri = lax.broadcasted_iota(jnp.int32, (LC, LC), 0)
        ci = lax.broadcasted_iota(jnp.int32, (LC, LC), 1)
        tril, triu = ri >= ci, ci >= ri
        lane = lax.broadcasted_iota(jnp.int32, (LC, LANES), 1)
        rowi = lax.broadcasted_iota(jnp.int32, (LC, LANES), 0)
        dot = functools.partial(jnp.dot, preferred_element_type=F32)
        dot_nt = functools.partial(lax.dot_general, dimension_numbers=_NT, preferred_element_type=F32)
        for g in range(SSM_G):
            gs = slice(g * SSM_N, (g + 1) * SSM_N)
            bg, cg = bm[:, gs], cm[:, gs]
            bgb, cgb = bg.astype(BF16), cg.astype(BF16)
            bgt, cgt = bg.T.astype(BF16), cg.T.astype(BF16)
            cb, cbt = dot(cgb, bgt), dot(bgb, cgt)
            dcb = jnp.zeros((LC, LC), F32)
            dcbt = jnp.zeros((LC, LC), F32)
            dcg = jnp.zeros((LC, SSM_N), F32)
            dbg = jnp.zeros((LC, SSM_N), F32)
            for q in range(PAIRS_PER_GROUP):
                pp = g * PAIRS_PER_GROUP + q
                ln = slice(pp * LANES, (pp + 1) * LANES)
                xp, dyp, ep, decp, adp = xd[:, ln], d_y[:, ln], e_all[:, ln], dec[:, ln], ad[:, ln]
                xpb, dypb = xp.astype(BF16), dyp.astype(BF16)
                hin, dho = hs_ref[pp], dh_ref[pp]
                hb, dhob = hin.astype(BF16), dho.astype(BF16)
                yoff = dot(cgb, hb) * ep
                dgb = (dyp * ep).astype(BF16)
                dh_ref[pp] = dho * adp + dot(cgt, dgb)
                dcg = dcg + dot_nt(dgb, hb)
                zf = xp * decp
                d_z = dot(bgb, dhob)
                dbg = dbg + dot_nt(zf.astype(BF16), dhob)
                dzz = d_z * zf
                last = jnp.sum(dzz, axis=0, keepdims=True) + jnp.sum(dho * hin, axis=0, keepdims=True) * adp
                gcs = dyp * yoff - dzz + jnp.where(rowi == LC - 1, last, 0.0)
                dxd = d_z * decp
                for hh in range(2):
                    row_b = jnp.broadcast_to(cst_ref[pl.ds(2 * pp + hh, 1), :], (LC, LC))
                    col_b = row_b.T
                    lmat = jnp.exp(jnp.where(tril, col_b - row_b, NEG))
                    lmat_t = jnp.exp(jnp.where(triu, row_b - col_b, NEG))
                    hm = (lane < SSM_P) if hh == 0 else (lane >= SSM_P)
                    dml = dot_nt(jnp.where(hm, dyp, 0.0).astype(BF16), xpb) * lmat
                    dmtl = dot_nt(jnp.where(hm, xp, 0.0).astype(BF16), dypb) * lmat_t
                    dcb = dcb + dml
                    dcbt = dcbt + dmtl
                    contrib = jnp.sum(dml * cb, axis=1, keepdims=True) - jnp.sum(dmtl * cbt, axis=1, keepdims=True)
                    gcs = gcs + jnp.where(lane == hh * SSM_P, contrib, 0.0)
                    dxd = dxd + jnp.where(hm, dot((cbt * lmat_t).astype(BF16), dypb), 0.0)
                gcs_ref[:, ln] = gcs
                dxd_ref[:, ln] = dxd
            dcg = dcg + dot(dcb.astype(BF16), bgb)
            dbg = dbg + dot(dcbt.astype(BF16), cgb)
            dx_ref[:, BW + g * SSM_N:BW + (g + 1) * SSM_N] = dbg * dact[:, BW + g * SSM_N:BW + (g + 1) * SSM_N]
            dx_ref[:, BW + GN + g * SSM_N:BW + GN + (g + 1) * SSM_N] = dcg * dact[:, BW + GN + g * SSM_N:BW + GN + (g + 1) * SSM_N]
        d_xd = dxd_ref[...]
        dx_ref[:, :BW] = (d_y * dk_ref[...] + d_xd * dtl) * dact[:, :BW]
        sel = _sel()
        dda = _shift_scan(_dot3(gcs_ref[...], sel), buf_ref, LC, reverse=True)
        ddt_ref[...] = _dot3(d_xd * xs, sel) + dda * a
        da_ref[...] += jnp.sum(dda * dt, axis=0, keepdims=True)
        dd_ref[...] += jnp.sum(d_y * xs, axis=0, keepdims=True)

    rev = lambda c: (nc - 1 - c, 0)
    return pl.pallas_call(
        body, name=name, grid=(nc,),
        in_specs=[pl.BlockSpec((LC, BW), rev), pl.BlockSpec((LC, SSM_CONV_DIM), rev), pl.BlockSpec((LC, LANES), rev),
                  pl.BlockSpec((1, LANES), lambda c: (0, 0)), pl.BlockSpec((1, BW), lambda c: (0, 0)),
                  pl.BlockSpec((None, N_PAIR, SSM_N, LANES), lambda c: (nc - 1 - c, 0, 0, 0))],
        out_specs=[pl.BlockSpec((LC, SSM_CONV_DIM), rev), pl.BlockSpec((LC, LANES), rev),
                   pl.BlockSpec((1, LANES), lambda c: (0, 0)), pl.BlockSpec((1, BW), lambda c: (0, 0))],
        out_shape=[jax.ShapeDtypeStruct((S, SSM_CONV_DIM), F32), jax.ShapeDtypeStruct((S, LANES), F32),
                   jax.ShapeDtypeStruct((1, LANES), F32), jax.ShapeDtypeStruct((1, BW), F32)],
        scratch_shapes=[pltpu.VMEM((N_PAIR, SSM_N, LANES), F32), pltpu.VMEM((3 * LC, LANES), F32),
                        pltpu.VMEM((LC, BW), F32), pltpu.VMEM((LANES, LC), F32),
                        pltpu.VMEM((LC, BW), F32), pltpu.VMEM((LC, BW), F32)],
        compiler_params=_cp(("arbitrary",)),
    )(dy, xbc, sp, alog, dskip_l, hs)


def addn(arrs, *, out_dtype, name):
    R, C = arrs[0].shape
    tr = _tile(R, max(8, (1 << 20) // C))

    def body(*refs):
        acc = refs[0][...].astype(F32)
        for r in refs[1:-1]:
            acc = acc + r[...].astype(F32)
        refs[-1][...] = acc.astype(out_dtype)

    blk = pl.BlockSpec((tr, C), lambda i: (i, 0))
    return pl.pallas_call(
        body, name=name, grid=(R // tr,), in_specs=[blk] * len(arrs), out_specs=blk,
        out_shape=jax.ShapeDtypeStruct((R, C), out_dtype), compiler_params=_cp(("parallel",)),
    )(*arrs)


def adamw(w, g, m, v, *, name):
    R, C = w.shape
    tr = _tile(R, max(8, (1 << 18) // C))
    c1, c2 = 1.0 / (1.0 - ADAM_B1 ** ADAM_STEP), 1.0 / (1.0 - ADAM_B2 ** ADAM_STEP)

    def body(w_ref, g_ref, m_ref, v_ref, d_ref, nm_ref, nv_ref):
        gv = g_ref[...]
        nm = ADAM_B1 * m_ref[...] + (1.0 - ADAM_B1) * gv
        nv = ADAM_B2 * v_ref[...] + (1.0 - ADAM_B2) * (gv * gv)
        nm_ref[...] = nm
        nv_ref[...] = nv
        d_ref[...] = -ADAM_LR * ((nm * c1) / (jnp.sqrt(nv * c2) + ADAM_EPS) + ADAM_WD * w_ref[...])

    blk = pl.BlockSpec((tr, C), lambda i: (i, 0))
    return pl.pallas_call(
        body, name=name, grid=(R // tr,), in_specs=[blk] * 4, out_specs=[blk] * 3,
        out_shape=[jax.ShapeDtypeStruct((R, C), F32)] * 3, compiler_params=_cp(("parallel",)),
    )(w, g, m, v)


_ANY = pl.BlockSpec(memory_space=pl.ANY)


def _place():
    return lax.axis_index("x"), lax.axis_index("y"), lax.axis_index("c")


def _rcopy(src, dst, sems_s, sems_r, k, to):
    return pltpu.make_async_remote_copy(src_ref=src, dst_ref=dst, send_sem=sems_s.at[k], recv_sem=sems_r.at[k],
                                        device_id=to, device_id_type=MESH)


def gather_chips(buf, *, name):
    _, _, R, C = buf.shape

    def body(in_ref, out_ref, ss, rs):
        del in_ref
        x, y, c = _place()
        chips = [(1 - x, y), (x, 1 - y), (1 - x, 1 - y)]
        me = 2 * x + y
        first = [_rcopy(out_ref.at[me, c], out_ref.at[me, c], ss, rs, j, (cx, cy, c)) for j, (cx, cy) in enumerate(chips)]
        for cp in first:
            cp.start()
        passed = []
        for j, (cx, cy) in enumerate(chips):
            blk = out_ref.at[2 * cx + cy, c]
            _rcopy(blk, blk, ss, rs, j, (x, y, c)).wait_recv()
            cp = _rcopy(blk, blk, ss, rs, 3 + j, (x, y, 1 - c))
            cp.start()
            passed.append(cp)
        for j, (cx, cy) in enumerate(chips):
            blk = out_ref.at[2 * cx + cy, 1 - c]
            _rcopy(blk, blk, ss, rs, 3 + j, (x, y, c)).wait_recv()
        for cp in first + passed:
            cp.wait_send()

    return pl.pallas_call(
        body, name=name, in_specs=[_ANY], out_specs=_ANY, out_shape=jax.ShapeDtypeStruct(buf.shape, buf.dtype),
        input_output_aliases={0: 0},
        scratch_shapes=[pltpu.SemaphoreType.DMA((6,)), pltpu.SemaphoreType.DMA((6,))],
    )(buf)


def gather_all(block, *, name):
    R, C = block.shape
    flips = [(fx, fy, fc) for fx in (0, 1) for fy in (0, 1) for fc in (0, 1)][1:]

    def body(in_ref, out_ref, ss, rs, ls):
        x, y, c = _place()
        me = 4 * x + 2 * y + c
        mine = pltpu.make_async_copy(in_ref, out_ref.at[me], ls)
        mine.start()
        sends = []
        for k, (fx, fy, fc) in enumerate(flips):
            px, py, pc = x ^ fx, y ^ fy, c ^ fc
            cp = _rcopy(in_ref, out_ref.at[me], ss, rs, k, (px, py, pc))
            cp.start()
            sends.append(cp)
        for k, (fx, fy, fc) in enumerate(flips):
            blk = out_ref.at[4 * (x ^ fx) + 2 * (y ^ fy) + (c ^ fc)]
            _rcopy(blk, blk, ss, rs, k, (x, y, c)).wait_recv()
        for cp in sends:
            cp.wait_send()
        mine.wait()

    return pl.pallas_call(
        body, name=name, in_specs=[_ANY], out_specs=_ANY, out_shape=jax.ShapeDtypeStruct((8, R, C), block.dtype),
        scratch_shapes=[pltpu.SemaphoreType.DMA((7,)), pltpu.SemaphoreType.DMA((7,)), pltpu.SemaphoreType.DMA],
    )(block)


def swap_partials(g_all, *, name):
    N, _, R, C = g_all.shape

    def body(in_ref, out_ref, ss, rs):
        x, y, c = _place()
        cps = [_rcopy(in_ref.at[k, 1 - c], out_ref.at[k], ss, rs, k, (x, y, 1 - c)) for k in range(N)]
        for cp in cps:
            cp.start()
        for cp in cps:
            cp.wait_recv()
        for cp in cps:
            cp.wait_send()

    return pl.pallas_call(
        body, name=name, in_specs=[_ANY], out_specs=_ANY, out_shape=jax.ShapeDtypeStruct((N, R, C), g_all.dtype),
        scratch_shapes=[pltpu.SemaphoreType.DMA((N,)), pltpu.SemaphoreType.DMA((N,))],
    )(g_all)


def share_halves(buf, *, name):
    def body(in_ref, out_ref, ss, rs):
        del in_ref
        x, y, c = _place()
        cp = _rcopy(out_ref.at[c], out_ref.at[c], ss, rs, 0, (x, y, 1 - c))
        cp.start()
        _rcopy(out_ref.at[1 - c], out_ref.at[1 - c], ss, rs, 0, (x, y, c)).wait_recv()
        cp.wait_send()

    return pl.pallas_call(
        body, name=name, in_specs=[_ANY], out_specs=_ANY, out_shape=jax.ShapeDtypeStruct(buf.shape, buf.dtype),
        input_output_aliases={0: 0},
        scratch_shapes=[pltpu.SemaphoreType.DMA((1,)), pltpu.SemaphoreType.DMA((1,))],
    )(buf)


def add_sibling(g_all, sib, c_arr, *, name):
    N, _, R, C = g_all.shape
    tr = _tile(R, max(8, (1 << 20) // C))

    def body(c_ref, a_ref, b_ref, o_ref):
        o_ref[...] = (a_ref[...].astype(F32) + b_ref[...].astype(F32)).astype(BF16)

    blk = pl.BlockSpec((None, tr, C), lambda k, r, c_ref: (k, r, 0))
    return pl.pallas_call(
        body, name=name,
        grid_spec=pltpu.PrefetchScalarGridSpec(
            num_scalar_prefetch=1, grid=(N, R // tr),
            in_specs=[pl.BlockSpec((None, None, tr, C), lambda k, r, c_ref: (k, c_ref[0], r, 0)), blk], out_specs=blk),
        out_shape=jax.ShapeDtypeStruct((N, R, C), BF16), compiler_params=_cp(("parallel", "parallel")),
    )(c_arr, g_all, sib)


def add_chips(part, rcv, place_arr, *, name):
    _, R, C = part.shape
    tr = _tile(R, max(8, (1 << 20) // C))

    def body(place_ref, p_ref, r0, r1, r2, o_ref):
        acc = p_ref[...].astype(F32)
        for r in (r0, r1, r2):
            acc = acc + r[...].astype(F32)
        o_ref[...] = acc

    def slot(j):
        return pl.BlockSpec((None, tr, C), lambda r, place_ref, j=j: (j, r, 0))

    return pl.pallas_call(
        body, name=name,
        grid_spec=pltpu.PrefetchScalarGridSpec(
            num_scalar_prefetch=1, grid=(R // tr,),
            in_specs=[pl.BlockSpec((None, tr, C), lambda r, place_ref: (place_ref[0], r, 0)), slot(0), slot(1), slot(2)],
            out_specs=pl.BlockSpec((None, tr, C), lambda r, place_ref: (place_ref[1], r, 0))),
        out_shape=jax.ShapeDtypeStruct((2, R, C), F32), compiler_params=_cp(("parallel",)),
    )(place_arr, part, rcv, rcv, rcv)


def scatter_chips(parts, *, name):
    _, R, C = parts.shape

    def body(in_ref, out_ref, ss, rs):
        x, y, c = _place()
        chips = [(1 - x, y), (x, 1 - y), (1 - x, 1 - y)]
        cps = [_rcopy(in_ref.at[2 * cx + cy], out_ref.at[j], ss, rs, j, (cx, cy, c)) for j, (cx, cy) in enumerate(chips)]
        for cp in cps:
            cp.start()
        for cp in cps:
            cp.wait_recv()
        for cp in cps:
            cp.wait_send()

    return pl.pallas_call(
        body, name=name, in_specs=[_ANY], out_specs=_ANY, out_shape=jax.ShapeDtypeStruct((3, R, C), parts.dtype),
        scratch_shapes=[pltpu.SemaphoreType.DMA((3,)), pltpu.SemaphoreType.DMA((3,))],
    )(parts)


WEIGHTS = ("norm_w", "w_in", "fg_bias", "ssm_conv_w", "ssm_conv_b", "dt_bias", "a_log", "d_skip", "ssm_norm_w", "sc_conv_w",
           "sc_conv_b", "cf_conv_w", "cf_conv_b", "cf_ln_w", "cf_ln_b", "w_gate", "b_gate", "w_branch", "w_out", "final_norm_w")
BIG = ("w_in", "w_gate", "w_branch", "w_out")
SMALL = tuple(n for n in WEIGHTS if n not in BIG)
SMALL_SHARDED = ("ssm_conv_w", "sc_conv_w", "cf_conv_w", "b_gate")
N_CHIP = 4
PACK_C = 1024
PACK_ROWS = 1024


def _pack(arrs, cols, row_mult):
    return _pack_groups([arrs], cols, row_mult)[0]


def _pack_groups(groups, cols, row_mult):
    n = sum(math.prod(a.shape) for a in groups[0])
    rows = -(-n // cols)
    rows = -(-rows // row_mult) * row_mult
    parts = []
    for arrs in groups:
        parts += [a.reshape(-1) for a in arrs]
        if rows * cols > n:
            parts.append(jnp.zeros((rows * cols - n,), arrs[0].dtype))
    return jnp.concatenate(parts).reshape(len(groups), rows, cols)


def _unpack(packed, shapes):
    flat = packed.reshape(-1)
    out, o = [], 0
    for s in shapes:
        n = math.prod(s)
        out.append(flat[o:o + n].reshape(s))
        o += n
    return out


def _orig_cols():
    cols = []
    for n, s in zip(ORIG_NAMES, ORIG_SIZES):
        if n == "dt":
            cols.append((OFF["small"] + DT_LANE, s))
        elif n == "f":
            cols.append((OFF["small"] + F_LANE, s))
        else:
            cols.append((OFF[n], s))
    return cols


def _rows_to_padded(per_chip):
    q_in = N_IN // N_CHIP
    orig_off = dict(zip(ORIG_NAMES, [sum(ORIG_SIZES[:i]) for i in range(len(ORIG_SIZES))]))
    size = dict(zip(ORIG_NAMES, ORIG_SIZES))
    parts = []
    for n in PAD_ORDER + ("dt", "f"):
        lo, hi = orig_off[n], orig_off[n] + size[n]
        for k in range(N_CHIP):
            a, b = max(lo, k * q_in), min(hi, (k + 1) * q_in)
            if a < b:
                parts.append(per_chip[k][a - k * q_in:b - k * q_in])
    parts.append(jnp.zeros((SMALL_W - size["dt"] - size["f"], per_chip[0].shape[1]), per_chip[0].dtype))
    return jnp.concatenate(parts, axis=0)


def _rows_from_padded(w_pad, lo, hi):
    parts, o = [], 0
    for start, s in _orig_cols():
        a, b = max(lo, o), min(hi, o + s)
        if a < b:
            parts.append(w_pad[start + a - o:start + b - o])
        o += s
    return jnp.concatenate(parts, axis=0)


def _lanes_row(parts, width=LANES):
    v = jnp.concatenate([p.reshape(-1) for p in parts])
    return jnp.pad(v, (0, width - v.shape[0])).reshape(1, width)


def kernel(x, norm_w, w_in, fg_bias, ssm_conv_w, ssm_conv_b, dt_bias, a_log, d_skip, ssm_norm_w, sc_conv_w, sc_conv_b, cf_conv_w, cf_conv_b, cf_ln_w, cf_ln_b, w_gate, b_gate, w_branch, w_out, final_norm_w, loss_target, m_norm_w, m_w_in, m_fg_bias, m_ssm_conv_w, m_ssm_conv_b, m_dt_bias, m_a_log, m_d_skip, m_ssm_norm_w, m_sc_conv_w, m_sc_conv_b, m_cf_conv_w, m_cf_conv_b, m_cf_ln_w, m_cf_ln_b, m_w_gate, m_b_gate, m_w_branch, m_w_out, m_final_norm_w, v_norm_w, v_w_in, v_fg_bias, v_ssm_conv_w, v_ssm_conv_b, v_dt_bias, v_a_log, v_d_skip, v_ssm_norm_w, v_sc_conv_w, v_sc_conv_b, v_cf_conv_w, v_cf_conv_b, v_cf_ln_w, v_cf_ln_b, v_w_gate, v_b_gate, v_w_branch, v_w_out, v_final_norm_w):
    wts = dict(norm_w=norm_w, w_in=w_in, fg_bias=fg_bias, ssm_conv_w=ssm_conv_w, ssm_conv_b=ssm_conv_b, dt_bias=dt_bias,
               a_log=a_log, d_skip=d_skip, ssm_norm_w=ssm_norm_w, sc_conv_w=sc_conv_w, sc_conv_b=sc_conv_b,
               cf_conv_w=cf_conv_w, cf_conv_b=cf_conv_b, cf_ln_w=cf_ln_w, cf_ln_b=cf_ln_b, w_gate=w_gate, b_gate=b_gate,
               w_branch=w_branch, w_out=w_out, final_norm_w=final_norm_w)
    mom = dict(norm_w=m_norm_w, w_in=m_w_in, fg_bias=m_fg_bias, ssm_conv_w=m_ssm_conv_w, ssm_conv_b=m_ssm_conv_b,
               dt_bias=m_dt_bias, a_log=m_a_log, d_skip=m_d_skip, ssm_norm_w=m_ssm_norm_w, sc_conv_w=m_sc_conv_w,
               sc_conv_b=m_sc_conv_b, cf_conv_w=m_cf_conv_w, cf_conv_b=m_cf_conv_b, cf_ln_w=m_cf_ln_w, cf_ln_b=m_cf_ln_b,
               w_gate=m_w_gate, b_gate=m_b_gate, w_branch=m_w_branch, w_out=m_w_out, final_norm_w=m_final_norm_w)
    vel = dict(norm_w=v_norm_w, w_in=v_w_in, fg_bias=v_fg_bias, ssm_conv_w=v_ssm_conv_w, ssm_conv_b=v_ssm_conv_b,
               dt_bias=v_dt_bias, a_log=v_a_log, d_skip=v_d_skip, ssm_norm_w=v_ssm_norm_w, sc_conv_w=v_sc_conv_w,
               sc_conv_b=v_sc_conv_b, cf_conv_w=v_cf_conv_w, cf_conv_b=v_cf_conv_b, cf_ln_w=v_cf_ln_w, cf_ln_b=v_cf_ln_b,
               w_gate=v_w_gate, b_gate=v_b_gate, w_branch=v_w_branch, w_out=v_w_out, final_norm_w=v_final_norm_w)
    L = norm_w.shape[0]
    S, D = x.shape[1], x.shape[2]
    assert D == D_MODEL and x.shape[0] == 1
    xi, yi, ci = _place()
    chip = 2 * xi + yi

    sh_shapes = [wts[n].shape for n in SMALL_SHARDED]
    got = gather_all(_pack([wts[n] for n in SMALL_SHARDED], LANES, 8), name="gather_small_w")
    per_chip = [_unpack(got[2 * k], sh_shapes) for k in range(N_CHIP)]
    full_small = {n: jnp.concatenate([per_chip[k][i] for k in range(N_CHIP)], axis=-1) for i, n in enumerate(SMALL_SHARDED)}

    q_in, q_d = N_IN // N_CHIP, D // N_CHIP
    sent_shapes = [(q_in, D), (N_BRANCH, q_d, D), (N_BRANCH, BW, q_d), (q_d, D)]
    lw = []
    for l in range(L):
        packed = _pack([w_in[l].T.astype(BF16), w_gate[l].astype(BF16), w_branch[l].astype(BF16), w_out[l].astype(BF16)],
                       PACK_C, PACK_ROWS)
        half_rows = packed.shape[0] // 2
        buf = lax.dynamic_update_slice(lax.empty((N_CHIP, 2, half_rows, PACK_C), BF16),
                                       packed.reshape(1, 2, half_rows, PACK_C), (chip, 0, 0, 0))
        got = gather_chips(buf, name="gather_w")
        pc = [_unpack(got[k], sent_shapes) for k in range(N_CHIP)]
        lw.append(dict(wpt=_rows_to_padded([pc[k][0] for k in range(N_CHIP)]),
                       wg=jnp.concatenate([pc[k][1] for k in range(N_CHIP)], axis=1),
                       wb=jnp.concatenate([pc[k][2] for k in range(N_CHIP)], axis=2),
                       wo=jnp.concatenate([pc[k][3] for k in range(N_CHIP)], axis=0)))

    saved = []
    xl = x[0]
    for l in range(L):
        w = lw[l]
        h = rms_fwd(xl, norm_w[l][None], name="rms_fwd")
        u = mm(h, w["wpt"], tb=True, name="mm_in")
        bias_small = _lanes_row([dt_bias[l], fg_bias[l]])
        sp, csum = small_fwd(u, bias_small, name="small_fwd")
        c8 = csum[:, F_LANE:F_LANE + FOX_HEADS].T
        c_col, c_row = c8[:, :, None], c8[:, None, :]
        o, lse, y_a = attn_fwd(u, c_row, name="attn_fwd")
        xbc = conv_fwd(u, full_small["ssm_conv_w"][l], ssm_conv_b[l][None], mode="plain", a_off=OFF["xbc"], b_off=0,
                       C=SSM_CONV_DIM, name="conv_ssm_fwd")
        alog_row = _lanes_row([a_log[l]])
        dskip_l = jnp.repeat(d_skip[l], SSM_P)[None]
        y_ssd, hs = ssd_fwd(xbc, sp, alog_row, dskip_l, name="ssd_fwd")
        cv_c = conv_fwd(u, full_small["sc_conv_w"][l], sc_conv_b[l][None], mode="mul", a_off=OFF["scc"], b_off=OFF["scx"],
                        C=BW, name="conv_sc_fwd")
        cv_d = conv_fwd(u, full_small["cf_conv_w"][l], cf_conv_b[l][None], mode="glu", a_off=OFF["glu"], b_off=OFF["glu"] + BW,
                        C=BW, name="conv_cf_fwd")
        y_b, y_c, y_d = post_fwd(u, y_ssd, cv_c, cv_d, ssm_norm_w[l][None], cf_ln_w[l][None], cf_ln_b[l][None], name="post_fwd")
        merged, gates, proj = merge_fwd(h, (y_a, y_b, y_c, y_d), w["wg"], full_small["b_gate"][l][:, None, :], w["wb"],
                                        name="merge_fwd")
        x_next = mm(merged, w["wo"], add=xl, name="mm_out")
        saved.append(dict(x=xl, h=h, u=u, bias_small=bias_small, sp=sp, c_col=c_col, c_row=c_row, o=o, lse=lse, xbc=xbc,
                          alog_row=alog_row, dskip_l=dskip_l, hs=hs, y_ssd=y_ssd, cv_c=cv_c, cv_d=cv_d,
                          ys=(y_a, y_b, y_c, y_d), merged=merged, gates=gates, proj=proj))
        xl = x_next

    sq, dx, d_final = loss_head(xl, final_norm_w[None], loss_target[0], name="loss_head")
    loss = lax.psum(sq[0, 0] * (0.5 / D), ("x", "y", "c"))

    small_g = {n: [None] * L for n in SMALL if n != "final_norm_w"}
    big_g = {n: [None] * L for n in BIG}
    for l in reversed(range(L)):
        w, sv = lw[l], saved[l]
        u, h = sv["u"], sv["h"]
        dm = mm(dx, w["wo"], tb=True, name="mm_dmerged")
        d_wo = mm(sv["merged"], dx, ta=True, out_dtype=BF16, name="mm_dwo")
        dp, dg, dbg = merge_bwd(dm, sv["gates"], sv["proj"], name="merge_bwd")
        dys = [mm(dp, w["wb"][i], tb=True, K=D, a_koff=i * D, name="mm_dy") for i in range(N_BRANCH)]
        d_wbt = [mm(dp, sv["ys"][i], ta=True, M=D, a_moff=i * D, out_dtype=BF16, name="mm_dwb")
                 for i in range(N_BRANCH)]
        d_wg = mm(h, dg, ta=True, out_dtype=BF16, name="mm_dwg")
        (do, delta, dga, dz, dscb, dgc, dgd, dy_ssd, dcv_c, dcv_d, dnw, dlnw, dlnb) = post_bwd(
            dys[0], dys[1], dys[2], dys[3], u, sv["o"], sv["y_ssd"], sv["cv_c"], sv["cv_d"],
            ssm_norm_w[l][None], cf_ln_w[l][None], cf_ln_b[l][None], name="post_bwd")
        delta_row = delta[:, :FOX_HEADS].T[:, None, :]
        lse_row = jnp.transpose(sv["lse"], (0, 2, 1))
        dq, dk, dv, dc_col = attn_bwd(u, do, sv["c_col"], lse_row, delta_row, name="attn_bwd")
        dscc, dscx, d_scw, d_scb = conv_bwd(dcv_c, u, full_small["sc_conv_w"][l], mode="mul", a_off=OFF["scc"],
                                            b_off=OFF["scx"], C=BW, name="conv_sc_bwd")
        dglua, dglug, d_cfw, d_cfb = conv_bwd(dcv_d, u, full_small["cf_conv_w"][l], mode="glu", a_off=OFF["glu"],
                                              b_off=OFF["glu"] + BW, C=BW, name="conv_cf_bwd")
        dxbc_pre, ddt, d_a, d_dl = ssd_bwd(dy_ssd, sv["xbc"], sv["sp"], sv["alog_row"], sv["dskip_l"], sv["hs"], name="ssd_bwd")
        dxbc, d_ssmw, d_ssmb = conv_bwd(dxbc_pre, u, full_small["ssm_conv_w"][l], mode="plain", a_off=OFF["xbc"], b_off=0,
                                        C=SSM_CONV_DIM, name="conv_ssm_bwd")
        dc_full = jnp.pad(dc_col[:, :, 0].T, ((0, 0), (F_LANE, LANES - F_LANE - FOX_HEADS)))
        du_small, dbias_small = small_bwd(dc_full, ddt, u, sv["bias_small"], name="small_bwd")
        by_name = dict(q=dq.astype(BF16), k=dk, v=dv, ga=dga, z=dz, scb=dscb, scc=dscc, scx=dscx, gc=dgc, gd=dgd, xbc=dxbc)
        du = jnp.concatenate([jnp.concatenate([dglua, dglug], axis=1) if n == "glu" else by_name[n] for n in PAD_ORDER]
                             + [du_small], axis=1)
        dh = mm(du, w["wpt"], add=mm(dg, w["wg"], tb=True, name="mm_dh_gate"), name="mm_dh_in")
        d_wpt = mm(du, h, ta=True, out_dtype=BF16, name="mm_dwp")
        dx, d_nw = rms_bwd(dh, sv["x"], norm_w[l][None], dx, name="rms_bwd")

        a_neg = -jnp.exp(a_log[l])
        sg = dict(norm_w=d_nw[0], fg_bias=dbias_small[0, F_LANE:F_LANE + FOX_HEADS], ssm_conv_w=d_ssmw, ssm_conv_b=d_ssmb[0],
                  dt_bias=dbias_small[0, DT_LANE:DT_LANE + SSM_HEADS], a_log=d_a[0, :SSM_HEADS] * a_neg,
                  d_skip=d_dl.reshape(SSM_HEADS, SSM_P).sum(-1), ssm_norm_w=dnw[0], sc_conv_w=d_scw, sc_conv_b=d_scb[0],
                  cf_conv_w=d_cfw, cf_conv_b=d_cfb[0], cf_ln_w=dlnw[0], cf_ln_b=dlnb[0], b_gate=dbg.reshape(N_BRANCH, D))
        for n in sg:
            small_g[n][l] = sg[n]

        dest = [[_rows_from_padded(d_wpt, k * q_in, (k + 1) * q_in), d_wg[k * q_d:(k + 1) * q_d]]
                + [t[k * q_d:(k + 1) * q_d] for t in d_wbt] + [d_wo[k * q_d:(k + 1) * q_d]] for k in range(N_CHIP)]
        g_all = _pack_groups(dest, PACK_C, PACK_ROWS)
        R = g_all.shape[1] // 2
        g_all = g_all.reshape(N_CHIP, 2, R, PACK_C)
        sib = swap_partials(g_all, name="swap_partials")
        part = add_sibling(g_all, sib, ci.reshape(1).astype(jnp.int32), name="add_sibling")
        rcv = scatter_chips(part, name="scatter_partials")
        full = share_halves(add_chips(part, rcv, jnp.stack([chip, ci]).astype(jnp.int32), name="add_chips"), name="share_halves")
        g_in, g_wg, g_wb, g_wo = _unpack(full, [(q_in, D), (q_d, N_BRANCH * D), (N_BRANCH, q_d, BW), (q_d, D)])
        big_g["w_in"][l] = g_in.T
        big_g["w_gate"][l] = jnp.transpose(g_wg.reshape(q_d, N_BRANCH, D), (1, 0, 2))
        big_g["w_branch"][l] = jnp.transpose(g_wb, (0, 2, 1))
        big_g["w_out"][l] = g_wo

    names = [n for n in SMALL if n != "final_norm_w"]
    stacked = [jnp.stack(small_g[n]) for n in names] + [d_final[0]]
    shapes = [a.shape for a in stacked]
    got = gather_all(_pack(stacked, LANES, 8), name="gather_small_g")
    tot = addn([got[d] for d in range(8)], out_dtype=F32, name="add_small_g")
    grads = dict(zip(names + ["final_norm_w"], _unpack(tot, shapes)))
    for n in SMALL_SHARDED:
        sz = wts[n].shape[-1]
        grads[n] = lax.dynamic_slice_in_dim(grads[n], chip * sz, sz, axis=grads[n].ndim - 1)
    for n in BIG:
        grads[n] = jnp.stack(big_g[n])

    delta, new_m, new_v = {}, {}, {}
    for n in BIG:
        two_d = lambda a: a.reshape(-1, a.shape[-1])
        d, nm, nv = adamw(two_d(wts[n]), two_d(grads[n]), two_d(mom[n]), two_d(vel[n]), name="adamw_" + n)
        delta[n], new_m[n], new_v[n] = (t.reshape(wts[n].shape) for t in (d, nm, nv))
    small_shapes = [wts[n].shape for n in SMALL]
    pk = lambda src: _pack([src[n] for n in SMALL], LANES, 8)
    d, nm, nv = adamw(pk(wts), pk(grads), pk(mom), pk(vel), name="adamw_small")
    for tgt, src in ((delta, d), (new_m, nm), (new_v, nv)):
        for n, a in zip(SMALL, _unpack(src, small_shapes)):
            tgt[n] = a

    return (loss, dx[None], *[grads[n] for n in WEIGHTS], *[delta[n] for n in WEIGHTS],
            *[new_m[n] for n in WEIGHTS], *[new_v[n] for n in WEIGHTS])
```

```python
import functools
import math

import jax
import jax.numpy as jnp
from jax import lax
from jax.experimental import pallas as pl
from jax.experimental.pallas import tpu as pltpu

F32, BF16 = jnp.float32, jnp.bfloat16
MESH = pl.DeviceIdType.MESH

D_MODEL = 2048
BW = D_MODEL // 2
FOX_HEADS, FOX_HD = 8, 128
SSM_HEADS, SSM_P, SSM_N, SSM_G = 16, 64, 128, 2
SSM_CONV_DIM = BW + 2 * SSM_G * SSM_N
N_BRANCH = 4
EPS = 1e-6
NEG = -1e30
ORIG_SIZES = (BW, BW, BW, FOX_HEADS, BW, BW, SSM_CONV_DIM, SSM_HEADS, BW, BW, BW, BW, 2 * BW, BW)
ORIG_NAMES = ("q", "k", "v", "f", "ga", "z", "xbc", "dt", "scb", "scc", "scx", "gc", "glu", "gd")
N_IN = sum(ORIG_SIZES)
PAD_ORDER = ("q", "k", "v", "ga", "z", "scb", "scc", "scx", "gc", "glu", "gd", "xbc")
SMALL_W = 512
DT_LANE, F_LANE = 0, SSM_HEADS
OFF = {}
_o = 0
for _n in PAD_ORDER:
    OFF[_n] = _o
    _o += ORIG_SIZES[ORIG_NAMES.index(_n)]
OFF["small"] = _o
NP = _o + SMALL_W
LANES = 128
CONV_HALO = 32

VMEM_LIMIT = 56 * 1024 * 1024
MM_VMEM_BUDGET = 40 * 1024 * 1024

ADAM_LR, ADAM_B1, ADAM_B2, ADAM_EPS, ADAM_WD, ADAM_STEP = 0.001, 0.9, 0.999, 1e-08, 0.01, 10

TILES = dict(row=512, post=256, att=1024, ssd=256, mm_m=1024, mm_n=1024, conv_c=256)


def _cp(sem=None):
    return pltpu.CompilerParams(dimension_semantics=sem, vmem_limit_bytes=VMEM_LIMIT)


def _tile(n, pref):
    t = 1 << (min(n, pref).bit_length() - 1)
    while n % t:
        t //= 2
    return t


def _sig(x):
    return 1.0 / (1.0 + jnp.exp(-x))


def _silu(x):
    return x * _sig(x)


def _dsilu(x):
    s = _sig(x)
    return s * (1.0 + x * (1.0 - s))


def _softplus(x):
    return jnp.maximum(x, 0.0) + jnp.log(1.0 + jnp.exp(-jnp.abs(x)))


def mm(a, b, *, name, out_dtype=F32, add=None, ta=False, tb=False, M=None, K=None, N=None,
       a_koff=0, a_moff=0, b_koff=0, b_noff=0):
    b3 = b.ndim == 3
    assert not b3 or (tb and b_koff == 0 and b_noff == 0)
    M = M or (a.shape[1] if ta else a.shape[0])
    K = K or (a.shape[0] if ta else a.shape[1])
    N = N or (b.shape[-2] if tb else b.shape[1])
    tm, tn = _tile(M, TILES["mm_m"]), _tile(N, TILES["mm_n"])
    sa, sb, so = a.dtype.itemsize, b.dtype.itemsize, jnp.dtype(out_dtype).itemsize

    def need(tk):
        return 2 * tm * tk * sa + 2 * tk * tn * sb + 2 * tm * tn * so + (8 * tm * tn if add is not None else 0) + 4 * tm * tn

    tk = b.shape[2] if b3 else K
    while need(tk) > MM_VMEM_BUDGET and tk % 256 == 0:
        tk //= 2
    assert K % tk == 0 and a_koff % tk == 0 and b_koff % tk == 0 and b_noff % tn == 0 and a_moff % tm == 0, (name, K, tk, tn)
    nk = K // tk
    ako, amo, bko, bno = a_koff // tk, a_moff // tm, b_koff // tk, b_noff // tn
    dims = (((0 if ta else 1,), (1 if tb else 0,)), ((), ()))

    def body(*refs):
        if add is not None:
            a_ref, b_ref, add_ref, o_ref = refs[:4]
        else:
            a_ref, b_ref, o_ref = refs[:3]
            add_ref = None
        prod = lax.dot_general(a_ref[...].astype(BF16), b_ref[...].astype(BF16), dims, preferred_element_type=F32)

        def finish(acc):
            if add_ref is not None:
                acc = acc + add_ref[...]
            o_ref[...] = acc.astype(out_dtype)

        if nk == 1:
            finish(prod)
        else:
            acc_ref = refs[-1]
            k = pl.program_id(2)

            @pl.when(k == 0)
            def _():
                acc_ref[...] = prod

            @pl.when(k > 0)
            def _():
                acc_ref[...] += prod

            @pl.when(k == nk - 1)
            def _():
                finish(acc_ref[...])

    if ta:
        a_spec = pl.BlockSpec((tk, tm), lambda i, j, k: (k + ako, i + amo))
    else:
        a_spec = pl.BlockSpec((tm, tk), lambda i, j, k: (i, k + ako))
    if b3:
        assert tk == b.shape[2], (name, tk)
        b_spec = pl.BlockSpec((None, tn, tk), lambda i, j, k: (k, j, 0))
    elif tb:
        b_spec = pl.BlockSpec((tn, tk), lambda i, j, k: (j + bno, k + bko))
    else:
        b_spec = pl.BlockSpec((tk, tn), lambda i, j, k: (k + bko, j + bno))
    in_specs = [a_spec, b_spec]
    args = [a, b]
    if add is not None:
        in_specs.append(pl.BlockSpec((tm, tn), lambda i, j, k: (i, j)))
        args.append(add)
    return pl.pallas_call(
        body, name=name, grid=(M // tm, N // tn, nk),
        in_specs=in_specs, out_specs=pl.BlockSpec((tm, tn), lambda i, j, k: (i, j)),
        out_shape=jax.ShapeDtypeStruct((M, N), out_dtype),
        scratch_shapes=[pltpu.VMEM((tm, tn), F32)] if nk > 1 else [],
        compiler_params=_cp(("parallel", "parallel", "arbitrary")),
    )(*args)


def rms_fwd(x, w, *, name):
    S, D = x.shape
    ts = _tile(S, TILES["row"])

    def body(x_ref, w_ref, o_ref):
        xf = x_ref[...]
        r = lax.rsqrt(jnp.mean(xf * xf, axis=-1, keepdims=True) + EPS)
        o_ref[...] = (xf * r * w_ref[...]).astype(BF16)

    return pl.pallas_call(
        body, name=name, grid=(S // ts,),
        in_specs=[pl.BlockSpec((ts, D), lambda i: (i, 0)), pl.BlockSpec((1, D), lambda i: (0, 0))],
        out_specs=pl.BlockSpec((ts, D), lambda i: (i, 0)),
        out_shape=jax.ShapeDtypeStruct((S, D), BF16), compiler_params=_cp(("parallel",)),
    )(x, w)


def rms_bwd(dh, x, w, dres, *, name):
    S, D = x.shape
    ts = _tile(S, TILES["row"])

    def body(dh_ref, x_ref, w_ref, dres_ref, dx_ref, dw_ref):
        xf = x_ref[...]
        r = lax.rsqrt(jnp.mean(xf * xf, axis=-1, keepdims=True) + EPS)
        xh = xf * r
        g = dh_ref[...]
        dxh = g * w_ref[...]
        dx_ref[...] = dres_ref[...] + r * (dxh - xh * jnp.mean(dxh * xh, axis=-1, keepdims=True))
        part = jnp.sum(g * xh, axis=0, keepdims=True)

        @pl.when(pl.program_id(0) == 0)
        def _():
            dw_ref[...] = part

        @pl.when(pl.program_id(0) > 0)
        def _():
            dw_ref[...] += part

    row = pl.BlockSpec((ts, D), lambda i: (i, 0))
    vec = pl.BlockSpec((1, D), lambda i: (0, 0))
    return pl.pallas_call(
        body, name=name, grid=(S // ts,), in_specs=[row, row, vec, row], out_specs=[row, vec],
        out_shape=[jax.ShapeDtypeStruct((S, D), F32), jax.ShapeDtypeStruct((1, D), F32)],
        compiler_params=_cp(("arbitrary",)),
    )(dh, x, w, dres)


def loss_head(x, w, target, *, name):
    S, D = x.shape
    ts = _tile(S, TILES["row"])

    def body(x_ref, w_ref, t_ref, loss_ref, dx_ref, dw_ref):
        xf = x_ref[...]
        r = lax.rsqrt(jnp.mean(xf * xf, axis=-1, keepdims=True) + EPS)
        xh = xf * r
        err = xh * w_ref[...] - t_ref[...]
        sq = jnp.sum(jnp.sum(err * err, axis=0, keepdims=True), axis=1, keepdims=True)
        dy = err * (1.0 / D)
        dxh = dy * w_ref[...]
        dx_ref[...] = r * (dxh - xh * jnp.mean(dxh * xh, axis=-1, keepdims=True))
        part = jnp.sum(dy * xh, axis=0, keepdims=True)

        @pl.when(pl.program_id(0) == 0)
        def _():
            dw_ref[...] = part
            loss_ref[...] = jnp.broadcast_to(sq, (8, LANES))

        @pl.when(pl.program_id(0) > 0)
        def _():
            dw_ref[...] += part
            loss_ref[...] += jnp.broadcast_to(sq, (8, LANES))

    row = pl.BlockSpec((ts, D), lambda i: (i, 0))
    vec = pl.BlockSpec((1, D), lambda i: (0, 0))
    return pl.pallas_call(
        body, name=name, grid=(S // ts,), in_specs=[row, vec, row],
        out_specs=[pl.BlockSpec((8, LANES), lambda i: (0, 0)), row, vec],
        out_shape=[jax.ShapeDtypeStruct((8, LANES), F32), jax.ShapeDtypeStruct((S, D), F32),
                   jax.ShapeDtypeStruct((1, D), F32)],
        compiler_params=_cp(("arbitrary",)),
    )(x, w, target)


def _shift_scan(v, buf_ref, n, reverse=False):
    buf_ref[pl.ds(0, n), :] = jnp.zeros((n, LANES), F32)
    buf_ref[pl.ds(2 * n, n), :] = jnp.zeros((n, LANES), F32)
    s = 1
    while s < n:
        buf_ref[pl.ds(n, n), :] = v
        v = v + buf_ref[pl.ds(n + s if reverse else n - s, n), :]
        s *= 2
    return v


def small_fwd(u, bias, *, name):
    S = u.shape[0]
    ts = _tile(S, TILES["row"])
    cb = OFF["small"] // LANES

    def body(u_ref, b_ref, sp_ref, c_ref, buf_ref, carry_ref):
        x = u_ref[...] + b_ref[...]
        sp_ref[...] = _softplus(x)
        lf = jnp.minimum(x, 0.0) - jnp.log(1.0 + jnp.exp(-jnp.abs(x)))

        @pl.when(pl.program_id(0) == 0)
        def _():
            carry_ref[...] = jnp.zeros((8, LANES), F32)

        c = _shift_scan(lf, buf_ref, ts) + carry_ref[pl.ds(0, 1), :]
        c_ref[...] = c
        carry_ref[...] = jnp.broadcast_to(c_ref[pl.ds(ts - 1, 1), :], (8, LANES))

    blk = pl.BlockSpec((ts, LANES), lambda i: (i, 0))
    return pl.pallas_call(
        body, name=name, grid=(S // ts,),
        in_specs=[pl.BlockSpec((ts, LANES), lambda i: (i, cb)), pl.BlockSpec((1, LANES), lambda i: (0, 0))],
        out_specs=[blk, blk], out_shape=[jax.ShapeDtypeStruct((S, LANES), F32)] * 2,
        scratch_shapes=[pltpu.VMEM((3 * ts, LANES), F32), pltpu.VMEM((8, LANES), F32)],
        compiler_params=_cp(("arbitrary",)),
    )(u, bias)


def small_bwd(dc, dsp, u, bias, *, name):
    S = u.shape[0]
    ts = _tile(S, TILES["row"])
    cb = OFF["small"] // LANES
    nt = S // ts

    def body(dc_ref, dsp_ref, u_ref, b_ref, du_ref, db_ref, buf_ref, carry_ref):
        x = u_ref[...] + b_ref[...]

        @pl.when(pl.program_id(0) == 0)
        def _():
            carry_ref[...] = jnp.zeros((8, LANES), F32)

        dlf = _shift_scan(dc_ref[...], buf_ref, ts, reverse=True) + carry_ref[pl.ds(0, 1), :]
        buf_ref[pl.ds(0, ts), :] = dlf
        carry_ref[...] = jnp.broadcast_to(buf_ref[pl.ds(0, 1), :], (8, LANES))
        sg = _sig(x)
        dx = dlf * (1.0 - sg) + dsp_ref[...] * sg
        du_ref[...] = jnp.concatenate([dx, jnp.zeros((ts, SMALL_W - LANES), F32)], axis=1).astype(BF16)
        part = jnp.sum(dx, axis=0, keepdims=True)

        @pl.when(pl.program_id(0) == 0)
        def _():
            db_ref[...] = part

        @pl.when(pl.program_id(0) > 0)
        def _():
            db_ref[...] += part

    rev = pl.BlockSpec((ts, LANES), lambda i: (nt - 1 - i, 0))
    return pl.pallas_call(
        body, name=name, grid=(nt,),
        in_specs=[rev, rev, pl.BlockSpec((ts, LANES), lambda i: (nt - 1 - i, cb)), pl.BlockSpec((1, LANES), lambda i: (0, 0))],
        out_specs=[pl.BlockSpec((ts, SMALL_W), lambda i: (nt - 1 - i, 0)), pl.BlockSpec((1, LANES), lambda i: (0, 0))],
        out_shape=[jax.ShapeDtypeStruct((S, SMALL_W), BF16), jax.ShapeDtypeStruct((1, LANES), F32)],
        scratch_shapes=[pltpu.VMEM((3 * ts, LANES), F32), pltpu.VMEM((8, LANES), F32)],
        compiler_params=_cp(("arbitrary",)),
    )(dc, dsp, u, bias)


def _conv_in(mode, a, b):
    if mode == "plain":
        return a
    if mode == "mul":
        return a * b
    return a * _sig(b)


SUBLANES = 8
MANY_TAPS = 8


def _row_taps(src_ref, sh_ref, n_rows, many):
    if many:
        for b in range(1, SUBLANES):
            sh_ref[b, pl.ds(0, n_rows - SUBLANES), :] = src_ref[pl.ds(b, n_rows - SUBLANES), :]

    def tap(off, n):
        a, b = divmod(off, SUBLANES)
        if not many or b == 0:
            return src_ref[pl.ds(off, n), :]
        return sh_ref[b, pl.ds(SUBLANES * a, n), :]

    return tap


def conv_fwd(u, w, bias, *, mode, a_off, b_off, C, name):
    S = u.shape[0]
    Kc = w.shape[0]
    ts, cb, H = _tile(S, TILES["row"]), _tile(C, TILES["conv_c"]), CONV_HALO
    two = mode != "plain"
    rb = ts // H
    many = Kc >= MANY_TAPS

    def body(*refs):
        if two:
            a_ref, ap_ref, b_ref, bp_ref, w_ref, bias_ref, o_ref, x_ref = refs[:8]
            cur, prev = _conv_in(mode, a_ref[...], b_ref[...]), _conv_in(mode, ap_ref[...], bp_ref[...])
        else:
            a_ref, ap_ref, w_ref, bias_ref, o_ref, x_ref = refs[:6]
            cur, prev = a_ref[...], ap_ref[...]
        x_ref[pl.ds(0, H), :] = jnp.where(pl.program_id(1) == 0, 0.0, prev)
        x_ref[pl.ds(H, ts), :] = cur
        tap = _row_taps(x_ref, refs[-1], ts + H, many)
        acc = jnp.broadcast_to(bias_ref[...], (ts, cb))
        for j in range(Kc):
            acc = acc + w_ref[pl.ds(j, 1), :] * tap(H - (Kc - 1) + j, ts)
        o_ref[...] = acc

    def cur_spec(off):
        return pl.BlockSpec((ts, cb), lambda c, i: (i, c + off // cb))

    def prev_spec(off):
        return pl.BlockSpec((H, cb), lambda c, i: (jnp.maximum(i * rb - 1, 0), c + off // cb))

    in_specs, args = [cur_spec(a_off), prev_spec(a_off)], [u, u]
    if two:
        in_specs += [cur_spec(b_off), prev_spec(b_off)]
        args += [u, u]
    in_specs += [pl.BlockSpec((Kc, cb), lambda c, i: (0, c)), pl.BlockSpec((1, cb), lambda c, i: (0, c))]
    return pl.pallas_call(
        body, name=name, grid=(C // cb, S // ts), in_specs=in_specs,
        out_specs=pl.BlockSpec((ts, cb), lambda c, i: (i, c)), out_shape=jax.ShapeDtypeStruct((S, C), F32),
        scratch_shapes=[pltpu.VMEM((ts + H, cb), F32), pltpu.VMEM((SUBLANES if many else 1, ts + H, cb), F32)],
        compiler_params=_cp(("parallel", "arbitrary")),
    )(*args, w, bias)


def conv_bwd(dy, u, w, *, mode, a_off, b_off, C, name):
    S = u.shape[0]
    Kc = w.shape[0]
    ts, cb, H = _tile(S, TILES["row"]), _tile(C, TILES["conv_c"]), CONV_HALO
    two = mode != "plain"
    rb, nt = ts // H, S // ts

    many = Kc >= MANY_TAPS

    def body(*refs):
        if two:
            dy_ref, dyn_ref, a_ref, ap_ref, b_ref, bp_ref, w_ref, da_ref, db_ref, dw_ref, dbias_ref, x_ref, g_ref = refs[:13]
            a, b = a_ref[...], b_ref[...]
            cur, prev = _conv_in(mode, a, b), _conv_in(mode, ap_ref[...], bp_ref[...])
        else:
            dy_ref, dyn_ref, a_ref, ap_ref, w_ref, da_ref, dw_ref, dbias_ref, x_ref, g_ref = refs[:10]
            cur, prev = a_ref[...], ap_ref[...]
        i = pl.program_id(1)
        x_ref[pl.ds(0, H), :] = jnp.where(i == 0, 0.0, prev)
        x_ref[pl.ds(H, ts), :] = cur
        g = dy_ref[...]
        g_ref[pl.ds(0, ts), :] = g
        g_ref[pl.ds(ts, H), :] = jnp.where(i == nt - 1, 0.0, dyn_ref[...])
        x_tap = _row_taps(x_ref, refs[-2], ts + H, many)
        g_tap = _row_taps(g_ref, refs[-1], ts + H, many)

        @pl.when(i == 0)
        def _():
            dw_ref[...] = jnp.zeros((Kc, cb), F32)
            dbias_ref[...] = jnp.zeros((1, cb), F32)

        dbias_ref[...] += jnp.sum(g, axis=0, keepdims=True)
        dx = jnp.zeros((ts, cb), F32)
        for j in range(Kc):
            dx = dx + w_ref[pl.ds(j, 1), :] * g_tap(Kc - 1 - j, ts)
            dw_ref[pl.ds(j, 1), :] += jnp.sum(g * x_tap(H - (Kc - 1) + j, ts), axis=0, keepdims=True)
        if mode == "plain":
            da_ref[...] = dx.astype(BF16)
        elif mode == "mul":
            da_ref[...] = (dx * b).astype(BF16)
            db_ref[...] = (dx * a).astype(BF16)
        else:
            sg = _sig(b)
            da_ref[...] = (dx * sg).astype(BF16)
            db_ref[...] = (dx * a * sg * (1.0 - sg)).astype(BF16)

    def cur_spec(off):
        return pl.BlockSpec((ts, cb), lambda c, i: (i, c + off // cb))

    def prev_spec(off):
        return pl.BlockSpec((H, cb), lambda c, i: (jnp.maximum(i * rb - 1, 0), c + off // cb))

    out_blk = pl.BlockSpec((ts, cb), lambda c, i: (i, c))
    in_specs = [out_blk, pl.BlockSpec((H, cb), lambda c, i: (jnp.minimum((i + 1) * rb, S // H - 1), c)),
                cur_spec(a_off), prev_spec(a_off)]
    args = [dy, dy, u, u]
    if two:
        in_specs += [cur_spec(b_off), prev_spec(b_off)]
        args += [u, u]
    in_specs.append(pl.BlockSpec((Kc, cb), lambda c, i: (0, c)))
    n_d = 2 if two else 1
    return pl.pallas_call(
        body, name=name, grid=(C // cb, nt), in_specs=in_specs,
        out_specs=[out_blk] * n_d + [pl.BlockSpec((Kc, cb), lambda c, i: (0, c)), pl.BlockSpec((1, cb), lambda c, i: (0, c))],
        out_shape=[jax.ShapeDtypeStruct((S, C), BF16)] * n_d + [jax.ShapeDtypeStruct((Kc, C), F32), jax.ShapeDtypeStruct((1, C), F32)],
        scratch_shapes=[pltpu.VMEM((ts + H, cb), F32), pltpu.VMEM((ts + H, cb), F32)]
        + [pltpu.VMEM((SUBLANES if many else 1, ts + H, cb), F32)] * 2,
        compiler_params=_cp(("parallel", "arbitrary")),
    )(*args, w)


def _usec(ts, name):
    return pl.BlockSpec((ts, BW), lambda i, o=OFF[name] // BW: (i, o))


def _acc_rows(ref, part):
    @pl.when(pl.program_id(0) == 0)
    def _():
        ref[...] = part

    @pl.when(pl.program_id(0) > 0)
    def _():
        ref[...] += part


def post_fwd(u, y_ssd, cv_c, cv_d, nw, lnw, lnb, *, name):
    S = u.shape[0]
    ts = _tile(S, TILES["post"])

    def body(z_ref, scb_ref, gc_ref, gd_ref, ys_ref, cc_ref, cd_ref, nw_ref, lw_ref, lb_ref, yb_ref, yc_ref, yd_ref):
        t = ys_ref[...] * _silu(z_ref[...])
        r = lax.rsqrt(jnp.mean(t * t, axis=-1, keepdims=True) + EPS)
        yb_ref[...] = (t * r * nw_ref[...]).astype(BF16)
        yc_ref[...] = (scb_ref[...] * cc_ref[...] * _silu(gc_ref[...])).astype(BF16)
        cf = cd_ref[...]
        mu = jnp.mean(cf, axis=-1, keepdims=True)
        xc = cf - mu
        rl = lax.rsqrt(jnp.mean(xc * xc, axis=-1, keepdims=True) + EPS)
        yln = xc * rl * lw_ref[...] + lb_ref[...]
        yd_ref[...] = (_silu(yln) * _silu(gd_ref[...])).astype(BF16)

    row = pl.BlockSpec((ts, BW), lambda i: (i, 0))
    vec = pl.BlockSpec((1, BW), lambda i: (0, 0))
    return pl.pallas_call(
        body, name=name, grid=(S // ts,),
        in_specs=[_usec(ts, "z"), _usec(ts, "scb"), _usec(ts, "gc"), _usec(ts, "gd"), row, row, row, vec, vec, vec],
        out_specs=[row] * 3, out_shape=[jax.ShapeDtypeStruct((S, BW), BF16)] * 3, compiler_params=_cp(("parallel",)),
    )(u, u, u, u, y_ssd, cv_c, cv_d, nw, lnw, lnb)


def post_bwd(dy_a, dy_b, dy_c, dy_d, u, o, y_ssd, cv_c, cv_d, nw, lnw, lnb, *, name):
    S = u.shape[0]
    ts = _tile(S, TILES["post"])

    def body(dya_ref, dyb_ref, dyc_ref, dyd_ref, ga_ref, z_ref, scb_ref, gc_ref, gd_ref, o_ref, ys_ref, cc_ref, cd_ref,
             nw_ref, lw_ref, lb_ref,
             do_ref, dl_ref, dga_ref, dz_ref, dscb_ref, dgc_ref, dgd_ref, dys_ref, dcc_ref, dcd_ref, dnw_ref, dlw_ref, dlb_ref):
        ga, ov, dya = ga_ref[...], o_ref[...], dya_ref[...]
        dob = (dya * _silu(ga)).astype(BF16)
        do_ref[...] = dob
        dga_ref[...] = (dya * ov * _dsilu(ga)).astype(BF16)
        prod = dob.astype(F32) * ov
        lane = lax.broadcasted_iota(jnp.int32, (ts, LANES), 1)
        delta = jnp.zeros((ts, LANES), F32)
        for h in range(FOX_HEADS):
            col = jnp.sum(prod[:, h * FOX_HD:(h + 1) * FOX_HD], axis=1, keepdims=True)
            delta = jnp.where(lane == h, col, delta)
        dl_ref[...] = delta
        ys, z, dyb = ys_ref[...], z_ref[...], dyb_ref[...]
        sz = _silu(z)
        t = ys * sz
        r = lax.rsqrt(jnp.mean(t * t, axis=-1, keepdims=True) + EPS)
        th = t * r
        dth = dyb * nw_ref[...]
        dt_ = r * (dth - th * jnp.mean(dth * th, axis=-1, keepdims=True))
        dys_ref[...] = dt_ * sz
        dz_ref[...] = (dt_ * ys * _dsilu(z)).astype(BF16)
        _acc_rows(dnw_ref, jnp.sum(dyb * th, axis=0, keepdims=True))
        scb, cc, gc, dyc = scb_ref[...], cc_ref[...], gc_ref[...], dyc_ref[...]
        sg = _silu(gc)
        dscb_ref[...] = (dyc * cc * sg).astype(BF16)
        dcc_ref[...] = dyc * scb * sg
        dgc_ref[...] = (dyc * scb * cc * _dsilu(gc)).astype(BF16)
        cf, gd, dyd = cd_ref[...], gd_ref[...], dyd_ref[...]
        mu = jnp.mean(cf, axis=-1, keepdims=True)
        xc = cf - mu
        rl = lax.rsqrt(jnp.mean(xc * xc, axis=-1, keepdims=True) + EPS)
        xh = xc * rl
        yln = xh * lw_ref[...] + lb_ref[...]
        dyln = dyd * _silu(gd) * _dsilu(yln)
        dgd_ref[...] = (dyd * _silu(yln) * _dsilu(gd)).astype(BF16)
        _acc_rows(dlw_ref, jnp.sum(dyln * xh, axis=0, keepdims=True))
        _acc_rows(dlb_ref, jnp.sum(dyln, axis=0, keepdims=True))
        dxh = dyln * lw_ref[...]
        dcd_ref[...] = rl * (dxh - jnp.mean(dxh, axis=-1, keepdims=True) - xh * jnp.mean(dxh * xh, axis=-1, keepdims=True))

    row = pl.BlockSpec((ts, BW), lambda i: (i, 0))
    vec = pl.BlockSpec((1, BW), lambda i: (0, 0))
    sd = jax.ShapeDtypeStruct
    return pl.pallas_call(
        body, name=name, grid=(S // ts,),
        in_specs=[row] * 4 + [_usec(ts, n) for n in ("ga", "z", "scb", "gc", "gd")] + [row] * 4 + [vec] * 3,
        out_specs=[row, pl.BlockSpec((ts, LANES), lambda i: (i, 0))] + [row] * 8 + [vec] * 3,
        out_shape=[sd((S, BW), BF16), sd((S, LANES), F32)] + [sd((S, BW), BF16)] * 5 + [sd((S, BW), F32)] * 3 + [sd((1, BW), F32)] * 3,
        compiler_params=_cp(("arbitrary",)),
    )(dy_a, dy_b, dy_c, dy_d, u, u, u, u, u, o, y_ssd, cv_c, cv_d, nw, lnw, lnb)


def merge_fwd(h, ys, wg, bg, wb, *, name):
    S, D = h.shape
    tm, tn = _tile(S, 1024), _tile(D, 512)
    nb = D // tn

    def body(h_ref, y0, y1, y2, y3, wg_ref, bg_ref, wb_ref, m_ref, g_ref, p_ref, acc_ref):
        i = pl.program_id(2)
        g = jnp.dot(h_ref[...], wg_ref[...], preferred_element_type=F32) + bg_ref[...]
        gate = _sig(g)
        for b, y_ref in enumerate((y0, y1, y2, y3)):
            @pl.when(i == b)
            def _(y_ref=y_ref):
                p = jnp.dot(y_ref[...], wb_ref[...], preferred_element_type=F32)
                g_ref[...] = gate.astype(BF16)
                p_ref[...] = p.astype(BF16)
                if b == 0:
                    acc_ref[...] = gate * p
                else:
                    acc_ref[...] += gate * p

        @pl.when(i == N_BRANCH - 1)
        def _():
            m_ref[...] = acc_ref[...].astype(BF16)

    yspec = pl.BlockSpec((tm, BW), lambda m, n, i: (m, 0))
    return pl.pallas_call(
        body, name=name, grid=(S // tm, nb, N_BRANCH),
        in_specs=[pl.BlockSpec((tm, D), lambda m, n, i: (m, 0)), yspec, yspec, yspec, yspec,
                  pl.BlockSpec((None, D, tn), lambda m, n, i: (i, 0, n)),
                  pl.BlockSpec((None, 1, tn), lambda m, n, i: (i, 0, n)),
                  pl.BlockSpec((None, BW, tn), lambda m, n, i: (i, 0, n))],
        out_specs=[pl.BlockSpec((tm, tn), lambda m, n, i: (m, n)),
                   pl.BlockSpec((tm, tn), lambda m, n, i: (m, i * nb + n)),
                   pl.BlockSpec((tm, tn), lambda m, n, i: (m, i * nb + n))],
        out_shape=[jax.ShapeDtypeStruct((S, D), BF16), jax.ShapeDtypeStruct((S, N_BRANCH * D), BF16),
                   jax.ShapeDtypeStruct((S, N_BRANCH * D), BF16)],
        scratch_shapes=[pltpu.VMEM((tm, tn), F32)],
        compiler_params=_cp(("parallel", "parallel", "arbitrary")),
    )(h, *ys, wg, bg, wb)


def merge_bwd(dm, gates, proj, *, name):
    S, D = dm.shape
    ts, tn = _tile(S, TILES["row"]), _tile(D, 512)
    nb = D // tn

    def body(dm_ref, g_ref, p_ref, dp_ref, dg_ref, db_ref):
        d = dm_ref[...]
        g = g_ref[...].astype(F32)
        dp_ref[...] = (d * g).astype(BF16)
        dg = d * p_ref[...].astype(F32) * g * (1.0 - g)
        dg_ref[...] = dg.astype(BF16)
        part = jnp.sum(dg, axis=0, keepdims=True)

        @pl.when(pl.program_id(2) == 0)
        def _():
            db_ref[...] = part

        @pl.when(pl.program_id(2) > 0)
        def _():
            db_ref[...] += part

    wide = pl.BlockSpec((ts, tn), lambda b, n, i: (i, b * nb + n))
    return pl.pallas_call(
        body, name=name, grid=(N_BRANCH, nb, S // ts),
        in_specs=[pl.BlockSpec((ts, tn), lambda b, n, i: (i, n)), wide, wide],
        out_specs=[wide, wide, pl.BlockSpec((1, tn), lambda b, n, i: (0, b * nb + n))],
        out_shape=[jax.ShapeDtypeStruct((S, N_BRANCH * D), BF16)] * 2 + [jax.ShapeDtypeStruct((1, N_BRANCH * D), F32)],
        compiler_params=_cp(("parallel", "parallel", "arbitrary")),
    )(dm, gates, proj)


_NT = (((1,), (1,)), ((), ()))
_TN = (((0,), (0,)), ((), ()))


def _tri_pairs(n, by_key):
    if by_key:
        pairs = [(i, j) for j in range(n) for i in range(j, n)]
    else:
        pairs = [(i, j) for i in range(n) for j in range(i + 1)]
    return (jnp.array([p[0] for p in pairs], jnp.int32), jnp.array([p[1] for p in pairs], jnp.int32))


def attn_fwd(u, c_row, *, name):
    S = u.shape[0]
    T = _tile(S, TILES["att"])
    n = S // T
    qo, ko, vo, go = (OFF[k] // FOX_HD for k in ("q", "k", "v", "ga"))
    scale = FOX_HD ** -0.5
    it, jt = _tri_pairs(n, by_key=False)

    def body(it_ref, jt_ref, q_ref, k_ref, v_ref, ck_ref, ga_ref, o_ref, lse_ref, ya_ref, m_ref, l_ref, acc_ref):
        i, j = it_ref[pl.program_id(1)], jt_ref[pl.program_id(1)]

        @pl.when(j == 0)
        def _():
            m_ref[...] = jnp.full((T, 1), NEG, F32)
            l_ref[...] = jnp.zeros((T, 1), F32)
            acc_ref[...] = jnp.zeros((T, FOX_HD), F32)

        def step(masked):
            qb = (q_ref[...] * scale).astype(BF16)
            s = lax.dot_general(qb, k_ref[...].astype(BF16), _NT, preferred_element_type=F32) - ck_ref[...]
            if masked:
                row = lax.broadcasted_iota(jnp.int32, (T, T), 0)
                col = lax.broadcasted_iota(jnp.int32, (T, T), 1)
                s = jnp.where(col <= row, s, NEG)
            m_old = m_ref[...]
            m_new = jnp.maximum(m_old, jnp.max(s, axis=1, keepdims=True))
            alpha = jnp.exp(m_old - m_new)
            p = jnp.exp(s - m_new)
            l_ref[...] = alpha * l_ref[...] + jnp.sum(p, axis=1, keepdims=True)
            p_hi = p.astype(BF16)
            p_lo = (p - p_hi.astype(F32)).astype(BF16)
            vb = v_ref[...].astype(BF16)
            pv = jnp.dot(p_hi, vb, preferred_element_type=F32) + jnp.dot(p_lo, vb, preferred_element_type=F32)
            acc_ref[...] = alpha * acc_ref[...] + pv
            m_ref[...] = m_new

        @pl.when(j < i)
        def _():
            step(False)

        @pl.when(j == i)
        def _():
            step(True)
            o = acc_ref[...] / l_ref[...]
            o_ref[...] = o
            lse_ref[...] = m_ref[...] + jnp.log(l_ref[...])
            ya_ref[...] = (o * _silu(ga_ref[...])).astype(BF16)

    def qsec(off):
        return pl.BlockSpec((T, FOX_HD), lambda h, p, it, jt: (it[p], off + h))

    def ksec(off):
        return pl.BlockSpec((T, FOX_HD), lambda h, p, it, jt: (jt[p], off + h))

    out = pl.BlockSpec((T, FOX_HD), lambda h, p, it, jt: (it[p], h))
    colv = pl.BlockSpec((None, T, 1), lambda h, p, it, jt: (h, it[p], 0))
    return pl.pallas_call(
        body, name=name,
        grid_spec=pltpu.PrefetchScalarGridSpec(
            num_scalar_prefetch=2, grid=(FOX_HEADS, n * (n + 1) // 2),
            in_specs=[qsec(qo), ksec(ko), ksec(vo), pl.BlockSpec((None, 1, T), lambda h, p, it, jt: (h, 0, jt[p])), qsec(go)],
            out_specs=[out, colv, out],
            scratch_shapes=[pltpu.VMEM((T, 1), F32), pltpu.VMEM((T, 1), F32), pltpu.VMEM((T, FOX_HD), F32)]),
        out_shape=[jax.ShapeDtypeStruct((S, BW), F32), jax.ShapeDtypeStruct((FOX_HEADS, S, 1), F32),
                   jax.ShapeDtypeStruct((S, BW), BF16)],
        compiler_params=_cp(("parallel", "arbitrary")),
    )(it, jt, u, u, u, c_row, u)


def attn_bwd(u, do, c_col, lse_row, delta_row, *, name):
    S = u.shape[0]
    T = _tile(S, TILES["att"])
    n = S // T
    qo, ko, vo = (OFF[k] // FOX_HD for k in ("q", "k", "v"))
    scale = FOX_HD ** -0.5
    it, jt = _tri_pairs(n, by_key=True)

    def body(it_ref, jt_ref, q_ref, k_ref, v_ref, do_ref, ck_ref, lse_ref, dl_ref, dq_ref, dk_ref, dv_ref, dc_ref,
             dka_ref, dva_ref, dca_ref):
        i, j = it_ref[pl.program_id(1)], jt_ref[pl.program_id(1)]

        @pl.when(pl.program_id(1) == 0)
        def _():
            dq_ref[...] = jnp.zeros((S, FOX_HD), F32)

        @pl.when(i == j)
        def _():
            dka_ref[...] = jnp.zeros((T, FOX_HD), F32)
            dva_ref[...] = jnp.zeros((T, FOX_HD), F32)
            dca_ref[...] = jnp.zeros((T, 1), F32)

        def step(masked):
            qb, dob = (q_ref[...] * scale).astype(BF16), do_ref[...]
            kf = k_ref[...]
            st = lax.dot_general(kf.astype(BF16), qb, _NT, preferred_element_type=F32) - ck_ref[...]
            pt = jnp.exp(st - lse_ref[...])
            if masked:
                kpos = lax.broadcasted_iota(jnp.int32, (T, T), 0)
                qpos = lax.broadcasted_iota(jnp.int32, (T, T), 1)
                pt = jnp.where(kpos <= qpos, pt, 0.0)
            dva_ref[...] += jnp.dot(pt.astype(BF16), dob, preferred_element_type=F32)
            dpt = lax.dot_general(v_ref[...].astype(BF16), dob, _NT, preferred_element_type=F32)
            dst = pt * (dpt - dl_ref[...])
            dca_ref[...] -= jnp.sum(dst, axis=1, keepdims=True)
            dsb = dst.astype(BF16)
            dka_ref[...] += jnp.dot(dsb, qb, preferred_element_type=F32)
            rows = pl.ds(pl.multiple_of(i * T, T), T)
            dq_ref[rows, :] += lax.dot_general(dsb, (kf * scale).astype(BF16), _TN, preferred_element_type=F32)

        @pl.when(i > j)
        def _():
            step(False)

        @pl.when(i == j)
        def _():
            step(True)

        @pl.when(i == n - 1)
        def _():
            dk_ref[...] = dka_ref[...].astype(BF16)
            dv_ref[...] = dva_ref[...].astype(BF16)
            dc_ref[...] = dca_ref[...]

    def qsec(off):
        return pl.BlockSpec((T, FOX_HD), lambda h, p, it, jt: (it[p], off + h))

    def ksec(off):
        return pl.BlockSpec((T, FOX_HD), lambda h, p, it, jt: (jt[p], off + h))

    qrow = pl.BlockSpec((None, 1, T), lambda h, p, it, jt: (h, 0, it[p]))
    kout = pl.BlockSpec((T, FOX_HD), lambda h, p, it, jt: (jt[p], h))
    kcol = pl.BlockSpec((None, T, 1), lambda h, p, it, jt: (h, jt[p], 0))
    return pl.pallas_call(
        body, name=name,
        grid_spec=pltpu.PrefetchScalarGridSpec(
            num_scalar_prefetch=2, grid=(FOX_HEADS, n * (n + 1) // 2),
            in_specs=[qsec(qo), ksec(ko), ksec(vo), pl.BlockSpec((T, FOX_HD), lambda h, p, it, jt: (it[p], h)),
                      kcol, qrow, qrow],
            out_specs=[pl.BlockSpec((S, FOX_HD), lambda h, p, it, jt: (0, h)), kout, kout, kcol],
            scratch_shapes=[pltpu.VMEM((T, FOX_HD), F32), pltpu.VMEM((T, FOX_HD), F32), pltpu.VMEM((T, 1), F32)]),
        out_shape=[jax.ShapeDtypeStruct((S, BW), F32), jax.ShapeDtypeStruct((S, BW), BF16),
                   jax.ShapeDtypeStruct((S, BW), BF16), jax.ShapeDtypeStruct((FOX_HEADS, S, 1), F32)],
        compiler_params=_cp(("parallel", "arbitrary")),
    )(it, jt, u, u, u, do, c_col, lse_row, delta_row)


N_PAIR = SSM_HEADS // 2
PAIRS_PER_GROUP = N_PAIR // SSM_G


def _sel_t():
    r = lax.broadcasted_iota(jnp.int32, (LANES, BW), 0)
    c = lax.broadcasted_iota(jnp.int32, (LANES, BW), 1)
    return (lax.shift_right_logical(c, 6) == r).astype(BF16)


def _sel():
    r = lax.broadcasted_iota(jnp.int32, (BW, LANES), 0)
    c = lax.broadcasted_iota(jnp.int32, (BW, LANES), 1)
    return (lax.shift_right_logical(r, 6) == c).astype(BF16)


def _dot3(x, m):
    hi = x.astype(BF16)
    r1 = x - hi.astype(F32)
    mid = r1.astype(BF16)
    lo = (r1 - mid.astype(F32)).astype(BF16)
    d = functools.partial(jnp.dot, preferred_element_type=F32)
    return d(hi, m) + d(mid, m) + d(lo, m)


def _ssd_common(x_ref, sp_ref, al_ref, buf_ref, big_ref, cst_ref, LC):
    pre = x_ref[...]
    sg = _sig(pre)
    act = pre * sg
    dt = sp_ref[...]
    a = -jnp.exp(al_ref[...])
    cs = _shift_scan(dt * a, buf_ref, LC)
    sel_t = _sel_t()
    dtl = _dot3(dt, sel_t)
    csl = _dot3(cs, sel_t)
    big_ref[...] = csl
    csl_last = big_ref[pl.ds(LC - 1, 1), :]
    cst_ref[...] = cs.T
    return pre, sg, act, dt, a, cs, dtl, csl, csl_last


def ssd_fwd(xbc, sp, alog, dskip_l, *, name):
    S = xbc.shape[0]
    LC = _tile(S, TILES["ssd"])
    nc = S // LC

    def body(x_ref, sp_ref, al_ref, dk_ref, y_ref, hs_ref, st_ref, buf_ref, big_ref, cst_ref):
        @pl.when(pl.program_id(0) == 0)
        def _():
            st_ref[...] = jnp.zeros((N_PAIR, SSM_N, LANES), F32)

        pre, sg, act, dt, a, cs, dtl, csl, csl_last = _ssd_common(x_ref, sp_ref, al_ref, buf_ref, big_ref, cst_ref, LC)
        xs, bm, cm = act[:, :BW], act[:, BW:BW + SSM_G * SSM_N], act[:, BW + SSM_G * SSM_N:]
        e_all = jnp.exp(csl)
        dec = jnp.exp(csl_last - csl)
        ad = jnp.exp(csl_last)
        xd = xs * dtl
        tril = lax.broadcasted_iota(jnp.int32, (LC, LC), 0) >= lax.broadcasted_iota(jnp.int32, (LC, LC), 1)
        lane = lax.broadcasted_iota(jnp.int32, (LC, LANES), 1)
        for g in range(SSM_G):
            bgt = bm[:, g * SSM_N:(g + 1) * SSM_N].T.astype(BF16)
            cgb = cm[:, g * SSM_N:(g + 1) * SSM_N].astype(BF16)
            cb = jnp.dot(cgb, bgt, preferred_element_type=F32)
            for q in range(PAIRS_PER_GROUP):
                pp = g * PAIRS_PER_GROUP + q
                ln = slice(pp * LANES, (pp + 1) * LANES)
                xp = xd[:, ln]
                xpb = xp.astype(BF16)
                yh = []
                for hh in range(2):
                    row_b = jnp.broadcast_to(cst_ref[pl.ds(2 * pp + hh, 1), :], (LC, LC))
                    lmat = jnp.exp(jnp.where(tril, row_b.T - row_b, NEG))
                    yh.append(jnp.dot((cb * lmat).astype(BF16), xpb, preferred_element_type=F32))
                hin = st_ref[pp]
                hs_ref[pp] = hin
                yoff = jnp.dot(cgb, hin.astype(BF16), preferred_element_type=F32) * e_all[:, ln]
                y_ref[:, ln] = jnp.where(lane < SSM_P, yh[0], yh[1]) + yoff + xs[:, ln] * dk_ref[:, ln]
                st_ref[pp] = hin * ad[:, ln] + jnp.dot(bgt, (xp * dec[:, ln]).astype(BF16), preferred_element_type=F32)

    return pl.pallas_call(
        body, name=name, grid=(nc,),
        in_specs=[pl.BlockSpec((LC, SSM_CONV_DIM), lambda c: (c, 0)), pl.BlockSpec((LC, LANES), lambda c: (c, 0)),
                  pl.BlockSpec((1, LANES), lambda c: (0, 0)), pl.BlockSpec((1, BW), lambda c: (0, 0))],
        out_specs=[pl.BlockSpec((LC, BW), lambda c: (c, 0)), pl.BlockSpec((None, N_PAIR, SSM_N, LANES), lambda c: (c, 0, 0, 0))],
        out_shape=[jax.ShapeDtypeStruct((S, BW), F32), jax.ShapeDtypeStruct((nc, N_PAIR, SSM_N, LANES), F32)],
        scratch_shapes=[pltpu.VMEM((N_PAIR, SSM_N, LANES), F32), pltpu.VMEM((3 * LC, LANES), F32),
                        pltpu.VMEM((LC, BW), F32), pltpu.VMEM((LANES, LC), F32)],
        compiler_params=_cp(("arbitrary",)),
    )(xbc, sp, alog, dskip_l)


def ssd_bwd(dy, xbc, sp, alog, dskip_l, hs, *, name):
    S = xbc.shape[0]
    LC = _tile(S, TILES["ssd"])
    nc = S // LC
    GN = SSM_G * SSM_N

    def body(dy_ref, x_ref, sp_ref, al_ref, dk_ref, hs_ref, dx_ref, ddt_ref, da_ref, dd_ref,
             dh_ref, buf_ref, big_ref, cst_ref, gcs_ref, dxd_ref):
        @pl.when(pl.program_id(0) == 0)
        def _():
            dh_ref[...] = jnp.zeros((N_PAIR, SSM_N, LANES), F32)
            da_ref[...] = jnp.zeros((1, LANES), F32)
            dd_ref[...] = jnp.zeros((1, BW), F32)

        pre, sg, act, dt, a, cs, dtl, csl, csl_last = _ssd_common(x_ref, sp_ref, al_ref, buf_ref, big_ref, cst_ref, LC)
        dact = sg * (1.0 + pre * (1.0 - sg))
        xs, bm, cm = act[:, :BW], act[:, BW:BW + GN], act[:, BW + GN:]
        e_all = jnp.exp(csl)
        dec = jnp.exp(csl_last - csl)
        ad = jnp.exp(csl_last)
        xd = xs * dtl
        d_y = dy_ref[...]
        ri = lax.broadcasted_iota(jnp.int32, (LC, LC), 0)
        ci = lax.broadcasted_iota(jnp.int32, (LC, LC), 1)
        tril, triu = ri >= ci, ci >= ri
        lane = lax.broadcasted_iota(jnp.int32, (LC, LANES), 1)
        rowi = lax.broadcasted_iota(jnp.int32, (LC, LANES), 0)
        dot = functools.partial(jnp.dot, preferred_element_type=F32)
        dot_nt = functools.partial(lax.dot_general, dimension_numbers=_NT, preferred_element_type=F32)
        for g in range(SSM_G):
            gs = slice(g * SSM_N, (g + 1) * SSM_N)
            bg, cg = bm[:, gs], cm[:, gs]
            bgb, cgb = bg.astype(BF16), cg.astype(BF16)
            bgt, cgt = bg.T.astype(BF16), cg.T.astype(BF16)
            cb, cbt = dot(cgb, bgt), dot(bgb, cgt)
            dcb = jnp.zeros((LC, LC), F32)
            dcbt = jnp.zeros((LC, LC), F32)
            dcg = jnp.zeros((LC, SSM_N), F32)
            dbg = jnp.zeros((LC, SSM_N), F32)
            for q in range(PAIRS_PER_GROUP):
                pp = g * PAIRS_PER_GROUP + q
                ln = slice(pp * LANES, (pp + 1) * LANES)
                xp, dyp, ep, decp, adp = xd[:, ln], d_y[:, ln], e_all[:, ln], dec[:, ln], ad[:, ln]
                xpb, dypb = xp.astype(BF16), dyp.astype(BF16)
                hin, dho = hs_ref[pp], dh_ref[pp]
                hb, dhob = hin.astype(BF16), dho.astype(BF16)
                yoff = dot(cgb, hb) * ep
                dgb = (dyp * ep).astype(BF16)
                dh_ref[pp] = dho * adp + dot(cgt, dgb)
                dcg = dcg + dot_nt(dgb, hb)
                zf = xp * decp
                d_z = dot(bgb, dhob)
                dbg = dbg + dot_nt(zf.astype(BF16), dhob)
                dzz = d_z * zf
                last = jnp.sum(dzz, axis=0, keepdims=True) + jnp.sum(dho * hin, axis=0, keepdims=True) * adp
                gcs = dyp * yoff - dzz + jnp.where(rowi == LC - 1, last, 0.0)
                dxd = d_z * decp
                for hh in range(2):
                    row_b = jnp.broadcast_to(cst_ref[pl.ds(2 * pp + hh, 1), :], (LC, LC))
                    col_b = row_b.T
                    lmat = jnp.exp(jnp.where(tril, col_b - row_b, NEG))
                    lmat_t = jnp.exp(jnp.where(triu, row_b - col_b, NEG))
                    hm = (lane < SSM_P) if hh == 0 else (lane >= SSM_P)
                    dml = dot_nt(jnp.where(hm, dyp, 0.0).astype(BF16), xpb) * lmat
                    dmtl = dot_nt(jnp.where(hm, xp, 0.0).astype(BF16), dypb) * lmat_t
                    dcb = dcb + dml
                    dcbt = dcbt + dmtl
                    contrib = jnp.sum(dml * cb, axis=1, keepdims=True) - jnp.sum(dmtl * cbt, axis=1, keepdims=True)
                    gcs = gcs + jnp.where(lane == hh * SSM_P, contrib, 0.0)
                    dxd = dxd + jnp.where(hm, dot((cbt * lmat_t).astype(BF16), dypb), 0.0)
                gcs_ref[:, ln] = gcs
                dxd_ref[:, ln] = dxd
            dcg = dcg + dot(dcb.astype(BF16), bgb)
            dbg = dbg + dot(dcbt.astype(BF16), cgb)
            dx_ref[:, BW + g * SSM_N:BW + (g + 1) * SSM_N] = dbg * dact[:, BW + g * SSM_N:BW + (g + 1) * SSM_N]
            dx_ref[:, BW + GN + g * SSM_N:BW + GN + (g + 1) * SSM_N] = dcg * dact[:, BW + GN + g * SSM_N:BW + GN + (g + 1) * SSM_N]
        d_xd = dxd_ref[...]
        dx_ref[:, :BW] = (d_y * dk_ref[...] + d_xd * dtl) * dact[:, :BW]
        sel = _sel()
        dda = _shift_scan(_dot3(gcs_ref[...], sel), buf_ref, LC, reverse=True)
        ddt_ref[...] = _dot3(d_xd * xs, sel) + dda * a
        da_ref[...] += jnp.sum(dda * dt, axis=0, keepdims=True)
        dd_ref[...] += jnp.sum(d_y * xs, axis=0, keepdims=True)

    rev = lambda c: (nc - 1 - c, 0)
    return pl.pallas_call(
        body, name=name, grid=(nc,),
        in_specs=[pl.BlockSpec((LC, BW), rev), pl.BlockSpec((LC, SSM_CONV_DIM), rev), pl.BlockSpec((LC, LANES), rev),
                  pl.BlockSpec((1, LANES), lambda c: (0, 0)), pl.BlockSpec((1, BW), lambda c: (0, 0)),
                  pl.BlockSpec((None, N_PAIR, SSM_N, LANES), lambda c: (nc - 1 - c, 0, 0, 0))],
        out_specs=[pl.BlockSpec((LC, SSM_CONV_DIM), rev), pl.BlockSpec((LC, LANES), rev),
                   pl.BlockSpec((1, LANES), lambda c: (0, 0)), pl.BlockSpec((1, BW), lambda c: (0, 0))],
        out_shape=[jax.ShapeDtypeStruct((S, SSM_CONV_DIM), F32), jax.ShapeDtypeStruct((S, LANES), F32),
                   jax.ShapeDtypeStruct((1, LANES), F32), jax.ShapeDtypeStruct((1, BW), F32)],
        scratch_shapes=[pltpu.VMEM((N_PAIR, SSM_N, LANES), F32), pltpu.VMEM((3 * LC, LANES), F32),
                        pltpu.VMEM((LC, BW), F32), pltpu.VMEM((LANES, LC), F32),
                        pltpu.VMEM((LC, BW), F32), pltpu.VMEM((LC, BW), F32)],
        compiler_params=_cp(("arbitrary",)),
    )(dy, xbc, sp, alog, dskip_l, hs)


def addn(arrs, *, out_dtype, name):
    R, C = arrs[0].shape
    tr = _tile(R, max(8, (1 << 20) // C))

    def body(*refs):
        acc = refs[0][...].astype(F32)
        for r in refs[1:-1]:
            acc = acc + r[...].astype(F32)
        refs[-1][...] = acc.astype(out_dtype)

    blk = pl.BlockSpec((tr, C), lambda i: (i, 0))
    return pl.pallas_call(
        body, name=name, grid=(R // tr,), in_specs=[blk] * len(arrs), out_specs=blk,
        out_shape=jax.ShapeDtypeStruct((R, C), out_dtype), compiler_params=_cp(("parallel",)),
    )(*arrs)


def adamw(w, g, m, v, *, name):
    R, C = w.shape
    tr = _tile(R, max(8, (1 << 18) // C))
    c1, c2 = 1.0 / (1.0 - ADAM_B1 ** ADAM_STEP), 1.0 / (1.0 - ADAM_B2 ** ADAM_STEP)

    def body(w_ref, g_ref, m_ref, v_ref, d_ref, nm_ref, nv_ref):
        gv = g_ref[...]
        nm = ADAM_B1 * m_ref[...] + (1.0 - ADAM_B1) * gv
        nv = ADAM_B2 * v_ref[...] + (1.0 - ADAM_B2) * (gv * gv)
        nm_ref[...] = nm
        nv_ref[...] = nv
        d_ref[...] = -ADAM_LR * ((nm * c1) / (jnp.sqrt(nv * c2) + ADAM_EPS) + ADAM_WD * w_ref[...])

    blk = pl.BlockSpec((tr, C), lambda i: (i, 0))
    return pl.pallas_call(
        body, name=name, grid=(R // tr,), in_specs=[blk] * 4, out_specs=[blk] * 3,
        out_shape=[jax.ShapeDtypeStruct((R, C), F32)] * 3, compiler_params=_cp(("parallel",)),
    )(w, g, m, v)


_ANY = pl.BlockSpec(memory_space=pl.ANY)


def _place():
    return lax.axis_index("x"), lax.axis_index("y"), lax.axis_index("c")


def _rcopy(src, dst, sems_s, sems_r, k, to):
    return pltpu.make_async_remote_copy(src_ref=src, dst_ref=dst, send_sem=sems_s.at[k], recv_sem=sems_r.at[k],
                                        device_id=to, device_id_type=MESH)


def gather_chips(buf, *, name):
    _, _, R, C = buf.shape

    def body(in_ref, out_ref, ss, rs):
        del in_ref
        x, y, c = _place()
        chips = [(1 - x, y), (x, 1 - y), (1 - x, 1 - y)]
        me = 2 * x + y
        first = [_rcopy(out_ref.at[me, c], out_ref.at[me, c], ss, rs, j, (cx, cy, c)) for j, (cx, cy) in enumerate(chips)]
        for cp in first:
            cp.start()
        passed = []
        for j, (cx, cy) in enumerate(chips):
            blk = out_ref.at[2 * cx + cy, c]
            _rcopy(blk, blk, ss, rs, j, (x, y, c)).wait_recv()
            cp = _rcopy(blk, blk, ss, rs, 3 + j, (x, y, 1 - c))
            cp.start()
            passed.append(cp)
        for j, (cx, cy) in enumerate(chips):
            blk = out_ref.at[2 * cx + cy, 1 - c]
            _rcopy(blk, blk, ss, rs, 3 + j, (x, y, c)).wait_recv()
        for cp in first + passed:
            cp.wait_send()

    return pl.pallas_call(
        body, name=name, in_specs=[_ANY], out_specs=_ANY, out_shape=jax.ShapeDtypeStruct(buf.shape, buf.dtype),
        input_output_aliases={0: 0},
        scratch_shapes=[pltpu.SemaphoreType.DMA((6,)), pltpu.SemaphoreType.DMA((6,))],
    )(buf)


def gather_all(block, *, name):
    R, C = block.shape
    flips = [(fx, fy, fc) for fx in (0, 1) for fy in (0, 1) for fc in (0, 1)][1:]

    def body(in_ref, out_ref, ss, rs, ls):
        x, y, c = _place()
        me = 4 * x + 2 * y + c
        mine = pltpu.make_async_copy(in_ref, out_ref.at[me], ls)
        mine.start()
        sends = []
        for k, (fx, fy, fc) in enumerate(flips):
            px, py, pc = x ^ fx, y ^ fy, c ^ fc
            cp = _rcopy(in_ref, out_ref.at[me], ss, rs, k, (px, py, pc))
            cp.start()
            sends.append(cp)
        for k, (fx, fy, fc) in enumerate(flips):
            blk = out_ref.at[4 * (x ^ fx) + 2 * (y ^ fy) + (c ^ fc)]
            _rcopy(blk, blk, ss, rs, k, (x, y, c)).wait_recv()
        for cp in sends:
            cp.wait_send()
        mine.wait()

    return pl.pallas_call(
        body, name=name, in_specs=[_ANY], out_specs=_ANY, out_shape=jax.ShapeDtypeStruct((8, R, C), block.dtype),
        scratch_shapes=[pltpu.SemaphoreType.DMA((7,)), pltpu.SemaphoreType.DMA((7,)), pltpu.SemaphoreType.DMA],
    )(block)


def swap_partials(g_all, *, name):
    N, _, R, C = g_all.shape

    def body(in_ref, out_ref, ss, rs):
        x, y, c = _place()
        cps = [_rcopy(in_ref.at[k, 1 - c], out_ref.at[k], ss, rs, k, (x, y, 1 - c)) for k in range(N)]
        for cp in cps:
            cp.start()
        for cp in cps:
            cp.wait_recv()
        for cp in cps:
            cp.wait_send()

    return pl.pallas_call(
        body, name=name, in_specs=[_ANY], out_specs=_ANY, out_shape=jax.ShapeDtypeStruct((N, R, C), g_all.dtype),
        scratch_shapes=[pltpu.SemaphoreType.DMA((N,)), pltpu.SemaphoreType.DMA((N,))],
    )(g_all)


def share_halves(buf, *, name):
    def body(in_ref, out_ref, ss, rs):
        del in_ref
        x, y, c = _place()
        cp = _rcopy(out_ref.at[c], out_ref.at[c], ss, rs, 0, (x, y, 1 - c))
        cp.start()
        _rcopy(out_ref.at[1 - c], out_ref.at[1 - c], ss, rs, 0, (x, y, c)).wait_recv()
        cp.wait_send()

    return pl.pallas_call(
        body, name=name, in_specs=[_ANY], out_specs=_ANY, out_shape=jax.ShapeDtypeStruct(buf.shape, buf.dtype),
        input_output_aliases={0: 0},
        scratch_shapes=[pltpu.SemaphoreType.DMA((1,)), pltpu.SemaphoreType.DMA((1,))],
    )(buf)


def add_sibling(g_all, sib, c_arr, *, name):
    N, _, R, C = g_all.shape
    tr = _tile(R, max(8, (1 << 20) // C))

    def body(c_ref, a_ref, b_ref, o_ref):
        o_ref[...] = (a_ref[...].astype(F32) + b_ref[...].astype(F32)).astype(BF16)

    blk = pl.BlockSpec((None, tr, C), lambda k, r, c_ref: (k, r, 0))
    return pl.pallas_call(
        body, name=name,
        grid_spec=pltpu.PrefetchScalarGridSpec(
            num_scalar_prefetch=1, grid=(N, R // tr),
            in_specs=[pl.BlockSpec((None, None, tr, C), lambda k, r, c_ref: (k, c_ref[0], r, 0)), blk], out_specs=blk),
        out_shape=jax.ShapeDtypeStruct((N, R, C), BF16), compiler_params=_cp(("parallel", "parallel")),
    )(c_arr, g_all, sib)


def add_chips(part, rcv, place_arr, *, name):
    _, R, C = part.shape
    tr = _tile(R, max(8, (1 << 20) // C))

    def body(place_ref, p_ref, r0, r1, r2, o_ref):
        acc = p_ref[...].astype(F32)
        for r in (r0, r1, r2):
            acc = acc + r[...].astype(F32)
        o_ref[...] = acc

    def slot(j):
        return pl.BlockSpec((None, tr, C), lambda r, place_ref, j=j: (j, r, 0))

    return pl.pallas_call(
        body, name=name,
        grid_spec=pltpu.PrefetchScalarGridSpec(
            num_scalar_prefetch=1, grid=(R // tr,),
            in_specs=[pl.BlockSpec((None, tr, C), lambda r, place_ref: (place_ref[0], r, 0)), slot(0), slot(1), slot(2)],
            out_specs=pl.BlockSpec((None, tr, C), lambda r, place_ref: (place_ref[1], r, 0))),
        out_shape=jax.ShapeDtypeStruct((2, R, C), F32), compiler_params=_cp(("parallel",)),
    )(place_arr, part, rcv, rcv, rcv)


def scatter_chips(parts, *, name):
    _, R, C = parts.shape

    def body(in_ref, out_ref, ss, rs):
        x, y, c = _place()
        chips = [(1 - x, y), (x, 1 - y), (1 - x, 1 - y)]
        cps = [_rcopy(in_ref.at[2 * cx + cy], out_ref.at[j], ss, rs, j, (cx, cy, c)) for j, (cx, cy) in enumerate(chips)]
        for cp in cps:
            cp.start()
        for cp in cps:
            cp.wait_recv()
        for cp in cps:
            cp.wait_send()

    return pl.pallas_call(
        body, name=name, in_specs=[_ANY], out_specs=_ANY, out_shape=jax.ShapeDtypeStruct((3, R, C), parts.dtype),
        scratch_shapes=[pltpu.SemaphoreType.DMA((3,)), pltpu.SemaphoreType.DMA((3,))],
    )(parts)


WEIGHTS = ("norm_w", "w_in", "fg_bias", "ssm_conv_w", "ssm_conv_b", "dt_bias", "a_log", "d_skip", "ssm_norm_w", "sc_conv_w",
           "sc_conv_b", "cf_conv_w", "cf_conv_b", "cf_ln_w", "cf_ln_b", "w_gate", "b_gate", "w_branch", "w_out", "final_norm_w")
BIG = ("w_in", "w_gate", "w_branch", "w_out")
SMALL = tuple(n for n in WEIGHTS if n not in BIG)
SMALL_SHARDED = ("ssm_conv_w", "sc_conv_w", "cf_conv_w", "b_gate")
N_CHIP = 4
PACK_C = D_MODEL
PACK_ROWS = 1024


def _pack(arrs, cols, row_mult):
    return _pack_groups([arrs], cols, row_mult)[0]


def _pack_groups(groups, cols, row_mult):
    n = sum(math.prod(a.shape) for a in groups[0])
    rows = -(-n // cols)
    rows = -(-rows // row_mult) * row_mult
    parts = []
    for arrs in groups:
        parts += [a.reshape(-1) for a in arrs]
        if rows * cols > n:
            parts.append(jnp.zeros((rows * cols - n,), arrs[0].dtype))
    return jnp.concatenate(parts).reshape(len(groups), rows, cols)


PIECE_ROWS = 16


def _pack_rows(groups, cols, row_mult):
    as_list = lambda a: list(a) if isinstance(a, (list, tuple)) else [a]
    n_rows = lambda a: sum(math.prod(b.shape) // cols for b in as_list(a))
    rows = sum(-(-n_rows(a) // PIECE_ROWS) * PIECE_ROWS for a in groups[0])
    total = -(-rows // row_mult) * row_mult
    dtype = as_list(groups[0][0])[0].dtype
    parts = []
    for arrs in groups:
        for a in arrs:
            parts += [b.reshape(-1, cols) for b in as_list(a)]
            pad = -n_rows(a) % PIECE_ROWS
            if pad:
                parts.append(jnp.zeros((pad, cols), dtype))
        if total > rows:
            parts.append(jnp.zeros((total - rows, cols), dtype))
    return jnp.concatenate(parts, axis=0).reshape(len(groups), total, cols)


def _unpack_rows(packed, shapes):
    cols = packed.shape[-1]
    out, o = [], 0
    for s in shapes:
        r = math.prod(s) // cols
        out.append(packed[o:o + r].reshape(s))
        o += -(-r // PIECE_ROWS) * PIECE_ROWS
    return out


def _unpack(packed, shapes):
    flat = packed.reshape(-1)
    out, o = [], 0
    for s in shapes:
        n = math.prod(s)
        out.append(flat[o:o + n].reshape(s))
        o += n
    return out


def _orig_cols():
    cols = []
    for n, s in zip(ORIG_NAMES, ORIG_SIZES):
        if n == "dt":
            cols.append((OFF["small"] + DT_LANE, s))
        elif n == "f":
            cols.append((OFF["small"] + F_LANE, s))
        else:
            cols.append((OFF[n], s))
    return cols


def _rows_to_padded(per_chip):
    q_in = N_IN // N_CHIP
    orig_off = dict(zip(ORIG_NAMES, [sum(ORIG_SIZES[:i]) for i in range(len(ORIG_SIZES))]))
    size = dict(zip(ORIG_NAMES, ORIG_SIZES))
    parts = []
    for n in PAD_ORDER + ("dt", "f"):
        lo, hi = orig_off[n], orig_off[n] + size[n]
        for k in range(N_CHIP):
            a, b = max(lo, k * q_in), min(hi, (k + 1) * q_in)
            if a < b:
                parts.append(per_chip[k][a - k * q_in:b - k * q_in])
    parts.append(jnp.zeros((SMALL_W - size["dt"] - size["f"], per_chip[0].shape[1]), per_chip[0].dtype))
    return jnp.concatenate(parts, axis=0)


def _rows_from_padded(w_pad, lo, hi):
    parts, o = [], 0
    for start, s in _orig_cols():
        a, b = max(lo, o), min(hi, o + s)
        if a < b:
            parts.append(w_pad[start + a - o:start + b - o])
        o += s
    return parts


def _lanes_row(parts, width=LANES):
    v = jnp.concatenate([p.reshape(-1) for p in parts])
    return jnp.pad(v, (0, width - v.shape[0])).reshape(1, width)


def kernel(x, norm_w, w_in, fg_bias, ssm_conv_w, ssm_conv_b, dt_bias, a_log, d_skip, ssm_norm_w, sc_conv_w, sc_conv_b, cf_conv_w, cf_conv_b, cf_ln_w, cf_ln_b, w_gate, b_gate, w_branch, w_out, final_norm_w, loss_target, m_norm_w, m_w_in, m_fg_bias, m_ssm_conv_w, m_ssm_conv_b, m_dt_bias, m_a_log, m_d_skip, m_ssm_norm_w, m_sc_conv_w, m_sc_conv_b, m_cf_conv_w, m_cf_conv_b, m_cf_ln_w, m_cf_ln_b, m_w_gate, m_b_gate, m_w_branch, m_w_out, m_final_norm_w, v_norm_w, v_w_in, v_fg_bias, v_ssm_conv_w, v_ssm_conv_b, v_dt_bias, v_a_log, v_d_skip, v_ssm_norm_w, v_sc_conv_w, v_sc_conv_b, v_cf_conv_w, v_cf_conv_b, v_cf_ln_w, v_cf_ln_b, v_w_gate, v_b_gate, v_w_branch, v_w_out, v_final_norm_w):
    wts = dict(norm_w=norm_w, w_in=w_in, fg_bias=fg_bias, ssm_conv_w=ssm_conv_w, ssm_conv_b=ssm_conv_b, dt_bias=dt_bias,
               a_log=a_log, d_skip=d_skip, ssm_norm_w=ssm_norm_w, sc_conv_w=sc_conv_w, sc_conv_b=sc_conv_b,
               cf_conv_w=cf_conv_w, cf_conv_b=cf_conv_b, cf_ln_w=cf_ln_w, cf_ln_b=cf_ln_b, w_gate=w_gate, b_gate=b_gate,
               w_branch=w_branch, w_out=w_out, final_norm_w=final_norm_w)
    mom = dict(norm_w=m_norm_w, w_in=m_w_in, fg_bias=m_fg_bias, ssm_conv_w=m_ssm_conv_w, ssm_conv_b=m_ssm_conv_b,
               dt_bias=m_dt_bias, a_log=m_a_log, d_skip=m_d_skip, ssm_norm_w=m_ssm_norm_w, sc_conv_w=m_sc_conv_w,
               sc_conv_b=m_sc_conv_b, cf_conv_w=m_cf_conv_w, cf_conv_b=m_cf_conv_b, cf_ln_w=m_cf_ln_w, cf_ln_b=m_cf_ln_b,
               w_gate=m_w_gate, b_gate=m_b_gate, w_branch=m_w_branch, w_out=m_w_out, final_norm_w=m_final_norm_w)
    vel = dict(norm_w=v_norm_w, w_in=v_w_in, fg_bias=v_fg_bias, ssm_conv_w=v_ssm_conv_w, ssm_conv_b=v_ssm_conv_b,
               dt_bias=v_dt_bias, a_log=v_a_log, d_skip=v_d_skip, ssm_norm_w=v_ssm_norm_w, sc_conv_w=v_sc_conv_w,
               sc_conv_b=v_sc_conv_b, cf_conv_w=v_cf_conv_w, cf_conv_b=v_cf_conv_b, cf_ln_w=v_cf_ln_w, cf_ln_b=v_cf_ln_b,
               w_gate=v_w_gate, b_gate=v_b_gate, w_branch=v_w_branch, w_out=v_w_out, final_norm_w=v_final_norm_w)
    L = norm_w.shape[0]
    S, D = x.shape[1], x.shape[2]
    assert D == D_MODEL and x.shape[0] == 1
    xi, yi, ci = _place()
    chip = 2 * xi + yi

    sh_shapes = [wts[n].shape for n in SMALL_SHARDED]
    got = gather_all(_pack([wts[n] for n in SMALL_SHARDED], LANES, 8), name="gather_small_w")
    per_chip = [_unpack(got[2 * k], sh_shapes) for k in range(N_CHIP)]
    full_small = {n: jnp.concatenate([per_chip[k][i] for k in range(N_CHIP)], axis=-1) for i, n in enumerate(SMALL_SHARDED)}

    q_in, q_d = N_IN // N_CHIP, D // N_CHIP
    sent_shapes = [(q_in, D), (N_BRANCH, q_d, D), (N_BRANCH, BW, q_d), (q_d, D)]
    lw = []
    for l in range(L):
        packed = _pack_rows([[w_in[l].T.astype(BF16), w_gate[l].astype(BF16), w_branch[l].astype(BF16),
                              w_out[l].astype(BF16)]], PACK_C, PACK_ROWS)[0]
        half_rows = packed.shape[0] // 2
        buf = lax.dynamic_update_slice(lax.empty((N_CHIP, 2, half_rows, PACK_C), BF16),
                                       packed.reshape(1, 2, half_rows, PACK_C), (chip, 0, 0, 0))
        got = gather_chips(buf, name="gather_w")
        pc = [_unpack_rows(got[k].reshape(2 * half_rows, PACK_C), sent_shapes) for k in range(N_CHIP)]
        lw.append(dict(wpt=_rows_to_padded([pc[k][0] for k in range(N_CHIP)]),
                       wg=jnp.concatenate([pc[k][1] for k in range(N_CHIP)], axis=1),
                       wb=jnp.concatenate([pc[k][2] for k in range(N_CHIP)], axis=2),
                       wo=jnp.concatenate([pc[k][3] for k in range(N_CHIP)], axis=0)))

    saved = []
    xl = x[0]
    for l in range(L):
        w = lw[l]
        h = rms_fwd(xl, norm_w[l][None], name="rms_fwd")
        u = mm(h, w["wpt"], tb=True, name="mm_in")
        bias_small = _lanes_row([dt_bias[l], fg_bias[l]])
        sp, csum = small_fwd(u, bias_small, name="small_fwd")
        c8 = csum[:, F_LANE:F_LANE + FOX_HEADS].T
        c_col, c_row = c8[:, :, None], c8[:, None, :]
        o, lse, y_a = attn_fwd(u, c_row, name="attn_fwd")
        xbc = conv_fwd(u, full_small["ssm_conv_w"][l], ssm_conv_b[l][None], mode="plain", a_off=OFF["xbc"], b_off=0,
                       C=SSM_CONV_DIM, name="conv_ssm_fwd")
        alog_row = _lanes_row([a_log[l]])
        dskip_l = jnp.repeat(d_skip[l], SSM_P)[None]
        y_ssd, hs = ssd_fwd(xbc, sp, alog_row, dskip_l, name="ssd_fwd")
        cv_c = conv_fwd(u, full_small["sc_conv_w"][l], sc_conv_b[l][None], mode="mul", a_off=OFF["scc"], b_off=OFF["scx"],
                        C=BW, name="conv_sc_fwd")
        cv_d = conv_fwd(u, full_small["cf_conv_w"][l], cf_conv_b[l][None], mode="glu", a_off=OFF["glu"], b_off=OFF["glu"] + BW,
                        C=BW, name="conv_cf_fwd")
        y_b, y_c, y_d = post_fwd(u, y_ssd, cv_c, cv_d, ssm_norm_w[l][None], cf_ln_w[l][None], cf_ln_b[l][None], name="post_fwd")
        merged, gates, proj = merge_fwd(h, (y_a, y_b, y_c, y_d), w["wg"], full_small["b_gate"][l][:, None, :], w["wb"],
                                        name="merge_fwd")
        x_next = mm(merged, w["wo"], add=xl, name="mm_out")
        saved.append(dict(x=xl, h=h, u=u, bias_small=bias_small, sp=sp, c_col=c_col, c_row=c_row, o=o, lse=lse, xbc=xbc,
                          alog_row=alog_row, dskip_l=dskip_l, hs=hs, y_ssd=y_ssd, cv_c=cv_c, cv_d=cv_d,
                          ys=(y_a, y_b, y_c, y_d), merged=merged, gates=gates, proj=proj))
        xl = x_next

    sq, dx, d_final = loss_head(xl, final_norm_w[None], loss_target[0], name="loss_head")
    loss = lax.psum(sq[0, 0] * (0.5 / D), ("x", "y", "c"))

    small_g = {n: [None] * L for n in SMALL if n != "final_norm_w"}
    big_g = {n: [None] * L for n in BIG}
    for l in reversed(range(L)):
        w, sv = lw[l], saved[l]
        u, h = sv["u"], sv["h"]
        dm = mm(dx, w["wo"], tb=True, name="mm_dmerged")
        d_wo = mm(sv["merged"], dx, ta=True, out_dtype=BF16, name="mm_dwo")
        dp, dg, dbg = merge_bwd(dm, sv["gates"], sv["proj"], name="merge_bwd")
        dys = [mm(dp, w["wb"][i], tb=True, K=D, a_koff=i * D, name="mm_dy") for i in range(N_BRANCH)]
        d_wbt = [mm(dp, sv["ys"][i], ta=True, M=D, a_moff=i * D, out_dtype=BF16, name="mm_dwb")
                 for i in range(N_BRANCH)]
        d_wg = mm(h, dg, ta=True, out_dtype=BF16, name="mm_dwg")
        (do, delta, dga, dz, dscb, dgc, dgd, dy_ssd, dcv_c, dcv_d, dnw, dlnw, dlnb) = post_bwd(
            dys[0], dys[1], dys[2], dys[3], u, sv["o"], sv["y_ssd"], sv["cv_c"], sv["cv_d"],
            ssm_norm_w[l][None], cf_ln_w[l][None], cf_ln_b[l][None], name="post_bwd")
        delta_row = delta[:, :FOX_HEADS].T[:, None, :]
        lse_row = jnp.transpose(sv["lse"], (0, 2, 1))
        dq, dk, dv, dc_col = attn_bwd(u, do, sv["c_col"], lse_row, delta_row, name="attn_bwd")
        dscc, dscx, d_scw, d_scb = conv_bwd(dcv_c, u, full_small["sc_conv_w"][l], mode="mul", a_off=OFF["scc"],
                                            b_off=OFF["scx"], C=BW, name="conv_sc_bwd")
        dglua, dglug, d_cfw, d_cfb = conv_bwd(dcv_d, u, full_small["cf_conv_w"][l], mode="glu", a_off=OFF["glu"],
                                              b_off=OFF["glu"] + BW, C=BW, name="conv_cf_bwd")
        dxbc_pre, ddt, d_a, d_dl = ssd_bwd(dy_ssd, sv["xbc"], sv["sp"], sv["alog_row"], sv["dskip_l"], sv["hs"], name="ssd_bwd")
        dxbc, d_ssmw, d_ssmb = conv_bwd(dxbc_pre, u, full_small["ssm_conv_w"][l], mode="plain", a_off=OFF["xbc"], b_off=0,
                                        C=SSM_CONV_DIM, name="conv_ssm_bwd")
        dc_full = jnp.pad(dc_col[:, :, 0].T, ((0, 0), (F_LANE, LANES - F_LANE - FOX_HEADS)))
        du_small, dbias_small = small_bwd(dc_full, ddt, u, sv["bias_small"], name="small_bwd")
        by_name = dict(q=dq.astype(BF16), k=dk, v=dv, ga=dga, z=dz, scb=dscb, scc=dscc, scx=dscx, gc=dgc, gd=dgd, xbc=dxbc)
        du = jnp.concatenate([jnp.concatenate([dglua, dglug], axis=1) if n == "glu" else by_name[n] for n in PAD_ORDER]
                             + [du_small], axis=1)
        dh = mm(du, w["wpt"], add=mm(dg, w["wg"], tb=True, name="mm_dh_gate"), name="mm_dh_in")
        d_wpt = mm(du, h, ta=True, out_dtype=BF16, name="mm_dwp")
        dx, d_nw = rms_bwd(dh, sv["x"], norm_w[l][None], dx, name="rms_bwd")

        a_neg = -jnp.exp(a_log[l])
        sg = dict(norm_w=d_nw[0], fg_bias=dbias_small[0, F_LANE:F_LANE + FOX_HEADS], ssm_conv_w=d_ssmw, ssm_conv_b=d_ssmb[0],
                  dt_bias=dbias_small[0, DT_LANE:DT_LANE + SSM_HEADS], a_log=d_a[0, :SSM_HEADS] * a_neg,
                  d_skip=d_dl.reshape(SSM_HEADS, SSM_P).sum(-1), ssm_norm_w=dnw[0], sc_conv_w=d_scw, sc_conv_b=d_scb[0],
                  cf_conv_w=d_cfw, cf_conv_b=d_cfb[0], cf_ln_w=dlnw[0], cf_ln_b=dlnb[0], b_gate=dbg.reshape(N_BRANCH, D))
        for n in sg:
            small_g[n][l] = sg[n]

        dest = [[_rows_from_padded(d_wpt, k * q_in, (k + 1) * q_in), d_wg[k * q_d:(k + 1) * q_d]]
                + [t[k * q_d:(k + 1) * q_d] for t in d_wbt] + [d_wo[k * q_d:(k + 1) * q_d]] for k in range(N_CHIP)]
        g_all = _pack_rows(dest, PACK_C, PACK_ROWS)
        R = g_all.shape[1] // 2
        g_all = g_all.reshape(N_CHIP, 2, R, PACK_C)
        sib = swap_partials(g_all, name="swap_partials")
        part = add_sibling(g_all, sib, ci.reshape(1).astype(jnp.int32), name="add_sibling")
        rcv = scatter_chips(part, name="scatter_partials")
        full = share_halves(add_chips(part, rcv, jnp.stack([chip, ci]).astype(jnp.int32), name="add_chips"), name="share_halves")
        g_in, g_wg, g_wb, g_wo = _unpack_rows(full.reshape(2 * R, PACK_C),
                                              [(q_in, D), (q_d, N_BRANCH * D), (N_BRANCH, q_d, BW), (q_d, D)])
        big_g["w_in"][l] = g_in.T
        big_g["w_gate"][l] = jnp.transpose(g_wg.reshape(q_d, N_BRANCH, D), (1, 0, 2))
        big_g["w_branch"][l] = jnp.transpose(g_wb, (0, 2, 1))
        big_g["w_out"][l] = g_wo

    names = [n for n in SMALL if n != "final_norm_w"]
    stacked = [jnp.stack(small_g[n]) for n in names] + [d_final[0]]
    shapes = [a.shape for a in stacked]
    got = gather_all(_pack(stacked, LANES, 8), name="gather_small_g")
    tot = addn([got[d] for d in range(8)], out_dtype=F32, name="add_small_g")
    grads = dict(zip(names + ["final_norm_w"], _unpack(tot, shapes)))
    for n in SMALL_SHARDED:
        sz = wts[n].shape[-1]
        grads[n] = lax.dynamic_slice_in_dim(grads[n], chip * sz, sz, axis=grads[n].ndim - 1)
    for n in BIG:
        grads[n] = jnp.stack(big_g[n])

    delta, new_m, new_v = {}, {}, {}
    for n in BIG:
        two_d = lambda a: a.reshape(-1, a.shape[-1])
        d, nm, nv = adamw(two_d(wts[n]), two_d(grads[n]), two_d(mom[n]), two_d(vel[n]), name="adamw_" + n)
        delta[n], new_m[n], new_v[n] = (t.reshape(wts[n].shape) for t in (d, nm, nv))
    small_shapes = [wts[n].shape for n in SMALL]
    pk = lambda src: _pack([src[n] for n in SMALL], LANES, 8)
    d, nm, nv = adamw(pk(wts), pk(grads), pk(mom), pk(vel), name="adamw_small")
    for tgt, src in ((delta, d), (new_m, nm), (new_v, nv)):
        for n, a in zip(SMALL, _unpack(src, small_shapes)):
            tgt[n] = a

    return (loss, dx[None], *[grads[n] for n in WEIGHTS], *[delta[n] for n in WEIGHTS],
            *[new_m[n] for n in WEIGHTS], *[new_v[n] for n in WEIGHTS])
```

```python
import functools
import math

import jax
import jax.numpy as jnp
from jax import lax
from jax.experimental import pallas as pl
from jax.experimental.pallas import tpu as pltpu

F32, BF16 = jnp.float32, jnp.bfloat16
MESH = pl.DeviceIdType.MESH

D_MODEL = 2048
BW = D_MODEL // 2
FOX_HEADS, FOX_HD = 8, 128
SSM_HEADS, SSM_P, SSM_N, SSM_G = 16, 64, 128, 2
SSM_CONV_DIM = BW + 2 * SSM_G * SSM_N
N_BRANCH = 4
EPS = 1e-6
NEG = -1e30
ORIG_SIZES = (BW, BW, BW, FOX_HEADS, BW, BW, SSM_CONV_DIM, SSM_HEADS, BW, BW, BW, BW, 2 * BW, BW)
ORIG_NAMES = ("q", "k", "v", "f", "ga", "z", "xbc", "dt", "scb", "scc", "scx", "gc", "glu", "gd")
N_IN = sum(ORIG_SIZES)
PAD_ORDER = ("q", "k", "v", "ga", "z", "scb", "scc", "scx", "gc", "glu", "gd", "xbc")
SMALL_W = 512
DT_LANE, F_LANE = 0, SSM_HEADS
OFF = {}
_o = 0
for _n in PAD_ORDER:
    OFF[_n] = _o
    _o += ORIG_SIZES[ORIG_NAMES.index(_n)]
OFF["small"] = _o
NP = _o + SMALL_W
LANES = 128
CONV_HALO = 32

VMEM_LIMIT = 56 * 1024 * 1024
MM_VMEM_BUDGET = 40 * 1024 * 1024

ADAM_LR, ADAM_B1, ADAM_B2, ADAM_EPS, ADAM_WD, ADAM_STEP = 0.001, 0.9, 0.999, 1e-08, 0.01, 10

TILES = dict(row=512, post=256, att=1024, ssd=256, mm_m=1024, mm_n=1024, conv_c=256)


def _cp(sem=None):
    return pltpu.CompilerParams(dimension_semantics=sem, vmem_limit_bytes=VMEM_LIMIT)


def _tile(n, pref):
    t = 1 << (min(n, pref).bit_length() - 1)
    while n % t:
        t //= 2
    return t


def _sig(x):
    return 1.0 / (1.0 + jnp.exp(-x))


def _silu(x):
    return x * _sig(x)


def _dsilu(x):
    s = _sig(x)
    return s * (1.0 + x * (1.0 - s))


def _softplus(x):
    return jnp.maximum(x, 0.0) + jnp.log(1.0 + jnp.exp(-jnp.abs(x)))


def _comm_copies(kind, src_ref, dst_ref, ss, rs):
    x, y, c = _place()
    chips = [(1 - x, y), (x, 1 - y), (1 - x, 1 - y)]
    me = 2 * x + y
    sends, recvs = [], []
    for j, (cx, cy) in enumerate(chips):
        if kind == "gather_ici":
            mine, theirs = dst_ref.at[me, c], dst_ref.at[2 * cx + cy, c]
            sends.append(_rcopy(mine, mine, ss, rs, j, (cx, cy, c)))
            recvs.append(_rcopy(theirs, theirs, ss, rs, j, (x, y, c)))
        elif kind == "gather_d2d":
            landed, theirs = dst_ref.at[2 * cx + cy, c], dst_ref.at[2 * cx + cy, 1 - c]
            sends.append(_rcopy(landed, landed, ss, rs, j, (x, y, 1 - c)))
            recvs.append(_rcopy(theirs, theirs, ss, rs, j, (x, y, c)))
        else:
            sends.append(_rcopy(src_ref.at[2 * cx + cy], dst_ref.at[j], ss, rs, j, (cx, cy, c)))
            recvs.append(_rcopy(src_ref.at[0], dst_ref.at[j], ss, rs, j, (x, y, c)))
    return sends, recvs


def mm(a, b, *, name, out_dtype=F32, add=None, ta=False, tb=False, M=None, K=None, N=None,
       a_koff=0, a_moff=0, b_koff=0, b_noff=0, comm=None):
    b3 = b.ndim == 3
    assert not b3 or (tb and b_koff == 0 and b_noff == 0)
    M = M or (a.shape[1] if ta else a.shape[0])
    K = K or (a.shape[0] if ta else a.shape[1])
    N = N or (b.shape[-2] if tb else b.shape[1])
    tm, tn = _tile(M, TILES["mm_m"]), _tile(N, TILES["mm_n"])
    sa, sb, so = a.dtype.itemsize, b.dtype.itemsize, jnp.dtype(out_dtype).itemsize

    def need(tk):
        return 2 * tm * tk * sa + 2 * tk * tn * sb + 2 * tm * tn * so + (8 * tm * tn if add is not None else 0) + 4 * tm * tn

    tk = b.shape[2] if b3 else K
    while need(tk) > MM_VMEM_BUDGET and tk % 256 == 0:
        tk //= 2
    assert K % tk == 0 and a_koff % tk == 0 and b_koff % tk == 0 and b_noff % tn == 0 and a_moff % tm == 0, (name, K, tk, tn)
    nk = K // tk
    ako, amo, bko, bno = a_koff // tk, a_moff // tm, b_koff // tk, b_noff // tn
    dims = (((0 if ta else 1,), (1 if tb else 0,)), ((), ()))

    n_in = 2 + (add is not None) + (comm is not None)
    grid = (M // tm, N // tn, nk)

    def body(*refs):
        a_ref, b_ref = refs[:2]
        add_ref = refs[2] if add is not None else None
        o_ref = refs[n_in]
        acc_ref = refs[n_in + 1 + (comm is not None)] if nk > 1 else None
        if comm is not None:
            step = (pl.program_id(0) * grid[1] + pl.program_id(1)) * grid[2] + pl.program_id(2)
            sends, recvs = _comm_copies(comm[0], refs[n_in - 1], refs[n_in + 1], refs[-2], refs[-1])

            @pl.when(step == 0)
            def _():
                for cp in sends:
                    cp.start()

        prod = lax.dot_general(a_ref[...].astype(BF16), b_ref[...].astype(BF16), dims, preferred_element_type=F32)

        def finish(acc):
            if add_ref is not None:
                acc = acc + add_ref[...]
            o_ref[...] = acc.astype(out_dtype)

        if nk == 1:
            finish(prod)
        else:
            k = pl.program_id(2)

            @pl.when(k == 0)
            def _():
                acc_ref[...] = prod

            @pl.when(k > 0)
            def _():
                acc_ref[...] += prod

            @pl.when(k == nk - 1)
            def _():
                finish(acc_ref[...])

        if comm is not None:
            @pl.when(step == grid[0] * grid[1] * grid[2] - 1)
            def _():
                for cp in recvs:
                    cp.wait_recv()
                for cp in sends:
                    cp.wait_send()

    if ta:
        a_spec = pl.BlockSpec((tk, tm), lambda i, j, k: (k + ako, i + amo))
    else:
        a_spec = pl.BlockSpec((tm, tk), lambda i, j, k: (i, k + ako))
    if b3:
        assert tk == b.shape[2], (name, tk)
        b_spec = pl.BlockSpec((None, tn, tk), lambda i, j, k: (k, j, 0))
    elif tb:
        b_spec = pl.BlockSpec((tn, tk), lambda i, j, k: (j + bno, k + bko))
    else:
        b_spec = pl.BlockSpec((tk, tn), lambda i, j, k: (k + bko, j + bno))
    in_specs = [a_spec, b_spec]
    args = [a, b]
    if add is not None:
        in_specs.append(pl.BlockSpec((tm, tn), lambda i, j, k: (i, j)))
        args.append(add)
    out_spec = pl.BlockSpec((tm, tn), lambda i, j, k: (i, j))
    out_shape = jax.ShapeDtypeStruct((M, N), out_dtype)
    scratch = [pltpu.VMEM((tm, tn), F32)] if nk > 1 else []
    if comm is None:
        return pl.pallas_call(
            body, name=name, grid=grid, in_specs=in_specs, out_specs=out_spec, out_shape=out_shape,
            scratch_shapes=scratch, compiler_params=_cp(("parallel", "parallel", "arbitrary")),
        )(*args)
    kind, buf = comm
    any_spec = pl.BlockSpec(memory_space=pl.ANY)
    if kind == "scatter":
        comm_shape, alias = jax.ShapeDtypeStruct((3,) + buf.shape[1:], buf.dtype), {}
    else:
        comm_shape, alias = jax.ShapeDtypeStruct(buf.shape, buf.dtype), {n_in - 1: 1}
    return pl.pallas_call(
        body, name=name, grid=grid, in_specs=in_specs + [any_spec], out_specs=[out_spec, any_spec],
        out_shape=[out_shape, comm_shape], input_output_aliases=alias,
        scratch_shapes=scratch + [pltpu.SemaphoreType.DMA((3,)), pltpu.SemaphoreType.DMA((3,))],
        compiler_params=_cp(("arbitrary", "arbitrary", "arbitrary")),
    )(*args, buf)


def rms_fwd(x, w, *, name):
    S, D = x.shape
    ts = _tile(S, TILES["row"])

    def body(x_ref, w_ref, o_ref):
        xf = x_ref[...]
        r = lax.rsqrt(jnp.mean(xf * xf, axis=-1, keepdims=True) + EPS)
        o_ref[...] = (xf * r * w_ref[...]).astype(BF16)

    return pl.pallas_call(
        body, name=name, grid=(S // ts,),
        in_specs=[pl.BlockSpec((ts, D), lambda i: (i, 0)), pl.BlockSpec((1, D), lambda i: (0, 0))],
        out_specs=pl.BlockSpec((ts, D), lambda i: (i, 0)),
        out_shape=jax.ShapeDtypeStruct((S, D), BF16), compiler_params=_cp(("parallel",)),
    )(x, w)


def rms_bwd(dh, x, w, dres, *, name):
    S, D = x.shape
    ts = _tile(S, TILES["row"])

    def body(dh_ref, x_ref, w_ref, dres_ref, dx_ref, dw_ref):
        xf = x_ref[...]
        r = lax.rsqrt(jnp.mean(xf * xf, axis=-1, keepdims=True) + EPS)
        xh = xf * r
        g = dh_ref[...]
        dxh = g * w_ref[...]
        dx_ref[...] = dres_ref[...] + r * (dxh - xh * jnp.mean(dxh * xh, axis=-1, keepdims=True))
        part = jnp.sum(g * xh, axis=0, keepdims=True)

        @pl.when(pl.program_id(0) == 0)
        def _():
            dw_ref[...] = part

        @pl.when(pl.program_id(0) > 0)
        def _():
            dw_ref[...] += part

    row = pl.BlockSpec((ts, D), lambda i: (i, 0))
    vec = pl.BlockSpec((1, D), lambda i: (0, 0))
    return pl.pallas_call(
        body, name=name, grid=(S // ts,), in_specs=[row, row, vec, row], out_specs=[row, vec],
        out_shape=[jax.ShapeDtypeStruct((S, D), F32), jax.ShapeDtypeStruct((1, D), F32)],
        compiler_params=_cp(("arbitrary",)),
    )(dh, x, w, dres)


def loss_head(x, w, target, *, name):
    S, D = x.shape
    ts = _tile(S, TILES["row"])

    def body(x_ref, w_ref, t_ref, loss_ref, dx_ref, dw_ref):
        xf = x_ref[...]
        r = lax.rsqrt(jnp.mean(xf * xf, axis=-1, keepdims=True) + EPS)
        xh = xf * r
        err = xh * w_ref[...] - t_ref[...]
        sq = jnp.sum(jnp.sum(err * err, axis=0, keepdims=True), axis=1, keepdims=True)
        dy = err * (1.0 / D)
        dxh = dy * w_ref[...]
        dx_ref[...] = r * (dxh - xh * jnp.mean(dxh * xh, axis=-1, keepdims=True))
        part = jnp.sum(dy * xh, axis=0, keepdims=True)

        @pl.when(pl.program_id(0) == 0)
        def _():
            dw_ref[...] = part
            loss_ref[...] = jnp.broadcast_to(sq, (8, LANES))

        @pl.when(pl.program_id(0) > 0)
        def _():
            dw_ref[...] += part
            loss_ref[...] += jnp.broadcast_to(sq, (8, LANES))

    row = pl.BlockSpec((ts, D), lambda i: (i, 0))
    vec = pl.BlockSpec((1, D), lambda i: (0, 0))
    return pl.pallas_call(
        body, name=name, grid=(S // ts,), in_specs=[row, vec, row],
        out_specs=[pl.BlockSpec((8, LANES), lambda i: (0, 0)), row, vec],
        out_shape=[jax.ShapeDtypeStruct((8, LANES), F32), jax.ShapeDtypeStruct((S, D), F32),
                   jax.ShapeDtypeStruct((1, D), F32)],
        compiler_params=_cp(("arbitrary",)),
    )(x, w, target)


def _shift_scan(v, buf_ref, n, reverse=False):
    buf_ref[pl.ds(0, n), :] = jnp.zeros((n, LANES), F32)
    buf_ref[pl.ds(2 * n, n), :] = jnp.zeros((n, LANES), F32)
    s = 1
    while s < n:
        buf_ref[pl.ds(n, n), :] = v
        v = v + buf_ref[pl.ds(n + s if reverse else n - s, n), :]
        s *= 2
    return v


def small_fwd(u, bias, *, name):
    S = u.shape[0]
    ts = _tile(S, TILES["row"])
    cb = OFF["small"] // LANES

    def body(u_ref, b_ref, sp_ref, c_ref, buf_ref, carry_ref):
        x = u_ref[...] + b_ref[...]
        sp_ref[...] = _softplus(x)
        lf = jnp.minimum(x, 0.0) - jnp.log(1.0 + jnp.exp(-jnp.abs(x)))

        @pl.when(pl.program_id(0) == 0)
        def _():
            carry_ref[...] = jnp.zeros((8, LANES), F32)

        c = _shift_scan(lf, buf_ref, ts) + carry_ref[pl.ds(0, 1), :]
        c_ref[...] = c
        carry_ref[...] = jnp.broadcast_to(c_ref[pl.ds(ts - 1, 1), :], (8, LANES))

    blk = pl.BlockSpec((ts, LANES), lambda i: (i, 0))
    return pl.pallas_call(
        body, name=name, grid=(S // ts,),
        in_specs=[pl.BlockSpec((ts, LANES), lambda i: (i, cb)), pl.BlockSpec((1, LANES), lambda i: (0, 0))],
        out_specs=[blk, blk], out_shape=[jax.ShapeDtypeStruct((S, LANES), F32)] * 2,
        scratch_shapes=[pltpu.VMEM((3 * ts, LANES), F32), pltpu.VMEM((8, LANES), F32)],
        compiler_params=_cp(("arbitrary",)),
    )(u, bias)


def small_bwd(dc, dsp, u, bias, *, name):
    S = u.shape[0]
    ts = _tile(S, TILES["row"])
    cb = OFF["small"] // LANES
    nt = S // ts

    def body(dc_ref, dsp_ref, u_ref, b_ref, du_ref, db_ref, buf_ref, carry_ref):
        x = u_ref[...] + b_ref[...]

        @pl.when(pl.program_id(0) == 0)
        def _():
            carry_ref[...] = jnp.zeros((8, LANES), F32)

        dlf = _shift_scan(dc_ref[...], buf_ref, ts, reverse=True) + carry_ref[pl.ds(0, 1), :]
        buf_ref[pl.ds(0, ts), :] = dlf
        carry_ref[...] = jnp.broadcast_to(buf_ref[pl.ds(0, 1), :], (8, LANES))
        sg = _sig(x)
        dx = dlf * (1.0 - sg) + dsp_ref[...] * sg
        du_ref[...] = jnp.concatenate([dx, jnp.zeros((ts, SMALL_W - LANES), F32)], axis=1).astype(BF16)
        part = jnp.sum(dx, axis=0, keepdims=True)

        @pl.when(pl.program_id(0) == 0)
        def _():
            db_ref[...] = part

        @pl.when(pl.program_id(0) > 0)
        def _():
            db_ref[...] += part

    rev = pl.BlockSpec((ts, LANES), lambda i: (nt - 1 - i, 0))
    return pl.pallas_call(
        body, name=name, grid=(nt,),
        in_specs=[rev, rev, pl.BlockSpec((ts, LANES), lambda i: (nt - 1 - i, cb)), pl.BlockSpec((1, LANES), lambda i: (0, 0))],
        out_specs=[pl.BlockSpec((ts, SMALL_W), lambda i: (nt - 1 - i, 0)), pl.BlockSpec((1, LANES), lambda i: (0, 0))],
        out_shape=[jax.ShapeDtypeStruct((S, SMALL_W), BF16), jax.ShapeDtypeStruct((1, LANES), F32)],
        scratch_shapes=[pltpu.VMEM((3 * ts, LANES), F32), pltpu.VMEM((8, LANES), F32)],
        compiler_params=_cp(("arbitrary",)),
    )(dc, dsp, u, bias)


def _conv_in(mode, a, b):
    if mode == "plain":
        return a
    if mode == "mul":
        return a * b
    return a * _sig(b)


SUBLANES = 8
MANY_TAPS = 8


def _row_taps(src_ref, sh_ref, n_rows, many):
    if many:
        for b in range(1, SUBLANES):
            sh_ref[b, pl.ds(0, n_rows - SUBLANES), :] = src_ref[pl.ds(b, n_rows - SUBLANES), :]

    def tap(off, n):
        a, b = divmod(off, SUBLANES)
        if not many or b == 0:
            return src_ref[pl.ds(off, n), :]
        return sh_ref[b, pl.ds(SUBLANES * a, n), :]

    return tap


def conv_fwd(u, w, bias, *, mode, a_off, b_off, C, name):
    S = u.shape[0]
    Kc = w.shape[0]
    ts, cb, H = _tile(S, TILES["row"]), _tile(C, TILES["conv_c"]), CONV_HALO
    two = mode != "plain"
    rb = ts // H
    many = Kc >= MANY_TAPS

    def body(*refs):
        if two:
            a_ref, ap_ref, b_ref, bp_ref, w_ref, bias_ref, o_ref, x_ref = refs[:8]
            cur, prev = _conv_in(mode, a_ref[...], b_ref[...]), _conv_in(mode, ap_ref[...], bp_ref[...])
        else:
            a_ref, ap_ref, w_ref, bias_ref, o_ref, x_ref = refs[:6]
            cur, prev = a_ref[...], ap_ref[...]
        x_ref[pl.ds(0, H), :] = jnp.where(pl.program_id(1) == 0, 0.0, prev)
        x_ref[pl.ds(H, ts), :] = cur
        tap = _row_taps(x_ref, refs[-1], ts + H, many)
        acc = jnp.broadcast_to(bias_ref[...], (ts, cb))
        for j in range(Kc):
            acc = acc + w_ref[pl.ds(j, 1), :] * tap(H - (Kc - 1) + j, ts)
        o_ref[...] = acc

    def cur_spec(off):
        return pl.BlockSpec((ts, cb), lambda c, i: (i, c + off // cb))

    def prev_spec(off):
        return pl.BlockSpec((H, cb), lambda c, i: (jnp.maximum(i * rb - 1, 0), c + off // cb))

    in_specs, args = [cur_spec(a_off), prev_spec(a_off)], [u, u]
    if two:
        in_specs += [cur_spec(b_off), prev_spec(b_off)]
        args += [u, u]
    in_specs += [pl.BlockSpec((Kc, cb), lambda c, i: (0, c)), pl.BlockSpec((1, cb), lambda c, i: (0, c))]
    return pl.pallas_call(
        body, name=name, grid=(C // cb, S // ts), in_specs=in_specs,
        out_specs=pl.BlockSpec((ts, cb), lambda c, i: (i, c)), out_shape=jax.ShapeDtypeStruct((S, C), F32),
        scratch_shapes=[pltpu.VMEM((ts + H, cb), F32), pltpu.VMEM((SUBLANES if many else 1, ts + H, cb), F32)],
        compiler_params=_cp(("parallel", "arbitrary")),
    )(*args, w, bias)


def conv_bwd(dy, u, w, *, mode, a_off, b_off, C, name):
    S = u.shape[0]
    Kc = w.shape[0]
    ts, cb, H = _tile(S, TILES["row"]), _tile(C, TILES["conv_c"]), CONV_HALO
    two = mode != "plain"
    rb, nt = ts // H, S // ts

    many = Kc >= MANY_TAPS

    def body(*refs):
        if two:
            dy_ref, dyn_ref, a_ref, ap_ref, b_ref, bp_ref, w_ref, da_ref, db_ref, dw_ref, dbias_ref, x_ref, g_ref = refs[:13]
            a, b = a_ref[...], b_ref[...]
            cur, prev = _conv_in(mode, a, b), _conv_in(mode, ap_ref[...], bp_ref[...])
        else:
            dy_ref, dyn_ref, a_ref, ap_ref, w_ref, da_ref, dw_ref, dbias_ref, x_ref, g_ref = refs[:10]
            cur, prev = a_ref[...], ap_ref[...]
        i = pl.program_id(1)
        x_ref[pl.ds(0, H), :] = jnp.where(i == 0, 0.0, prev)
        x_ref[pl.ds(H, ts), :] = cur
        g = dy_ref[...]
        g_ref[pl.ds(0, ts), :] = g
        g_ref[pl.ds(ts, H), :] = jnp.where(i == nt - 1, 0.0, dyn_ref[...])
        x_tap = _row_taps(x_ref, refs[-2], ts + H, many)
        g_tap = _row_taps(g_ref, refs[-1], ts + H, many)

        @pl.when(i == 0)
        def _():
            dw_ref[...] = jnp.zeros((Kc, cb), F32)
            dbias_ref[...] = jnp.zeros((1, cb), F32)

        dbias_ref[...] += jnp.sum(g, axis=0, keepdims=True)
        dx = jnp.zeros((ts, cb), F32)
        for j in range(Kc):
            dx = dx + w_ref[pl.ds(j, 1), :] * g_tap(Kc - 1 - j, ts)
            dw_ref[pl.ds(j, 1), :] += jnp.sum(g * x_tap(H - (Kc - 1) + j, ts), axis=0, keepdims=True)
        if mode == "plain":
            da_ref[...] = dx.astype(BF16)
        elif mode == "mul":
            da_ref[...] = (dx * b).astype(BF16)
            db_ref[...] = (dx * a).astype(BF16)
        else:
            sg = _sig(b)
            da_ref[...] = (dx * sg).astype(BF16)
            db_ref[...] = (dx * a * sg * (1.0 - sg)).astype(BF16)

    def cur_spec(off):
        return pl.BlockSpec((ts, cb), lambda c, i: (i, c + off // cb))

    def prev_spec(off):
        return pl.BlockSpec((H, cb), lambda c, i: (jnp.maximum(i * rb - 1, 0), c + off // cb))

    out_blk = pl.BlockSpec((ts, cb), lambda c, i: (i, c))
    in_specs = [out_blk, pl.BlockSpec((H, cb), lambda c, i: (jnp.minimum((i + 1) * rb, S // H - 1), c)),
                cur_spec(a_off), prev_spec(a_off)]
    args = [dy, dy, u, u]
    if two:
        in_specs += [cur_spec(b_off), prev_spec(b_off)]
        args += [u, u]
    in_specs.append(pl.BlockSpec((Kc, cb), lambda c, i: (0, c)))
    n_d = 2 if two else 1
    return pl.pallas_call(
        body, name=name, grid=(C // cb, nt), in_specs=in_specs,
        out_specs=[out_blk] * n_d + [pl.BlockSpec((Kc, cb), lambda c, i: (0, c)), pl.BlockSpec((1, cb), lambda c, i: (0, c))],
        out_shape=[jax.ShapeDtypeStruct((S, C), BF16)] * n_d + [jax.ShapeDtypeStruct((Kc, C), F32), jax.ShapeDtypeStruct((1, C), F32)],
        scratch_shapes=[pltpu.VMEM((ts + H, cb), F32), pltpu.VMEM((ts + H, cb), F32)]
        + [pltpu.VMEM((SUBLANES if many else 1, ts + H, cb), F32)] * 2,
        compiler_params=_cp(("parallel", "arbitrary")),
    )(*args, w)


def _usec(ts, name):
    return pl.BlockSpec((ts, BW), lambda i, o=OFF[name] // BW: (i, o))


def _acc_rows(ref, part):
    @pl.when(pl.program_id(0) == 0)
    def _():
        ref[...] = part

    @pl.when(pl.program_id(0) > 0)
    def _():
        ref[...] += part


def post_fwd(u, y_ssd, cv_c, cv_d, nw, lnw, lnb, *, name):
    S = u.shape[0]
    ts = _tile(S, TILES["post"])

    def body(z_ref, scb_ref, gc_ref, gd_ref, ys_ref, cc_ref, cd_ref, nw_ref, lw_ref, lb_ref, yb_ref, yc_ref, yd_ref):
        t = ys_ref[...] * _silu(z_ref[...])
        r = lax.rsqrt(jnp.mean(t * t, axis=-1, keepdims=True) + EPS)
        yb_ref[...] = (t * r * nw_ref[...]).astype(BF16)
        yc_ref[...] = (scb_ref[...] * cc_ref[...] * _silu(gc_ref[...])).astype(BF16)
        cf = cd_ref[...]
        mu = jnp.mean(cf, axis=-1, keepdims=True)
        xc = cf - mu
        rl = lax.rsqrt(jnp.mean(xc * xc, axis=-1, keepdims=True) + EPS)
        yln = xc * rl * lw_ref[...] + lb_ref[...]
        yd_ref[...] = (_silu(yln) * _silu(gd_ref[...])).astype(BF16)

    row = pl.BlockSpec((ts, BW), lambda i: (i, 0))
    vec = pl.BlockSpec((1, BW), lambda i: (0, 0))
    return pl.pallas_call(
        body, name=name, grid=(S // ts,),
        in_specs=[_usec(ts, "z"), _usec(ts, "scb"), _usec(ts, "gc"), _usec(ts, "gd"), row, row, row, vec, vec, vec],
        out_specs=[row] * 3, out_shape=[jax.ShapeDtypeStruct((S, BW), BF16)] * 3, compiler_params=_cp(("parallel",)),
    )(u, u, u, u, y_ssd, cv_c, cv_d, nw, lnw, lnb)


def post_bwd(dy_a, dy_b, dy_c, dy_d, u, o, y_ssd, cv_c, cv_d, nw, lnw, lnb, *, name):
    S = u.shape[0]
    ts = _tile(S, TILES["post"])

    def body(dya_ref, dyb_ref, dyc_ref, dyd_ref, ga_ref, z_ref, scb_ref, gc_ref, gd_ref, o_ref, ys_ref, cc_ref, cd_ref,
             nw_ref, lw_ref, lb_ref,
             do_ref, dl_ref, dga_ref, dz_ref, dscb_ref, dgc_ref, dgd_ref, dys_ref, dcc_ref, dcd_ref, dnw_ref, dlw_ref, dlb_ref):
        ga, ov, dya = ga_ref[...], o_ref[...], dya_ref[...]
        dob = (dya * _silu(ga)).astype(BF16)
        do_ref[...] = dob
        dga_ref[...] = (dya * ov * _dsilu(ga)).astype(BF16)
        prod = dob.astype(F32) * ov
        lane = lax.broadcasted_iota(jnp.int32, (ts, LANES), 1)
        delta = jnp.zeros((ts, LANES), F32)
        for h in range(FOX_HEADS):
            col = jnp.sum(prod[:, h * FOX_HD:(h + 1) * FOX_HD], axis=1, keepdims=True)
            delta = jnp.where(lane == h, col, delta)
        dl_ref[...] = delta
        ys, z, dyb = ys_ref[...], z_ref[...], dyb_ref[...]
        sz = _silu(z)
        t = ys * sz
        r = lax.rsqrt(jnp.mean(t * t, axis=-1, keepdims=True) + EPS)
        th = t * r
        dth = dyb * nw_ref[...]
        dt_ = r * (dth - th * jnp.mean(dth * th, axis=-1, keepdims=True))
        dys_ref[...] = dt_ * sz
        dz_ref[...] = (dt_ * ys * _dsilu(z)).astype(BF16)
        _acc_rows(dnw_ref, jnp.sum(dyb * th, axis=0, keepdims=True))
        scb, cc, gc, dyc = scb_ref[...], cc_ref[...], gc_ref[...], dyc_ref[...]
        sg = _silu(gc)
        dscb_ref[...] = (dyc * cc * sg).astype(BF16)
        dcc_ref[...] = dyc * scb * sg
        dgc_ref[...] = (dyc * scb * cc * _dsilu(gc)).astype(BF16)
        cf, gd, dyd = cd_ref[...], gd_ref[...], dyd_ref[...]
        mu = jnp.mean(cf, axis=-1, keepdims=True)
        xc = cf - mu
        rl = lax.rsqrt(jnp.mean(xc * xc, axis=-1, keepdims=True) + EPS)
        xh = xc * rl
        yln = xh * lw_ref[...] + lb_ref[...]
        dyln = dyd * _silu(gd) * _dsilu(yln)
        dgd_ref[...] = (dyd * _silu(yln) * _dsilu(gd)).astype(BF16)
        _acc_rows(dlw_ref, jnp.sum(dyln * xh, axis=0, keepdims=True))
        _acc_rows(dlb_ref, jnp.sum(dyln, axis=0, keepdims=True))
        dxh = dyln * lw_ref[...]
        dcd_ref[...] = rl * (dxh - jnp.mean(dxh, axis=-1, keepdims=True) - xh * jnp.mean(dxh * xh, axis=-1, keepdims=True))

    row = pl.BlockSpec((ts, BW), lambda i: (i, 0))
    vec = pl.BlockSpec((1, BW), lambda i: (0, 0))
    sd = jax.ShapeDtypeStruct
    return pl.pallas_call(
        body, name=name, grid=(S // ts,),
        in_specs=[row] * 4 + [_usec(ts, n) for n in ("ga", "z", "scb", "gc", "gd")] + [row] * 4 + [vec] * 3,
        out_specs=[row, pl.BlockSpec((ts, LANES), lambda i: (i, 0))] + [row] * 8 + [vec] * 3,
        out_shape=[sd((S, BW), BF16), sd((S, LANES), F32)] + [sd((S, BW), BF16)] * 5 + [sd((S, BW), F32)] * 3 + [sd((1, BW), F32)] * 3,
        compiler_params=_cp(("arbitrary",)),
    )(dy_a, dy_b, dy_c, dy_d, u, u, u, u, u, o, y_ssd, cv_c, cv_d, nw, lnw, lnb)


def merge_fwd(h, ys, wg, bg, wb, *, name):
    S, D = h.shape
    tm, tn = _tile(S, 1024), _tile(D, 512)
    nb = D // tn

    def body(h_ref, y0, y1, y2, y3, wg_ref, bg_ref, wb_ref, m_ref, g_ref, p_ref, acc_ref):
        i = pl.program_id(2)
        g = jnp.dot(h_ref[...], wg_ref[...], preferred_element_type=F32) + bg_ref[...]
        gate = _sig(g)
        for b, y_ref in enumerate((y0, y1, y2, y3)):
            @pl.when(i == b)
            def _(y_ref=y_ref):
                p = jnp.dot(y_ref[...], wb_ref[...], preferred_element_type=F32)
                g_ref[...] = gate.astype(BF16)
                p_ref[...] = p.astype(BF16)
                if b == 0:
                    acc_ref[...] = gate * p
                else:
                    acc_ref[...] += gate * p

        @pl.when(i == N_BRANCH - 1)
        def _():
            m_ref[...] = acc_ref[...].astype(BF16)

    yspec = pl.BlockSpec((tm, BW), lambda m, n, i: (m, 0))
    return pl.pallas_call(
        body, name=name, grid=(S // tm, nb, N_BRANCH),
        in_specs=[pl.BlockSpec((tm, D), lambda m, n, i: (m, 0)), yspec, yspec, yspec, yspec,
                  pl.BlockSpec((None, D, tn), lambda m, n, i: (i, 0, n)),
                  pl.BlockSpec((None, 1, tn), lambda m, n, i: (i, 0, n)),
                  pl.BlockSpec((None, BW, tn), lambda m, n, i: (i, 0, n))],
        out_specs=[pl.BlockSpec((tm, tn), lambda m, n, i: (m, n)),
                   pl.BlockSpec((tm, tn), lambda m, n, i: (m, i * nb + n)),
                   pl.BlockSpec((tm, tn), lambda m, n, i: (m, i * nb + n))],
        out_shape=[jax.ShapeDtypeStruct((S, D), BF16), jax.ShapeDtypeStruct((S, N_BRANCH * D), BF16),
                   jax.ShapeDtypeStruct((S, N_BRANCH * D), BF16)],
        scratch_shapes=[pltpu.VMEM((tm, tn), F32)],
        compiler_params=_cp(("parallel", "parallel", "arbitrary")),
    )(h, *ys, wg, bg, wb)


def merge_bwd(dm, gates, proj, *, name):
    S, D = dm.shape
    ts, tn = _tile(S, TILES["row"]), _tile(D, 512)
    nb = D // tn

    def body(dm_ref, g_ref, p_ref, dp_ref, dg_ref, db_ref):
        d = dm_ref[...]
        g = g_ref[...].astype(F32)
        dp_ref[...] = (d * g).astype(BF16)
        dg = d * p_ref[...].astype(F32) * g * (1.0 - g)
        dg_ref[...] = dg.astype(BF16)
        part = jnp.sum(dg, axis=0, keepdims=True)

        @pl.when(pl.program_id(2) == 0)
        def _():
            db_ref[...] = part

        @pl.when(pl.program_id(2) > 0)
        def _():
            db_ref[...] += part

    wide = pl.BlockSpec((ts, tn), lambda b, n, i: (i, b * nb + n))
    return pl.pallas_call(
        body, name=name, grid=(N_BRANCH, nb, S // ts),
        in_specs=[pl.BlockSpec((ts, tn), lambda b, n, i: (i, n)), wide, wide],
        out_specs=[wide, wide, pl.BlockSpec((1, tn), lambda b, n, i: (0, b * nb + n))],
        out_shape=[jax.ShapeDtypeStruct((S, N_BRANCH * D), BF16)] * 2 + [jax.ShapeDtypeStruct((1, N_BRANCH * D), F32)],
        compiler_params=_cp(("parallel", "parallel", "arbitrary")),
    )(dm, gates, proj)


_NT = (((1,), (1,)), ((), ()))
_TN = (((0,), (0,)), ((), ()))


def _tri_pairs(n, by_key):
    if by_key:
        pairs = [(i, j) for j in range(n) for i in range(j, n)]
    else:
        pairs = [(i, j) for i in range(n) for j in range(i + 1)]
    return (jnp.array([p[0] for p in pairs], jnp.int32), jnp.array([p[1] for p in pairs], jnp.int32))


def attn_fwd(u, c_row, *, name):
    S = u.shape[0]
    T = _tile(S, TILES["att"])
    n = S // T
    qo, ko, vo, go = (OFF[k] // FOX_HD for k in ("q", "k", "v", "ga"))
    scale = FOX_HD ** -0.5
    it, jt = _tri_pairs(n, by_key=False)

    def body(it_ref, jt_ref, q_ref, k_ref, v_ref, ck_ref, ga_ref, o_ref, lse_ref, ya_ref, m_ref, l_ref, acc_ref):
        i, j = it_ref[pl.program_id(1)], jt_ref[pl.program_id(1)]

        @pl.when(j == 0)
        def _():
            m_ref[...] = jnp.full((T, 1), NEG, F32)
            l_ref[...] = jnp.zeros((T, 1), F32)
            acc_ref[...] = jnp.zeros((T, FOX_HD), F32)

        def step(masked):
            qb = (q_ref[...] * scale).astype(BF16)
            s = lax.dot_general(qb, k_ref[...].astype(BF16), _NT, preferred_element_type=F32) - ck_ref[...]
            if masked:
                row = lax.broadcasted_iota(jnp.int32, (T, T), 0)
                col = lax.broadcasted_iota(jnp.int32, (T, T), 1)
                s = jnp.where(col <= row, s, NEG)
            m_old = m_ref[...]
            m_new = jnp.maximum(m_old, jnp.max(s, axis=1, keepdims=True))
            alpha = jnp.exp(m_old - m_new)
            p = jnp.exp(s - m_new)
            l_ref[...] = alpha * l_ref[...] + jnp.sum(p, axis=1, keepdims=True)
            p_hi = p.astype(BF16)
            p_lo = (p - p_hi.astype(F32)).astype(BF16)
            vb = v_ref[...].astype(BF16)
            pv = jnp.dot(p_hi, vb, preferred_element_type=F32) + jnp.dot(p_lo, vb, preferred_element_type=F32)
            acc_ref[...] = alpha * acc_ref[...] + pv
            m_ref[...] = m_new

        @pl.when(j < i)
        def _():
            step(False)

        @pl.when(j == i)
        def _():
            step(True)
            o = acc_ref[...] / l_ref[...]
            o_ref[...] = o
            lse_ref[...] = m_ref[...] + jnp.log(l_ref[...])
            ya_ref[...] = (o * _silu(ga_ref[...])).astype(BF16)

    def qsec(off):
        return pl.BlockSpec((T, FOX_HD), lambda h, p, it, jt: (it[p], off + h))

    def ksec(off):
        return pl.BlockSpec((T, FOX_HD), lambda h, p, it, jt: (jt[p], off + h))

    out = pl.BlockSpec((T, FOX_HD), lambda h, p, it, jt: (it[p], h))
    colv = pl.BlockSpec((None, T, 1), lambda h, p, it, jt: (h, it[p], 0))
    return pl.pallas_call(
        body, name=name,
        grid_spec=pltpu.PrefetchScalarGridSpec(
            num_scalar_prefetch=2, grid=(FOX_HEADS, n * (n + 1) // 2),
            in_specs=[qsec(qo), ksec(ko), ksec(vo), pl.BlockSpec((None, 1, T), lambda h, p, it, jt: (h, 0, jt[p])), qsec(go)],
            out_specs=[out, colv, out],
            scratch_shapes=[pltpu.VMEM((T, 1), F32), pltpu.VMEM((T, 1), F32), pltpu.VMEM((T, FOX_HD), F32)]),
        out_shape=[jax.ShapeDtypeStruct((S, BW), F32), jax.ShapeDtypeStruct((FOX_HEADS, S, 1), F32),
                   jax.ShapeDtypeStruct((S, BW), BF16)],
        compiler_params=_cp(("parallel", "arbitrary")),
    )(it, jt, u, u, u, c_row, u)


def attn_bwd(u, do, c_col, lse_row, delta_row, *, name):
    S = u.shape[0]
    T = _tile(S, TILES["att"])
    n = S // T
    qo, ko, vo = (OFF[k] // FOX_HD for k in ("q", "k", "v"))
    scale = FOX_HD ** -0.5
    it, jt = _tri_pairs(n, by_key=True)

    def body(it_ref, jt_ref, q_ref, k_ref, v_ref, do_ref, ck_ref, lse_ref, dl_ref, dq_ref, dk_ref, dv_ref, dc_ref,
             dka_ref, dva_ref, dca_ref):
        i, j = it_ref[pl.program_id(1)], jt_ref[pl.program_id(1)]

        @pl.when(pl.program_id(1) == 0)
        def _():
            dq_ref[...] = jnp.zeros((S, FOX_HD), F32)

        @pl.when(i == j)
        def _():
            dka_ref[...] = jnp.zeros((T, FOX_HD), F32)
            dva_ref[...] = jnp.zeros((T, FOX_HD), F32)
            dca_ref[...] = jnp.zeros((T, 1), F32)

        def step(masked):
            qb, dob = (q_ref[...] * scale).astype(BF16), do_ref[...]
            kf = k_ref[...]
            st = lax.dot_general(kf.astype(BF16), qb, _NT, preferred_element_type=F32) - ck_ref[...]
            pt = jnp.exp(st - lse_ref[...])
            if masked:
                kpos = lax.broadcasted_iota(jnp.int32, (T, T), 0)
                qpos = lax.broadcasted_iota(jnp.int32, (T, T), 1)
                pt = jnp.where(kpos <= qpos, pt, 0.0)
            dva_ref[...] += jnp.dot(pt.astype(BF16), dob, preferred_element_type=F32)
            dpt = lax.dot_general(v_ref[...].astype(BF16), dob, _NT, preferred_element_type=F32)
            dst = pt * (dpt - dl_ref[...])
            dca_ref[...] -= jnp.sum(dst, axis=1, keepdims=True)
            dsb = dst.astype(BF16)
            dka_ref[...] += jnp.dot(dsb, qb, preferred_element_type=F32)
            rows = pl.ds(pl.multiple_of(i * T, T), T)
            dq_ref[rows, :] += lax.dot_general(dsb, (kf * scale).astype(BF16), _TN, preferred_element_type=F32)

        @pl.when(i > j)
        def _():
            step(False)

        @pl.when(i == j)
        def _():
            step(True)

        @pl.when(i == n - 1)
        def _():
            dk_ref[...] = dka_ref[...].astype(BF16)
            dv_ref[...] = dva_ref[...].astype(BF16)
            dc_ref[...] = dca_ref[...]

    def qsec(off):
        return pl.BlockSpec((T, FOX_HD), lambda h, p, it, jt: (it[p], off + h))

    def ksec(off):
        return pl.BlockSpec((T, FOX_HD), lambda h, p, it, jt: (jt[p], off + h))

    qrow = pl.BlockSpec((None, 1, T), lambda h, p, it, jt: (h, 0, it[p]))
    kout = pl.BlockSpec((T, FOX_HD), lambda h, p, it, jt: (jt[p], h))
    kcol = pl.BlockSpec((None, T, 1), lambda h, p, it, jt: (h, jt[p], 0))
    return pl.pallas_call(
        body, name=name,
        grid_spec=pltpu.PrefetchScalarGridSpec(
            num_scalar_prefetch=2, grid=(FOX_HEADS, n * (n + 1) // 2),
            in_specs=[qsec(qo), ksec(ko), ksec(vo), pl.BlockSpec((T, FOX_HD), lambda h, p, it, jt: (it[p], h)),
                      kcol, qrow, qrow],
            out_specs=[pl.BlockSpec((S, FOX_HD), lambda h, p, it, jt: (0, h)), kout, kout, kcol],
            scratch_shapes=[pltpu.VMEM((T, FOX_HD), F32), pltpu.VMEM((T, FOX_HD), F32), pltpu.VMEM((T, 1), F32)]),
        out_shape=[jax.ShapeDtypeStruct((S, BW), F32), jax.ShapeDtypeStruct((S, BW), BF16),
                   jax.ShapeDtypeStruct((S, BW), BF16), jax.ShapeDtypeStruct((FOX_HEADS, S, 1), F32)],
        compiler_params=_cp(("parallel", "arbitrary")),
    )(it, jt, u, u, u, do, c_col, lse_row, delta_row)


N_PAIR = SSM_HEADS // 2
PAIRS_PER_GROUP = N_PAIR // SSM_G


def _sel_t():
    r = lax.broadcasted_iota(jnp.int32, (LANES, BW), 0)
    c = lax.broadcasted_iota(jnp.int32, (LANES, BW), 1)
    return (lax.shift_right_logical(c, 6) == r).astype(BF16)


def _sel():
    r = lax.broadcasted_iota(jnp.int32, (BW, LANES), 0)
    c = lax.broadcasted_iota(jnp.int32, (BW, LANES), 1)
    return (lax.shift_right_logical(r, 6) == c).astype(BF16)


def _dot3(x, m):
    hi = x.astype(BF16)
    r1 = x - hi.astype(F32)
    mid = r1.astype(BF16)
    lo = (r1 - mid.astype(F32)).astype(BF16)
    d = functools.partial(jnp.dot, preferred_element_type=F32)
    return d(hi, m) + d(mid, m) + d(lo, m)


def _ssd_common(x_ref, sp_ref, al_ref, buf_ref, big_ref, cst_ref, LC):
    pre = x_ref[...]
    sg = _sig(pre)
    act = pre * sg
    dt = sp_ref[...]
    a = -jnp.exp(al_ref[...])
    cs = _shift_scan(dt * a, buf_ref, LC)
    sel_t = _sel_t()
    dtl = _dot3(dt, sel_t)
    csl = _dot3(cs, sel_t)
    big_ref[...] = csl
    csl_last = big_ref[pl.ds(LC - 1, 1), :]
    cst_ref[...] = cs.T
    return pre, sg, act, dt, a, cs, dtl, csl, csl_last


def ssd_fwd(xbc, sp, alog, dskip_l, *, name):
    S = xbc.shape[0]
    LC = _tile(S, TILES["ssd"])
    nc = S // LC

    def body(x_ref, sp_ref, al_ref, dk_ref, y_ref, hs_ref, st_ref, buf_ref, big_ref, cst_ref):
        @pl.when(pl.program_id(0) == 0)
        def _():
            st_ref[...] = jnp.zeros((N_PAIR, SSM_N, LANES), F32)

        pre, sg, act, dt, a, cs, dtl, csl, csl_last = _ssd_common(x_ref, sp_ref, al_ref, buf_ref, big_ref, cst_ref, LC)
        xs, bm, cm = act[:, :BW], act[:, BW:BW + SSM_G * SSM_N], act[:, BW + SSM_G * SSM_N:]
        e_all = jnp.exp(csl)
        dec = jnp.exp(csl_last - csl)
        ad = jnp.exp(csl_last)
        xd = xs * dtl
        tril = lax.broadcasted_iota(jnp.int32, (LC, LC), 0) >= lax.broadcasted_iota(jnp.int32, (LC, LC), 1)
        lane = lax.broadcasted_iota(jnp.int32, (LC, LANES), 1)
        for g in range(SSM_G):
            bgt = bm[:, g * SSM_N:(g + 1) * SSM_N].T.astype(BF16)
            cgb = cm[:, g * SSM_N:(g + 1) * SSM_N].astype(BF16)
            cb = jnp.dot(cgb, bgt, preferred_element_type=F32)
            for q in range(PAIRS_PER_GROUP):
                pp = g * PAIRS_PER_GROUP + q
                ln = slice(pp * LANES, (pp + 1) * LANES)
                xp = xd[:, ln]
                xpb = xp.astype(BF16)
                yh = []
                for hh in range(2):
                    row_b = jnp.broadcast_to(cst_ref[pl.ds(2 * pp + hh, 1), :], (LC, LC))
                    lmat = jnp.exp(jnp.where(tril, row_b.T - row_b, NEG))
                    yh.append(jnp.dot((cb * lmat).astype(BF16), xpb, preferred_element_type=F32))
                hin = st_ref[pp]
                hs_ref[pp] = hin
                yoff = jnp.dot(cgb, hin.astype(BF16), preferred_element_type=F32) * e_all[:, ln]
                y_ref[:, ln] = jnp.where(lane < SSM_P, yh[0], yh[1]) + yoff + xs[:, ln] * dk_ref[:, ln]
                st_ref[pp] = hin * ad[:, ln] + jnp.dot(bgt, (xp * dec[:, ln]).astype(BF16), preferred_element_type=F32)

    return pl.pallas_call(
        body, name=name, grid=(nc,),
        in_specs=[pl.BlockSpec((LC, SSM_CONV_DIM), lambda c: (c, 0)), pl.BlockSpec((LC, LANES), lambda c: (c, 0)),
                  pl.BlockSpec((1, LANES), lambda c: (0, 0)), pl.BlockSpec((1, BW), lambda c: (0, 0))],
        out_specs=[pl.BlockSpec((LC, BW), lambda c: (c, 0)), pl.BlockSpec((None, N_PAIR, SSM_N, LANES), lambda c: (c, 0, 0, 0))],
        out_shape=[jax.ShapeDtypeStruct((S, BW), F32), jax.ShapeDtypeStruct((nc, N_PAIR, SSM_N, LANES), F32)],
        scratch_shapes=[pltpu.VMEM((N_PAIR, SSM_N, LANES), F32), pltpu.VMEM((3 * LC, LANES), F32),
                        pltpu.VMEM((LC, BW), F32), pltpu.VMEM((LANES, LC), F32)],
        compiler_params=_cp(("arbitrary",)),
    )(xbc, sp, alog, dskip_l)


def ssd_bwd(dy, xbc, sp, alog, dskip_l, hs, *, name):
    S = xbc.shape[0]
    LC = _tile(S, TILES["ssd"])
    nc = S // LC
    GN = SSM_G * SSM_N

    def body(dy_ref, x_ref, sp_ref, al_ref, dk_ref, hs_ref, dx_ref, ddt_ref, da_ref, dd_ref,
             dh_ref, buf_ref, big_ref, cst_ref, gcs_ref, dxd_ref):
        @pl.when(pl.program_id(0) == 0)
        def _():
            dh_ref[...] = jnp.zeros((N_PAIR, SSM_N, LANES), F32)
            da_ref[...] = jnp.zeros((1, LANES), F32)
            dd_ref[...] = jnp.zeros((1, BW), F32)

        pre, sg, act, dt, a, cs, dtl, csl, csl_last = _ssd_common(x_ref, sp_ref, al_ref, buf_ref, big_ref, cst_ref, LC)
        dact = sg * (1.0 + pre * (1.0 - sg))
        xs, bm, cm = act[:, :BW], act[:, BW:BW + GN], act[:, BW + GN:]
        e_all = jnp.exp(csl)
        dec = jnp.exp(csl_last - csl)
        ad = jnp.exp(csl_last)
        xd = xs * dtl
        d_y = dy_ref[...]
        ri = lax.broadcasted_iota(jnp.int32, (LC, LC), 0)
        ci = lax.broadcasted_iota(jnp.int32, (LC, LC), 1)
        tril, triu = ri >= ci, ci >= ri
        lane = lax.broadcasted_iota(jnp.int32, (LC, LANES), 1)
        rowi = lax.broadcasted_iota(jnp.int32, (LC, LANES), 0)
        dot = functools.partial(jnp.dot, preferred_element_type=F32)
        dot_nt = functools.partial(lax.dot_general, dimension_numbers=_NT, preferred_element_type=F32)
        for g in range(SSM_G):
            gs = slice(g * SSM_N, (g + 1) * SSM_N)
            bg, cg = bm[:, gs], cm[:, gs]
            bgb, cgb = bg.astype(BF16), cg.astype(BF16)
            bgt, cgt = bg.T.astype(BF16), cg.T.astype(BF16)
            cb, cbt = dot(cgb, bgt), dot(bgb, cgt)
            dcb = jnp.zeros((LC, LC), F32)
            dcbt = jnp.zeros((LC, LC), F32)
            dcg = jnp.zeros((LC, SSM_N), F32)
            dbg = jnp.zeros((LC, SSM_N), F32)
            for q in range(PAIRS_PER_GROUP):
                pp = g * PAIRS_PER_GROUP + q
                ln = slice(pp * LANES, (pp + 1) * LANES)
                xp, dyp, ep, decp, adp = xd[:, ln], d_y[:, ln], e_all[:, ln], dec[:, ln], ad[:, ln]
                xpb, dypb = xp.astype(BF16), dyp.astype(BF16)
                hin, dho = hs_ref[pp], dh_ref[pp]
                hb, dhob = hin.astype(BF16), dho.astype(BF16)
                yoff = dot(cgb, hb) * ep
                dgb = (dyp * ep).astype(BF16)
                dh_ref[pp] = dho * adp + dot(cgt, dgb)
                dcg = dcg + dot_nt(dgb, hb)
                zf = xp * decp
                d_z = dot(bgb, dhob)
                dbg = dbg + dot_nt(zf.astype(BF16), dhob)
                dzz = d_z * zf
                last = jnp.sum(dzz, axis=0, keepdims=True) + jnp.sum(dho * hin, axis=0, keepdims=True) * adp
                gcs = dyp * yoff - dzz + jnp.where(rowi == LC - 1, last, 0.0)
                dxd = d_z * decp
                for hh in range(2):
                    row_b = jnp.broadcast_to(cst_ref[pl.ds(2 * pp + hh, 1), :], (LC, LC))
                    col_b = row_b.T
                    lmat = jnp.exp(jnp.where(tril, col_b - row_b, NEG))
                    lmat_t = jnp.exp(jnp.where(triu, row_b - col_b, NEG))
                    hm = (lane < SSM_P) if hh == 0 else (lane >= SSM_P)
                    dml = dot_nt(jnp.where(hm, dyp, 0.0).astype(BF16), xpb) * lmat
                    dmtl = dot_nt(jnp.where(hm, xp, 0.0).astype(BF16), dypb) * lmat_t
                    dcb = dcb + dml
                    dcbt = dcbt + dmtl
                    contrib = jnp.sum(dml * cb, axis=1, keepdims=True) - jnp.sum(dmtl * cbt, axis=1, keepdims=True)
                    gcs = gcs + jnp.where(lane == hh * SSM_P, contrib, 0.0)
                    dxd = dxd + jnp.where(hm, dot((cbt * lmat_t).astype(BF16), dypb), 0.0)
                gcs_ref[:, ln] = gcs
                dxd_ref[:, ln] = dxd
            dcg = dcg + dot(dcb.astype(BF16), bgb)
            dbg = dbg + dot(dcbt.astype(BF16), cgb)
            dx_ref[:, BW + g * SSM_N:BW + (g + 1) * SSM_N] = dbg * dact[:, BW + g * SSM_N:BW + (g + 1) * SSM_N]
            dx_ref[:, BW + GN + g * SSM_N:BW + GN + (g + 1) * SSM_N] = dcg * dact[:, BW + GN + g * SSM_N:BW + GN + (g + 1) * SSM_N]
        d_xd = dxd_ref[...]
        dx_ref[:, :BW] = (d_y * dk_ref[...] + d_xd * dtl) * dact[:, :BW]
        sel = _sel()
        dda = _shift_scan(_dot3(gcs_ref[...], sel), buf_ref, LC, reverse=True)
        ddt_ref[...] = _dot3(d_xd * xs, sel) + dda * a
        da_ref[...] += jnp.sum(dda * dt, axis=0, keepdims=True)
        dd_ref[...] += jnp.sum(d_y * xs, axis=0, keepdims=True)

    rev = lambda c: (nc - 1 - c, 0)
    return pl.pallas_call(
        body, name=name, grid=(nc,),
        in_specs=[pl.BlockSpec((LC, BW), rev), pl.BlockSpec((LC, SSM_CONV_DIM), rev), pl.BlockSpec((LC, LANES), rev),
                  pl.BlockSpec((1, LANES), lambda c: (0, 0)), pl.BlockSpec((1, BW), lambda c: (0, 0)),
                  pl.BlockSpec((None, N_PAIR, SSM_N, LANES), lambda c: (nc - 1 - c, 0, 0, 0))],
        out_specs=[pl.BlockSpec((LC, SSM_CONV_DIM), rev), pl.BlockSpec((LC, LANES), rev),
                   pl.BlockSpec((1, LANES), lambda c: (0, 0)), pl.BlockSpec((1, BW), lambda c: (0, 0))],
        out_shape=[jax.ShapeDtypeStruct((S, SSM_CONV_DIM), F32), jax.ShapeDtypeStruct((S, LANES), F32),
                   jax.ShapeDtypeStruct((1, LANES), F32), jax.ShapeDtypeStruct((1, BW), F32)],
        scratch_shapes=[pltpu.VMEM((N_PAIR, SSM_N, LANES), F32), pltpu.VMEM((3 * LC, LANES), F32),
                        pltpu.VMEM((LC, BW), F32), pltpu.VMEM((LANES, LC), F32),
                        pltpu.VMEM((LC, BW), F32), pltpu.VMEM((LC, BW), F32)],
        compiler_params=_cp(("arbitrary",)),
    )(dy, xbc, sp, alog, dskip_l, hs)


def addn(arrs, *, out_dtype, name):
    R, C = arrs[0].shape
    tr = _tile(R, max(8, (1 << 20) // C))

    def body(*refs):
        acc = refs[0][...].astype(F32)
        for r in refs[1:-1]:
            acc = acc + r[...].astype(F32)
        refs[-1][...] = acc.astype(out_dtype)

    blk = pl.BlockSpec((tr, C), lambda i: (i, 0))
    return pl.pallas_call(
        body, name=name, grid=(R // tr,), in_specs=[blk] * len(arrs), out_specs=blk,
        out_shape=jax.ShapeDtypeStruct((R, C), out_dtype), compiler_params=_cp(("parallel",)),
    )(*arrs)


def adamw(w, g, m, v, *, name):
    R, C = w.shape
    tr = _tile(R, max(8, (1 << 18) // C))
    c1, c2 = 1.0 / (1.0 - ADAM_B1 ** ADAM_STEP), 1.0 / (1.0 - ADAM_B2 ** ADAM_STEP)

    def body(w_ref, g_ref, m_ref, v_ref, d_ref, nm_ref, nv_ref):
        gv = g_ref[...]
        nm = ADAM_B1 * m_ref[...] + (1.0 - ADAM_B1) * gv
        nv = ADAM_B2 * v_ref[...] + (1.0 - ADAM_B2) * (gv * gv)
        nm_ref[...] = nm
        nv_ref[...] = nv
        d_ref[...] = -ADAM_LR * ((nm * c1) / (jnp.sqrt(nv * c2) + ADAM_EPS) + ADAM_WD * w_ref[...])

    blk = pl.BlockSpec((tr, C), lambda i: (i, 0))
    return pl.pallas_call(
        body, name=name, grid=(R // tr,), in_specs=[blk] * 4, out_specs=[blk] * 3,
        out_shape=[jax.ShapeDtypeStruct((R, C), F32)] * 3, compiler_params=_cp(("parallel",)),
    )(w, g, m, v)


_ANY = pl.BlockSpec(memory_space=pl.ANY)


def _place():
    return lax.axis_index("x"), lax.axis_index("y"), lax.axis_index("c")


def _rcopy(src, dst, sems_s, sems_r, k, to):
    return pltpu.make_async_remote_copy(src_ref=src, dst_ref=dst, send_sem=sems_s.at[k], recv_sem=sems_r.at[k],
                                        device_id=to, device_id_type=MESH)


def gather_chips(buf, *, name):
    _, _, R, C = buf.shape

    def body(in_ref, out_ref, ss, rs):
        del in_ref
        x, y, c = _place()
        chips = [(1 - x, y), (x, 1 - y), (1 - x, 1 - y)]
        me = 2 * x + y
        first = [_rcopy(out_ref.at[me, c], out_ref.at[me, c], ss, rs, j, (cx, cy, c)) for j, (cx, cy) in enumerate(chips)]
        for cp in first:
            cp.start()
        passed = []
        for j, (cx, cy) in enumerate(chips):
            blk = out_ref.at[2 * cx + cy, c]
            _rcopy(blk, blk, ss, rs, j, (x, y, c)).wait_recv()
            cp = _rcopy(blk, blk, ss, rs, 3 + j, (x, y, 1 - c))
            cp.start()
            passed.append(cp)
        for j, (cx, cy) in enumerate(chips):
            blk = out_ref.at[2 * cx + cy, 1 - c]
            _rcopy(blk, blk, ss, rs, 3 + j, (x, y, c)).wait_recv()
        for cp in first + passed:
            cp.wait_send()

    return pl.pallas_call(
        body, name=name, in_specs=[_ANY], out_specs=_ANY, out_shape=jax.ShapeDtypeStruct(buf.shape, buf.dtype),
        input_output_aliases={0: 0},
        scratch_shapes=[pltpu.SemaphoreType.DMA((6,)), pltpu.SemaphoreType.DMA((6,))],
    )(buf)


def gather_all(block, *, name):
    R, C = block.shape
    flips = [(fx, fy, fc) for fx in (0, 1) for fy in (0, 1) for fc in (0, 1)][1:]

    def body(in_ref, out_ref, ss, rs, ls):
        x, y, c = _place()
        me = 4 * x + 2 * y + c
        mine = pltpu.make_async_copy(in_ref, out_ref.at[me], ls)
        mine.start()
        sends = []
        for k, (fx, fy, fc) in enumerate(flips):
            px, py, pc = x ^ fx, y ^ fy, c ^ fc
            cp = _rcopy(in_ref, out_ref.at[me], ss, rs, k, (px, py, pc))
            cp.start()
            sends.append(cp)
        for k, (fx, fy, fc) in enumerate(flips):
            blk = out_ref.at[4 * (x ^ fx) + 2 * (y ^ fy) + (c ^ fc)]
            _rcopy(blk, blk, ss, rs, k, (x, y, c)).wait_recv()
        for cp in sends:
            cp.wait_send()
        mine.wait()

    return pl.pallas_call(
        body, name=name, in_specs=[_ANY], out_specs=_ANY, out_shape=jax.ShapeDtypeStruct((8, R, C), block.dtype),
        scratch_shapes=[pltpu.SemaphoreType.DMA((7,)), pltpu.SemaphoreType.DMA((7,)), pltpu.SemaphoreType.DMA],
    )(block)


def swap_partials(g_all, *, name):
    N, _, R, C = g_all.shape

    def body(in_ref, out_ref, ss, rs):
        x, y, c = _place()
        cps = [_rcopy(in_ref.at[k, 1 - c], out_ref.at[k], ss, rs, k, (x, y, 1 - c)) for k in range(N)]
        for cp in cps:
            cp.start()
        for cp in cps:
            cp.wait_recv()
        for cp in cps:
            cp.wait_send()

    return pl.pallas_call(
        body, name=name, in_specs=[_ANY], out_specs=_ANY, out_shape=jax.ShapeDtypeStruct((N, R, C), g_all.dtype),
        scratch_shapes=[pltpu.SemaphoreType.DMA((N,)), pltpu.SemaphoreType.DMA((N,))],
    )(g_all)


def share_halves(buf, *, name):
    def body(in_ref, out_ref, ss, rs):
        del in_ref
        x, y, c = _place()
        cp = _rcopy(out_ref.at[c], out_ref.at[c], ss, rs, 0, (x, y, 1 - c))
        cp.start()
        _rcopy(out_ref.at[1 - c], out_ref.at[1 - c], ss, rs, 0, (x, y, c)).wait_recv()
        cp.wait_send()

    return pl.pallas_call(
        body, name=name, in_specs=[_ANY], out_specs=_ANY, out_shape=jax.ShapeDtypeStruct(buf.shape, buf.dtype),
        input_output_aliases={0: 0},
        scratch_shapes=[pltpu.SemaphoreType.DMA((1,)), pltpu.SemaphoreType.DMA((1,))],
    )(buf)


def add_sibling(g_all, sib, c_arr, *, name):
    N, _, R, C = g_all.shape
    tr = _tile(R, max(8, (1 << 20) // C))

    def body(c_ref, a_ref, b_ref, o_ref):
        o_ref[...] = (a_ref[...].astype(F32) + b_ref[...].astype(F32)).astype(BF16)

    blk = pl.BlockSpec((None, tr, C), lambda k, r, c_ref: (k, r, 0))
    return pl.pallas_call(
        body, name=name,
        grid_spec=pltpu.PrefetchScalarGridSpec(
            num_scalar_prefetch=1, grid=(N, R // tr),
            in_specs=[pl.BlockSpec((None, None, tr, C), lambda k, r, c_ref: (k, c_ref[0], r, 0)), blk], out_specs=blk),
        out_shape=jax.ShapeDtypeStruct((N, R, C), BF16), compiler_params=_cp(("parallel", "parallel")),
    )(c_arr, g_all, sib)


def add_chips(part, rcv, place_arr, *, name):
    _, R, C = part.shape
    tr = _tile(R, max(8, (1 << 20) // C))

    def body(place_ref, p_ref, r0, r1, r2, o_ref):
        acc = p_ref[...].astype(F32)
        for r in (r0, r1, r2):
            acc = acc + r[...].astype(F32)
        o_ref[...] = acc

    def slot(j):
        return pl.BlockSpec((None, tr, C), lambda r, place_ref, j=j: (j, r, 0))

    return pl.pallas_call(
        body, name=name,
        grid_spec=pltpu.PrefetchScalarGridSpec(
            num_scalar_prefetch=1, grid=(R // tr,),
            in_specs=[pl.BlockSpec((None, tr, C), lambda r, place_ref: (place_ref[0], r, 0)), slot(0), slot(1), slot(2)],
            out_specs=pl.BlockSpec((None, tr, C), lambda r, place_ref: (place_ref[1], r, 0))),
        out_shape=jax.ShapeDtypeStruct((2, R, C), F32), compiler_params=_cp(("parallel",)),
    )(place_arr, part, rcv, rcv, rcv)


def scatter_chips(parts, *, name):
    _, R, C = parts.shape

    def body(in_ref, out_ref, ss, rs):
        x, y, c = _place()
        chips = [(1 - x, y), (x, 1 - y), (1 - x, 1 - y)]
        cps = [_rcopy(in_ref.at[2 * cx + cy], out_ref.at[j], ss, rs, j, (cx, cy, c)) for j, (cx, cy) in enumerate(chips)]
        for cp in cps:
            cp.start()
        for cp in cps:
            cp.wait_recv()
        for cp in cps:
            cp.wait_send()

    return pl.pallas_call(
        body, name=name, in_specs=[_ANY], out_specs=_ANY, out_shape=jax.ShapeDtypeStruct((3, R, C), parts.dtype),
        scratch_shapes=[pltpu.SemaphoreType.DMA((3,)), pltpu.SemaphoreType.DMA((3,))],
    )(parts)


WEIGHTS = ("norm_w", "w_in", "fg_bias", "ssm_conv_w", "ssm_conv_b", "dt_bias", "a_log", "d_skip", "ssm_norm_w", "sc_conv_w",
           "sc_conv_b", "cf_conv_w", "cf_conv_b", "cf_ln_w", "cf_ln_b", "w_gate", "b_gate", "w_branch", "w_out", "final_norm_w")
BIG = ("w_in", "w_gate", "w_branch", "w_out")
SMALL = tuple(n for n in WEIGHTS if n not in BIG)
SMALL_SHARDED = ("ssm_conv_w", "sc_conv_w", "cf_conv_w", "b_gate")
N_CHIP = 4
PACK_C = D_MODEL
PACK_ROWS = 1024


def _pack(arrs, cols, row_mult):
    return _pack_groups([arrs], cols, row_mult)[0]


def _pack_groups(groups, cols, row_mult):
    n = sum(math.prod(a.shape) for a in groups[0])
    rows = -(-n // cols)
    rows = -(-rows // row_mult) * row_mult
    parts = []
    for arrs in groups:
        parts += [a.reshape(-1) for a in arrs]
        if rows * cols > n:
            parts.append(jnp.zeros((rows * cols - n,), arrs[0].dtype))
    return jnp.concatenate(parts).reshape(len(groups), rows, cols)


PIECE_ROWS = 16


def _pack_rows(groups, cols, row_mult):
    as_list = lambda a: list(a) if isinstance(a, (list, tuple)) else [a]
    n_rows = lambda a: sum(math.prod(b.shape) // cols for b in as_list(a))
    rows = sum(-(-n_rows(a) // PIECE_ROWS) * PIECE_ROWS for a in groups[0])
    total = -(-rows // row_mult) * row_mult
    dtype = as_list(groups[0][0])[0].dtype
    parts = []
    for arrs in groups:
        for a in arrs:
            parts += [b.reshape(-1, cols) for b in as_list(a)]
            pad = -n_rows(a) % PIECE_ROWS
            if pad:
                parts.append(jnp.zeros((pad, cols), dtype))
        if total > rows:
            parts.append(jnp.zeros((total - rows, cols), dtype))
    return jnp.concatenate(parts, axis=0).reshape(len(groups), total, cols)


def _unpack_rows(packed, shapes):
    cols = packed.shape[-1]
    out, o = [], 0
    for s in shapes:
        r = math.prod(s) // cols
        out.append(packed[o:o + r].reshape(s))
        o += -(-r // PIECE_ROWS) * PIECE_ROWS
    return out


def _unpack(packed, shapes):
    flat = packed.reshape(-1)
    out, o = [], 0
    for s in shapes:
        n = math.prod(s)
        out.append(flat[o:o + n].reshape(s))
        o += n
    return out


def _orig_cols():
    cols = []
    for n, s in zip(ORIG_NAMES, ORIG_SIZES):
        if n == "dt":
            cols.append((OFF["small"] + DT_LANE, s))
        elif n == "f":
            cols.append((OFF["small"] + F_LANE, s))
        else:
            cols.append((OFF[n], s))
    return cols


def _rows_to_padded(per_chip):
    q_in = N_IN // N_CHIP
    orig_off = dict(zip(ORIG_NAMES, [sum(ORIG_SIZES[:i]) for i in range(len(ORIG_SIZES))]))
    size = dict(zip(ORIG_NAMES, ORIG_SIZES))
    parts = []
    for n in PAD_ORDER + ("dt", "f"):
        lo, hi = orig_off[n], orig_off[n] + size[n]
        for k in range(N_CHIP):
            a, b = max(lo, k * q_in), min(hi, (k + 1) * q_in)
            if a < b:
                parts.append(per_chip[k][a - k * q_in:b - k * q_in])
    parts.append(jnp.zeros((SMALL_W - size["dt"] - size["f"], per_chip[0].shape[1]), per_chip[0].dtype))
    return jnp.concatenate(parts, axis=0)


def _rows_from_padded(w_pad, lo, hi):
    parts, o = [], 0
    for start, s in _orig_cols():
        a, b = max(lo, o), min(hi, o + s)
        if a < b:
            parts.append(w_pad[start + a - o:start + b - o])
        o += s
    return parts


def _lanes_row(parts, width=LANES):
    v = jnp.concatenate([p.reshape(-1) for p in parts])
    return jnp.pad(v, (0, width - v.shape[0])).reshape(1, width)


def kernel(x, norm_w, w_in, fg_bias, ssm_conv_w, ssm_conv_b, dt_bias, a_log, d_skip, ssm_norm_w, sc_conv_w, sc_conv_b, cf_conv_w, cf_conv_b, cf_ln_w, cf_ln_b, w_gate, b_gate, w_branch, w_out, final_norm_w, loss_target, m_norm_w, m_w_in, m_fg_bias, m_ssm_conv_w, m_ssm_conv_b, m_dt_bias, m_a_log, m_d_skip, m_ssm_norm_w, m_sc_conv_w, m_sc_conv_b, m_cf_conv_w, m_cf_conv_b, m_cf_ln_w, m_cf_ln_b, m_w_gate, m_b_gate, m_w_branch, m_w_out, m_final_norm_w, v_norm_w, v_w_in, v_fg_bias, v_ssm_conv_w, v_ssm_conv_b, v_dt_bias, v_a_log, v_d_skip, v_ssm_norm_w, v_sc_conv_w, v_sc_conv_b, v_cf_conv_w, v_cf_conv_b, v_cf_ln_w, v_cf_ln_b, v_w_gate, v_b_gate, v_w_branch, v_w_out, v_final_norm_w):
    wts = dict(norm_w=norm_w, w_in=w_in, fg_bias=fg_bias, ssm_conv_w=ssm_conv_w, ssm_conv_b=ssm_conv_b, dt_bias=dt_bias,
               a_log=a_log, d_skip=d_skip, ssm_norm_w=ssm_norm_w, sc_conv_w=sc_conv_w, sc_conv_b=sc_conv_b,
               cf_conv_w=cf_conv_w, cf_conv_b=cf_conv_b, cf_ln_w=cf_ln_w, cf_ln_b=cf_ln_b, w_gate=w_gate, b_gate=b_gate,
               w_branch=w_branch, w_out=w_out, final_norm_w=final_norm_w)
    mom = dict(norm_w=m_norm_w, w_in=m_w_in, fg_bias=m_fg_bias, ssm_conv_w=m_ssm_conv_w, ssm_conv_b=m_ssm_conv_b,
               dt_bias=m_dt_bias, a_log=m_a_log, d_skip=m_d_skip, ssm_norm_w=m_ssm_norm_w, sc_conv_w=m_sc_conv_w,
               sc_conv_b=m_sc_conv_b, cf_conv_w=m_cf_conv_w, cf_conv_b=m_cf_conv_b, cf_ln_w=m_cf_ln_w, cf_ln_b=m_cf_ln_b,
               w_gate=m_w_gate, b_gate=m_b_gate, w_branch=m_w_branch, w_out=m_w_out, final_norm_w=m_final_norm_w)
    vel = dict(norm_w=v_norm_w, w_in=v_w_in, fg_bias=v_fg_bias, ssm_conv_w=v_ssm_conv_w, ssm_conv_b=v_ssm_conv_b,
               dt_bias=v_dt_bias, a_log=v_a_log, d_skip=v_d_skip, ssm_norm_w=v_ssm_norm_w, sc_conv_w=v_sc_conv_w,
               sc_conv_b=v_sc_conv_b, cf_conv_w=v_cf_conv_w, cf_conv_b=v_cf_conv_b, cf_ln_w=v_cf_ln_w, cf_ln_b=v_cf_ln_b,
               w_gate=v_w_gate, b_gate=v_b_gate, w_branch=v_w_branch, w_out=v_w_out, final_norm_w=v_final_norm_w)
    L = norm_w.shape[0]
    S, D = x.shape[1], x.shape[2]
    assert D == D_MODEL and x.shape[0] == 1
    xi, yi, ci = _place()
    chip = 2 * xi + yi

    sh_shapes = [wts[n].shape for n in SMALL_SHARDED]
    got = gather_all(_pack([wts[n] for n in SMALL_SHARDED], LANES, 8), name="gather_small_w")
    per_chip = [_unpack(got[2 * k], sh_shapes) for k in range(N_CHIP)]
    full_small = {n: jnp.concatenate([per_chip[k][i] for k in range(N_CHIP)], axis=-1) for i, n in enumerate(SMALL_SHARDED)}

    q_in, q_d = N_IN // N_CHIP, D // N_CHIP
    sent_shapes = [(q_in, D), (N_BRANCH, q_d, D), (N_BRANCH, BW, q_d), (q_d, D)]
    def own_shard(l):
        packed = _pack_rows([[w_in[l].T.astype(BF16), w_gate[l].astype(BF16), w_branch[l].astype(BF16),
                              w_out[l].astype(BF16)]], PACK_C, PACK_ROWS)[0]
        half_rows = packed.shape[0] // 2
        return lax.dynamic_update_slice(lax.empty((N_CHIP, 2, half_rows, PACK_C), BF16),
                                        packed.reshape(1, 2, half_rows, PACK_C), (chip, 0, 0, 0))

    def whole_weights(got):
        pc = [_unpack_rows(got[k].reshape(-1, PACK_C), sent_shapes) for k in range(N_CHIP)]
        return dict(wpt=_rows_to_padded([pc[k][0] for k in range(N_CHIP)]),
                    wg=jnp.concatenate([pc[k][1] for k in range(N_CHIP)], axis=1),
                    wb=jnp.concatenate([pc[k][2] for k in range(N_CHIP)], axis=2),
                    wo=jnp.concatenate([pc[k][3] for k in range(N_CHIP)], axis=0))

    lw = [whole_weights(gather_chips(own_shard(0), name="gather_w"))]
    saved = []
    xl = x[0]
    for l in range(L):
        w = lw[l]
        h = rms_fwd(xl, norm_w[l][None], name="rms_fwd")
        if l + 1 < L:
            u, nxt = mm(h, w["wpt"], tb=True, comm=("gather_ici", own_shard(l + 1)), name="mm_in_gather")
        else:
            u = mm(h, w["wpt"], tb=True, name="mm_in")
        bias_small = _lanes_row([dt_bias[l], fg_bias[l]])
        sp, csum = small_fwd(u, bias_small, name="small_fwd")
        c8 = csum[:, F_LANE:F_LANE + FOX_HEADS].T
        c_col, c_row = c8[:, :, None], c8[:, None, :]
        o, lse, y_a = attn_fwd(u, c_row, name="attn_fwd")
        xbc = conv_fwd(u, full_small["ssm_conv_w"][l], ssm_conv_b[l][None], mode="plain", a_off=OFF["xbc"], b_off=0,
                       C=SSM_CONV_DIM, name="conv_ssm_fwd")
        alog_row = _lanes_row([a_log[l]])
        dskip_l = jnp.repeat(d_skip[l], SSM_P)[None]
        y_ssd, hs = ssd_fwd(xbc, sp, alog_row, dskip_l, name="ssd_fwd")
        cv_c = conv_fwd(u, full_small["sc_conv_w"][l], sc_conv_b[l][None], mode="mul", a_off=OFF["scc"], b_off=OFF["scx"],
                        C=BW, name="conv_sc_fwd")
        cv_d = conv_fwd(u, full_small["cf_conv_w"][l], cf_conv_b[l][None], mode="glu", a_off=OFF["glu"], b_off=OFF["glu"] + BW,
                        C=BW, name="conv_cf_fwd")
        y_b, y_c, y_d = post_fwd(u, y_ssd, cv_c, cv_d, ssm_norm_w[l][None], cf_ln_w[l][None], cf_ln_b[l][None], name="post_fwd")
        merged, gates, proj = merge_fwd(h, (y_a, y_b, y_c, y_d), w["wg"], full_small["b_gate"][l][:, None, :], w["wb"],
                                        name="merge_fwd")
        if l + 1 < L:
            x_next, nxt = mm(merged, w["wo"], add=xl, comm=("gather_d2d", nxt), name="mm_out_gather")
            lw.append(whole_weights(nxt))
        else:
            x_next = mm(merged, w["wo"], add=xl, name="mm_out")
        saved.append(dict(x=xl, h=h, u=u, bias_small=bias_small, sp=sp, c_col=c_col, c_row=c_row, o=o, lse=lse, xbc=xbc,
                          alog_row=alog_row, dskip_l=dskip_l, hs=hs, y_ssd=y_ssd, cv_c=cv_c, cv_d=cv_d,
                          ys=(y_a, y_b, y_c, y_d), merged=merged, gates=gates, proj=proj))
        xl = x_next

    sq, dx, d_final = loss_head(xl, final_norm_w[None], loss_target[0], name="loss_head")
    loss = lax.psum(sq[0, 0] * (0.5 / D), ("x", "y", "c"))

    small_g = {n: [None] * L for n in SMALL if n != "final_norm_w"}
    big_g = {n: [None] * L for n in BIG}

    def finish_exchange(l, part, rcv):
        full = share_halves(add_chips(part, rcv, jnp.stack([chip, ci]).astype(jnp.int32), name="add_chips"), name="share_halves")
        g_in, g_wg, g_wb, g_wo = _unpack_rows(full.reshape(-1, PACK_C),
                                              [(q_in, D), (q_d, N_BRANCH * D), (N_BRANCH, q_d, BW), (q_d, D)])
        big_g["w_in"][l] = g_in.T
        big_g["w_gate"][l] = jnp.transpose(g_wg.reshape(q_d, N_BRANCH, D), (1, 0, 2))
        big_g["w_branch"][l] = jnp.transpose(g_wb, (0, 2, 1))
        big_g["w_out"][l] = g_wo

    pending = None
    for l in reversed(range(L)):
        w, sv = lw[l], saved[l]
        u, h = sv["u"], sv["h"]
        dm = mm(dx, w["wo"], tb=True, name="mm_dmerged")
        d_wo = mm(sv["merged"], dx, ta=True, out_dtype=BF16, name="mm_dwo")
        dp, dg, dbg = merge_bwd(dm, sv["gates"], sv["proj"], name="merge_bwd")
        dys = [mm(dp, w["wb"][i], tb=True, K=D, a_koff=i * D, name="mm_dy") for i in range(N_BRANCH)]
        d_wbt = [mm(dp, sv["ys"][i], ta=True, M=D, a_moff=i * D, out_dtype=BF16, name="mm_dwb")
                 for i in range(N_BRANCH)]
        d_wg = mm(h, dg, ta=True, out_dtype=BF16, name="mm_dwg")
        (do, delta, dga, dz, dscb, dgc, dgd, dy_ssd, dcv_c, dcv_d, dnw, dlnw, dlnb) = post_bwd(
            dys[0], dys[1], dys[2], dys[3], u, sv["o"], sv["y_ssd"], sv["cv_c"], sv["cv_d"],
            ssm_norm_w[l][None], cf_ln_w[l][None], cf_ln_b[l][None], name="post_bwd")
        delta_row = delta[:, :FOX_HEADS].T[:, None, :]
        lse_row = jnp.transpose(sv["lse"], (0, 2, 1))
        dq, dk, dv, dc_col = attn_bwd(u, do, sv["c_col"], lse_row, delta_row, name="attn_bwd")
        dscc, dscx, d_scw, d_scb = conv_bwd(dcv_c, u, full_small["sc_conv_w"][l], mode="mul", a_off=OFF["scc"],
                                            b_off=OFF["scx"], C=BW, name="conv_sc_bwd")
        dglua, dglug, d_cfw, d_cfb = conv_bwd(dcv_d, u, full_small["cf_conv_w"][l], mode="glu", a_off=OFF["glu"],
                                              b_off=OFF["glu"] + BW, C=BW, name="conv_cf_bwd")
        dxbc_pre, ddt, d_a, d_dl = ssd_bwd(dy_ssd, sv["xbc"], sv["sp"], sv["alog_row"], sv["dskip_l"], sv["hs"], name="ssd_bwd")
        dxbc, d_ssmw, d_ssmb = conv_bwd(dxbc_pre, u, full_small["ssm_conv_w"][l], mode="plain", a_off=OFF["xbc"], b_off=0,
                                        C=SSM_CONV_DIM, name="conv_ssm_bwd")
        dc_full = jnp.pad(dc_col[:, :, 0].T, ((0, 0), (F_LANE, LANES - F_LANE - FOX_HEADS)))
        du_small, dbias_small = small_bwd(dc_full, ddt, u, sv["bias_small"], name="small_bwd")
        by_name = dict(q=dq.astype(BF16), k=dk, v=dv, ga=dga, z=dz, scb=dscb, scc=dscc, scx=dscx, gc=dgc, gd=dgd, xbc=dxbc)
        du = jnp.concatenate([jnp.concatenate([dglua, dglug], axis=1) if n == "glu" else by_name[n] for n in PAD_ORDER]
                             + [du_small], axis=1)
        dh_gate = mm(dg, w["wg"], tb=True, name="mm_dh_gate")
        if pending is None:
            dh = mm(du, w["wpt"], add=dh_gate, name="mm_dh_in")
        else:
            dh, rcv = mm(du, w["wpt"], add=dh_gate, comm=("scatter", pending[1]), name="mm_dh_in_scatter")
            finish_exchange(pending[0], pending[1], rcv)
        d_wpt = mm(du, h, ta=True, out_dtype=BF16, name="mm_dwp")
        dx, d_nw = rms_bwd(dh, sv["x"], norm_w[l][None], dx, name="rms_bwd")

        a_neg = -jnp.exp(a_log[l])
        sg = dict(norm_w=d_nw[0], fg_bias=dbias_small[0, F_LANE:F_LANE + FOX_HEADS], ssm_conv_w=d_ssmw, ssm_conv_b=d_ssmb[0],
                  dt_bias=dbias_small[0, DT_LANE:DT_LANE + SSM_HEADS], a_log=d_a[0, :SSM_HEADS] * a_neg,
                  d_skip=d_dl.reshape(SSM_HEADS, SSM_P).sum(-1), ssm_norm_w=dnw[0], sc_conv_w=d_scw, sc_conv_b=d_scb[0],
                  cf_conv_w=d_cfw, cf_conv_b=d_cfb[0], cf_ln_w=dlnw[0], cf_ln_b=dlnb[0], b_gate=dbg.reshape(N_BRANCH, D))
        for n in sg:
            small_g[n][l] = sg[n]

        dest = [[_rows_from_padded(d_wpt, k * q_in, (k + 1) * q_in), d_wg[k * q_d:(k + 1) * q_d]]
                + [t[k * q_d:(k + 1) * q_d] for t in d_wbt] + [d_wo[k * q_d:(k + 1) * q_d]] for k in range(N_CHIP)]
        g_all = _pack_rows(dest, PACK_C, PACK_ROWS)
        R = g_all.shape[1] // 2
        g_all = g_all.reshape(N_CHIP, 2, R, PACK_C)
        sib = swap_partials(g_all, name="swap_partials")
        pending = (l, add_sibling(g_all, sib, ci.reshape(1).astype(jnp.int32), name="add_sibling"))
    finish_exchange(pending[0], pending[1], scatter_chips(pending[1], name="scatter_partials"))

    names = [n for n in SMALL if n != "final_norm_w"]
    stacked = [jnp.stack(small_g[n]) for n in names] + [d_final[0]]
    shapes = [a.shape for a in stacked]
    got = gather_all(_pack(stacked, LANES, 8), name="gather_small_g")
    tot = addn([got[d] for d in range(8)], out_dtype=F32, name="add_small_g")
    grads = dict(zip(names + ["final_norm_w"], _unpack(tot, shapes)))
    for n in SMALL_SHARDED:
        sz = wts[n].shape[-1]
        grads[n] = lax.dynamic_slice_in_dim(grads[n], chip * sz, sz, axis=grads[n].ndim - 1)
    for n in BIG:
        grads[n] = jnp.stack(big_g[n])

    delta, new_m, new_v = {}, {}, {}
    for n in BIG:
        two_d = lambda a: a.reshape(-1, a.shape[-1])
        d, nm, nv = adamw(two_d(wts[n]), two_d(grads[n]), two_d(mom[n]), two_d(vel[n]), name="adamw_" + n)
        delta[n], new_m[n], new_v[n] = (t.reshape(wts[n].shape) for t in (d, nm, nv))
    small_shapes = [wts[n].shape for n in SMALL]
    pk = lambda src: _pack([src[n] for n in SMALL], LANES, 8)
    d, nm, nv = adamw(pk(wts), pk(grads), pk(mom), pk(vel), name="adamw_small")
    for tgt, src in ((delta, d), (new_m, nm), (new_v, nv)):
        for n, a in zip(SMALL, _unpack(src, small_shapes)):
            tgt[n] = a

    return (loss, dx[None], *[grads[n] for n in WEIGHTS], *[delta[n] for n in WEIGHTS],
            *[new_m[n] for n in WEIGHTS], *[new_v[n] for n in WEIGHTS])
```

```python
import functools
import math

import jax
import jax.numpy as jnp
from jax import lax
from jax.experimental import pallas as pl
from jax.experimental.pallas import tpu as pltpu

F32, BF16 = jnp.float32, jnp.bfloat16
MESH = pl.DeviceIdType.MESH

D_MODEL = 2048
BW = D_MODEL // 2
FOX_HEADS, FOX_HD = 8, 128
SSM_HEADS, SSM_P, SSM_N, SSM_G = 16, 64, 128, 2
SSM_CONV_DIM = BW + 2 * SSM_G * SSM_N
N_BRANCH = 4
EPS = 1e-6
NEG = -1e30
LOG2E = 1.4426950408889634
ORIG_SIZES = (BW, BW, BW, FOX_HEADS, BW, BW, SSM_CONV_DIM, SSM_HEADS, BW, BW, BW, BW, 2 * BW, BW)
ORIG_NAMES = ("q", "k", "v", "f", "ga", "z", "xbc", "dt", "scb", "scc", "scx", "gc", "glu", "gd")
N_IN = sum(ORIG_SIZES)
PAD_ORDER = ("q", "k", "v", "ga", "z", "scb", "scc", "scx", "gc", "glu", "gd", "xbc")
SMALL_W = 512
DT_LANE, F_LANE = 0, SSM_HEADS
OFF = {}
_o = 0
for _n in PAD_ORDER:
    OFF[_n] = _o
    _o += ORIG_SIZES[ORIG_NAMES.index(_n)]
OFF["small"] = _o
NP = _o + SMALL_W
LANES = 128
CONV_HALO = 32

VMEM_LIMIT = 56 * 1024 * 1024
MM_VMEM_BUDGET = 40 * 1024 * 1024

ADAM_LR, ADAM_B1, ADAM_B2, ADAM_EPS, ADAM_WD, ADAM_STEP = 0.001, 0.9, 0.999, 1e-08, 0.01, 10

TILES = dict(row=512, post=256, att=1024, ssd=256, mm_m=1024, mm_n=1024, conv_c=256)


def _cp(sem=None):
    return pltpu.CompilerParams(dimension_semantics=sem, vmem_limit_bytes=VMEM_LIMIT)


def _tile(n, pref):
    t = 1 << (min(n, pref).bit_length() - 1)
    while n % t:
        t //= 2
    return t


def _sig(x):
    return 1.0 / (1.0 + jnp.exp(-x))


def _silu(x):
    return x * _sig(x)


def _dsilu(x):
    s = _sig(x)
    return s * (1.0 + x * (1.0 - s))


def _softplus(x):
    return jnp.maximum(x, 0.0) + jnp.log(1.0 + jnp.exp(-jnp.abs(x)))


def _comm_copies(kind, src_ref, dst_ref, ss, rs):
    x, y, c = _place()
    chips = [(1 - x, y), (x, 1 - y), (1 - x, 1 - y)]
    me = 2 * x + y
    sends, recvs = [], []
    for j, (cx, cy) in enumerate(chips):
        if kind == "gather_ici":
            mine, theirs = dst_ref.at[me, c], dst_ref.at[2 * cx + cy, c]
            sends.append(_rcopy(mine, mine, ss, rs, j, (cx, cy, c)))
            recvs.append(_rcopy(theirs, theirs, ss, rs, j, (x, y, c)))
        elif kind == "gather_d2d":
            landed, theirs = dst_ref.at[2 * cx + cy, c], dst_ref.at[2 * cx + cy, 1 - c]
            sends.append(_rcopy(landed, landed, ss, rs, j, (x, y, 1 - c)))
            recvs.append(_rcopy(theirs, theirs, ss, rs, j, (x, y, c)))
        else:
            sends.append(_rcopy(src_ref.at[2 * cx + cy], dst_ref.at[j], ss, rs, j, (cx, cy, c)))
            recvs.append(_rcopy(src_ref.at[0], dst_ref.at[j], ss, rs, j, (x, y, c)))
    return sends, recvs


def mm(a, b, *, name, out_dtype=F32, add=None, ta=False, tb=False, M=None, K=None, N=None,
       a_koff=0, a_moff=0, b_koff=0, b_noff=0, comm=None, n_groups=1):
    b3 = b.ndim == 3
    assert not b3 or (tb and b_koff == 0 and b_noff == 0)
    M = M or (a.shape[1] if ta else a.shape[0])
    K = K or (a.shape[0] if ta else a.shape[1])
    N = N or (b.shape[-2] if tb else b.shape[1])
    tm, tn = _tile(M, TILES["mm_m"]), _tile(N, TILES["mm_n"])
    sa, sb, so = a.dtype.itemsize, b.dtype.itemsize, jnp.dtype(out_dtype).itemsize

    def need(tk):
        return 2 * tm * tk * sa + 2 * tk * tn * sb + 2 * tm * tn * so + (8 * tm * tn if add is not None else 0) + 4 * tm * tn

    tk = b.shape[2] if b3 else K
    while need(tk) > MM_VMEM_BUDGET and tk % 256 == 0:
        tk //= 2
    assert K % tk == 0 and a_koff % tk == 0 and b_koff % tk == 0 and b_noff % tn == 0 and a_moff % tm == 0, (name, K, tk, tn)
    nk = K // tk
    ako, amo, bko, bno = a_koff // tk, a_moff // tm, b_koff // tk, b_noff // tn
    dims = (((0 if ta else 1,), (1 if tb else 0,)), ((), ()))

    n_in = 2 + (add is not None) + (comm is not None)
    grid = (M // tm, N // tn, nk)

    def body(*refs):
        a_ref, b_ref = refs[:2]
        add_ref = refs[2] if add is not None else None
        o_ref = refs[n_in]
        acc_ref = refs[n_in + 1 + (comm is not None)] if nk > 1 else None
        if comm is not None:
            step = (pl.program_id(0) * grid[1] + pl.program_id(1)) * grid[2] + pl.program_id(2)
            sends, recvs = _comm_copies(comm[0], refs[n_in - 1], refs[n_in + 1], refs[-2], refs[-1])

            @pl.when(step == 0)
            def _():
                for cp in sends:
                    cp.start()

        prod = lax.dot_general(a_ref[...].astype(BF16), b_ref[...].astype(BF16), dims, preferred_element_type=F32)

        def finish(acc):
            if add_ref is not None:
                acc = acc + add_ref[...]
            o_ref[...] = acc.astype(out_dtype)

        if nk == 1:
            finish(prod)
        else:
            k = pl.program_id(2)

            @pl.when(k == 0)
            def _():
                acc_ref[...] = prod

            @pl.when(k > 0)
            def _():
                acc_ref[...] += prod

            @pl.when(k == nk - 1)
            def _():
                finish(acc_ref[...])

        if comm is not None:
            @pl.when(step == grid[0] * grid[1] * grid[2] - 1)
            def _():
                for cp in recvs:
                    cp.wait_recv()
                for cp in sends:
                    cp.wait_send()

    if ta:
        a_spec = pl.BlockSpec((tk, tm), lambda i, j, k: (k + ako, i + amo))
    else:
        a_spec = pl.BlockSpec((tm, tk), lambda i, j, k: (i, k + ako))
    if b3:
        assert tk == b.shape[2], (name, tk)
        b_spec = pl.BlockSpec((None, tn, tk), lambda i, j, k: (k, j, 0))
    elif tb:
        b_spec = pl.BlockSpec((tn, tk), lambda i, j, k: (j + bno, k + bko))
    else:
        b_spec = pl.BlockSpec((tk, tn), lambda i, j, k: (k + bko, j + bno))
    in_specs = [a_spec, b_spec]
    args = [a, b]
    if add is not None:
        in_specs.append(pl.BlockSpec((tm, tn), lambda i, j, k: (i, j)))
        args.append(add)
    if n_groups == 1:
        out_spec = pl.BlockSpec((tm, tn), lambda i, j, k: (i, j))
        out_shape = jax.ShapeDtypeStruct((M, N), out_dtype)
    else:
        nb = N // n_groups // tn
        sh = nb.bit_length() - 1
        assert nb == 1 << sh, (name, nb)
        out_spec = pl.BlockSpec((None, tm, tn), lambda i, j, k: (lax.shift_right_logical(j, sh), i, lax.bitwise_and(j, nb - 1)))
        out_shape = jax.ShapeDtypeStruct((n_groups, M, N // n_groups), out_dtype)
    scratch = [pltpu.VMEM((tm, tn), F32)] if nk > 1 else []
    if comm is None:
        return pl.pallas_call(
            body, name=name, grid=grid, in_specs=in_specs, out_specs=out_spec, out_shape=out_shape,
            scratch_shapes=scratch, compiler_params=_cp(("parallel", "parallel", "arbitrary")),
        )(*args)
    kind, buf = comm
    any_spec = pl.BlockSpec(memory_space=pl.ANY)
    if kind == "scatter":
        comm_shape, alias = jax.ShapeDtypeStruct((3,) + buf.shape[1:], buf.dtype), {}
    else:
        comm_shape, alias = jax.ShapeDtypeStruct(buf.shape, buf.dtype), {n_in - 1: 1}
    return pl.pallas_call(
        body, name=name, grid=grid, in_specs=in_specs + [any_spec], out_specs=[out_spec, any_spec],
        out_shape=[out_shape, comm_shape], input_output_aliases=alias,
        scratch_shapes=scratch + [pltpu.SemaphoreType.DMA((3,)), pltpu.SemaphoreType.DMA((3,))],
        compiler_params=_cp(("arbitrary", "arbitrary", "arbitrary")),
    )(*args, buf)


def rms_fwd(x, w, *, name):
    S, D = x.shape
    ts = _tile(S, TILES["row"])

    def body(x_ref, w_ref, o_ref):
        xf = x_ref[...]
        r = lax.rsqrt(jnp.mean(xf * xf, axis=-1, keepdims=True) + EPS)
        o_ref[...] = (xf * r * w_ref[...]).astype(BF16)

    return pl.pallas_call(
        body, name=name, grid=(S // ts,),
        in_specs=[pl.BlockSpec((ts, D), lambda i: (i, 0)), pl.BlockSpec((1, D), lambda i: (0, 0))],
        out_specs=pl.BlockSpec((ts, D), lambda i: (i, 0)),
        out_shape=jax.ShapeDtypeStruct((S, D), BF16), compiler_params=_cp(("parallel",)),
    )(x, w)


def rms_bwd(dh, x, w, dres, *, name):
    S, D = x.shape
    ts = _tile(S, TILES["row"])

    def body(dh_ref, x_ref, w_ref, dres_ref, dx_ref, dw_ref):
        xf = x_ref[...]
        r = lax.rsqrt(jnp.mean(xf * xf, axis=-1, keepdims=True) + EPS)
        xh = xf * r
        g = dh_ref[...]
        dxh = g * w_ref[...]
        dx_ref[...] = dres_ref[...] + r * (dxh - xh * jnp.mean(dxh * xh, axis=-1, keepdims=True))
        part = jnp.sum(g * xh, axis=0, keepdims=True)

        @pl.when(pl.program_id(0) == 0)
        def _():
            dw_ref[...] = part

        @pl.when(pl.program_id(0) > 0)
        def _():
            dw_ref[...] += part

    row = pl.BlockSpec((ts, D), lambda i: (i, 0))
    vec = pl.BlockSpec((1, D), lambda i: (0, 0))
    return pl.pallas_call(
        body, name=name, grid=(S // ts,), in_specs=[row, row, vec, row], out_specs=[row, vec],
        out_shape=[jax.ShapeDtypeStruct((S, D), F32), jax.ShapeDtypeStruct((1, D), F32)],
        compiler_params=_cp(("arbitrary",)),
    )(dh, x, w, dres)


def loss_head(x, w, target, *, name):
    S, D = x.shape
    ts = _tile(S, TILES["row"])

    def body(x_ref, w_ref, t_ref, loss_ref, dx_ref, dw_ref):
        xf = x_ref[...]
        r = lax.rsqrt(jnp.mean(xf * xf, axis=-1, keepdims=True) + EPS)
        xh = xf * r
        err = xh * w_ref[...] - t_ref[...]
        sq = jnp.sum(jnp.sum(err * err, axis=0, keepdims=True), axis=1, keepdims=True)
        dy = err * (1.0 / D)
        dxh = dy * w_ref[...]
        dx_ref[...] = r * (dxh - xh * jnp.mean(dxh * xh, axis=-1, keepdims=True))
        part = jnp.sum(dy * xh, axis=0, keepdims=True)

        @pl.when(pl.program_id(0) == 0)
        def _():
            dw_ref[...] = part
            loss_ref[...] = jnp.broadcast_to(sq, (8, LANES))

        @pl.when(pl.program_id(0) > 0)
        def _():
            dw_ref[...] += part
            loss_ref[...] += jnp.broadcast_to(sq, (8, LANES))

    row = pl.BlockSpec((ts, D), lambda i: (i, 0))
    vec = pl.BlockSpec((1, D), lambda i: (0, 0))
    return pl.pallas_call(
        body, name=name, grid=(S // ts,), in_specs=[row, vec, row],
        out_specs=[pl.BlockSpec((8, LANES), lambda i: (0, 0)), row, vec],
        out_shape=[jax.ShapeDtypeStruct((8, LANES), F32), jax.ShapeDtypeStruct((S, D), F32),
                   jax.ShapeDtypeStruct((1, D), F32)],
        compiler_params=_cp(("arbitrary",)),
    )(x, w, target)


def _shift_scan(v, buf_ref, n, reverse=False):
    buf_ref[pl.ds(0, n), :] = jnp.zeros((n, LANES), F32)
    buf_ref[pl.ds(2 * n, n), :] = jnp.zeros((n, LANES), F32)
    s = 1
    while s < n:
        buf_ref[pl.ds(n, n), :] = v
        v = v + buf_ref[pl.ds(n + s if reverse else n - s, n), :]
        s *= 2
    return v


def small_fwd(u, bias, *, name):
    S = u.shape[0]
    ts = _tile(S, TILES["row"])
    cb = OFF["small"] // LANES

    def body(u_ref, b_ref, sp_ref, c_ref, buf_ref, carry_ref):
        x = u_ref[...] + b_ref[...]
        sp_ref[...] = _softplus(x)
        lf = jnp.minimum(x, 0.0) - jnp.log(1.0 + jnp.exp(-jnp.abs(x)))

        @pl.when(pl.program_id(0) == 0)
        def _():
            carry_ref[...] = jnp.zeros((8, LANES), F32)

        c = _shift_scan(lf, buf_ref, ts) + carry_ref[pl.ds(0, 1), :]
        c_ref[...] = c
        carry_ref[...] = jnp.broadcast_to(c_ref[pl.ds(ts - 1, 1), :], (8, LANES))

    blk = pl.BlockSpec((ts, LANES), lambda i: (i, 0))
    return pl.pallas_call(
        body, name=name, grid=(S // ts,),
        in_specs=[pl.BlockSpec((ts, LANES), lambda i: (i, cb)), pl.BlockSpec((1, LANES), lambda i: (0, 0))],
        out_specs=[blk, blk], out_shape=[jax.ShapeDtypeStruct((S, LANES), F32)] * 2,
        scratch_shapes=[pltpu.VMEM((3 * ts, LANES), F32), pltpu.VMEM((8, LANES), F32)],
        compiler_params=_cp(("arbitrary",)),
    )(u, bias)


def small_bwd(dc, dsp, u, bias, *, name):
    S = u.shape[0]
    ts = _tile(S, TILES["row"])
    cb = OFF["small"] // LANES
    nt = S // ts

    def body(dc_ref, dsp_ref, u_ref, b_ref, du_ref, db_ref, buf_ref, carry_ref):
        x = u_ref[...] + b_ref[...]

        @pl.when(pl.program_id(0) == 0)
        def _():
            carry_ref[...] = jnp.zeros((8, LANES), F32)

        dlf = _shift_scan(dc_ref[...], buf_ref, ts, reverse=True) + carry_ref[pl.ds(0, 1), :]
        buf_ref[pl.ds(0, ts), :] = dlf
        carry_ref[...] = jnp.broadcast_to(buf_ref[pl.ds(0, 1), :], (8, LANES))
        sg = _sig(x)
        dx = dlf * (1.0 - sg) + dsp_ref[...] * sg
        du_ref[...] = jnp.concatenate([dx, jnp.zeros((ts, SMALL_W - LANES), F32)], axis=1).astype(BF16)
        part = jnp.sum(dx, axis=0, keepdims=True)

        @pl.when(pl.program_id(0) == 0)
        def _():
            db_ref[...] = part

        @pl.when(pl.program_id(0) > 0)
        def _():
            db_ref[...] += part

    rev = pl.BlockSpec((ts, LANES), lambda i: (nt - 1 - i, 0))
    return pl.pallas_call(
        body, name=name, grid=(nt,),
        in_specs=[rev, rev, pl.BlockSpec((ts, LANES), lambda i: (nt - 1 - i, cb)), pl.BlockSpec((1, LANES), lambda i: (0, 0))],
        out_specs=[pl.BlockSpec((ts, SMALL_W), lambda i: (nt - 1 - i, 0)), pl.BlockSpec((1, LANES), lambda i: (0, 0))],
        out_shape=[jax.ShapeDtypeStruct((S, SMALL_W), BF16), jax.ShapeDtypeStruct((1, LANES), F32)],
        scratch_shapes=[pltpu.VMEM((3 * ts, LANES), F32), pltpu.VMEM((8, LANES), F32)],
        compiler_params=_cp(("arbitrary",)),
    )(dc, dsp, u, bias)


def _conv_in(mode, a, b):
    if mode == "plain":
        return a
    if mode == "mul":
        return a * b
    return a * _sig(b)


SUBLANES = 8
MANY_TAPS = 8


def _row_taps(src_ref, sh_ref, n_rows, many):
    if many:
        for b in range(1, SUBLANES):
            sh_ref[b, pl.ds(0, n_rows - SUBLANES), :] = src_ref[pl.ds(b, n_rows - SUBLANES), :]

    def tap(off, n):
        a, b = divmod(off, SUBLANES)
        if not many or b == 0:
            return src_ref[pl.ds(off, n), :]
        return sh_ref[b, pl.ds(SUBLANES * a, n), :]

    return tap


def conv_fwd(u, w, bias, *, mode, a_off, b_off, C, name):
    S = u.shape[0]
    Kc = w.shape[0]
    ts, cb, H = _tile(S, TILES["row"]), _tile(C, TILES["conv_c"]), CONV_HALO
    two = mode != "plain"
    rb = ts // H
    many = Kc >= MANY_TAPS

    def body(*refs):
        if two:
            a_ref, ap_ref, b_ref, bp_ref, w_ref, bias_ref, o_ref, x_ref = refs[:8]
            cur, prev = _conv_in(mode, a_ref[...], b_ref[...]), _conv_in(mode, ap_ref[...], bp_ref[...])
        else:
            a_ref, ap_ref, w_ref, bias_ref, o_ref, x_ref = refs[:6]
            cur, prev = a_ref[...], ap_ref[...]
        x_ref[pl.ds(0, H), :] = jnp.where(pl.program_id(1) == 0, 0.0, prev)
        x_ref[pl.ds(H, ts), :] = cur
        tap = _row_taps(x_ref, refs[-1], ts + H, many)
        acc = jnp.broadcast_to(bias_ref[...], (ts, cb))
        for j in range(Kc):
            acc = acc + w_ref[pl.ds(j, 1), :] * tap(H - (Kc - 1) + j, ts)
        o_ref[...] = acc

    def cur_spec(off):
        return pl.BlockSpec((ts, cb), lambda c, i: (i, c + off // cb))

    def prev_spec(off):
        return pl.BlockSpec((H, cb), lambda c, i: (jnp.maximum(i * rb - 1, 0), c + off // cb))

    in_specs, args = [cur_spec(a_off), prev_spec(a_off)], [u, u]
    if two:
        in_specs += [cur_spec(b_off), prev_spec(b_off)]
        args += [u, u]
    in_specs += [pl.BlockSpec((Kc, cb), lambda c, i: (0, c)), pl.BlockSpec((1, cb), lambda c, i: (0, c))]
    return pl.pallas_call(
        body, name=name, grid=(C // cb, S // ts), in_specs=in_specs,
        out_specs=pl.BlockSpec((ts, cb), lambda c, i: (i, c)), out_shape=jax.ShapeDtypeStruct((S, C), F32),
        scratch_shapes=[pltpu.VMEM((ts + H, cb), F32), pltpu.VMEM((SUBLANES if many else 1, ts + H, cb), F32)],
        compiler_params=_cp(("parallel", "arbitrary")),
    )(*args, w, bias)


def conv_bwd(dy, u, w, *, mode, a_off, b_off, C, name):
    S = u.shape[0]
    Kc = w.shape[0]
    ts, cb, H = _tile(S, TILES["row"]), _tile(C, TILES["conv_c"]), CONV_HALO
    two = mode != "plain"
    rb, nt = ts // H, S // ts

    many = Kc >= MANY_TAPS

    def body(*refs):
        if two:
            dy_ref, dyn_ref, a_ref, ap_ref, b_ref, bp_ref, w_ref, da_ref, db_ref, dw_ref, dbias_ref, x_ref, g_ref = refs[:13]
            a, b = a_ref[...], b_ref[...]
            cur, prev = _conv_in(mode, a, b), _conv_in(mode, ap_ref[...], bp_ref[...])
        else:
            dy_ref, dyn_ref, a_ref, ap_ref, w_ref, da_ref, dw_ref, dbias_ref, x_ref, g_ref = refs[:10]
            cur, prev = a_ref[...], ap_ref[...]
        i = pl.program_id(1)
        x_ref[pl.ds(0, H), :] = jnp.where(i == 0, 0.0, prev)
        x_ref[pl.ds(H, ts), :] = cur
        g = dy_ref[...]
        g_ref[pl.ds(0, ts), :] = g
        g_ref[pl.ds(ts, H), :] = jnp.where(i == nt - 1, 0.0, dyn_ref[...])
        x_tap = _row_taps(x_ref, refs[-2], ts + H, many)
        g_tap = _row_taps(g_ref, refs[-1], ts + H, many)

        @pl.when(i == 0)
        def _():
            dw_ref[...] = jnp.zeros((Kc, cb), F32)
            dbias_ref[...] = jnp.zeros((1, cb), F32)

        dbias_ref[...] += jnp.sum(g, axis=0, keepdims=True)
        dx = jnp.zeros((ts, cb), F32)
        for j in range(Kc):
            dx = dx + w_ref[pl.ds(j, 1), :] * g_tap(Kc - 1 - j, ts)
            dw_ref[pl.ds(j, 1), :] += jnp.sum(g * x_tap(H - (Kc - 1) + j, ts), axis=0, keepdims=True)
        if mode == "plain":
            da_ref[...] = dx.astype(BF16)
        elif mode == "mul":
            da_ref[...] = (dx * b).astype(BF16)
            db_ref[...] = (dx * a).astype(BF16)
        else:
            sg = _sig(b)
            da_ref[...] = (dx * sg).astype(BF16)
            db_ref[...] = (dx * a * sg * (1.0 - sg)).astype(BF16)

    def cur_spec(off):
        return pl.BlockSpec((ts, cb), lambda c, i: (i, c + off // cb))

    def prev_spec(off):
        return pl.BlockSpec((H, cb), lambda c, i: (jnp.maximum(i * rb - 1, 0), c + off // cb))

    out_blk = pl.BlockSpec((ts, cb), lambda c, i: (i, c))
    in_specs = [out_blk, pl.BlockSpec((H, cb), lambda c, i: (jnp.minimum((i + 1) * rb, S // H - 1), c)),
                cur_spec(a_off), prev_spec(a_off)]
    args = [dy, dy, u, u]
    if two:
        in_specs += [cur_spec(b_off), prev_spec(b_off)]
        args += [u, u]
    in_specs.append(pl.BlockSpec((Kc, cb), lambda c, i: (0, c)))
    n_d = 2 if two else 1
    return pl.pallas_call(
        body, name=name, grid=(C // cb, nt), in_specs=in_specs,
        out_specs=[out_blk] * n_d + [pl.BlockSpec((Kc, cb), lambda c, i: (0, c)), pl.BlockSpec((1, cb), lambda c, i: (0, c))],
        out_shape=[jax.ShapeDtypeStruct((S, C), BF16)] * n_d + [jax.ShapeDtypeStruct((Kc, C), F32), jax.ShapeDtypeStruct((1, C), F32)],
        scratch_shapes=[pltpu.VMEM((ts + H, cb), F32), pltpu.VMEM((ts + H, cb), F32)]
        + [pltpu.VMEM((SUBLANES if many else 1, ts + H, cb), F32)] * 2,
        compiler_params=_cp(("parallel", "arbitrary")),
    )(*args, w)


def _usec(ts, name):
    return pl.BlockSpec((ts, BW), lambda i, o=OFF[name] // BW: (i, o))


def _acc_rows(ref, part):
    @pl.when(pl.program_id(0) == 0)
    def _():
        ref[...] = part

    @pl.when(pl.program_id(0) > 0)
    def _():
        ref[...] += part


def post_fwd(u, y_ssd, cv_c, cv_d, nw, lnw, lnb, *, name):
    S = u.shape[0]
    ts = _tile(S, TILES["post"])

    def body(z_ref, scb_ref, gc_ref, gd_ref, ys_ref, cc_ref, cd_ref, nw_ref, lw_ref, lb_ref, yb_ref, yc_ref, yd_ref):
        t = ys_ref[...] * _silu(z_ref[...])
        r = lax.rsqrt(jnp.mean(t * t, axis=-1, keepdims=True) + EPS)
        yb_ref[...] = (t * r * nw_ref[...]).astype(BF16)
        yc_ref[...] = (scb_ref[...] * cc_ref[...] * _silu(gc_ref[...])).astype(BF16)
        cf = cd_ref[...]
        mu = jnp.mean(cf, axis=-1, keepdims=True)
        xc = cf - mu
        rl = lax.rsqrt(jnp.mean(xc * xc, axis=-1, keepdims=True) + EPS)
        yln = xc * rl * lw_ref[...] + lb_ref[...]
        yd_ref[...] = (_silu(yln) * _silu(gd_ref[...])).astype(BF16)

    row = pl.BlockSpec((ts, BW), lambda i: (i, 0))
    vec = pl.BlockSpec((1, BW), lambda i: (0, 0))
    return pl.pallas_call(
        body, name=name, grid=(S // ts,),
        in_specs=[_usec(ts, "z"), _usec(ts, "scb"), _usec(ts, "gc"), _usec(ts, "gd"), row, row, row, vec, vec, vec],
        out_specs=[row] * 3, out_shape=[jax.ShapeDtypeStruct((S, BW), BF16)] * 3, compiler_params=_cp(("parallel",)),
    )(u, u, u, u, y_ssd, cv_c, cv_d, nw, lnw, lnb)


def post_bwd(dy_a, dy_b, dy_c, dy_d, u, o, y_ssd, cv_c, cv_d, nw, lnw, lnb, *, name):
    S = u.shape[0]
    ts = _tile(S, TILES["post"])

    def body(dya_ref, dyb_ref, dyc_ref, dyd_ref, ga_ref, z_ref, scb_ref, gc_ref, gd_ref, o_ref, ys_ref, cc_ref, cd_ref,
             nw_ref, lw_ref, lb_ref,
             do_ref, dl_ref, dga_ref, dz_ref, dscb_ref, dgc_ref, dgd_ref, dys_ref, dcc_ref, dcd_ref, dnw_ref, dlw_ref, dlb_ref):
        ga, ov, dya = ga_ref[...], o_ref[...], dya_ref[...]
        dob = (dya * _silu(ga)).astype(BF16)
        do_ref[...] = dob
        dga_ref[...] = (dya * ov * _dsilu(ga)).astype(BF16)
        prod = dob.astype(F32) * ov
        lane = lax.broadcasted_iota(jnp.int32, (ts, LANES), 1)
        delta = jnp.zeros((ts, LANES), F32)
        for h in range(FOX_HEADS):
            col = jnp.sum(prod[:, h * FOX_HD:(h + 1) * FOX_HD], axis=1, keepdims=True)
            delta = jnp.where(lane == h, col, delta)
        dl_ref[...] = delta
        ys, z, dyb = ys_ref[...], z_ref[...], dyb_ref[...]
        sz = _silu(z)
        t = ys * sz
        r = lax.rsqrt(jnp.mean(t * t, axis=-1, keepdims=True) + EPS)
        th = t * r
        dth = dyb * nw_ref[...]
        dt_ = r * (dth - th * jnp.mean(dth * th, axis=-1, keepdims=True))
        dys_ref[...] = dt_ * sz
        dz_ref[...] = (dt_ * ys * _dsilu(z)).astype(BF16)
        _acc_rows(dnw_ref, jnp.sum(dyb * th, axis=0, keepdims=True))
        scb, cc, gc, dyc = scb_ref[...], cc_ref[...], gc_ref[...], dyc_ref[...]
        sg = _silu(gc)
        dscb_ref[...] = (dyc * cc * sg).astype(BF16)
        dcc_ref[...] = dyc * scb * sg
        dgc_ref[...] = (dyc * scb * cc * _dsilu(gc)).astype(BF16)
        cf, gd, dyd = cd_ref[...], gd_ref[...], dyd_ref[...]
        mu = jnp.mean(cf, axis=-1, keepdims=True)
        xc = cf - mu
        rl = lax.rsqrt(jnp.mean(xc * xc, axis=-1, keepdims=True) + EPS)
        xh = xc * rl
        yln = xh * lw_ref[...] + lb_ref[...]
        dyln = dyd * _silu(gd) * _dsilu(yln)
        dgd_ref[...] = (dyd * _silu(yln) * _dsilu(gd)).astype(BF16)
        _acc_rows(dlw_ref, jnp.sum(dyln * xh, axis=0, keepdims=True))
        _acc_rows(dlb_ref, jnp.sum(dyln, axis=0, keepdims=True))
        dxh = dyln * lw_ref[...]
        dcd_ref[...] = rl * (dxh - jnp.mean(dxh, axis=-1, keepdims=True) - xh * jnp.mean(dxh * xh, axis=-1, keepdims=True))

    row = pl.BlockSpec((ts, BW), lambda i: (i, 0))
    vec = pl.BlockSpec((1, BW), lambda i: (0, 0))
    sd = jax.ShapeDtypeStruct
    return pl.pallas_call(
        body, name=name, grid=(S // ts,),
        in_specs=[row] * 4 + [_usec(ts, n) for n in ("ga", "z", "scb", "gc", "gd")] + [row] * 4 + [vec] * 3,
        out_specs=[row, pl.BlockSpec((ts, LANES), lambda i: (i, 0))] + [row] * 8 + [vec] * 3,
        out_shape=[sd((S, BW), BF16), sd((S, LANES), F32)] + [sd((S, BW), BF16)] * 5 + [sd((S, BW), F32)] * 3 + [sd((1, BW), F32)] * 3,
        compiler_params=_cp(("arbitrary",)),
    )(dy_a, dy_b, dy_c, dy_d, u, u, u, u, u, o, y_ssd, cv_c, cv_d, nw, lnw, lnb)


def merge_fwd(h, ys, wg, bg, wb, *, name):
    S, D = h.shape
    tm, tn = _tile(S, 1024), _tile(D, 512)
    nb = D // tn

    def body(h_ref, y0, y1, y2, y3, wg_ref, bg_ref, wb_ref, m_ref, g_ref, p_ref, acc_ref):
        i = pl.program_id(2)
        g = jnp.dot(h_ref[...], wg_ref[...], preferred_element_type=F32) + bg_ref[...]
        gate = _sig(g)
        for b, y_ref in enumerate((y0, y1, y2, y3)):
            @pl.when(i == b)
            def _(y_ref=y_ref):
                p = jnp.dot(y_ref[...], wb_ref[...], preferred_element_type=F32)
                g_ref[...] = gate.astype(BF16)
                p_ref[...] = p.astype(BF16)
                if b == 0:
                    acc_ref[...] = gate * p
                else:
                    acc_ref[...] += gate * p

        @pl.when(i == N_BRANCH - 1)
        def _():
            m_ref[...] = acc_ref[...].astype(BF16)

    yspec = pl.BlockSpec((tm, BW), lambda m, n, i: (m, 0))
    return pl.pallas_call(
        body, name=name, grid=(S // tm, nb, N_BRANCH),
        in_specs=[pl.BlockSpec((tm, D), lambda m, n, i: (m, 0)), yspec, yspec, yspec, yspec,
                  pl.BlockSpec((None, D, tn), lambda m, n, i: (i, 0, n)),
                  pl.BlockSpec((None, 1, tn), lambda m, n, i: (i, 0, n)),
                  pl.BlockSpec((None, BW, tn), lambda m, n, i: (i, 0, n))],
        out_specs=[pl.BlockSpec((tm, tn), lambda m, n, i: (m, n)),
                   pl.BlockSpec((tm, tn), lambda m, n, i: (m, i * nb + n)),
                   pl.BlockSpec((tm, tn), lambda m, n, i: (m, i * nb + n))],
        out_shape=[jax.ShapeDtypeStruct((S, D), BF16), jax.ShapeDtypeStruct((S, N_BRANCH * D), BF16),
                   jax.ShapeDtypeStruct((S, N_BRANCH * D), BF16)],
        scratch_shapes=[pltpu.VMEM((tm, tn), F32)],
        compiler_params=_cp(("parallel", "parallel", "arbitrary")),
    )(h, *ys, wg, bg, wb)


def merge_bwd(dm, gates, proj, *, name):
    S, D = dm.shape
    ts, tn = _tile(S, TILES["row"]), _tile(D, 512)
    nb = D // tn

    def body(dm_ref, g_ref, p_ref, dp_ref, dg_ref, db_ref):
        d = dm_ref[...]
        g = g_ref[...].astype(F32)
        dp_ref[...] = (d * g).astype(BF16)
        dg = d * p_ref[...].astype(F32) * g * (1.0 - g)
        dg_ref[...] = dg.astype(BF16)
        part = jnp.sum(dg, axis=0, keepdims=True)

        @pl.when(pl.program_id(2) == 0)
        def _():
            db_ref[...] = part

        @pl.when(pl.program_id(2) > 0)
        def _():
            db_ref[...] += part

    wide = pl.BlockSpec((ts, tn), lambda b, n, i: (i, b * nb + n))
    return pl.pallas_call(
        body, name=name, grid=(N_BRANCH, nb, S // ts),
        in_specs=[pl.BlockSpec((ts, tn), lambda b, n, i: (i, n)), wide, wide],
        out_specs=[wide, wide, pl.BlockSpec((1, tn), lambda b, n, i: (0, b * nb + n))],
        out_shape=[jax.ShapeDtypeStruct((S, N_BRANCH * D), BF16)] * 2 + [jax.ShapeDtypeStruct((1, N_BRANCH * D), F32)],
        compiler_params=_cp(("parallel", "parallel", "arbitrary")),
    )(dm, gates, proj)


_NT = (((1,), (1,)), ((), ()))
_TN = (((0,), (0,)), ((), ()))


def _tri_pairs(n, by_key):
    if by_key:
        pairs = [(i, j) for j in range(n) for i in range(j, n)]
    else:
        pairs = [(i, j) for i in range(n) for j in range(i + 1)]
    return (jnp.array([p[0] for p in pairs], jnp.int32), jnp.array([p[1] for p in pairs], jnp.int32))


def attn_fwd(u, c_row, *, name):
    S = u.shape[0]
    T = _tile(S, TILES["att"])
    n = S // T
    qo, ko, vo, go = (OFF[k] // FOX_HD for k in ("q", "k", "v", "ga"))
    scale = FOX_HD ** -0.5
    it, jt = _tri_pairs(n, by_key=False)

    def body(it_ref, jt_ref, q_ref, k_ref, v_ref, ck_ref, ga_ref, o_ref, lse_ref, ya_ref, m_ref, l_ref, acc_ref):
        i, j = it_ref[pl.program_id(1)], jt_ref[pl.program_id(1)]

        @pl.when(j == 0)
        def _():
            m_ref[...] = jnp.full((T, 1), NEG, F32)
            l_ref[...] = jnp.zeros((T, 1), F32)
            acc_ref[...] = jnp.zeros((T, FOX_HD), F32)

        def step(masked):
            qb = (q_ref[...] * (scale * LOG2E)).astype(BF16)
            s = lax.dot_general(qb, k_ref[...].astype(BF16), _NT, preferred_element_type=F32) - ck_ref[...]
            if masked:
                row = lax.broadcasted_iota(jnp.int32, (T, T), 0)
                col = lax.broadcasted_iota(jnp.int32, (T, T), 1)
                s = jnp.where(col <= row, s, NEG)
            m_old = m_ref[...]
            m_new = jnp.maximum(m_old, jnp.max(s, axis=1, keepdims=True))
            alpha = jnp.exp2(m_old - m_new)
            p = jnp.exp2(s - m_new)
            l_ref[...] = alpha * l_ref[...] + jnp.sum(p, axis=1, keepdims=True)
            p_hi = p.astype(BF16)
            p_lo = (p - p_hi.astype(F32)).astype(BF16)
            vb = v_ref[...].astype(BF16)
            pv = jnp.dot(p_hi, vb, preferred_element_type=F32) + jnp.dot(p_lo, vb, preferred_element_type=F32)
            acc_ref[...] = alpha * acc_ref[...] + pv
            m_ref[...] = m_new

        @pl.when(j < i)
        def _():
            step(False)

        @pl.when(j == i)
        def _():
            step(True)
            o = acc_ref[...] / l_ref[...]
            o_ref[...] = o
            lse_ref[...] = m_ref[...] + jnp.log(l_ref[...]) * LOG2E
            ya_ref[...] = (o * _silu(ga_ref[...])).astype(BF16)

    def qsec(off):
        return pl.BlockSpec((T, FOX_HD), lambda h, p, it, jt: (it[p], off + h))

    def ksec(off):
        return pl.BlockSpec((T, FOX_HD), lambda h, p, it, jt: (jt[p], off + h))

    out = pl.BlockSpec((T, FOX_HD), lambda h, p, it, jt: (it[p], h))
    colv = pl.BlockSpec((None, T, 1), lambda h, p, it, jt: (h, it[p], 0))
    return pl.pallas_call(
        body, name=name,
        grid_spec=pltpu.PrefetchScalarGridSpec(
            num_scalar_prefetch=2, grid=(FOX_HEADS, n * (n + 1) // 2),
            in_specs=[qsec(qo), ksec(ko), ksec(vo), pl.BlockSpec((None, 1, T), lambda h, p, it, jt: (h, 0, jt[p])), qsec(go)],
            out_specs=[out, colv, out],
            scratch_shapes=[pltpu.VMEM((T, 1), F32), pltpu.VMEM((T, 1), F32), pltpu.VMEM((T, FOX_HD), F32)]),
        out_shape=[jax.ShapeDtypeStruct((S, BW), F32), jax.ShapeDtypeStruct((FOX_HEADS, S, 1), F32),
                   jax.ShapeDtypeStruct((S, BW), BF16)],
        compiler_params=_cp(("parallel", "arbitrary")),
    )(it, jt, u, u, u, c_row, u)


def attn_bwd(u, do, c_col, lse_row, delta_row, *, name):
    S = u.shape[0]
    T = _tile(S, TILES["att"])
    n = S // T
    qo, ko, vo = (OFF[k] // FOX_HD for k in ("q", "k", "v"))
    scale = FOX_HD ** -0.5
    it, jt = _tri_pairs(n, by_key=True)

    def body(it_ref, jt_ref, q_ref, k_ref, v_ref, do_ref, ck_ref, lse_ref, dl_ref, dq_ref, dk_ref, dv_ref, dc_ref,
             dka_ref, dva_ref, dca_ref):
        i, j = it_ref[pl.program_id(1)], jt_ref[pl.program_id(1)]

        @pl.when(pl.program_id(1) == 0)
        def _():
            dq_ref[...] = jnp.zeros((S, FOX_HD), F32)

        @pl.when(i == j)
        def _():
            dka_ref[...] = jnp.zeros((T, FOX_HD), F32)
            dva_ref[...] = jnp.zeros((T, FOX_HD), F32)
            dca_ref[...] = jnp.zeros((T, 1), F32)

        def step(masked):
            qf, dob = q_ref[...], do_ref[...]
            qb = (qf * scale).astype(BF16)
            kf = k_ref[...]
            st = lax.dot_general(kf.astype(BF16), (qf * (scale * LOG2E)).astype(BF16), _NT, preferred_element_type=F32)
            pt = jnp.exp2(st - ck_ref[...] - lse_ref[...])
            if masked:
                kpos = lax.broadcasted_iota(jnp.int32, (T, T), 0)
                qpos = lax.broadcasted_iota(jnp.int32, (T, T), 1)
                pt = jnp.where(kpos <= qpos, pt, 0.0)
            dva_ref[...] += jnp.dot(pt.astype(BF16), dob, preferred_element_type=F32)
            dpt = lax.dot_general(v_ref[...].astype(BF16), dob, _NT, preferred_element_type=F32)
            dst = pt * (dpt - dl_ref[...])
            dca_ref[...] -= jnp.sum(dst, axis=1, keepdims=True)
            dsb = dst.astype(BF16)
            dka_ref[...] += jnp.dot(dsb, qb, preferred_element_type=F32)
            rows = pl.ds(pl.multiple_of(i * T, T), T)
            dq_ref[rows, :] += lax.dot_general(dsb, (kf * scale).astype(BF16), _TN, preferred_element_type=F32)

        @pl.when(i > j)
        def _():
            step(False)

        @pl.when(i == j)
        def _():
            step(True)

        @pl.when(i == n - 1)
        def _():
            dk_ref[...] = dka_ref[...].astype(BF16)
            dv_ref[...] = dva_ref[...].astype(BF16)
            dc_ref[...] = dca_ref[...]

    def qsec(off):
        return pl.BlockSpec((T, FOX_HD), lambda h, p, it, jt: (it[p], off + h))

    def ksec(off):
        return pl.BlockSpec((T, FOX_HD), lambda h, p, it, jt: (jt[p], off + h))

    qrow = pl.BlockSpec((None, 1, T), lambda h, p, it, jt: (h, 0, it[p]))
    kout = pl.BlockSpec((T, FOX_HD), lambda h, p, it, jt: (jt[p], h))
    kcol = pl.BlockSpec((None, T, 1), lambda h, p, it, jt: (h, jt[p], 0))
    return pl.pallas_call(
        body, name=name,
        grid_spec=pltpu.PrefetchScalarGridSpec(
            num_scalar_prefetch=2, grid=(FOX_HEADS, n * (n + 1) // 2),
            in_specs=[qsec(qo), ksec(ko), ksec(vo), pl.BlockSpec((T, FOX_HD), lambda h, p, it, jt: (it[p], h)),
                      kcol, qrow, qrow],
            out_specs=[pl.BlockSpec((S, FOX_HD), lambda h, p, it, jt: (0, h)), kout, kout, kcol],
            scratch_shapes=[pltpu.VMEM((T, FOX_HD), F32), pltpu.VMEM((T, FOX_HD), F32), pltpu.VMEM((T, 1), F32)]),
        out_shape=[jax.ShapeDtypeStruct((S, BW), F32), jax.ShapeDtypeStruct((S, BW), BF16),
                   jax.ShapeDtypeStruct((S, BW), BF16), jax.ShapeDtypeStruct((FOX_HEADS, S, 1), F32)],
        compiler_params=_cp(("parallel", "arbitrary")),
    )(it, jt, u, u, u, do, c_col, lse_row, delta_row)


N_PAIR = SSM_HEADS // 2
PAIRS_PER_GROUP = N_PAIR // SSM_G


def _sel_t():
    r = lax.broadcasted_iota(jnp.int32, (LANES, BW), 0)
    c = lax.broadcasted_iota(jnp.int32, (LANES, BW), 1)
    return (lax.shift_right_logical(c, 6) == r).astype(BF16)


def _sel():
    r = lax.broadcasted_iota(jnp.int32, (BW, LANES), 0)
    c = lax.broadcasted_iota(jnp.int32, (BW, LANES), 1)
    return (lax.shift_right_logical(r, 6) == c).astype(BF16)


def _dot3(x, m):
    hi = x.astype(BF16)
    r1 = x - hi.astype(F32)
    mid = r1.astype(BF16)
    lo = (r1 - mid.astype(F32)).astype(BF16)
    d = functools.partial(jnp.dot, preferred_element_type=F32)
    return d(hi, m) + d(mid, m) + d(lo, m)


def _ssd_common(x_ref, sp_ref, al_ref, buf_ref, big_ref, cst_ref, LC):
    pre = x_ref[...]
    sg = _sig(pre)
    act = pre * sg
    dt = sp_ref[...]
    a = -jnp.exp(al_ref[...])
    cs = _shift_scan(dt * a, buf_ref, LC)
    sel_t = _sel_t()
    dtl = _dot3(dt, sel_t)
    csl = _dot3(cs, sel_t)
    big_ref[...] = csl
    csl_last = big_ref[pl.ds(LC - 1, 1), :]
    cst_ref[...] = cs.T
    return pre, sg, act, dt, a, cs, dtl, csl, csl_last


def ssd_fwd(xbc, sp, alog, dskip_l, *, name):
    S = xbc.shape[0]
    LC = _tile(S, TILES["ssd"])
    nc = S // LC

    def body(x_ref, sp_ref, al_ref, dk_ref, y_ref, hs_ref, st_ref, buf_ref, big_ref, cst_ref):
        @pl.when(pl.program_id(0) == 0)
        def _():
            st_ref[...] = jnp.zeros((N_PAIR, SSM_N, LANES), F32)

        pre, sg, act, dt, a, cs, dtl, csl, csl_last = _ssd_common(x_ref, sp_ref, al_ref, buf_ref, big_ref, cst_ref, LC)
        xs, bm, cm = act[:, :BW], act[:, BW:BW + SSM_G * SSM_N], act[:, BW + SSM_G * SSM_N:]
        e_all = jnp.exp(csl)
        dec = jnp.exp(csl_last - csl)
        ad = jnp.exp(csl_last)
        xd = xs * dtl
        tril = lax.broadcasted_iota(jnp.int32, (LC, LC), 0) >= lax.broadcasted_iota(jnp.int32, (LC, LC), 1)
        lane = lax.broadcasted_iota(jnp.int32, (LC, LANES), 1)
        for g in range(SSM_G):
            bgt = bm[:, g * SSM_N:(g + 1) * SSM_N].T.astype(BF16)
            cgb = cm[:, g * SSM_N:(g + 1) * SSM_N].astype(BF16)
            cb = jnp.dot(cgb, bgt, preferred_element_type=F32)
            for q in range(PAIRS_PER_GROUP):
                pp = g * PAIRS_PER_GROUP + q
                ln = slice(pp * LANES, (pp + 1) * LANES)
                xp = xd[:, ln]
                xpb = xp.astype(BF16)
                yh = []
                for hh in range(2):
                    row_b = jnp.broadcast_to(cst_ref[pl.ds(2 * pp + hh, 1), :], (LC, LC))
                    lmat = jnp.exp(jnp.where(tril, row_b.T - row_b, NEG))
                    yh.append(jnp.dot((cb * lmat).astype(BF16), xpb, preferred_element_type=F32))
                hin = st_ref[pp]
                hs_ref[pp] = hin
                yoff = jnp.dot(cgb, hin.astype(BF16), preferred_element_type=F32) * e_all[:, ln]
                y_ref[:, ln] = jnp.where(lane < SSM_P, yh[0], yh[1]) + yoff + xs[:, ln] * dk_ref[:, ln]
                st_ref[pp] = hin * ad[:, ln] + jnp.dot(bgt, (xp * dec[:, ln]).astype(BF16), preferred_element_type=F32)

    return pl.pallas_call(
        body, name=name, grid=(nc,),
        in_specs=[pl.BlockSpec((LC, SSM_CONV_DIM), lambda c: (c, 0)), pl.BlockSpec((LC, LANES), lambda c: (c, 0)),
                  pl.BlockSpec((1, LANES), lambda c: (0, 0)), pl.BlockSpec((1, BW), lambda c: (0, 0))],
        out_specs=[pl.BlockSpec((LC, BW), lambda c: (c, 0)), pl.BlockSpec((None, N_PAIR, SSM_N, LANES), lambda c: (c, 0, 0, 0))],
        out_shape=[jax.ShapeDtypeStruct((S, BW), F32), jax.ShapeDtypeStruct((nc, N_PAIR, SSM_N, LANES), F32)],
        scratch_shapes=[pltpu.VMEM((N_PAIR, SSM_N, LANES), F32), pltpu.VMEM((3 * LC, LANES), F32),
                        pltpu.VMEM((LC, BW), F32), pltpu.VMEM((LANES, LC), F32)],
        compiler_params=_cp(("arbitrary",)),
    )(xbc, sp, alog, dskip_l)


def ssd_bwd(dy, xbc, sp, alog, dskip_l, hs, *, name):
    S = xbc.shape[0]
    LC = _tile(S, TILES["ssd"])
    nc = S // LC
    GN = SSM_G * SSM_N

    def body(dy_ref, x_ref, sp_ref, al_ref, dk_ref, hs_ref, dx_ref, ddt_ref, da_ref, dd_ref,
             dh_ref, buf_ref, big_ref, cst_ref, gcs_ref, dxd_ref):
        @pl.when(pl.program_id(0) == 0)
        def _():
            dh_ref[...] = jnp.zeros((N_PAIR, SSM_N, LANES), F32)
            da_ref[...] = jnp.zeros((1, LANES), F32)
            dd_ref[...] = jnp.zeros((1, BW), F32)

        pre, sg, act, dt, a, cs, dtl, csl, csl_last = _ssd_common(x_ref, sp_ref, al_ref, buf_ref, big_ref, cst_ref, LC)
        dact = sg * (1.0 + pre * (1.0 - sg))
        xs, bm, cm = act[:, :BW], act[:, BW:BW + GN], act[:, BW + GN:]
        e_all = jnp.exp(csl)
        dec = jnp.exp(csl_last - csl)
        ad = jnp.exp(csl_last)
        xd = xs * dtl
        d_y = dy_ref[...]
        ri = lax.broadcasted_iota(jnp.int32, (LC, LC), 0)
        ci = lax.broadcasted_iota(jnp.int32, (LC, LC), 1)
        tril, triu = ri >= ci, ci >= ri
        lane = lax.broadcasted_iota(jnp.int32, (LC, LANES), 1)
        rowi = lax.broadcasted_iota(jnp.int32, (LC, LANES), 0)
        dot = functools.partial(jnp.dot, preferred_element_type=F32)
        dot_nt = functools.partial(lax.dot_general, dimension_numbers=_NT, preferred_element_type=F32)
        for g in range(SSM_G):
            gs = slice(g * SSM_N, (g + 1) * SSM_N)
            bg, cg = bm[:, gs], cm[:, gs]
            bgb, cgb = bg.astype(BF16), cg.astype(BF16)
            bgt, cgt = bg.T.astype(BF16), cg.T.astype(BF16)
            cb, cbt = dot(cgb, bgt), dot(bgb, cgt)
            dcb = jnp.zeros((LC, LC), F32)
            dcbt = jnp.zeros((LC, LC), F32)
            dcg = jnp.zeros((LC, SSM_N), F32)
            dbg = jnp.zeros((LC, SSM_N), F32)
            for q in range(PAIRS_PER_GROUP):
                pp = g * PAIRS_PER_GROUP + q
                ln = slice(pp * LANES, (pp + 1) * LANES)
                xp, dyp, ep, decp, adp = xd[:, ln], d_y[:, ln], e_all[:, ln], dec[:, ln], ad[:, ln]
                xpb, dypb = xp.astype(BF16), dyp.astype(BF16)
                hin, dho = hs_ref[pp], dh_ref[pp]
                hb, dhob = hin.astype(BF16), dho.astype(BF16)
                yoff = dot(cgb, hb) * ep
                dgb = (dyp * ep).astype(BF16)
                dh_ref[pp] = dho * adp + dot(cgt, dgb)
                dcg = dcg + dot_nt(dgb, hb)
                zf = xp * decp
                d_z = dot(bgb, dhob)
                dbg = dbg + dot_nt(zf.astype(BF16), dhob)
                dzz = d_z * zf
                last = jnp.sum(dzz, axis=0, keepdims=True) + jnp.sum(dho * hin, axis=0, keepdims=True) * adp
                gcs = dyp * yoff - dzz + jnp.where(rowi == LC - 1, last, 0.0)
                dxd = d_z * decp
                for hh in range(2):
                    row_b = jnp.broadcast_to(cst_ref[pl.ds(2 * pp + hh, 1), :], (LC, LC))
                    col_b = row_b.T
                    lmat = jnp.exp(jnp.where(tril, col_b - row_b, NEG))
                    lmat_t = jnp.exp(jnp.where(triu, row_b - col_b, NEG))
                    hm = (lane < SSM_P) if hh == 0 else (lane >= SSM_P)
                    dml = dot_nt(jnp.where(hm, dyp, 0.0).astype(BF16), xpb) * lmat
                    dmtl = dot_nt(jnp.where(hm, xp, 0.0).astype(BF16), dypb) * lmat_t
                    dcb = dcb + dml
                    dcbt = dcbt + dmtl
                    contrib = jnp.sum(dml * cb, axis=1, keepdims=True) - jnp.sum(dmtl * cbt, axis=1, keepdims=True)
                    gcs = gcs + jnp.where(lane == hh * SSM_P, contrib, 0.0)
                    dxd = dxd + jnp.where(hm, dot((cbt * lmat_t).astype(BF16), dypb), 0.0)
                gcs_ref[:, ln] = gcs
                dxd_ref[:, ln] = dxd
            dcg = dcg + dot(dcb.astype(BF16), bgb)
            dbg = dbg + dot(dcbt.astype(BF16), cgb)
            dx_ref[:, BW + g * SSM_N:BW + (g + 1) * SSM_N] = dbg * dact[:, BW + g * SSM_N:BW + (g + 1) * SSM_N]
            dx_ref[:, BW + GN + g * SSM_N:BW + GN + (g + 1) * SSM_N] = dcg * dact[:, BW + GN + g * SSM_N:BW + GN + (g + 1) * SSM_N]
        d_xd = dxd_ref[...]
        dx_ref[:, :BW] = (d_y * dk_ref[...] + d_xd * dtl) * dact[:, :BW]
        sel = _sel()
        dda = _shift_scan(_dot3(gcs_ref[...], sel), buf_ref, LC, reverse=True)
        ddt_ref[...] = _dot3(d_xd * xs, sel) + dda * a
        da_ref[...] += jnp.sum(dda * dt, axis=0, keepdims=True)
        dd_ref[...] += jnp.sum(d_y * xs, axis=0, keepdims=True)

    rev = lambda c: (nc - 1 - c, 0)
    return pl.pallas_call(
        body, name=name, grid=(nc,),
        in_specs=[pl.BlockSpec((LC, BW), rev), pl.BlockSpec((LC, SSM_CONV_DIM), rev), pl.BlockSpec((LC, LANES), rev),
                  pl.BlockSpec((1, LANES), lambda c: (0, 0)), pl.BlockSpec((1, BW), lambda c: (0, 0)),
                  pl.BlockSpec((None, N_PAIR, SSM_N, LANES), lambda c: (nc - 1 - c, 0, 0, 0))],
        out_specs=[pl.BlockSpec((LC, SSM_CONV_DIM), rev), pl.BlockSpec((LC, LANES), rev),
                   pl.BlockSpec((1, LANES), lambda c: (0, 0)), pl.BlockSpec((1, BW), lambda c: (0, 0))],
        out_shape=[jax.ShapeDtypeStruct((S, SSM_CONV_DIM), F32), jax.ShapeDtypeStruct((S, LANES), F32),
                   jax.ShapeDtypeStruct((1, LANES), F32), jax.ShapeDtypeStruct((1, BW), F32)],
        scratch_shapes=[pltpu.VMEM((N_PAIR, SSM_N, LANES), F32), pltpu.VMEM((3 * LC, LANES), F32),
                        pltpu.VMEM((LC, BW), F32), pltpu.VMEM((LANES, LC), F32),
                        pltpu.VMEM((LC, BW), F32), pltpu.VMEM((LC, BW), F32)],
        compiler_params=_cp(("arbitrary",)),
    )(dy, xbc, sp, alog, dskip_l, hs)


def addn(arrs, *, out_dtype, name):
    R, C = arrs[0].shape
    tr = _tile(R, max(8, (1 << 20) // C))

    def body(*refs):
        acc = refs[0][...].astype(F32)
        for r in refs[1:-1]:
            acc = acc + r[...].astype(F32)
        refs[-1][...] = acc.astype(out_dtype)

    blk = pl.BlockSpec((tr, C), lambda i: (i, 0))
    return pl.pallas_call(
        body, name=name, grid=(R // tr,), in_specs=[blk] * len(arrs), out_specs=blk,
        out_shape=jax.ShapeDtypeStruct((R, C), out_dtype), compiler_params=_cp(("parallel",)),
    )(*arrs)


def adamw(w, g, m, v, *, name):
    R, C = w.shape
    tr = _tile(R, max(8, (1 << 18) // C))
    c1, c2 = 1.0 / (1.0 - ADAM_B1 ** ADAM_STEP), 1.0 / (1.0 - ADAM_B2 ** ADAM_STEP)

    def body(w_ref, g_ref, m_ref, v_ref, d_ref, nm_ref, nv_ref):
        gv = g_ref[...]
        nm = ADAM_B1 * m_ref[...] + (1.0 - ADAM_B1) * gv
        nv = ADAM_B2 * v_ref[...] + (1.0 - ADAM_B2) * (gv * gv)
        nm_ref[...] = nm
        nv_ref[...] = nv
        d_ref[...] = -ADAM_LR * ((nm * c1) / (jnp.sqrt(nv * c2) + ADAM_EPS) + ADAM_WD * w_ref[...])

    blk = pl.BlockSpec((tr, C), lambda i: (i, 0))
    return pl.pallas_call(
        body, name=name, grid=(R // tr,), in_specs=[blk] * 4, out_specs=[blk] * 3,
        out_shape=[jax.ShapeDtypeStruct((R, C), F32)] * 3, compiler_params=_cp(("parallel",)),
    )(w, g, m, v)


_ANY = pl.BlockSpec(memory_space=pl.ANY)


def _place():
    return lax.axis_index("x"), lax.axis_index("y"), lax.axis_index("c")


def _rcopy(src, dst, sems_s, sems_r, k, to):
    return pltpu.make_async_remote_copy(src_ref=src, dst_ref=dst, send_sem=sems_s.at[k], recv_sem=sems_r.at[k],
                                        device_id=to, device_id_type=MESH)


def gather_chips(buf, *, name):
    _, _, R, C = buf.shape

    def body(in_ref, out_ref, ss, rs):
        del in_ref
        x, y, c = _place()
        chips = [(1 - x, y), (x, 1 - y), (1 - x, 1 - y)]
        me = 2 * x + y
        first = [_rcopy(out_ref.at[me, c], out_ref.at[me, c], ss, rs, j, (cx, cy, c)) for j, (cx, cy) in enumerate(chips)]
        for cp in first:
            cp.start()
        passed = []
        for j, (cx, cy) in enumerate(chips):
            blk = out_ref.at[2 * cx + cy, c]
            _rcopy(blk, blk, ss, rs, j, (x, y, c)).wait_recv()
            cp = _rcopy(blk, blk, ss, rs, 3 + j, (x, y, 1 - c))
            cp.start()
            passed.append(cp)
        for j, (cx, cy) in enumerate(chips):
            blk = out_ref.at[2 * cx + cy, 1 - c]
            _rcopy(blk, blk, ss, rs, 3 + j, (x, y, c)).wait_recv()
        for cp in first + passed:
            cp.wait_send()

    return pl.pallas_call(
        body, name=name, in_specs=[_ANY], out_specs=_ANY, out_shape=jax.ShapeDtypeStruct(buf.shape, buf.dtype),
        input_output_aliases={0: 0},
        scratch_shapes=[pltpu.SemaphoreType.DMA((6,)), pltpu.SemaphoreType.DMA((6,))],
    )(buf)


def gather_all(block, *, name):
    R, C = block.shape
    flips = [(fx, fy, fc) for fx in (0, 1) for fy in (0, 1) for fc in (0, 1)][1:]

    def body(in_ref, out_ref, ss, rs, ls):
        x, y, c = _place()
        me = 4 * x + 2 * y + c
        mine = pltpu.make_async_copy(in_ref, out_ref.at[me], ls)
        mine.start()
        sends = []
        for k, (fx, fy, fc) in enumerate(flips):
            px, py, pc = x ^ fx, y ^ fy, c ^ fc
            cp = _rcopy(in_ref, out_ref.at[me], ss, rs, k, (px, py, pc))
            cp.start()
            sends.append(cp)
        for k, (fx, fy, fc) in enumerate(flips):
            blk = out_ref.at[4 * (x ^ fx) + 2 * (y ^ fy) + (c ^ fc)]
            _rcopy(blk, blk, ss, rs, k, (x, y, c)).wait_recv()
        for cp in sends:
            cp.wait_send()
        mine.wait()

    return pl.pallas_call(
        body, name=name, in_specs=[_ANY], out_specs=_ANY, out_shape=jax.ShapeDtypeStruct((8, R, C), block.dtype),
        scratch_shapes=[pltpu.SemaphoreType.DMA((7,)), pltpu.SemaphoreType.DMA((7,)), pltpu.SemaphoreType.DMA],
    )(block)


def swap_partials(g_all, *, name):
    N, _, R, C = g_all.shape

    def body(in_ref, out_ref, ss, rs):
        x, y, c = _place()
        cps = [_rcopy(in_ref.at[k, 1 - c], out_ref.at[k], ss, rs, k, (x, y, 1 - c)) for k in range(N)]
        for cp in cps:
            cp.start()
        for cp in cps:
            cp.wait_recv()
        for cp in cps:
            cp.wait_send()

    return pl.pallas_call(
        body, name=name, in_specs=[_ANY], out_specs=_ANY, out_shape=jax.ShapeDtypeStruct((N, R, C), g_all.dtype),
        scratch_shapes=[pltpu.SemaphoreType.DMA((N,)), pltpu.SemaphoreType.DMA((N,))],
    )(g_all)


def share_halves(buf, *, name):
    def body(in_ref, out_ref, ss, rs):
        del in_ref
        x, y, c = _place()
        cp = _rcopy(out_ref.at[c], out_ref.at[c], ss, rs, 0, (x, y, 1 - c))
        cp.start()
        _rcopy(out_ref.at[1 - c], out_ref.at[1 - c], ss, rs, 0, (x, y, c)).wait_recv()
        cp.wait_send()

    return pl.pallas_call(
        body, name=name, in_specs=[_ANY], out_specs=_ANY, out_shape=jax.ShapeDtypeStruct(buf.shape, buf.dtype),
        input_output_aliases={0: 0},
        scratch_shapes=[pltpu.SemaphoreType.DMA((1,)), pltpu.SemaphoreType.DMA((1,))],
    )(buf)


def add_sibling(g_all, sib, c_arr, *, name):
    N, _, R, C = g_all.shape
    tr = _tile(R, max(8, (1 << 20) // C))

    def body(c_ref, a_ref, b_ref, o_ref):
        o_ref[...] = (a_ref[...].astype(F32) + b_ref[...].astype(F32)).astype(BF16)

    blk = pl.BlockSpec((None, tr, C), lambda k, r, c_ref: (k, r, 0))
    return pl.pallas_call(
        body, name=name,
        grid_spec=pltpu.PrefetchScalarGridSpec(
            num_scalar_prefetch=1, grid=(N, R // tr),
            in_specs=[pl.BlockSpec((None, None, tr, C), lambda k, r, c_ref: (k, c_ref[0], r, 0)), blk], out_specs=blk),
        out_shape=jax.ShapeDtypeStruct((N, R, C), BF16), compiler_params=_cp(("parallel", "parallel")),
    )(c_arr, g_all, sib)


def add_chips(part, rcv, place_arr, *, name):
    _, R, C = part.shape
    tr = _tile(R, max(8, (1 << 20) // C))

    def body(place_ref, p_ref, r0, r1, r2, o_ref):
        acc = p_ref[...].astype(F32)
        for r in (r0, r1, r2):
            acc = acc + r[...].astype(F32)
        o_ref[...] = acc

    def slot(j):
        return pl.BlockSpec((None, tr, C), lambda r, place_ref, j=j: (j, r, 0))

    return pl.pallas_call(
        body, name=name,
        grid_spec=pltpu.PrefetchScalarGridSpec(
            num_scalar_prefetch=1, grid=(R // tr,),
            in_specs=[pl.BlockSpec((None, tr, C), lambda r, place_ref: (place_ref[0], r, 0)), slot(0), slot(1), slot(2)],
            out_specs=pl.BlockSpec((None, tr, C), lambda r, place_ref: (place_ref[1], r, 0))),
        out_shape=jax.ShapeDtypeStruct((2, R, C), F32), compiler_params=_cp(("parallel",)),
    )(place_arr, part, rcv, rcv, rcv)


WEIGHTS = ("norm_w", "w_in", "fg_bias", "ssm_conv_w", "ssm_conv_b", "dt_bias", "a_log", "d_skip", "ssm_norm_w", "sc_conv_w",
           "sc_conv_b", "cf_conv_w", "cf_conv_b", "cf_ln_w", "cf_ln_b", "w_gate", "b_gate", "w_branch", "w_out", "final_norm_w")
BIG = ("w_in", "w_gate", "w_branch", "w_out")
SMALL = tuple(n for n in WEIGHTS if n not in BIG)
SMALL_SHARDED = ("ssm_conv_w", "sc_conv_w", "cf_conv_w", "b_gate")
N_CHIP = 4
PACK_C = D_MODEL
PACK_ROWS = 1024


def _pack(arrs, cols, row_mult):
    return _pack_groups([arrs], cols, row_mult)[0]


def _pack_groups(groups, cols, row_mult):
    n = sum(math.prod(a.shape) for a in groups[0])
    rows = -(-n // cols)
    rows = -(-rows // row_mult) * row_mult
    parts = []
    for arrs in groups:
        parts += [a.reshape(-1) for a in arrs]
        if rows * cols > n:
            parts.append(jnp.zeros((rows * cols - n,), arrs[0].dtype))
    return jnp.concatenate(parts).reshape(len(groups), rows, cols)


PIECE_ROWS = 16


def _pack_rows(groups, cols, row_mult):
    as_list = lambda a: list(a) if isinstance(a, (list, tuple)) else [a]
    n_rows = lambda a: sum(math.prod(b.shape) // cols for b in as_list(a))
    rows = sum(-(-n_rows(a) // PIECE_ROWS) * PIECE_ROWS for a in groups[0])
    total = -(-rows // row_mult) * row_mult
    dtype = as_list(groups[0][0])[0].dtype
    parts = []
    for arrs in groups:
        for a in arrs:
            parts += [b.reshape(-1, cols) for b in as_list(a)]
            pad = -n_rows(a) % PIECE_ROWS
            if pad:
                parts.append(jnp.zeros((pad, cols), dtype))
        if total > rows:
            parts.append(jnp.zeros((total - rows, cols), dtype))
    return jnp.concatenate(parts, axis=0).reshape(len(groups), total, cols)


def _unpack_rows(packed, shapes):
    cols = packed.shape[-1]
    out, o = [], 0
    for s in shapes:
        r = math.prod(s) // cols
        out.append(packed[o:o + r].reshape(s))
        o += -(-r // PIECE_ROWS) * PIECE_ROWS
    return out


def _unpack(packed, shapes):
    flat = packed.reshape(-1)
    out, o = [], 0
    for s in shapes:
        n = math.prod(s)
        out.append(flat[o:o + n].reshape(s))
        o += n
    return out


def _orig_cols():
    cols = []
    for n, s in zip(ORIG_NAMES, ORIG_SIZES):
        if n == "dt":
            cols.append((OFF["small"] + DT_LANE, s))
        elif n == "f":
            cols.append((OFF["small"] + F_LANE, s))
        else:
            cols.append((OFF[n], s))
    return cols


def _rows_to_padded(per_chip):
    q_in = N_IN // N_CHIP
    orig_off = dict(zip(ORIG_NAMES, [sum(ORIG_SIZES[:i]) for i in range(len(ORIG_SIZES))]))
    size = dict(zip(ORIG_NAMES, ORIG_SIZES))
    parts = []
    for n in PAD_ORDER + ("dt", "f"):
        lo, hi = orig_off[n], orig_off[n] + size[n]
        for k in range(N_CHIP):
            a, b = max(lo, k * q_in), min(hi, (k + 1) * q_in)
            if a < b:
                parts.append(per_chip[k][a - k * q_in:b - k * q_in])
    parts.append(jnp.zeros((SMALL_W - size["dt"] - size["f"], per_chip[0].shape[1]), per_chip[0].dtype))
    return jnp.concatenate(parts, axis=0)


def _rows_from_padded(w_pad, lo, hi):
    parts, o = [], 0
    for start, s in _orig_cols():
        a, b = max(lo, o), min(hi, o + s)
        if a < b:
            parts.append(w_pad[start + a - o:start + b - o])
        o += s
    return parts


def _lanes_row(parts, width=LANES):
    v = jnp.concatenate([p.reshape(-1) for p in parts])
    return jnp.pad(v, (0, width - v.shape[0])).reshape(1, width)


def kernel(x, norm_w, w_in, fg_bias, ssm_conv_w, ssm_conv_b, dt_bias, a_log, d_skip, ssm_norm_w, sc_conv_w, sc_conv_b, cf_conv_w, cf_conv_b, cf_ln_w, cf_ln_b, w_gate, b_gate, w_branch, w_out, final_norm_w, loss_target, m_norm_w, m_w_in, m_fg_bias, m_ssm_conv_w, m_ssm_conv_b, m_dt_bias, m_a_log, m_d_skip, m_ssm_norm_w, m_sc_conv_w, m_sc_conv_b, m_cf_conv_w, m_cf_conv_b, m_cf_ln_w, m_cf_ln_b, m_w_gate, m_b_gate, m_w_branch, m_w_out, m_final_norm_w, v_norm_w, v_w_in, v_fg_bias, v_ssm_conv_w, v_ssm_conv_b, v_dt_bias, v_a_log, v_d_skip, v_ssm_norm_w, v_sc_conv_w, v_sc_conv_b, v_cf_conv_w, v_cf_conv_b, v_cf_ln_w, v_cf_ln_b, v_w_gate, v_b_gate, v_w_branch, v_w_out, v_final_norm_w):
    wts = dict(norm_w=norm_w, w_in=w_in, fg_bias=fg_bias, ssm_conv_w=ssm_conv_w, ssm_conv_b=ssm_conv_b, dt_bias=dt_bias,
               a_log=a_log, d_skip=d_skip, ssm_norm_w=ssm_norm_w, sc_conv_w=sc_conv_w, sc_conv_b=sc_conv_b,
               cf_conv_w=cf_conv_w, cf_conv_b=cf_conv_b, cf_ln_w=cf_ln_w, cf_ln_b=cf_ln_b, w_gate=w_gate, b_gate=b_gate,
               w_branch=w_branch, w_out=w_out, final_norm_w=final_norm_w)
    mom = dict(norm_w=m_norm_w, w_in=m_w_in, fg_bias=m_fg_bias, ssm_conv_w=m_ssm_conv_w, ssm_conv_b=m_ssm_conv_b,
               dt_bias=m_dt_bias, a_log=m_a_log, d_skip=m_d_skip, ssm_norm_w=m_ssm_norm_w, sc_conv_w=m_sc_conv_w,
               sc_conv_b=m_sc_conv_b, cf_conv_w=m_cf_conv_w, cf_conv_b=m_cf_conv_b, cf_ln_w=m_cf_ln_w, cf_ln_b=m_cf_ln_b,
               w_gate=m_w_gate, b_gate=m_b_gate, w_branch=m_w_branch, w_out=m_w_out, final_norm_w=m_final_norm_w)
    vel = dict(norm_w=v_norm_w, w_in=v_w_in, fg_bias=v_fg_bias, ssm_conv_w=v_ssm_conv_w, ssm_conv_b=v_ssm_conv_b,
               dt_bias=v_dt_bias, a_log=v_a_log, d_skip=v_d_skip, ssm_norm_w=v_ssm_norm_w, sc_conv_w=v_sc_conv_w,
               sc_conv_b=v_sc_conv_b, cf_conv_w=v_cf_conv_w, cf_conv_b=v_cf_conv_b, cf_ln_w=v_cf_ln_w, cf_ln_b=v_cf_ln_b,
               w_gate=v_w_gate, b_gate=v_b_gate, w_branch=v_w_branch, w_out=v_w_out, final_norm_w=v_final_norm_w)
    L = norm_w.shape[0]
    S, D = x.shape[1], x.shape[2]
    assert D == D_MODEL and x.shape[0] == 1
    xi, yi, ci = _place()
    chip = 2 * xi + yi

    sh_shapes = [wts[n].shape for n in SMALL_SHARDED]
    got = gather_all(_pack([wts[n] for n in SMALL_SHARDED], LANES, 8), name="gather_small_w")
    per_chip = [_unpack(got[2 * k], sh_shapes) for k in range(N_CHIP)]
    full_small = {n: jnp.concatenate([per_chip[k][i] for k in range(N_CHIP)], axis=-1) for i, n in enumerate(SMALL_SHARDED)}

    q_in, q_d = N_IN // N_CHIP, D // N_CHIP
    sent_shapes = [(q_in, D), (N_BRANCH, q_d, D), (N_BRANCH, BW, q_d), (q_d, D)]
    def own_shard(l):
        packed = _pack_rows([[w_in[l].T.astype(BF16), w_gate[l].astype(BF16), w_branch[l].astype(BF16),
                              w_out[l].astype(BF16)]], PACK_C, PACK_ROWS)[0]
        half_rows = packed.shape[0] // 2
        return lax.dynamic_update_slice(lax.empty((N_CHIP, 2, half_rows, PACK_C), BF16),
                                        packed.reshape(1, 2, half_rows, PACK_C), (chip, 0, 0, 0))

    def whole_weights(got):
        pc = [_unpack_rows(got[k].reshape(-1, PACK_C), sent_shapes) for k in range(N_CHIP)]
        return dict(wpt=_rows_to_padded([pc[k][0] for k in range(N_CHIP)]),
                    wg=jnp.concatenate([pc[k][1] for k in range(N_CHIP)], axis=1),
                    wb=jnp.concatenate([pc[k][2] for k in range(N_CHIP)], axis=2),
                    wo=jnp.concatenate([pc[k][3] for k in range(N_CHIP)], axis=0))

    lw = [whole_weights(gather_chips(own_shard(0), name="gather_w"))]
    saved = []
    xl = x[0]
    for l in range(L):
        w = lw[l]
        h = rms_fwd(xl, norm_w[l][None], name="rms_fwd")
        if l + 1 < L:
            u, nxt = mm(h, w["wpt"], tb=True, comm=("gather_ici", own_shard(l + 1)), name="mm_in_gather")
        else:
            u = mm(h, w["wpt"], tb=True, name="mm_in")
        bias_small = _lanes_row([dt_bias[l], fg_bias[l]])
        sp, csum = small_fwd(u, bias_small, name="small_fwd")
        c8 = csum[:, F_LANE:F_LANE + FOX_HEADS].T * LOG2E
        c_col, c_row = c8[:, :, None], c8[:, None, :]
        o, lse, y_a = attn_fwd(u, c_row, name="attn_fwd")
        xbc = conv_fwd(u, full_small["ssm_conv_w"][l], ssm_conv_b[l][None], mode="plain", a_off=OFF["xbc"], b_off=0,
                       C=SSM_CONV_DIM, name="conv_ssm_fwd")
        alog_row = _lanes_row([a_log[l]])
        dskip_l = jnp.repeat(d_skip[l], SSM_P)[None]
        y_ssd, hs = ssd_fwd(xbc, sp, alog_row, dskip_l, name="ssd_fwd")
        cv_c = conv_fwd(u, full_small["sc_conv_w"][l], sc_conv_b[l][None], mode="mul", a_off=OFF["scc"], b_off=OFF["scx"],
                        C=BW, name="conv_sc_fwd")
        cv_d = conv_fwd(u, full_small["cf_conv_w"][l], cf_conv_b[l][None], mode="glu", a_off=OFF["glu"], b_off=OFF["glu"] + BW,
                        C=BW, name="conv_cf_fwd")
        y_b, y_c, y_d = post_fwd(u, y_ssd, cv_c, cv_d, ssm_norm_w[l][None], cf_ln_w[l][None], cf_ln_b[l][None], name="post_fwd")
        merged, gates, proj = merge_fwd(h, (y_a, y_b, y_c, y_d), w["wg"], full_small["b_gate"][l][:, None, :], w["wb"],
                                        name="merge_fwd")
        if l + 1 < L:
            x_next, nxt = mm(merged, w["wo"], add=xl, comm=("gather_d2d", nxt), name="mm_out_gather")
            lw.append(whole_weights(nxt))
        else:
            x_next = mm(merged, w["wo"], add=xl, name="mm_out")
        saved.append(dict(x=xl, h=h, u=u, bias_small=bias_small, sp=sp, c_col=c_col, c_row=c_row, o=o, lse=lse, xbc=xbc,
                          alog_row=alog_row, dskip_l=dskip_l, hs=hs, y_ssd=y_ssd, cv_c=cv_c, cv_d=cv_d,
                          ys=(y_a, y_b, y_c, y_d), merged=merged, gates=gates, proj=proj))
        xl = x_next

    sq, dx, d_final = loss_head(xl, final_norm_w[None], loss_target[0], name="loss_head")
    loss = lax.psum(sq[0, 0] * (0.5 / D), ("x", "y", "c"))

    small_g = {n: [None] * L for n in SMALL if n != "final_norm_w"}
    big_g = {n: [None] * L for n in BIG}

    def finish_exchange(l, part, rcv):
        full = share_halves(add_chips(part, rcv, jnp.stack([chip, ci]).astype(jnp.int32), name="add_chips"), name="share_halves")
        g_in, g_wg, g_wb, g_wo = _unpack_rows(full.reshape(-1, PACK_C),
                                              [(q_in, D), (N_BRANCH, q_d, D), (N_BRANCH, q_d, BW), (q_d, D)])
        big_g["w_in"][l] = g_in.T
        big_g["w_gate"][l] = g_wg
        big_g["w_branch"][l] = jnp.transpose(g_wb, (0, 2, 1))
        big_g["w_out"][l] = g_wo

    for l in reversed(range(L)):
        w, sv = lw[l], saved[l]
        u, h = sv["u"], sv["h"]
        dm = mm(dx, w["wo"], tb=True, name="mm_dmerged")
        d_wo = mm(sv["merged"], dx, ta=True, out_dtype=BF16, name="mm_dwo")
        dp, dg, dbg = merge_bwd(dm, sv["gates"], sv["proj"], name="merge_bwd")
        dys = [mm(dp, w["wb"][i], tb=True, K=D, a_koff=i * D, name="mm_dy") for i in range(N_BRANCH)]
        d_wbt = [mm(dp, sv["ys"][i], ta=True, M=D, a_moff=i * D, out_dtype=BF16, name="mm_dwb")
                 for i in range(N_BRANCH)]
        d_wg = mm(h, dg, ta=True, out_dtype=BF16, n_groups=N_BRANCH, name="mm_dwg")
        (do, delta, dga, dz, dscb, dgc, dgd, dy_ssd, dcv_c, dcv_d, dnw, dlnw, dlnb) = post_bwd(
            dys[0], dys[1], dys[2], dys[3], u, sv["o"], sv["y_ssd"], sv["cv_c"], sv["cv_d"],
            ssm_norm_w[l][None], cf_ln_w[l][None], cf_ln_b[l][None], name="post_bwd")
        delta_row = delta[:, :FOX_HEADS].T[:, None, :]
        lse_row = jnp.transpose(sv["lse"], (0, 2, 1))
        dq, dk, dv, dc_col = attn_bwd(u, do, sv["c_col"], lse_row, delta_row, name="attn_bwd")
        dscc, dscx, d_scw, d_scb = conv_bwd(dcv_c, u, full_small["sc_conv_w"][l], mode="mul", a_off=OFF["scc"],
                                            b_off=OFF["scx"], C=BW, name="conv_sc_bwd")
        dglua, dglug, d_cfw, d_cfb = conv_bwd(dcv_d, u, full_small["cf_conv_w"][l], mode="glu", a_off=OFF["glu"],
                                              b_off=OFF["glu"] + BW, C=BW, name="conv_cf_bwd")
        dxbc_pre, ddt, d_a, d_dl = ssd_bwd(dy_ssd, sv["xbc"], sv["sp"], sv["alog_row"], sv["dskip_l"], sv["hs"], name="ssd_bwd")
        dxbc, d_ssmw, d_ssmb = conv_bwd(dxbc_pre, u, full_small["ssm_conv_w"][l], mode="plain", a_off=OFF["xbc"], b_off=0,
                                        C=SSM_CONV_DIM, name="conv_ssm_bwd")
        dc_full = jnp.pad(dc_col[:, :, 0].T, ((0, 0), (F_LANE, LANES - F_LANE - FOX_HEADS)))
        du_small, dbias_small = small_bwd(dc_full, ddt, u, sv["bias_small"], name="small_bwd")
        by_name = dict(q=dq.astype(BF16), k=dk, v=dv, ga=dga, z=dz, scb=dscb, scc=dscc, scx=dscx, gc=dgc, gd=dgd, xbc=dxbc)
        du = jnp.concatenate([jnp.concatenate([dglua, dglug], axis=1) if n == "glu" else by_name[n] for n in PAD_ORDER]
                             + [du_small], axis=1)
        d_wpt = mm(du, h, ta=True, out_dtype=BF16, name="mm_dwp")

        dest = [[_rows_from_padded(d_wpt, k * q_in, (k + 1) * q_in), d_wg[:, k * q_d:(k + 1) * q_d]]
                + [t[k * q_d:(k + 1) * q_d] for t in d_wbt] + [d_wo[k * q_d:(k + 1) * q_d]] for k in range(N_CHIP)]
        g_all = _pack_rows(dest, PACK_C, PACK_ROWS)
        g_all = g_all.reshape(N_CHIP, 2, g_all.shape[1] // 2, PACK_C)
        sib = swap_partials(g_all, name="swap_partials")
        part = add_sibling(g_all, sib, ci.reshape(1).astype(jnp.int32), name="add_sibling")
        dh_gate = mm(dg, w["wg"], tb=True, name="mm_dh_gate")
        dh, rcv = mm(du, w["wpt"], add=dh_gate, comm=("scatter", part), name="mm_dh_in_scatter")
        finish_exchange(l, part, rcv)
        dx, d_nw = rms_bwd(dh, sv["x"], norm_w[l][None], dx, name="rms_bwd")

        a_neg = -jnp.exp(a_log[l])
        sg = dict(norm_w=d_nw[0], fg_bias=dbias_small[0, F_LANE:F_LANE + FOX_HEADS], ssm_conv_w=d_ssmw, ssm_conv_b=d_ssmb[0],
                  dt_bias=dbias_small[0, DT_LANE:DT_LANE + SSM_HEADS], a_log=d_a[0, :SSM_HEADS] * a_neg,
                  d_skip=d_dl.reshape(SSM_HEADS, SSM_P).sum(-1), ssm_norm_w=dnw[0], sc_conv_w=d_scw, sc_conv_b=d_scb[0],
                  cf_conv_w=d_cfw, cf_conv_b=d_cfb[0], cf_ln_w=dlnw[0], cf_ln_b=dlnb[0], b_gate=dbg.reshape(N_BRANCH, D))
        for n in sg:
            small_g[n][l] = sg[n]


    names = [n for n in SMALL if n != "final_norm_w"]
    stacked = [jnp.stack(small_g[n]) for n in names] + [d_final[0]]
    shapes = [a.shape for a in stacked]
    got = gather_all(_pack(stacked, LANES, 8), name="gather_small_g")
    tot = addn([got[d] for d in range(8)], out_dtype=F32, name="add_small_g")
    grads = dict(zip(names + ["final_norm_w"], _unpack(tot, shapes)))
    for n in SMALL_SHARDED:
        sz = wts[n].shape[-1]
        grads[n] = lax.dynamic_slice_in_dim(grads[n], chip * sz, sz, axis=grads[n].ndim - 1)
    for n in BIG:
        grads[n] = jnp.stack(big_g[n])

    delta, new_m, new_v = {}, {}, {}
    for n in BIG:
        two_d = lambda a: a.reshape(-1, a.shape[-1])
        d, nm, nv = adamw(two_d(wts[n]), two_d(grads[n]), two_d(mom[n]), two_d(vel[n]), name="adamw_" + n)
        delta[n], new_m[n], new_v[n] = (t.reshape(wts[n].shape) for t in (d, nm, nv))
    small_shapes = [wts[n].shape for n in SMALL]
    pk = lambda src: _pack([src[n] for n in SMALL], LANES, 8)
    d, nm, nv = adamw(pk(wts), pk(grads), pk(mom), pk(vel), name="adamw_small")
    for tgt, src in ((delta, d), (new_m, nm), (new_v, nv)):
        for n, a in zip(SMALL, _unpack(src, small_shapes)):
            tgt[n] = a

    return (loss, dx[None], *[grads[n] for n in WEIGHTS], *[delta[n] for n in WEIGHTS],
            *[new_m[n] for n in WEIGHTS], *[new_v[n] for n in WEIGHTS])
```

```python
import functools
import math

import jax
import jax.numpy as jnp
from jax import lax
from jax.experimental import pallas as pl
from jax.experimental.pallas import tpu as pltpu

F32, BF16 = jnp.float32, jnp.bfloat16
MESH = pl.DeviceIdType.MESH

D_MODEL = 2048
BW = D_MODEL // 2
FOX_HEADS, FOX_HD = 8, 128
SSM_HEADS, SSM_P, SSM_N, SSM_G = 16, 64, 128, 2
SSM_CONV_DIM = BW + 2 * SSM_G * SSM_N
N_BRANCH = 4
EPS = 1e-6
NEG = -1e30
LOG2E = 1.4426950408889634
ORIG_SIZES = (BW, BW, BW, FOX_HEADS, BW, BW, SSM_CONV_DIM, SSM_HEADS, BW, BW, BW, BW, 2 * BW, BW)
ORIG_NAMES = ("q", "k", "v", "f", "ga", "z", "xbc", "dt", "scb", "scc", "scx", "gc", "glu", "gd")
N_IN = sum(ORIG_SIZES)
PAD_ORDER = ("q", "k", "v", "ga", "z", "scb", "scc", "scx", "gc", "glu", "gd", "xbc")
SMALL_W = 512
DT_LANE, F_LANE = 0, SSM_HEADS
OFF = {}
_o = 0
for _n in PAD_ORDER:
    OFF[_n] = _o
    _o += ORIG_SIZES[ORIG_NAMES.index(_n)]
OFF["small"] = _o
NP = _o + SMALL_W
LANES = 128
CONV_HALO = 32

VMEM_LIMIT = 56 * 1024 * 1024
MM_VMEM_BUDGET = 40 * 1024 * 1024

ADAM_LR, ADAM_B1, ADAM_B2, ADAM_EPS, ADAM_WD, ADAM_STEP = 0.001, 0.9, 0.999, 1e-08, 0.01, 10

TILES = dict(row=512, post=256, att=1024, ssd=256, mm_m=1024, mm_n=1024, conv_c=256)


def _cp(sem=None):
    return pltpu.CompilerParams(dimension_semantics=sem, vmem_limit_bytes=VMEM_LIMIT)


def _tile(n, pref):
    t = 1 << (min(n, pref).bit_length() - 1)
    while n % t:
        t //= 2
    return t


def _sig(x):
    return 1.0 / (1.0 + jnp.exp(-x))


def _silu(x):
    return x * _sig(x)


def _dsilu(x):
    s = _sig(x)
    return s * (1.0 + x * (1.0 - s))


def _softplus(x):
    return jnp.maximum(x, 0.0) + jnp.log(1.0 + jnp.exp(-jnp.abs(x)))


def _comm_copies(kind, src_ref, dst_ref, ss, rs):
    x, y, c = _place()
    chips = [(1 - x, y), (x, 1 - y), (1 - x, 1 - y)]
    me = 2 * x + y
    sends, recvs = [], []
    for j, (cx, cy) in enumerate(chips):
        if kind == "gather_ici":
            mine, theirs = dst_ref.at[me, c], dst_ref.at[2 * cx + cy, c]
            sends.append(_rcopy(mine, mine, ss, rs, j, (cx, cy, c)))
            recvs.append(_rcopy(theirs, theirs, ss, rs, j, (x, y, c)))
        elif kind == "gather_d2d":
            landed, theirs = dst_ref.at[2 * cx + cy, c], dst_ref.at[2 * cx + cy, 1 - c]
            sends.append(_rcopy(landed, landed, ss, rs, j, (x, y, 1 - c)))
            recvs.append(_rcopy(theirs, theirs, ss, rs, j, (x, y, c)))
        else:
            sends.append(_rcopy(src_ref.at[2 * cx + cy], dst_ref.at[j], ss, rs, j, (cx, cy, c)))
            recvs.append(_rcopy(src_ref.at[0], dst_ref.at[j], ss, rs, j, (x, y, c)))
    return sends, recvs


def mm(a, b, *, name, out_dtype=F32, add=None, ta=False, tb=False, M=None, K=None, N=None,
       a_koff=0, a_moff=0, b_koff=0, b_noff=0, comm=None, n_groups=1):
    b3 = b.ndim == 3
    assert not b3 or (tb and b_koff == 0 and b_noff == 0)
    M = M or (a.shape[1] if ta else a.shape[0])
    K = K or (a.shape[0] if ta else a.shape[1])
    N = N or (b.shape[-2] if tb else b.shape[1])
    tm, tn = _tile(M, TILES["mm_m"]), _tile(N, TILES["mm_n"])
    sa, sb, so = a.dtype.itemsize, b.dtype.itemsize, jnp.dtype(out_dtype).itemsize

    def need(tk):
        return 2 * tm * tk * sa + 2 * tk * tn * sb + 2 * tm * tn * so + (8 * tm * tn if add is not None else 0) + 4 * tm * tn

    tk = b.shape[2] if b3 else K
    while need(tk) > MM_VMEM_BUDGET and tk % 256 == 0:
        tk //= 2
    assert K % tk == 0 and a_koff % tk == 0 and b_koff % tk == 0 and b_noff % tn == 0 and a_moff % tm == 0, (name, K, tk, tn)
    nk = K // tk
    ako, amo, bko, bno = a_koff // tk, a_moff // tm, b_koff // tk, b_noff // tn
    dims = (((0 if ta else 1,), (1 if tb else 0,)), ((), ()))

    n_in = 2 + (add is not None) + (comm is not None)
    grid = (M // tm, N // tn, nk)

    def body(*refs):
        a_ref, b_ref = refs[:2]
        add_ref = refs[2] if add is not None else None
        o_ref = refs[n_in]
        acc_ref = refs[n_in + 1 + (comm is not None)] if nk > 1 else None
        if comm is not None:
            step = (pl.program_id(0) * grid[1] + pl.program_id(1)) * grid[2] + pl.program_id(2)
            sends, recvs = _comm_copies(comm[0], refs[n_in - 1], refs[n_in + 1], refs[-2], refs[-1])

            @pl.when(step == 0)
            def _():
                for cp in sends:
                    cp.start()

        prod = lax.dot_general(a_ref[...].astype(BF16), b_ref[...].astype(BF16), dims, preferred_element_type=F32)

        def finish(acc):
            if add_ref is not None:
                acc = acc + add_ref[...]
            o_ref[...] = acc.astype(out_dtype)

        if nk == 1:
            finish(prod)
        else:
            k = pl.program_id(2)

            @pl.when(k == 0)
            def _():
                acc_ref[...] = prod

            @pl.when(k > 0)
            def _():
                acc_ref[...] += prod

            @pl.when(k == nk - 1)
            def _():
                finish(acc_ref[...])

        if comm is not None:
            @pl.when(step == grid[0] * grid[1] * grid[2] - 1)
            def _():
                for cp in recvs:
                    cp.wait_recv()
                for cp in sends:
                    cp.wait_send()

    if ta:
        a_spec = pl.BlockSpec((tk, tm), lambda i, j, k: (k + ako, i + amo))
    else:
        a_spec = pl.BlockSpec((tm, tk), lambda i, j, k: (i, k + ako))
    if b3:
        assert tk == b.shape[2], (name, tk)
        b_spec = pl.BlockSpec((None, tn, tk), lambda i, j, k: (k, j, 0))
    elif tb:
        b_spec = pl.BlockSpec((tn, tk), lambda i, j, k: (j + bno, k + bko))
    else:
        b_spec = pl.BlockSpec((tk, tn), lambda i, j, k: (k + bko, j + bno))
    in_specs = [a_spec, b_spec]
    args = [a, b]
    if add is not None:
        in_specs.append(pl.BlockSpec((tm, tn), lambda i, j, k: (i, j)))
        args.append(add)
    if n_groups == 1:
        out_spec = pl.BlockSpec((tm, tn), lambda i, j, k: (i, j))
        out_shape = jax.ShapeDtypeStruct((M, N), out_dtype)
    else:
        nb = N // n_groups // tn
        sh = nb.bit_length() - 1
        assert nb == 1 << sh, (name, nb)
        out_spec = pl.BlockSpec((None, tm, tn), lambda i, j, k: (lax.shift_right_logical(j, sh), i, lax.bitwise_and(j, nb - 1)))
        out_shape = jax.ShapeDtypeStruct((n_groups, M, N // n_groups), out_dtype)
    scratch = [pltpu.VMEM((tm, tn), F32)] if nk > 1 else []
    if comm is None:
        return pl.pallas_call(
            body, name=name, grid=grid, in_specs=in_specs, out_specs=out_spec, out_shape=out_shape,
            scratch_shapes=scratch, compiler_params=_cp(("parallel", "parallel", "arbitrary")),
        )(*args)
    kind, buf = comm
    any_spec = pl.BlockSpec(memory_space=pl.ANY)
    if kind == "scatter":
        comm_shape, alias = jax.ShapeDtypeStruct((3,) + buf.shape[1:], buf.dtype), {}
    else:
        comm_shape, alias = jax.ShapeDtypeStruct(buf.shape, buf.dtype), {n_in - 1: 1}
    return pl.pallas_call(
        body, name=name, grid=grid, in_specs=in_specs + [any_spec], out_specs=[out_spec, any_spec],
        out_shape=[out_shape, comm_shape], input_output_aliases=alias,
        scratch_shapes=scratch + [pltpu.SemaphoreType.DMA((3,)), pltpu.SemaphoreType.DMA((3,))],
        compiler_params=_cp(("arbitrary", "arbitrary", "arbitrary")),
    )(*args, buf)


def rms_fwd(x, w, *, name):
    S, D = x.shape
    ts = _tile(S, TILES["row"])

    def body(x_ref, w_ref, o_ref):
        xf = x_ref[...]
        r = lax.rsqrt(jnp.mean(xf * xf, axis=-1, keepdims=True) + EPS)
        o_ref[...] = (xf * r * w_ref[...]).astype(BF16)

    return pl.pallas_call(
        body, name=name, grid=(S // ts,),
        in_specs=[pl.BlockSpec((ts, D), lambda i: (i, 0)), pl.BlockSpec((1, D), lambda i: (0, 0))],
        out_specs=pl.BlockSpec((ts, D), lambda i: (i, 0)),
        out_shape=jax.ShapeDtypeStruct((S, D), BF16), compiler_params=_cp(("parallel",)),
    )(x, w)


def rms_bwd(dh, x, w, dres, *, name):
    S, D = x.shape
    ts = _tile(S, TILES["row"])

    def body(dh_ref, x_ref, w_ref, dres_ref, dx_ref, dw_ref):
        xf = x_ref[...]
        r = lax.rsqrt(jnp.mean(xf * xf, axis=-1, keepdims=True) + EPS)
        xh = xf * r
        g = dh_ref[...]
        dxh = g * w_ref[...]
        dx_ref[...] = dres_ref[...] + r * (dxh - xh * jnp.mean(dxh * xh, axis=-1, keepdims=True))
        part = jnp.sum(g * xh, axis=0, keepdims=True)

        @pl.when(pl.program_id(0) == 0)
        def _():
            dw_ref[...] = part

        @pl.when(pl.program_id(0) > 0)
        def _():
            dw_ref[...] += part

    row = pl.BlockSpec((ts, D), lambda i: (i, 0))
    vec = pl.BlockSpec((1, D), lambda i: (0, 0))
    return pl.pallas_call(
        body, name=name, grid=(S // ts,), in_specs=[row, row, vec, row], out_specs=[row, vec],
        out_shape=[jax.ShapeDtypeStruct((S, D), F32), jax.ShapeDtypeStruct((1, D), F32)],
        compiler_params=_cp(("arbitrary",)),
    )(dh, x, w, dres)


def loss_head(x, w, target, *, name):
    S, D = x.shape
    ts = _tile(S, TILES["row"])

    def body(x_ref, w_ref, t_ref, loss_ref, dx_ref, dw_ref):
        xf = x_ref[...]
        r = lax.rsqrt(jnp.mean(xf * xf, axis=-1, keepdims=True) + EPS)
        xh = xf * r
        err = xh * w_ref[...] - t_ref[...]
        sq = jnp.sum(jnp.sum(err * err, axis=0, keepdims=True), axis=1, keepdims=True)
        dy = err * (1.0 / D)
        dxh = dy * w_ref[...]
        dx_ref[...] = r * (dxh - xh * jnp.mean(dxh * xh, axis=-1, keepdims=True))
        part = jnp.sum(dy * xh, axis=0, keepdims=True)

        @pl.when(pl.program_id(0) == 0)
        def _():
            dw_ref[...] = part
            loss_ref[...] = jnp.broadcast_to(sq, (8, LANES))

        @pl.when(pl.program_id(0) > 0)
        def _():
            dw_ref[...] += part
            loss_ref[...] += jnp.broadcast_to(sq, (8, LANES))

    row = pl.BlockSpec((ts, D), lambda i: (i, 0))
    vec = pl.BlockSpec((1, D), lambda i: (0, 0))
    return pl.pallas_call(
        body, name=name, grid=(S // ts,), in_specs=[row, vec, row],
        out_specs=[pl.BlockSpec((8, LANES), lambda i: (0, 0)), row, vec],
        out_shape=[jax.ShapeDtypeStruct((8, LANES), F32), jax.ShapeDtypeStruct((S, D), F32),
                   jax.ShapeDtypeStruct((1, D), F32)],
        compiler_params=_cp(("arbitrary",)),
    )(x, w, target)


def _shift_scan(v, buf_ref, n, reverse=False):
    buf_ref[pl.ds(0, n), :] = jnp.zeros((n, LANES), F32)
    buf_ref[pl.ds(2 * n, n), :] = jnp.zeros((n, LANES), F32)
    s = 1
    while s < n:
        buf_ref[pl.ds(n, n), :] = v
        v = v + buf_ref[pl.ds(n + s if reverse else n - s, n), :]
        s *= 2
    return v


def small_fwd(u, bias, *, name):
    S = u.shape[0]
    ts = _tile(S, TILES["row"])
    cb = OFF["small"] // LANES

    def body(u_ref, b_ref, sp_ref, c_ref, buf_ref, carry_ref):
        x = u_ref[...] + b_ref[...]
        sp_ref[...] = _softplus(x)
        lf = jnp.minimum(x, 0.0) - jnp.log(1.0 + jnp.exp(-jnp.abs(x)))

        @pl.when(pl.program_id(0) == 0)
        def _():
            carry_ref[...] = jnp.zeros((8, LANES), F32)

        c = _shift_scan(lf, buf_ref, ts) + carry_ref[pl.ds(0, 1), :]
        c_ref[...] = c
        carry_ref[...] = jnp.broadcast_to(c_ref[pl.ds(ts - 1, 1), :], (8, LANES))

    blk = pl.BlockSpec((ts, LANES), lambda i: (i, 0))
    return pl.pallas_call(
        body, name=name, grid=(S // ts,),
        in_specs=[pl.BlockSpec((ts, LANES), lambda i: (i, cb)), pl.BlockSpec((1, LANES), lambda i: (0, 0))],
        out_specs=[blk, blk], out_shape=[jax.ShapeDtypeStruct((S, LANES), F32)] * 2,
        scratch_shapes=[pltpu.VMEM((3 * ts, LANES), F32), pltpu.VMEM((8, LANES), F32)],
        compiler_params=_cp(("arbitrary",)),
    )(u, bias)


def small_bwd(dc, dsp, u, bias, *, name):
    S = u.shape[0]
    ts = _tile(S, TILES["row"])
    cb = OFF["small"] // LANES
    nt = S // ts

    def body(dc_ref, dsp_ref, u_ref, b_ref, du_ref, db_ref, buf_ref, carry_ref):
        x = u_ref[...] + b_ref[...]

        @pl.when(pl.program_id(0) == 0)
        def _():
            carry_ref[...] = jnp.zeros((8, LANES), F32)

        dlf = _shift_scan(dc_ref[...], buf_ref, ts, reverse=True) + carry_ref[pl.ds(0, 1), :]
        buf_ref[pl.ds(0, ts), :] = dlf
        carry_ref[...] = jnp.broadcast_to(buf_ref[pl.ds(0, 1), :], (8, LANES))
        sg = _sig(x)
        dx = dlf * (1.0 - sg) + dsp_ref[...] * sg
        du_ref[...] = jnp.concatenate([dx, jnp.zeros((ts, SMALL_W - LANES), F32)], axis=1).astype(BF16)
        part = jnp.sum(dx, axis=0, keepdims=True)

        @pl.when(pl.program_id(0) == 0)
        def _():
            db_ref[...] = part

        @pl.when(pl.program_id(0) > 0)
        def _():
            db_ref[...] += part

    rev = pl.BlockSpec((ts, LANES), lambda i: (nt - 1 - i, 0))
    return pl.pallas_call(
        body, name=name, grid=(nt,),
        in_specs=[rev, rev, pl.BlockSpec((ts, LANES), lambda i: (nt - 1 - i, cb)), pl.BlockSpec((1, LANES), lambda i: (0, 0))],
        out_specs=[pl.BlockSpec((ts, SMALL_W), lambda i: (nt - 1 - i, 0)), pl.BlockSpec((1, LANES), lambda i: (0, 0))],
        out_shape=[jax.ShapeDtypeStruct((S, SMALL_W), BF16), jax.ShapeDtypeStruct((1, LANES), F32)],
        scratch_shapes=[pltpu.VMEM((3 * ts, LANES), F32), pltpu.VMEM((8, LANES), F32)],
        compiler_params=_cp(("arbitrary",)),
    )(dc, dsp, u, bias)


def _conv_in(mode, a, b):
    if mode == "plain":
        return a
    if mode == "mul":
        return a * b
    return a * _sig(b)


SUBLANES = 8
MANY_TAPS = 8


def _row_taps(src_ref, sh_ref, n_rows, many):
    if many:
        for b in range(1, SUBLANES):
            sh_ref[b, pl.ds(0, n_rows - SUBLANES), :] = src_ref[pl.ds(b, n_rows - SUBLANES), :]

    def tap(off, n):
        a, b = divmod(off, SUBLANES)
        if not many or b == 0:
            return src_ref[pl.ds(off, n), :]
        return sh_ref[b, pl.ds(SUBLANES * a, n), :]

    return tap


def conv_fwd(u, w, bias, *, mode, a_off, b_off, C, name):
    S = u.shape[0]
    Kc = w.shape[0]
    ts, cb, H = _tile(S, TILES["row"]), _tile(C, TILES["conv_c"]), CONV_HALO
    two = mode != "plain"
    rb = ts // H
    many = Kc >= MANY_TAPS

    def body(*refs):
        if two:
            a_ref, ap_ref, b_ref, bp_ref, w_ref, bias_ref, o_ref, x_ref = refs[:8]
            cur, prev = _conv_in(mode, a_ref[...], b_ref[...]), _conv_in(mode, ap_ref[...], bp_ref[...])
        else:
            a_ref, ap_ref, w_ref, bias_ref, o_ref, x_ref = refs[:6]
            cur, prev = a_ref[...], ap_ref[...]
        x_ref[pl.ds(0, H), :] = jnp.where(pl.program_id(1) == 0, 0.0, prev)
        x_ref[pl.ds(H, ts), :] = cur
        tap = _row_taps(x_ref, refs[-1], ts + H, many)
        acc = jnp.broadcast_to(bias_ref[...], (ts, cb))
        for j in range(Kc):
            acc = acc + w_ref[pl.ds(j, 1), :] * tap(H - (Kc - 1) + j, ts)
        o_ref[...] = acc

    def cur_spec(off):
        return pl.BlockSpec((ts, cb), lambda c, i: (i, c + off // cb))

    def prev_spec(off):
        return pl.BlockSpec((H, cb), lambda c, i: (jnp.maximum(i * rb - 1, 0), c + off // cb))

    in_specs, args = [cur_spec(a_off), prev_spec(a_off)], [u, u]
    if two:
        in_specs += [cur_spec(b_off), prev_spec(b_off)]
        args += [u, u]
    in_specs += [pl.BlockSpec((Kc, cb), lambda c, i: (0, c)), pl.BlockSpec((1, cb), lambda c, i: (0, c))]
    return pl.pallas_call(
        body, name=name, grid=(C // cb, S // ts), in_specs=in_specs,
        out_specs=pl.BlockSpec((ts, cb), lambda c, i: (i, c)), out_shape=jax.ShapeDtypeStruct((S, C), F32),
        scratch_shapes=[pltpu.VMEM((ts + H, cb), F32), pltpu.VMEM((SUBLANES if many else 1, ts + H, cb), F32)],
        compiler_params=_cp(("parallel", "arbitrary")),
    )(*args, w, bias)


def conv_bwd(dy, u, w, *, mode, a_off, b_off, C, name):
    S = u.shape[0]
    Kc = w.shape[0]
    ts, cb, H = _tile(S, TILES["row"]), _tile(C, TILES["conv_c"]), CONV_HALO
    two = mode != "plain"
    rb, nt = ts // H, S // ts

    many = Kc >= MANY_TAPS

    def body(*refs):
        if two:
            dy_ref, dyn_ref, a_ref, ap_ref, b_ref, bp_ref, w_ref, da_ref, db_ref, dw_ref, dbias_ref, x_ref, g_ref = refs[:13]
            a, b = a_ref[...], b_ref[...]
            cur, prev = _conv_in(mode, a, b), _conv_in(mode, ap_ref[...], bp_ref[...])
        else:
            dy_ref, dyn_ref, a_ref, ap_ref, w_ref, da_ref, dw_ref, dbias_ref, x_ref, g_ref = refs[:10]
            cur, prev = a_ref[...], ap_ref[...]
        i = pl.program_id(1)
        x_ref[pl.ds(0, H), :] = jnp.where(i == 0, 0.0, prev)
        x_ref[pl.ds(H, ts), :] = cur
        g = dy_ref[...]
        g_ref[pl.ds(0, ts), :] = g
        g_ref[pl.ds(ts, H), :] = jnp.where(i == nt - 1, 0.0, dyn_ref[...])
        x_tap = _row_taps(x_ref, refs[-2], ts + H, many)
        g_tap = _row_taps(g_ref, refs[-1], ts + H, many)

        @pl.when(i == 0)
        def _():
            dw_ref[...] = jnp.zeros((Kc, cb), F32)
            dbias_ref[...] = jnp.zeros((1, cb), F32)

        dbias_ref[...] += jnp.sum(g, axis=0, keepdims=True)
        dx = jnp.zeros((ts, cb), F32)
        for j in range(Kc):
            dx = dx + w_ref[pl.ds(j, 1), :] * g_tap(Kc - 1 - j, ts)
            dw_ref[pl.ds(j, 1), :] += jnp.sum(g * x_tap(H - (Kc - 1) + j, ts), axis=0, keepdims=True)
        if mode == "plain":
            da_ref[...] = dx.astype(BF16)
        elif mode == "mul":
            da_ref[...] = (dx * b).astype(BF16)
            db_ref[...] = (dx * a).astype(BF16)
        else:
            sg = _sig(b)
            da_ref[...] = (dx * sg).astype(BF16)
            db_ref[...] = (dx * a * sg * (1.0 - sg)).astype(BF16)

    def cur_spec(off):
        return pl.BlockSpec((ts, cb), lambda c, i: (i, c + off // cb))

    def prev_spec(off):
        return pl.BlockSpec((H, cb), lambda c, i: (jnp.maximum(i * rb - 1, 0), c + off // cb))

    out_blk = pl.BlockSpec((ts, cb), lambda c, i: (i, c))
    in_specs = [out_blk, pl.BlockSpec((H, cb), lambda c, i: (jnp.minimum((i + 1) * rb, S // H - 1), c)),
                cur_spec(a_off), prev_spec(a_off)]
    args = [dy, dy, u, u]
    if two:
        in_specs += [cur_spec(b_off), prev_spec(b_off)]
        args += [u, u]
    in_specs.append(pl.BlockSpec((Kc, cb), lambda c, i: (0, c)))
    n_d = 2 if two else 1
    return pl.pallas_call(
        body, name=name, grid=(C // cb, nt), in_specs=in_specs,
        out_specs=[out_blk] * n_d + [pl.BlockSpec((Kc, cb), lambda c, i: (0, c)), pl.BlockSpec((1, cb), lambda c, i: (0, c))],
        out_shape=[jax.ShapeDtypeStruct((S, C), BF16)] * n_d + [jax.ShapeDtypeStruct((Kc, C), F32), jax.ShapeDtypeStruct((1, C), F32)],
        scratch_shapes=[pltpu.VMEM((ts + H, cb), F32), pltpu.VMEM((ts + H, cb), F32)]
        + [pltpu.VMEM((SUBLANES if many else 1, ts + H, cb), F32)] * 2,
        compiler_params=_cp(("parallel", "arbitrary")),
    )(*args, w)


def _usec(ts, name):
    return pl.BlockSpec((ts, BW), lambda i, o=OFF[name] // BW: (i, o))


def _acc_rows(ref, part):
    @pl.when(pl.program_id(0) == 0)
    def _():
        ref[...] = part

    @pl.when(pl.program_id(0) > 0)
    def _():
        ref[...] += part


def post_fwd(u, y_ssd, cv_c, cv_d, nw, lnw, lnb, *, name):
    S = u.shape[0]
    ts = _tile(S, TILES["post"])

    def body(z_ref, scb_ref, gc_ref, gd_ref, ys_ref, cc_ref, cd_ref, nw_ref, lw_ref, lb_ref, yb_ref, yc_ref, yd_ref):
        t = ys_ref[...] * _silu(z_ref[...])
        r = lax.rsqrt(jnp.mean(t * t, axis=-1, keepdims=True) + EPS)
        yb_ref[...] = (t * r * nw_ref[...]).astype(BF16)
        yc_ref[...] = (scb_ref[...] * cc_ref[...] * _silu(gc_ref[...])).astype(BF16)
        cf = cd_ref[...]
        mu = jnp.mean(cf, axis=-1, keepdims=True)
        xc = cf - mu
        rl = lax.rsqrt(jnp.mean(xc * xc, axis=-1, keepdims=True) + EPS)
        yln = xc * rl * lw_ref[...] + lb_ref[...]
        yd_ref[...] = (_silu(yln) * _silu(gd_ref[...])).astype(BF16)

    row = pl.BlockSpec((ts, BW), lambda i: (i, 0))
    vec = pl.BlockSpec((1, BW), lambda i: (0, 0))
    return pl.pallas_call(
        body, name=name, grid=(S // ts,),
        in_specs=[_usec(ts, "z"), _usec(ts, "scb"), _usec(ts, "gc"), _usec(ts, "gd"), row, row, row, vec, vec, vec],
        out_specs=[row] * 3, out_shape=[jax.ShapeDtypeStruct((S, BW), BF16)] * 3, compiler_params=_cp(("parallel",)),
    )(u, u, u, u, y_ssd, cv_c, cv_d, nw, lnw, lnb)


def post_bwd(dy_a, dy_b, dy_c, dy_d, u, o, y_ssd, cv_c, cv_d, nw, lnw, lnb, *, name):
    S = u.shape[0]
    ts = _tile(S, TILES["post"])

    def body(dya_ref, dyb_ref, dyc_ref, dyd_ref, ga_ref, z_ref, scb_ref, gc_ref, gd_ref, o_ref, ys_ref, cc_ref, cd_ref,
             nw_ref, lw_ref, lb_ref,
             do_ref, dl_ref, dga_ref, dz_ref, dscb_ref, dgc_ref, dgd_ref, dys_ref, dcc_ref, dcd_ref, dnw_ref, dlw_ref, dlb_ref):
        ga, ov, dya = ga_ref[...], o_ref[...], dya_ref[...]
        dob = (dya * _silu(ga)).astype(BF16)
        do_ref[...] = dob
        dga_ref[...] = (dya * ov * _dsilu(ga)).astype(BF16)
        prod = dob.astype(F32) * ov
        lane = lax.broadcasted_iota(jnp.int32, (ts, LANES), 1)
        delta = jnp.zeros((ts, LANES), F32)
        for h in range(FOX_HEADS):
            col = jnp.sum(prod[:, h * FOX_HD:(h + 1) * FOX_HD], axis=1, keepdims=True)
            delta = jnp.where(lane == h, col, delta)
        dl_ref[...] = delta
        ys, z, dyb = ys_ref[...], z_ref[...], dyb_ref[...]
        sz = _silu(z)
        t = ys * sz
        r = lax.rsqrt(jnp.mean(t * t, axis=-1, keepdims=True) + EPS)
        th = t * r
        dth = dyb * nw_ref[...]
        dt_ = r * (dth - th * jnp.mean(dth * th, axis=-1, keepdims=True))
        dys_ref[...] = dt_ * sz
        dz_ref[...] = (dt_ * ys * _dsilu(z)).astype(BF16)
        _acc_rows(dnw_ref, jnp.sum(dyb * th, axis=0, keepdims=True))
        scb, cc, gc, dyc = scb_ref[...], cc_ref[...], gc_ref[...], dyc_ref[...]
        sg = _silu(gc)
        dscb_ref[...] = (dyc * cc * sg).astype(BF16)
        dcc_ref[...] = dyc * scb * sg
        dgc_ref[...] = (dyc * scb * cc * _dsilu(gc)).astype(BF16)
        cf, gd, dyd = cd_ref[...], gd_ref[...], dyd_ref[...]
        mu = jnp.mean(cf, axis=-1, keepdims=True)
        xc = cf - mu
        rl = lax.rsqrt(jnp.mean(xc * xc, axis=-1, keepdims=True) + EPS)
        xh = xc * rl
        yln = xh * lw_ref[...] + lb_ref[...]
        dyln = dyd * _silu(gd) * _dsilu(yln)
        dgd_ref[...] = (dyd * _silu(yln) * _dsilu(gd)).astype(BF16)
        _acc_rows(dlw_ref, jnp.sum(dyln * xh, axis=0, keepdims=True))
        _acc_rows(dlb_ref, jnp.sum(dyln, axis=0, keepdims=True))
        dxh = dyln * lw_ref[...]
        dcd_ref[...] = rl * (dxh - jnp.mean(dxh, axis=-1, keepdims=True) - xh * jnp.mean(dxh * xh, axis=-1, keepdims=True))

    row = pl.BlockSpec((ts, BW), lambda i: (i, 0))
    vec = pl.BlockSpec((1, BW), lambda i: (0, 0))
    sd = jax.ShapeDtypeStruct
    return pl.pallas_call(
        body, name=name, grid=(S // ts,),
        in_specs=[row] * 4 + [_usec(ts, n) for n in ("ga", "z", "scb", "gc", "gd")] + [row] * 4 + [vec] * 3,
        out_specs=[row, pl.BlockSpec((ts, LANES), lambda i: (i, 0))] + [row] * 8 + [vec] * 3,
        out_shape=[sd((S, BW), BF16), sd((S, LANES), F32)] + [sd((S, BW), BF16)] * 5 + [sd((S, BW), F32)] * 3 + [sd((1, BW), F32)] * 3,
        compiler_params=_cp(("arbitrary",)),
    )(dy_a, dy_b, dy_c, dy_d, u, u, u, u, u, o, y_ssd, cv_c, cv_d, nw, lnw, lnb)


def merge_fwd(h, ys, wg, bg, wb, *, name):
    S, D = h.shape
    tm, tn = _tile(S, 1024), _tile(D, 512)
    nb = D // tn

    def body(h_ref, y0, y1, y2, y3, wg_ref, bg_ref, wb_ref, m_ref, g_ref, p_ref, acc_ref):
        i = pl.program_id(2)
        g = jnp.dot(h_ref[...], wg_ref[...], preferred_element_type=F32) + bg_ref[...]
        gate = _sig(g)
        for b, y_ref in enumerate((y0, y1, y2, y3)):
            @pl.when(i == b)
            def _(y_ref=y_ref):
                p = jnp.dot(y_ref[...], wb_ref[...], preferred_element_type=F32)
                g_ref[...] = gate.astype(BF16)
                p_ref[...] = p.astype(BF16)
                if b == 0:
                    acc_ref[...] = gate * p
                else:
                    acc_ref[...] += gate * p

        @pl.when(i == N_BRANCH - 1)
        def _():
            m_ref[...] = acc_ref[...].astype(BF16)

    yspec = pl.BlockSpec((tm, BW), lambda m, n, i: (m, 0))
    return pl.pallas_call(
        body, name=name, grid=(S // tm, nb, N_BRANCH),
        in_specs=[pl.BlockSpec((tm, D), lambda m, n, i: (m, 0)), yspec, yspec, yspec, yspec,
                  pl.BlockSpec((None, D, tn), lambda m, n, i: (i, 0, n)),
                  pl.BlockSpec((None, 1, tn), lambda m, n, i: (i, 0, n)),
                  pl.BlockSpec((None, BW, tn), lambda m, n, i: (i, 0, n))],
        out_specs=[pl.BlockSpec((tm, tn), lambda m, n, i: (m, n)),
                   pl.BlockSpec((tm, tn), lambda m, n, i: (m, i * nb + n)),
                   pl.BlockSpec((tm, tn), lambda m, n, i: (m, i * nb + n))],
        out_shape=[jax.ShapeDtypeStruct((S, D), BF16), jax.ShapeDtypeStruct((S, N_BRANCH * D), BF16),
                   jax.ShapeDtypeStruct((S, N_BRANCH * D), BF16)],
        scratch_shapes=[pltpu.VMEM((tm, tn), F32)],
        compiler_params=_cp(("parallel", "parallel", "arbitrary")),
    )(h, *ys, wg, bg, wb)


def merge_bwd(dm, gates, proj, *, name):
    S, D = dm.shape
    ts, tn = _tile(S, TILES["row"]), _tile(D, 512)
    nb = D // tn

    def body(dm_ref, g_ref, p_ref, dp_ref, dg_ref, db_ref):
        d = dm_ref[...]
        g = g_ref[...].astype(F32)
        dp_ref[...] = (d * g).astype(BF16)
        dg = d * p_ref[...].astype(F32) * g * (1.0 - g)
        dg_ref[...] = dg.astype(BF16)
        part = jnp.sum(dg, axis=0, keepdims=True)

        @pl.when(pl.program_id(2) == 0)
        def _():
            db_ref[...] = part

        @pl.when(pl.program_id(2) > 0)
        def _():
            db_ref[...] += part

    wide = pl.BlockSpec((ts, tn), lambda b, n, i: (i, b * nb + n))
    return pl.pallas_call(
        body, name=name, grid=(N_BRANCH, nb, S // ts),
        in_specs=[pl.BlockSpec((ts, tn), lambda b, n, i: (i, n)), wide, wide],
        out_specs=[wide, wide, pl.BlockSpec((1, tn), lambda b, n, i: (0, b * nb + n))],
        out_shape=[jax.ShapeDtypeStruct((S, N_BRANCH * D), BF16)] * 2 + [jax.ShapeDtypeStruct((1, N_BRANCH * D), F32)],
        compiler_params=_cp(("parallel", "parallel", "arbitrary")),
    )(dm, gates, proj)


_NT = (((1,), (1,)), ((), ()))
_TN = (((0,), (0,)), ((), ()))


def _tri_pairs(n, by_key):
    if by_key:
        pairs = [(i, j) for j in range(n) for i in range(j, n)]
    else:
        pairs = [(i, j) for i in range(n) for j in range(i + 1)]
    return (jnp.array([p[0] for p in pairs], jnp.int32), jnp.array([p[1] for p in pairs], jnp.int32))


def attn_fwd(u, c_row, *, name):
    S = u.shape[0]
    T = _tile(S, TILES["att"])
    n = S // T
    qo, ko, vo, go = (OFF[k] // FOX_HD for k in ("q", "k", "v", "ga"))
    scale = FOX_HD ** -0.5
    it, jt = _tri_pairs(n, by_key=False)

    def body(it_ref, jt_ref, q_ref, k_ref, v_ref, ck_ref, ga_ref, o_ref, lse_ref, ya_ref, m_ref, l_ref, acc_ref):
        i, j = it_ref[pl.program_id(1)], jt_ref[pl.program_id(1)]

        @pl.when(j == 0)
        def _():
            m_ref[...] = jnp.full((T, 1), NEG, F32)
            l_ref[...] = jnp.zeros((T, 1), F32)
            acc_ref[...] = jnp.zeros((T, FOX_HD), F32)

        def step(masked):
            qb = (q_ref[...] * (scale * LOG2E)).astype(BF16)
            s = lax.dot_general(qb, k_ref[...].astype(BF16), _NT, preferred_element_type=F32) - ck_ref[...]
            if masked:
                row = lax.broadcasted_iota(jnp.int32, (T, T), 0)
                col = lax.broadcasted_iota(jnp.int32, (T, T), 1)
                s = jnp.where(col <= row, s, NEG)
            m_old = m_ref[...]
            m_new = jnp.maximum(m_old, jnp.max(s, axis=1, keepdims=True))
            alpha = jnp.exp2(m_old - m_new)
            p = jnp.exp2(s - m_new)
            l_ref[...] = alpha * l_ref[...] + jnp.sum(p, axis=1, keepdims=True)
            p_hi = p.astype(BF16)
            p_lo = (p - p_hi.astype(F32)).astype(BF16)
            vb = v_ref[...].astype(BF16)
            pv = jnp.dot(p_hi, vb, preferred_element_type=F32) + jnp.dot(p_lo, vb, preferred_element_type=F32)
            acc_ref[...] = alpha * acc_ref[...] + pv
            m_ref[...] = m_new

        @pl.when(j < i)
        def _():
            step(False)

        @pl.when(j == i)
        def _():
            step(True)
            o = acc_ref[...] / l_ref[...]
            o_ref[...] = o
            lse_ref[...] = m_ref[...] + jnp.log(l_ref[...]) * LOG2E
            ya_ref[...] = (o * _silu(ga_ref[...])).astype(BF16)

    def qsec(off):
        return pl.BlockSpec((T, FOX_HD), lambda h, p, it, jt: (it[p], off + h))

    def ksec(off):
        return pl.BlockSpec((T, FOX_HD), lambda h, p, it, jt: (jt[p], off + h))

    out = pl.BlockSpec((T, FOX_HD), lambda h, p, it, jt: (it[p], h))
    colv = pl.BlockSpec((None, T, 1), lambda h, p, it, jt: (h, it[p], 0))
    return pl.pallas_call(
        body, name=name,
        grid_spec=pltpu.PrefetchScalarGridSpec(
            num_scalar_prefetch=2, grid=(FOX_HEADS, n * (n + 1) // 2),
            in_specs=[qsec(qo), ksec(ko), ksec(vo), pl.BlockSpec((None, 1, T), lambda h, p, it, jt: (h, 0, jt[p])), qsec(go)],
            out_specs=[out, colv, out],
            scratch_shapes=[pltpu.VMEM((T, 1), F32), pltpu.VMEM((T, 1), F32), pltpu.VMEM((T, FOX_HD), F32)]),
        out_shape=[jax.ShapeDtypeStruct((S, BW), F32), jax.ShapeDtypeStruct((FOX_HEADS, S, 1), F32),
                   jax.ShapeDtypeStruct((S, BW), BF16)],
        compiler_params=_cp(("parallel", "arbitrary")),
    )(it, jt, u, u, u, c_row, u)


def attn_bwd(u, do, c_col, lse_row, delta_row, *, name):
    S = u.shape[0]
    T = _tile(S, TILES["att"])
    n = S // T
    qo, ko, vo = (OFF[k] // FOX_HD for k in ("q", "k", "v"))
    scale = FOX_HD ** -0.5
    it, jt = _tri_pairs(n, by_key=True)

    def body(it_ref, jt_ref, q_ref, k_ref, v_ref, do_ref, ck_ref, lse_ref, dl_ref, dq_ref, dk_ref, dv_ref, dc_ref,
             dka_ref, dva_ref, dca_ref):
        i, j = it_ref[pl.program_id(1)], jt_ref[pl.program_id(1)]

        @pl.when(pl.program_id(1) == 0)
        def _():
            dq_ref[...] = jnp.zeros((S, FOX_HD), F32)

        @pl.when(i == j)
        def _():
            dka_ref[...] = jnp.zeros((T, FOX_HD), F32)
            dva_ref[...] = jnp.zeros((T, FOX_HD), F32)
            dca_ref[...] = jnp.zeros((T, 1), F32)

        def step(masked):
            qf, dob = q_ref[...], do_ref[...]
            qb = (qf * scale).astype(BF16)
            kf = k_ref[...]
            st = lax.dot_general(kf.astype(BF16), (qf * (scale * LOG2E)).astype(BF16), _NT, preferred_element_type=F32)
            pt = jnp.exp2(st - ck_ref[...] - lse_ref[...])
            if masked:
                kpos = lax.broadcasted_iota(jnp.int32, (T, T), 0)
                qpos = lax.broadcasted_iota(jnp.int32, (T, T), 1)
                pt = jnp.where(kpos <= qpos, pt, 0.0)
            dva_ref[...] += jnp.dot(pt.astype(BF16), dob, preferred_element_type=F32)
            dpt = lax.dot_general(v_ref[...].astype(BF16), dob, _NT, preferred_element_type=F32)
            dst = pt * (dpt - dl_ref[...])
            dca_ref[...] -= jnp.sum(dst, axis=1, keepdims=True)
            dsb = dst.astype(BF16)
            dka_ref[...] += jnp.dot(dsb, qb, preferred_element_type=F32)
            rows = pl.ds(pl.multiple_of(i * T, T), T)
            dq_ref[rows, :] += lax.dot_general(dsb, (kf * scale).astype(BF16), _TN, preferred_element_type=F32)

        @pl.when(i > j)
        def _():
            step(False)

        @pl.when(i == j)
        def _():
            step(True)

        @pl.when(i == n - 1)
        def _():
            dk_ref[...] = dka_ref[...].astype(BF16)
            dv_ref[...] = dva_ref[...].astype(BF16)
            dc_ref[...] = dca_ref[...]

    def qsec(off):
        return pl.BlockSpec((T, FOX_HD), lambda h, p, it, jt: (it[p], off + h))

    def ksec(off):
        return pl.BlockSpec((T, FOX_HD), lambda h, p, it, jt: (jt[p], off + h))

    qrow = pl.BlockSpec((None, 1, T), lambda h, p, it, jt: (h, 0, it[p]))
    kout = pl.BlockSpec((T, FOX_HD), lambda h, p, it, jt: (jt[p], h))
    kcol = pl.BlockSpec((None, T, 1), lambda h, p, it, jt: (h, jt[p], 0))
    return pl.pallas_call(
        body, name=name,
        grid_spec=pltpu.PrefetchScalarGridSpec(
            num_scalar_prefetch=2, grid=(FOX_HEADS, n * (n + 1) // 2),
            in_specs=[qsec(qo), ksec(ko), ksec(vo), pl.BlockSpec((T, FOX_HD), lambda h, p, it, jt: (it[p], h)),
                      kcol, qrow, qrow],
            out_specs=[pl.BlockSpec((S, FOX_HD), lambda h, p, it, jt: (0, h)), kout, kout, kcol],
            scratch_shapes=[pltpu.VMEM((T, FOX_HD), F32), pltpu.VMEM((T, FOX_HD), F32), pltpu.VMEM((T, 1), F32)]),
        out_shape=[jax.ShapeDtypeStruct((S, BW), F32), jax.ShapeDtypeStruct((S, BW), BF16),
                   jax.ShapeDtypeStruct((S, BW), BF16), jax.ShapeDtypeStruct((FOX_HEADS, S, 1), F32)],
        compiler_params=_cp(("parallel", "arbitrary")),
    )(it, jt, u, u, u, do, c_col, lse_row, delta_row)


N_PAIR = SSM_HEADS // 2
PAIRS_PER_GROUP = N_PAIR // SSM_G


def _sel_t():
    r = lax.broadcasted_iota(jnp.int32, (LANES, BW), 0)
    c = lax.broadcasted_iota(jnp.int32, (LANES, BW), 1)
    return (lax.shift_right_logical(c, 6) == r).astype(BF16)


def _sel():
    r = lax.broadcasted_iota(jnp.int32, (BW, LANES), 0)
    c = lax.broadcasted_iota(jnp.int32, (BW, LANES), 1)
    return (lax.shift_right_logical(r, 6) == c).astype(BF16)


def _dot3(x, m):
    hi = x.astype(BF16)
    r1 = x - hi.astype(F32)
    mid = r1.astype(BF16)
    lo = (r1 - mid.astype(F32)).astype(BF16)
    d = functools.partial(jnp.dot, preferred_element_type=F32)
    return d(hi, m) + d(mid, m) + d(lo, m)


def _ssd_common(x_ref, sp_ref, al_ref, buf_ref, big_ref, cst_ref, LC):
    pre = x_ref[...]
    sg = _sig(pre)
    act = pre * sg
    dt = sp_ref[...]
    a = -jnp.exp(al_ref[...])
    cs = _shift_scan(dt * a, buf_ref, LC)
    sel_t = _sel_t()
    dtl = _dot3(dt, sel_t)
    csl = _dot3(cs, sel_t)
    big_ref[...] = csl
    csl_last = big_ref[pl.ds(LC - 1, 1), :]
    cst_ref[...] = cs.T
    return pre, sg, act, dt, a, cs, dtl, csl, csl_last


def ssd_fwd(xbc, sp, alog, dskip_l, *, name):
    S = xbc.shape[0]
    LC = _tile(S, TILES["ssd"])
    nc = S // LC

    def body(x_ref, sp_ref, al_ref, dk_ref, y_ref, hs_ref, st_ref, buf_ref, big_ref, cst_ref):
        @pl.when(pl.program_id(0) == 0)
        def _():
            st_ref[...] = jnp.zeros((N_PAIR, SSM_N, LANES), F32)

        pre, sg, act, dt, a, cs, dtl, csl, csl_last = _ssd_common(x_ref, sp_ref, al_ref, buf_ref, big_ref, cst_ref, LC)
        xs, bm, cm = act[:, :BW], act[:, BW:BW + SSM_G * SSM_N], act[:, BW + SSM_G * SSM_N:]
        e_all = jnp.exp(csl)
        dec = jnp.exp(csl_last - csl)
        ad = jnp.exp(csl_last)
        xd = xs * dtl
        tril = lax.broadcasted_iota(jnp.int32, (LC, LC), 0) >= lax.broadcasted_iota(jnp.int32, (LC, LC), 1)
        lane = lax.broadcasted_iota(jnp.int32, (LC, LANES), 1)
        for g in range(SSM_G):
            bgt = bm[:, g * SSM_N:(g + 1) * SSM_N].T.astype(BF16)
            cgb = cm[:, g * SSM_N:(g + 1) * SSM_N].astype(BF16)
            cb = jnp.dot(cgb, bgt, preferred_element_type=F32)
            for q in range(PAIRS_PER_GROUP):
                pp = g * PAIRS_PER_GROUP + q
                ln = slice(pp * LANES, (pp + 1) * LANES)
                xp = xd[:, ln]
                xpb = xp.astype(BF16)
                yh = []
                for hh in range(2):
                    row_b = jnp.broadcast_to(cst_ref[pl.ds(2 * pp + hh, 1), :], (LC, LC))
                    lmat = jnp.exp(jnp.where(tril, row_b.T - row_b, NEG))
                    yh.append(jnp.dot((cb * lmat).astype(BF16), xpb, preferred_element_type=F32))
                hin = st_ref[pp]
                hs_ref[pp] = hin
                yoff = jnp.dot(cgb, hin.astype(BF16), preferred_element_type=F32) * e_all[:, ln]
                y_ref[:, ln] = jnp.where(lane < SSM_P, yh[0], yh[1]) + yoff + xs[:, ln] * dk_ref[:, ln]
                st_ref[pp] = hin * ad[:, ln] + jnp.dot(bgt, (xp * dec[:, ln]).astype(BF16), preferred_element_type=F32)

    return pl.pallas_call(
        body, name=name, grid=(nc,),
        in_specs=[pl.BlockSpec((LC, SSM_CONV_DIM), lambda c: (c, 0)), pl.BlockSpec((LC, LANES), lambda c: (c, 0)),
                  pl.BlockSpec((1, LANES), lambda c: (0, 0)), pl.BlockSpec((1, BW), lambda c: (0, 0))],
        out_specs=[pl.BlockSpec((LC, BW), lambda c: (c, 0)), pl.BlockSpec((None, N_PAIR, SSM_N, LANES), lambda c: (c, 0, 0, 0))],
        out_shape=[jax.ShapeDtypeStruct((S, BW), F32), jax.ShapeDtypeStruct((nc, N_PAIR, SSM_N, LANES), F32)],
        scratch_shapes=[pltpu.VMEM((N_PAIR, SSM_N, LANES), F32), pltpu.VMEM((3 * LC, LANES), F32),
                        pltpu.VMEM((LC, BW), F32), pltpu.VMEM((LANES, LC), F32)],
        compiler_params=_cp(("arbitrary",)),
    )(xbc, sp, alog, dskip_l)


def ssd_bwd(dy, xbc, sp, alog, dskip_l, hs, *, name):
    S = xbc.shape[0]
    LC = _tile(S, TILES["ssd"])
    nc = S // LC
    GN = SSM_G * SSM_N

    def body(dy_ref, x_ref, sp_ref, al_ref, dk_ref, hs_ref, dx_ref, ddt_ref, da_ref, dd_ref,
             dh_ref, buf_ref, big_ref, cst_ref, gcs_ref, dxd_ref):
        @pl.when(pl.program_id(0) == 0)
        def _():
            dh_ref[...] = jnp.zeros((N_PAIR, SSM_N, LANES), F32)
            da_ref[...] = jnp.zeros((1, LANES), F32)
            dd_ref[...] = jnp.zeros((1, BW), F32)

        pre, sg, act, dt, a, cs, dtl, csl, csl_last = _ssd_common(x_ref, sp_ref, al_ref, buf_ref, big_ref, cst_ref, LC)
        dact = sg * (1.0 + pre * (1.0 - sg))
        xs, bm, cm = act[:, :BW], act[:, BW:BW + GN], act[:, BW + GN:]
        e_all = jnp.exp(csl)
        dec = jnp.exp(csl_last - csl)
        ad = jnp.exp(csl_last)
        xd = xs * dtl
        d_y = dy_ref[...]
        ri = lax.broadcasted_iota(jnp.int32, (LC, LC), 0)
        ci = lax.broadcasted_iota(jnp.int32, (LC, LC), 1)
        tril, triu = ri >= ci, ci >= ri
        lane = lax.broadcasted_iota(jnp.int32, (LC, LANES), 1)
        rowi = lax.broadcasted_iota(jnp.int32, (LC, LANES), 0)
        dot = functools.partial(jnp.dot, preferred_element_type=F32)
        dot_nt = functools.partial(lax.dot_general, dimension_numbers=_NT, preferred_element_type=F32)
        for g in range(SSM_G):
            gs = slice(g * SSM_N, (g + 1) * SSM_N)
            bg, cg = bm[:, gs], cm[:, gs]
            bgb, cgb = bg.astype(BF16), cg.astype(BF16)
            bgt, cgt = bg.T.astype(BF16), cg.T.astype(BF16)
            cb, cbt = dot(cgb, bgt), dot(bgb, cgt)
            dcb = jnp.zeros((LC, LC), F32)
            dcbt = jnp.zeros((LC, LC), F32)
            dcg = jnp.zeros((LC, SSM_N), F32)
            dbg = jnp.zeros((LC, SSM_N), F32)
            for q in range(PAIRS_PER_GROUP):
                pp = g * PAIRS_PER_GROUP + q
                ln = slice(pp * LANES, (pp + 1) * LANES)
                xp, dyp, ep, decp, adp = xd[:, ln], d_y[:, ln], e_all[:, ln], dec[:, ln], ad[:, ln]
                xpb, dypb = xp.astype(BF16), dyp.astype(BF16)
                hin, dho = hs_ref[pp], dh_ref[pp]
                hb, dhob = hin.astype(BF16), dho.astype(BF16)
                yoff = dot(cgb, hb) * ep
                dgb = (dyp * ep).astype(BF16)
                dh_ref[pp] = dho * adp + dot(cgt, dgb)
                dcg = dcg + dot_nt(dgb, hb)
                zf = xp * decp
                d_z = dot(bgb, dhob)
                dbg = dbg + dot_nt(zf.astype(BF16), dhob)
                dzz = d_z * zf
                last = jnp.sum(dzz, axis=0, keepdims=True) + jnp.sum(dho * hin, axis=0, keepdims=True) * adp
                gcs = dyp * yoff - dzz + jnp.where(rowi == LC - 1, last, 0.0)
                dxd = d_z * decp
                for hh in range(2):
                    row_b = jnp.broadcast_to(cst_ref[pl.ds(2 * pp + hh, 1), :], (LC, LC))
                    col_b = row_b.T
                    lmat = jnp.exp(jnp.where(tril, col_b - row_b, NEG))
                    lmat_t = jnp.exp(jnp.where(triu, row_b - col_b, NEG))
                    hm = (lane < SSM_P) if hh == 0 else (lane >= SSM_P)
                    dml = dot_nt(jnp.where(hm, dyp, 0.0).astype(BF16), xpb) * lmat
                    dmtl = dot_nt(jnp.where(hm, xp, 0.0).astype(BF16), dypb) * lmat_t
                    dcb = dcb + dml
                    dcbt = dcbt + dmtl
                    contrib = jnp.sum(dml * cb, axis=1, keepdims=True) - jnp.sum(dmtl * cbt, axis=1, keepdims=True)
                    gcs = gcs + jnp.where(lane == hh * SSM_P, contrib, 0.0)
                    dxd = dxd + jnp.where(hm, dot((cbt * lmat_t).astype(BF16), dypb), 0.0)
                gcs_ref[:, ln] = gcs
                dxd_ref[:, ln] = dxd
            dcg = dcg + dot(dcb.astype(BF16), bgb)
            dbg = dbg + dot(dcbt.astype(BF16), cgb)
            dx_ref[:, BW + g * SSM_N:BW + (g + 1) * SSM_N] = dbg * dact[:, BW + g * SSM_N:BW + (g + 1) * SSM_N]
            dx_ref[:, BW + GN + g * SSM_N:BW + GN + (g + 1) * SSM_N] = dcg * dact[:, BW + GN + g * SSM_N:BW + GN + (g + 1) * SSM_N]
        d_xd = dxd_ref[...]
        dx_ref[:, :BW] = (d_y * dk_ref[...] + d_xd * dtl) * dact[:, :BW]
        sel = _sel()
        dda = _shift_scan(_dot3(gcs_ref[...], sel), buf_ref, LC, reverse=True)
        ddt_ref[...] = _dot3(d_xd * xs, sel) + dda * a
        da_ref[...] += jnp.sum(dda * dt, axis=0, keepdims=True)
        dd_ref[...] += jnp.sum(d_y * xs, axis=0, keepdims=True)

    rev = lambda c: (nc - 1 - c, 0)
    return pl.pallas_call(
        body, name=name, grid=(nc,),
        in_specs=[pl.BlockSpec((LC, BW), rev), pl.BlockSpec((LC, SSM_CONV_DIM), rev), pl.BlockSpec((LC, LANES), rev),
                  pl.BlockSpec((1, LANES), lambda c: (0, 0)), pl.BlockSpec((1, BW), lambda c: (0, 0)),
                  pl.BlockSpec((None, N_PAIR, SSM_N, LANES), lambda c: (nc - 1 - c, 0, 0, 0))],
        out_specs=[pl.BlockSpec((LC, SSM_CONV_DIM), rev), pl.BlockSpec((LC, LANES), rev),
                   pl.BlockSpec((1, LANES), lambda c: (0, 0)), pl.BlockSpec((1, BW), lambda c: (0, 0))],
        out_shape=[jax.ShapeDtypeStruct((S, SSM_CONV_DIM), F32), jax.ShapeDtypeStruct((S, LANES), F32),
                   jax.ShapeDtypeStruct((1, LANES), F32), jax.ShapeDtypeStruct((1, BW), F32)],
        scratch_shapes=[pltpu.VMEM((N_PAIR, SSM_N, LANES), F32), pltpu.VMEM((3 * LC, LANES), F32),
                        pltpu.VMEM((LC, BW), F32), pltpu.VMEM((LANES, LC), F32),
                        pltpu.VMEM((LC, BW), F32), pltpu.VMEM((LC, BW), F32)],
        compiler_params=_cp(("arbitrary",)),
    )(dy, xbc, sp, alog, dskip_l, hs)


def addn(arrs, *, out_dtype, name):
    R, C = arrs[0].shape
    tr = _tile(R, max(8, (1 << 20) // C))

    def body(*refs):
        acc = refs[0][...].astype(F32)
        for r in refs[1:-1]:
            acc = acc + r[...].astype(F32)
        refs[-1][...] = acc.astype(out_dtype)

    blk = pl.BlockSpec((tr, C), lambda i: (i, 0))
    return pl.pallas_call(
        body, name=name, grid=(R // tr,), in_specs=[blk] * len(arrs), out_specs=blk,
        out_shape=jax.ShapeDtypeStruct((R, C), out_dtype), compiler_params=_cp(("parallel",)),
    )(*arrs)


def adamw(w, g, m, v, *, name):
    R, C = w.shape
    tr = _tile(R, max(8, (1 << 18) // C))
    c1, c2 = 1.0 / (1.0 - ADAM_B1 ** ADAM_STEP), 1.0 / (1.0 - ADAM_B2 ** ADAM_STEP)

    def body(w_ref, g_ref, m_ref, v_ref, d_ref, nm_ref, nv_ref):
        gv = g_ref[...]
        nm = ADAM_B1 * m_ref[...] + (1.0 - ADAM_B1) * gv
        nv = ADAM_B2 * v_ref[...] + (1.0 - ADAM_B2) * (gv * gv)
        nm_ref[...] = nm
        nv_ref[...] = nv
        d_ref[...] = -ADAM_LR * ((nm * c1) / (jnp.sqrt(nv * c2) + ADAM_EPS) + ADAM_WD * w_ref[...])

    blk = pl.BlockSpec((tr, C), lambda i: (i, 0))
    return pl.pallas_call(
        body, name=name, grid=(R // tr,), in_specs=[blk] * 4, out_specs=[blk] * 3,
        out_shape=[jax.ShapeDtypeStruct((R, C), F32)] * 3, compiler_params=_cp(("parallel",)),
    )(w, g, m, v)


_ANY = pl.BlockSpec(memory_space=pl.ANY)


def _place():
    return lax.axis_index("x"), lax.axis_index("y"), lax.axis_index("c")


def _rcopy(src, dst, sems_s, sems_r, k, to):
    return pltpu.make_async_remote_copy(src_ref=src, dst_ref=dst, send_sem=sems_s.at[k], recv_sem=sems_r.at[k],
                                        device_id=to, device_id_type=MESH)


def gather_chips(buf, *, name):
    _, _, R, C = buf.shape

    def body(in_ref, out_ref, ss, rs):
        del in_ref
        x, y, c = _place()
        chips = [(1 - x, y), (x, 1 - y), (1 - x, 1 - y)]
        me = 2 * x + y
        first = [_rcopy(out_ref.at[me, c], out_ref.at[me, c], ss, rs, j, (cx, cy, c)) for j, (cx, cy) in enumerate(chips)]
        for cp in first:
            cp.start()
        passed = []
        for j, (cx, cy) in enumerate(chips):
            blk = out_ref.at[2 * cx + cy, c]
            _rcopy(blk, blk, ss, rs, j, (x, y, c)).wait_recv()
            cp = _rcopy(blk, blk, ss, rs, 3 + j, (x, y, 1 - c))
            cp.start()
            passed.append(cp)
        for j, (cx, cy) in enumerate(chips):
            blk = out_ref.at[2 * cx + cy, 1 - c]
            _rcopy(blk, blk, ss, rs, 3 + j, (x, y, c)).wait_recv()
        for cp in first + passed:
            cp.wait_send()

    return pl.pallas_call(
        body, name=name, in_specs=[_ANY], out_specs=_ANY, out_shape=jax.ShapeDtypeStruct(buf.shape, buf.dtype),
        input_output_aliases={0: 0},
        scratch_shapes=[pltpu.SemaphoreType.DMA((6,)), pltpu.SemaphoreType.DMA((6,))],
    )(buf)


def gather_all(block, *, name):
    R, C = block.shape
    flips = [(fx, fy, fc) for fx in (0, 1) for fy in (0, 1) for fc in (0, 1)][1:]

    def body(in_ref, out_ref, ss, rs, ls):
        x, y, c = _place()
        me = 4 * x + 2 * y + c
        mine = pltpu.make_async_copy(in_ref, out_ref.at[me], ls)
        mine.start()
        sends = []
        for k, (fx, fy, fc) in enumerate(flips):
            px, py, pc = x ^ fx, y ^ fy, c ^ fc
            cp = _rcopy(in_ref, out_ref.at[me], ss, rs, k, (px, py, pc))
            cp.start()
            sends.append(cp)
        for k, (fx, fy, fc) in enumerate(flips):
            blk = out_ref.at[4 * (x ^ fx) + 2 * (y ^ fy) + (c ^ fc)]
            _rcopy(blk, blk, ss, rs, k, (x, y, c)).wait_recv()
        for cp in sends:
            cp.wait_send()
        mine.wait()

    return pl.pallas_call(
        body, name=name, in_specs=[_ANY], out_specs=_ANY, out_shape=jax.ShapeDtypeStruct((8, R, C), block.dtype),
        scratch_shapes=[pltpu.SemaphoreType.DMA((7,)), pltpu.SemaphoreType.DMA((7,)), pltpu.SemaphoreType.DMA],
    )(block)


def swap_partials(g_all, *, name):
    N, _, R, C = g_all.shape

    def body(in_ref, out_ref, ss, rs):
        x, y, c = _place()
        cps = [_rcopy(in_ref.at[k, 1 - c], out_ref.at[k], ss, rs, k, (x, y, 1 - c)) for k in range(N)]
        for cp in cps:
            cp.start()
        for cp in cps:
            cp.wait_recv()
        for cp in cps:
            cp.wait_send()

    return pl.pallas_call(
        body, name=name, in_specs=[_ANY], out_specs=_ANY, out_shape=jax.ShapeDtypeStruct((N, R, C), g_all.dtype),
        scratch_shapes=[pltpu.SemaphoreType.DMA((N,)), pltpu.SemaphoreType.DMA((N,))],
    )(g_all)


def share_halves(buf, *, name):
    def body(in_ref, out_ref, ss, rs):
        del in_ref
        x, y, c = _place()
        cp = _rcopy(out_ref.at[c], out_ref.at[c], ss, rs, 0, (x, y, 1 - c))
        cp.start()
        _rcopy(out_ref.at[1 - c], out_ref.at[1 - c], ss, rs, 0, (x, y, c)).wait_recv()
        cp.wait_send()

    return pl.pallas_call(
        body, name=name, in_specs=[_ANY], out_specs=_ANY, out_shape=jax.ShapeDtypeStruct(buf.shape, buf.dtype),
        input_output_aliases={0: 0},
        scratch_shapes=[pltpu.SemaphoreType.DMA((1,)), pltpu.SemaphoreType.DMA((1,))],
    )(buf)


def add_sibling(g_all, sib, c_arr, *, name):
    N, _, R, C = g_all.shape
    tr = _tile(R, max(8, (1 << 20) // C))

    def body(c_ref, a_ref, b_ref, o_ref):
        o_ref[...] = (a_ref[...].astype(F32) + b_ref[...].astype(F32)).astype(BF16)

    blk = pl.BlockSpec((None, tr, C), lambda k, r, c_ref: (k, r, 0))
    return pl.pallas_call(
        body, name=name,
        grid_spec=pltpu.PrefetchScalarGridSpec(
            num_scalar_prefetch=1, grid=(N, R // tr),
            in_specs=[pl.BlockSpec((None, None, tr, C), lambda k, r, c_ref: (k, c_ref[0], r, 0)), blk], out_specs=blk),
        out_shape=jax.ShapeDtypeStruct((N, R, C), BF16), compiler_params=_cp(("parallel", "parallel")),
    )(c_arr, g_all, sib)


def add_chips(part, rcv, place_arr, *, name):
    _, R, C = part.shape
    tr = _tile(R, max(8, (1 << 20) // C))

    def body(place_ref, p_ref, r0, r1, r2, o_ref):
        acc = p_ref[...].astype(F32)
        for r in (r0, r1, r2):
            acc = acc + r[...].astype(F32)
        o_ref[...] = acc

    def slot(j):
        return pl.BlockSpec((None, tr, C), lambda r, place_ref, j=j: (j, r, 0))

    return pl.pallas_call(
        body, name=name,
        grid_spec=pltpu.PrefetchScalarGridSpec(
            num_scalar_prefetch=1, grid=(R // tr,),
            in_specs=[pl.BlockSpec((None, tr, C), lambda r, place_ref: (place_ref[0], r, 0)), slot(0), slot(1), slot(2)],
            out_specs=pl.BlockSpec((None, tr, C), lambda r, place_ref: (place_ref[1], r, 0))),
        out_shape=jax.ShapeDtypeStruct((2, R, C), F32), compiler_params=_cp(("parallel",)),
    )(place_arr, part, rcv, rcv, rcv)


WEIGHTS = ("norm_w", "w_in", "fg_bias", "ssm_conv_w", "ssm_conv_b", "dt_bias", "a_log", "d_skip", "ssm_norm_w", "sc_conv_w",
           "sc_conv_b", "cf_conv_w", "cf_conv_b", "cf_ln_w", "cf_ln_b", "w_gate", "b_gate", "w_branch", "w_out", "final_norm_w")
BIG = ("w_in", "w_gate", "w_branch", "w_out")
SMALL = tuple(n for n in WEIGHTS if n not in BIG)
SMALL_SHARDED = ("ssm_conv_w", "sc_conv_w", "cf_conv_w", "b_gate")
N_CHIP = 4
PACK_C = D_MODEL
PACK_ROWS = 1024


def _pack(arrs, cols, row_mult):
    return _pack_groups([arrs], cols, row_mult)[0]


def _pack_groups(groups, cols, row_mult):
    n = sum(math.prod(a.shape) for a in groups[0])
    rows = -(-n // cols)
    rows = -(-rows // row_mult) * row_mult
    parts = []
    for arrs in groups:
        parts += [a.reshape(-1) for a in arrs]
        if rows * cols > n:
            parts.append(jnp.zeros((rows * cols - n,), arrs[0].dtype))
    return jnp.concatenate(parts).reshape(len(groups), rows, cols)


PIECE_ROWS = 16


def _pack_rows(groups, cols, row_mult):
    as_list = lambda a: list(a) if isinstance(a, (list, tuple)) else [a]
    n_rows = lambda a: sum(math.prod(b.shape) // cols for b in as_list(a))
    rows = [sum(-(-n_rows(a) // PIECE_ROWS) * PIECE_ROWS for a in arrs) for arrs in groups]
    total = -(-max(rows) // row_mult) * row_mult
    dtype = as_list(groups[0][0])[0].dtype
    parts = []
    for arrs, r in zip(groups, rows):
        for a in arrs:
            parts += [b.reshape(-1, cols) for b in as_list(a)]
            pad = -n_rows(a) % PIECE_ROWS
            if pad:
                parts.append(jnp.zeros((pad, cols), dtype))
        if total > r:
            parts.append(jnp.zeros((total - r, cols), dtype))
    return jnp.concatenate(parts, axis=0).reshape(len(groups), total, cols)


def _unpack_rows(packed, shapes):
    cols = packed.shape[-1]
    out, o = [], 0
    for s in shapes:
        r = math.prod(s) // cols
        out.append(packed[o:o + r].reshape(s))
        o += -(-r // PIECE_ROWS) * PIECE_ROWS
    return out


def _unpack(packed, shapes):
    flat = packed.reshape(-1)
    out, o = [], 0
    for s in shapes:
        n = math.prod(s)
        out.append(flat[o:o + n].reshape(s))
        o += n
    return out


def _orig_cols():
    cols = []
    for n, s in zip(ORIG_NAMES, ORIG_SIZES):
        if n == "dt":
            cols.append((OFF["small"] + DT_LANE, s))
        elif n == "f":
            cols.append((OFF["small"] + F_LANE, s))
        else:
            cols.append((OFF[n], s))
    return cols


def _rows_to_padded(per_chip):
    q_in = N_IN // N_CHIP
    orig_off = dict(zip(ORIG_NAMES, [sum(ORIG_SIZES[:i]) for i in range(len(ORIG_SIZES))]))
    size = dict(zip(ORIG_NAMES, ORIG_SIZES))
    parts = []
    for n in PAD_ORDER + ("dt", "f"):
        lo, hi = orig_off[n], orig_off[n] + size[n]
        for k in range(N_CHIP):
            a, b = max(lo, k * q_in), min(hi, (k + 1) * q_in)
            if a < b:
                parts.append(per_chip[k][a - k * q_in:b - k * q_in])
    parts.append(jnp.zeros((SMALL_W - size["dt"] - size["f"], per_chip[0].shape[1]), per_chip[0].dtype))
    return jnp.concatenate(parts, axis=0)


def _owner_windows(k):
    q_in = N_IN // N_CHIP
    lo, hi = k * q_in, (k + 1) * q_in
    out, o = [], 0
    for start, s in _orig_cols():
        a, b = max(lo, o), min(hi, o + s)
        if a < b:
            p0, p1 = start + a - o, start + b - o
            w0, w1 = p0 // PIECE_ROWS * PIECE_ROWS, -(-p1 // PIECE_ROWS) * PIECE_ROWS
            out.append((w0, w1, p0 - w0, p1 - p0))
        o += s
    return out


def _owner_rows(windows_packed, k):
    parts, o = [], 0
    for w0, w1, lead, rows in _owner_windows(k):
        parts.append(windows_packed[o + lead:o + lead + rows])
        o += w1 - w0
    return jnp.concatenate(parts, axis=0)


IN_WINDOW_ROWS = max(sum(w1 - w0 for w0, w1, _, _ in _owner_windows(k)) for k in range(4))


def _lanes_row(parts, width=LANES):
    v = jnp.concatenate([p.reshape(-1) for p in parts])
    return jnp.pad(v, (0, width - v.shape[0])).reshape(1, width)


def kernel(x, norm_w, w_in, fg_bias, ssm_conv_w, ssm_conv_b, dt_bias, a_log, d_skip, ssm_norm_w, sc_conv_w, sc_conv_b, cf_conv_w, cf_conv_b, cf_ln_w, cf_ln_b, w_gate, b_gate, w_branch, w_out, final_norm_w, loss_target, m_norm_w, m_w_in, m_fg_bias, m_ssm_conv_w, m_ssm_conv_b, m_dt_bias, m_a_log, m_d_skip, m_ssm_norm_w, m_sc_conv_w, m_sc_conv_b, m_cf_conv_w, m_cf_conv_b, m_cf_ln_w, m_cf_ln_b, m_w_gate, m_b_gate, m_w_branch, m_w_out, m_final_norm_w, v_norm_w, v_w_in, v_fg_bias, v_ssm_conv_w, v_ssm_conv_b, v_dt_bias, v_a_log, v_d_skip, v_ssm_norm_w, v_sc_conv_w, v_sc_conv_b, v_cf_conv_w, v_cf_conv_b, v_cf_ln_w, v_cf_ln_b, v_w_gate, v_b_gate, v_w_branch, v_w_out, v_final_norm_w):
    wts = dict(norm_w=norm_w, w_in=w_in, fg_bias=fg_bias, ssm_conv_w=ssm_conv_w, ssm_conv_b=ssm_conv_b, dt_bias=dt_bias,
               a_log=a_log, d_skip=d_skip, ssm_norm_w=ssm_norm_w, sc_conv_w=sc_conv_w, sc_conv_b=sc_conv_b,
               cf_conv_w=cf_conv_w, cf_conv_b=cf_conv_b, cf_ln_w=cf_ln_w, cf_ln_b=cf_ln_b, w_gate=w_gate, b_gate=b_gate,
               w_branch=w_branch, w_out=w_out, final_norm_w=final_norm_w)
    mom = dict(norm_w=m_norm_w, w_in=m_w_in, fg_bias=m_fg_bias, ssm_conv_w=m_ssm_conv_w, ssm_conv_b=m_ssm_conv_b,
               dt_bias=m_dt_bias, a_log=m_a_log, d_skip=m_d_skip, ssm_norm_w=m_ssm_norm_w, sc_conv_w=m_sc_conv_w,
               sc_conv_b=m_sc_conv_b, cf_conv_w=m_cf_conv_w, cf_conv_b=m_cf_conv_b, cf_ln_w=m_cf_ln_w, cf_ln_b=m_cf_ln_b,
               w_gate=m_w_gate, b_gate=m_b_gate, w_branch=m_w_branch, w_out=m_w_out, final_norm_w=m_final_norm_w)
    vel = dict(norm_w=v_norm_w, w_in=v_w_in, fg_bias=v_fg_bias, ssm_conv_w=v_ssm_conv_w, ssm_conv_b=v_ssm_conv_b,
               dt_bias=v_dt_bias, a_log=v_a_log, d_skip=v_d_skip, ssm_norm_w=v_ssm_norm_w, sc_conv_w=v_sc_conv_w,
               sc_conv_b=v_sc_conv_b, cf_conv_w=v_cf_conv_w, cf_conv_b=v_cf_conv_b, cf_ln_w=v_cf_ln_w, cf_ln_b=v_cf_ln_b,
               w_gate=v_w_gate, b_gate=v_b_gate, w_branch=v_w_branch, w_out=v_w_out, final_norm_w=v_final_norm_w)
    L = norm_w.shape[0]
    S, D = x.shape[1], x.shape[2]
    assert D == D_MODEL and x.shape[0] == 1
    xi, yi, ci = _place()
    chip = 2 * xi + yi

    sh_shapes = [wts[n].shape for n in SMALL_SHARDED]
    got = gather_all(_pack([wts[n] for n in SMALL_SHARDED], LANES, 8), name="gather_small_w")
    per_chip = [_unpack(got[2 * k], sh_shapes) for k in range(N_CHIP)]
    full_small = {n: jnp.concatenate([per_chip[k][i] for k in range(N_CHIP)], axis=-1) for i, n in enumerate(SMALL_SHARDED)}

    q_in, q_d = N_IN // N_CHIP, D // N_CHIP
    sent_shapes = [(q_in, D), (N_BRANCH, q_d, D), (N_BRANCH, BW, q_d), (q_d, D)]
    def own_shard(l):
        packed = _pack_rows([[w_in[l].T.astype(BF16), w_gate[l].astype(BF16), w_branch[l].astype(BF16),
                              w_out[l].astype(BF16)]], PACK_C, PACK_ROWS)[0]
        half_rows = packed.shape[0] // 2
        return lax.dynamic_update_slice(lax.empty((N_CHIP, 2, half_rows, PACK_C), BF16),
                                        packed.reshape(1, 2, half_rows, PACK_C), (chip, 0, 0, 0))

    def whole_weights(got):
        pc = [_unpack_rows(got[k].reshape(-1, PACK_C), sent_shapes) for k in range(N_CHIP)]
        return dict(wpt=_rows_to_padded([pc[k][0] for k in range(N_CHIP)]),
                    wg=jnp.concatenate([pc[k][1] for k in range(N_CHIP)], axis=1),
                    wb=jnp.concatenate([pc[k][2] for k in range(N_CHIP)], axis=2),
                    wo=jnp.concatenate([pc[k][3] for k in range(N_CHIP)], axis=0))

    lw = [whole_weights(gather_chips(own_shard(0), name="gather_w"))]
    saved = []
    xl = x[0]
    for l in range(L):
        w = lw[l]
        h = rms_fwd(xl, norm_w[l][None], name="rms_fwd")
        if l + 1 < L:
            u, nxt = mm(h, w["wpt"], tb=True, comm=("gather_ici", own_shard(l + 1)), name="mm_in_gather")
        else:
            u = mm(h, w["wpt"], tb=True, name="mm_in")
        bias_small = _lanes_row([dt_bias[l], fg_bias[l]])
        sp, csum = small_fwd(u, bias_small, name="small_fwd")
        c8 = csum[:, F_LANE:F_LANE + FOX_HEADS].T * LOG2E
        c_col, c_row = c8[:, :, None], c8[:, None, :]
        o, lse, y_a = attn_fwd(u, c_row, name="attn_fwd")
        xbc = conv_fwd(u, full_small["ssm_conv_w"][l], ssm_conv_b[l][None], mode="plain", a_off=OFF["xbc"], b_off=0,
                       C=SSM_CONV_DIM, name="conv_ssm_fwd")
        alog_row = _lanes_row([a_log[l]])
        dskip_l = jnp.repeat(d_skip[l], SSM_P)[None]
        y_ssd, hs = ssd_fwd(xbc, sp, alog_row, dskip_l, name="ssd_fwd")
        cv_c = conv_fwd(u, full_small["sc_conv_w"][l], sc_conv_b[l][None], mode="mul", a_off=OFF["scc"], b_off=OFF["scx"],
                        C=BW, name="conv_sc_fwd")
        cv_d = conv_fwd(u, full_small["cf_conv_w"][l], cf_conv_b[l][None], mode="glu", a_off=OFF["glu"], b_off=OFF["glu"] + BW,
                        C=BW, name="conv_cf_fwd")
        y_b, y_c, y_d = post_fwd(u, y_ssd, cv_c, cv_d, ssm_norm_w[l][None], cf_ln_w[l][None], cf_ln_b[l][None], name="post_fwd")
        merged, gates, proj = merge_fwd(h, (y_a, y_b, y_c, y_d), w["wg"], full_small["b_gate"][l][:, None, :], w["wb"],
                                        name="merge_fwd")
        if l + 1 < L:
            x_next, nxt = mm(merged, w["wo"], add=xl, comm=("gather_d2d", nxt), name="mm_out_gather")
            lw.append(whole_weights(nxt))
        else:
            x_next = mm(merged, w["wo"], add=xl, name="mm_out")
        saved.append(dict(x=xl, h=h, u=u, bias_small=bias_small, sp=sp, c_col=c_col, c_row=c_row, o=o, lse=lse, xbc=xbc,
                          alog_row=alog_row, dskip_l=dskip_l, hs=hs, y_ssd=y_ssd, cv_c=cv_c, cv_d=cv_d,
                          ys=(y_a, y_b, y_c, y_d), merged=merged, gates=gates, proj=proj))
        xl = x_next

    sq, dx, d_final = loss_head(xl, final_norm_w[None], loss_target[0], name="loss_head")
    loss = lax.psum(sq[0, 0] * (0.5 / D), ("x", "y", "c"))

    small_g = {n: [None] * L for n in SMALL if n != "final_norm_w"}
    big_g = {n: [None] * L for n in BIG}

    def finish_exchange(l, part, rcv):
        full = share_halves(add_chips(part, rcv, jnp.stack([chip, ci]).astype(jnp.int32), name="add_chips"), name="share_halves")
        g_wg, g_wb, g_wo, g_win = _unpack_rows(full.reshape(-1, PACK_C),
                                               [(N_BRANCH, q_d, D), (N_BRANCH, q_d, BW), (q_d, D), (IN_WINDOW_ROWS, D)])
        g_in = lax.switch(chip, [functools.partial(_owner_rows, k=k) for k in range(N_CHIP)], g_win)
        big_g["w_in"][l] = g_in.T
        big_g["w_gate"][l] = g_wg
        big_g["w_branch"][l] = jnp.transpose(g_wb, (0, 2, 1))
        big_g["w_out"][l] = g_wo

    for l in reversed(range(L)):
        w, sv = lw[l], saved[l]
        u, h = sv["u"], sv["h"]
        dm = mm(dx, w["wo"], tb=True, name="mm_dmerged")
        d_wo = mm(sv["merged"], dx, ta=True, out_dtype=BF16, name="mm_dwo")
        dp, dg, dbg = merge_bwd(dm, sv["gates"], sv["proj"], name="merge_bwd")
        dys = [mm(dp, w["wb"][i], tb=True, K=D, a_koff=i * D, name="mm_dy") for i in range(N_BRANCH)]
        d_wbt = [mm(dp, sv["ys"][i], ta=True, M=D, a_moff=i * D, out_dtype=BF16, name="mm_dwb")
                 for i in range(N_BRANCH)]
        d_wg = mm(h, dg, ta=True, out_dtype=BF16, n_groups=N_BRANCH, name="mm_dwg")
        (do, delta, dga, dz, dscb, dgc, dgd, dy_ssd, dcv_c, dcv_d, dnw, dlnw, dlnb) = post_bwd(
            dys[0], dys[1], dys[2], dys[3], u, sv["o"], sv["y_ssd"], sv["cv_c"], sv["cv_d"],
            ssm_norm_w[l][None], cf_ln_w[l][None], cf_ln_b[l][None], name="post_bwd")
        delta_row = delta[:, :FOX_HEADS].T[:, None, :]
        lse_row = jnp.transpose(sv["lse"], (0, 2, 1))
        dq, dk, dv, dc_col = attn_bwd(u, do, sv["c_col"], lse_row, delta_row, name="attn_bwd")
        dscc, dscx, d_scw, d_scb = conv_bwd(dcv_c, u, full_small["sc_conv_w"][l], mode="mul", a_off=OFF["scc"],
                                            b_off=OFF["scx"], C=BW, name="conv_sc_bwd")
        dglua, dglug, d_cfw, d_cfb = conv_bwd(dcv_d, u, full_small["cf_conv_w"][l], mode="glu", a_off=OFF["glu"],
                                              b_off=OFF["glu"] + BW, C=BW, name="conv_cf_bwd")
        dxbc_pre, ddt, d_a, d_dl = ssd_bwd(dy_ssd, sv["xbc"], sv["sp"], sv["alog_row"], sv["dskip_l"], sv["hs"], name="ssd_bwd")
        dxbc, d_ssmw, d_ssmb = conv_bwd(dxbc_pre, u, full_small["ssm_conv_w"][l], mode="plain", a_off=OFF["xbc"], b_off=0,
                                        C=SSM_CONV_DIM, name="conv_ssm_bwd")
        dc_full = jnp.pad(dc_col[:, :, 0].T, ((0, 0), (F_LANE, LANES - F_LANE - FOX_HEADS)))
        du_small, dbias_small = small_bwd(dc_full, ddt, u, sv["bias_small"], name="small_bwd")
        by_name = dict(q=dq.astype(BF16), k=dk, v=dv, ga=dga, z=dz, scb=dscb, scc=dscc, scx=dscx, gc=dgc, gd=dgd, xbc=dxbc)
        du = jnp.concatenate([jnp.concatenate([dglua, dglug], axis=1) if n == "glu" else by_name[n] for n in PAD_ORDER]
                             + [du_small], axis=1)
        d_wpt = mm(du, h, ta=True, out_dtype=BF16, name="mm_dwp")

        dest = [[d_wg[:, k * q_d:(k + 1) * q_d]] + [t[k * q_d:(k + 1) * q_d] for t in d_wbt] + [d_wo[k * q_d:(k + 1) * q_d]]
                + [[d_wpt[w0:w1] for w0, w1, _, _ in _owner_windows(k)]] for k in range(N_CHIP)]
        g_all = _pack_rows(dest, PACK_C, PACK_ROWS)
        g_all = g_all.reshape(N_CHIP, 2, g_all.shape[1] // 2, PACK_C)
        sib = swap_partials(g_all, name="swap_partials")
        part = add_sibling(g_all, sib, ci.reshape(1).astype(jnp.int32), name="add_sibling")
        dh_gate = mm(dg, w["wg"], tb=True, name="mm_dh_gate")
        dh, rcv = mm(du, w["wpt"], add=dh_gate, comm=("scatter", part), name="mm_dh_in_scatter")
        finish_exchange(l, part, rcv)
        dx, d_nw = rms_bwd(dh, sv["x"], norm_w[l][None], dx, name="rms_bwd")

        a_neg = -jnp.exp(a_log[l])
        sg = dict(norm_w=d_nw[0], fg_bias=dbias_small[0, F_LANE:F_LANE + FOX_HEADS], ssm_conv_w=d_ssmw, ssm_conv_b=d_ssmb[0],
                  dt_bias=dbias_small[0, DT_LANE:DT_LANE + SSM_HEADS], a_log=d_a[0, :SSM_HEADS] * a_neg,
                  d_skip=d_dl.reshape(SSM_HEADS, SSM_P).sum(-1), ssm_norm_w=dnw[0], sc_conv_w=d_scw, sc_conv_b=d_scb[0],
                  cf_conv_w=d_cfw, cf_conv_b=d_cfb[0], cf_ln_w=dlnw[0], cf_ln_b=dlnb[0], b_gate=dbg.reshape(N_BRANCH, D))
        for n in sg:
            small_g[n][l] = sg[n]


    names = [n for n in SMALL if n != "final_norm_w"]
    stacked = [jnp.stack(small_g[n]) for n in names] + [d_final[0]]
    shapes = [a.shape for a in stacked]
    got = gather_all(_pack(stacked, LANES, 8), name="gather_small_g")
    tot = addn([got[d] for d in range(8)], out_dtype=F32, name="add_small_g")
    grads = dict(zip(names + ["final_norm_w"], _unpack(tot, shapes)))
    for n in SMALL_SHARDED:
        sz = wts[n].shape[-1]
        grads[n] = lax.dynamic_slice_in_dim(grads[n], chip * sz, sz, axis=grads[n].ndim - 1)
    for n in BIG:
        grads[n] = jnp.stack(big_g[n])

    delta, new_m, new_v = {}, {}, {}
    for n in BIG:
        two_d = lambda a: a.reshape(-1, a.shape[-1])
        d, nm, nv = adamw(two_d(wts[n]), two_d(grads[n]), two_d(mom[n]), two_d(vel[n]), name="adamw_" + n)
        delta[n], new_m[n], new_v[n] = (t.reshape(wts[n].shape) for t in (d, nm, nv))
    small_shapes = [wts[n].shape for n in SMALL]
    pk = lambda src: _pack([src[n] for n in SMALL], LANES, 8)
    d, nm, nv = adamw(pk(wts), pk(grads), pk(mom), pk(vel), name="adamw_small")
    for tgt, src in ((delta, d), (new_m, nm), (new_v, nv)):
        for n, a in zip(SMALL, _unpack(src, small_shapes)):
            tgt[n] = a

    return (loss, dx[None], *[grads[n] for n in WEIGHTS], *[delta[n] for n in WEIGHTS],
            *[new_m[n] for n in WEIGHTS], *[new_v[n] for n in WEIGHTS])
```

```python
import functools
import math

import jax
import jax.numpy as jnp
from jax import lax
from jax.experimental import pallas as pl
from jax.experimental.pallas import tpu as pltpu

F32, BF16 = jnp.float32, jnp.bfloat16
MESH = pl.DeviceIdType.MESH

D_MODEL = 2048
BW = D_MODEL // 2
FOX_HEADS, FOX_HD = 8, 128
SSM_HEADS, SSM_P, SSM_N, SSM_G = 16, 64, 128, 2
SSM_CONV_DIM = BW + 2 * SSM_G * SSM_N
N_BRANCH = 4
EPS = 1e-6
NEG = -1e30
LOG2E = 1.4426950408889634
ORIG_SIZES = (BW, BW, BW, FOX_HEADS, BW, BW, SSM_CONV_DIM, SSM_HEADS, BW, BW, BW, BW, 2 * BW, BW)
ORIG_NAMES = ("q", "k", "v", "f", "ga", "z", "xbc", "dt", "scb", "scc", "scx", "gc", "glu", "gd")
N_IN = sum(ORIG_SIZES)
PAD_ORDER = ("q", "k", "v", "ga", "z", "scb", "scc", "scx", "gc", "glu", "gd", "xbc")
SMALL_W = 512
DT_LANE, F_LANE = 0, SSM_HEADS
OFF = {}
_o = 0
for _n in PAD_ORDER:
    OFF[_n] = _o
    _o += ORIG_SIZES[ORIG_NAMES.index(_n)]
OFF["small"] = _o
NP = _o + SMALL_W
LANES = 128
CONV_HALO = 32

VMEM_LIMIT = 56 * 1024 * 1024
MM_VMEM_BUDGET = 40 * 1024 * 1024

ADAM_LR, ADAM_B1, ADAM_B2, ADAM_EPS, ADAM_WD, ADAM_STEP = 0.001, 0.9, 0.999, 1e-08, 0.01, 10

TILES = dict(row=512, post=256, att=1024, ssd=256, mm_m=1024, mm_n=1024, conv_c=256)


def _cp(sem=None):
    return pltpu.CompilerParams(dimension_semantics=sem, vmem_limit_bytes=VMEM_LIMIT)


def _tile(n, pref):
    t = 1 << (min(n, pref).bit_length() - 1)
    while n % t:
        t //= 2
    return t


def _sig(x):
    return 1.0 / (1.0 + jnp.exp(-x))


def _silu(x):
    return x * _sig(x)


def _dsilu(x):
    s = _sig(x)
    return s * (1.0 + x * (1.0 - s))


def _softplus(x):
    return jnp.maximum(x, 0.0) + jnp.log(1.0 + jnp.exp(-jnp.abs(x)))


def _comm_copies(kind, src_ref, dst_ref, ss, rs):
    x, y, c = _place()
    chips = [(1 - x, y), (x, 1 - y), (1 - x, 1 - y)]
    me = 2 * x + y
    sends, recvs = [], []
    for j, (cx, cy) in enumerate(chips):
        if kind == "gather_ici":
            mine, theirs = dst_ref.at[me, c], dst_ref.at[2 * cx + cy, c]
            sends.append(_rcopy(mine, mine, ss, rs, j, (cx, cy, c)))
            recvs.append(_rcopy(theirs, theirs, ss, rs, j, (x, y, c)))
        elif kind == "gather_d2d":
            landed, theirs = dst_ref.at[2 * cx + cy, c], dst_ref.at[2 * cx + cy, 1 - c]
            sends.append(_rcopy(landed, landed, ss, rs, j, (x, y, 1 - c)))
            recvs.append(_rcopy(theirs, theirs, ss, rs, j, (x, y, c)))
        else:
            sends.append(_rcopy(src_ref.at[2 * cx + cy], dst_ref.at[j], ss, rs, j, (cx, cy, c)))
            recvs.append(_rcopy(src_ref.at[0], dst_ref.at[j], ss, rs, j, (x, y, c)))
    return sends, recvs


def mm(a, b, *, name, out_dtype=F32, add=None, ta=False, tb=False, M=None, K=None, N=None,
       a_koff=0, a_moff=0, b_koff=0, b_noff=0, comm=None, n_groups=1):
    b3 = b.ndim == 3
    assert not b3 or (tb and b_koff == 0 and b_noff == 0)
    M = M or (a.shape[1] if ta else a.shape[0])
    K = K or (a.shape[0] if ta else a.shape[1])
    N = N or (b.shape[-2] if tb else b.shape[1])
    tm, tn = _tile(M, TILES["mm_m"]), _tile(N, TILES["mm_n"])
    sa, sb, so = a.dtype.itemsize, b.dtype.itemsize, jnp.dtype(out_dtype).itemsize

    def need(tk):
        return 2 * tm * tk * sa + 2 * tk * tn * sb + 2 * tm * tn * so + (8 * tm * tn if add is not None else 0) + 4 * tm * tn

    tk = b.shape[2] if b3 else K
    while need(tk) > MM_VMEM_BUDGET and tk % 256 == 0:
        tk //= 2
    assert K % tk == 0 and a_koff % tk == 0 and b_koff % tk == 0 and b_noff % tn == 0 and a_moff % tm == 0, (name, K, tk, tn)
    nk = K // tk
    ako, amo, bko, bno = a_koff // tk, a_moff // tm, b_koff // tk, b_noff // tn
    dims = (((0 if ta else 1,), (1 if tb else 0,)), ((), ()))

    n_in = 2 + (add is not None) + (comm is not None)
    grid = (M // tm, N // tn, nk)

    def body(*refs):
        a_ref, b_ref = refs[:2]
        add_ref = refs[2] if add is not None else None
        o_ref = refs[n_in]
        acc_ref = refs[n_in + 1 + (comm is not None)] if nk > 1 else None
        if comm is not None:
            step = (pl.program_id(0) * grid[1] + pl.program_id(1)) * grid[2] + pl.program_id(2)
            sends, recvs = _comm_copies(comm[0], refs[n_in - 1], refs[n_in + 1], refs[-2], refs[-1])

            @pl.when(step == 0)
            def _():
                for cp in sends:
                    cp.start()

        prod = lax.dot_general(a_ref[...].astype(BF16), b_ref[...].astype(BF16), dims, preferred_element_type=F32)

        def finish(acc):
            if add_ref is not None:
                acc = acc + add_ref[...]
            o_ref[...] = acc.astype(out_dtype)

        if nk == 1:
            finish(prod)
        else:
            k = pl.program_id(2)

            @pl.when(k == 0)
            def _():
                acc_ref[...] = prod

            @pl.when(k > 0)
            def _():
                acc_ref[...] += prod

            @pl.when(k == nk - 1)
            def _():
                finish(acc_ref[...])

        if comm is not None:
            @pl.when(step == grid[0] * grid[1] * grid[2] - 1)
            def _():
                for cp in recvs:
                    cp.wait_recv()
                for cp in sends:
                    cp.wait_send()

    if ta:
        a_spec = pl.BlockSpec((tk, tm), lambda i, j, k: (k + ako, i + amo))
    else:
        a_spec = pl.BlockSpec((tm, tk), lambda i, j, k: (i, k + ako))
    if b3:
        assert tk == b.shape[2], (name, tk)
        b_spec = pl.BlockSpec((None, tn, tk), lambda i, j, k: (k, j, 0))
    elif tb:
        b_spec = pl.BlockSpec((tn, tk), lambda i, j, k: (j + bno, k + bko))
    else:
        b_spec = pl.BlockSpec((tk, tn), lambda i, j, k: (k + bko, j + bno))
    in_specs = [a_spec, b_spec]
    args = [a, b]
    if add is not None:
        in_specs.append(pl.BlockSpec((tm, tn), lambda i, j, k: (i, j)))
        args.append(add)
    if n_groups == 1:
        out_spec = pl.BlockSpec((tm, tn), lambda i, j, k: (i, j))
        out_shape = jax.ShapeDtypeStruct((M, N), out_dtype)
    else:
        nb = N // n_groups // tn
        sh = nb.bit_length() - 1
        assert nb == 1 << sh, (name, nb)
        out_spec = pl.BlockSpec((None, tm, tn), lambda i, j, k: (lax.shift_right_logical(j, sh), i, lax.bitwise_and(j, nb - 1)))
        out_shape = jax.ShapeDtypeStruct((n_groups, M, N // n_groups), out_dtype)
    scratch = [pltpu.VMEM((tm, tn), F32)] if nk > 1 else []
    if comm is None:
        return pl.pallas_call(
            body, name=name, grid=grid, in_specs=in_specs, out_specs=out_spec, out_shape=out_shape,
            scratch_shapes=scratch, compiler_params=_cp(("parallel", "parallel", "arbitrary")),
        )(*args)
    kind, buf = comm
    any_spec = pl.BlockSpec(memory_space=pl.ANY)
    if kind == "scatter":
        comm_shape, alias = jax.ShapeDtypeStruct((3,) + buf.shape[1:], buf.dtype), {}
    else:
        comm_shape, alias = jax.ShapeDtypeStruct(buf.shape, buf.dtype), {n_in - 1: 1}
    return pl.pallas_call(
        body, name=name, grid=grid, in_specs=in_specs + [any_spec], out_specs=[out_spec, any_spec],
        out_shape=[out_shape, comm_shape], input_output_aliases=alias,
        scratch_shapes=scratch + [pltpu.SemaphoreType.DMA((3,)), pltpu.SemaphoreType.DMA((3,))],
        compiler_params=_cp(("arbitrary", "arbitrary", "arbitrary")),
    )(*args, buf)


def rms_fwd(x, w, *, name):
    S, D = x.shape
    ts = _tile(S, TILES["row"])

    def body(x_ref, w_ref, o_ref):
        xf = x_ref[...]
        r = lax.rsqrt(jnp.mean(xf * xf, axis=-1, keepdims=True) + EPS)
        o_ref[...] = (xf * r * w_ref[...]).astype(BF16)

    return pl.pallas_call(
        body, name=name, grid=(S // ts,),
        in_specs=[pl.BlockSpec((ts, D), lambda i: (i, 0)), pl.BlockSpec((1, D), lambda i: (0, 0))],
        out_specs=pl.BlockSpec((ts, D), lambda i: (i, 0)),
        out_shape=jax.ShapeDtypeStruct((S, D), BF16), compiler_params=_cp(("parallel",)),
    )(x, w)


def rms_bwd(dh, x, w, dres, *, name):
    S, D = x.shape
    ts = _tile(S, TILES["row"])

    def body(dh_ref, x_ref, w_ref, dres_ref, dx_ref, dw_ref):
        xf = x_ref[...]
        r = lax.rsqrt(jnp.mean(xf * xf, axis=-1, keepdims=True) + EPS)
        xh = xf * r
        g = dh_ref[...]
        dxh = g * w_ref[...]
        dx_ref[...] = dres_ref[...] + r * (dxh - xh * jnp.mean(dxh * xh, axis=-1, keepdims=True))
        part = jnp.sum(g * xh, axis=0, keepdims=True)

        @pl.when(pl.program_id(0) == 0)
        def _():
            dw_ref[...] = part

        @pl.when(pl.program_id(0) > 0)
        def _():
            dw_ref[...] += part

    row = pl.BlockSpec((ts, D), lambda i: (i, 0))
    vec = pl.BlockSpec((1, D), lambda i: (0, 0))
    return pl.pallas_call(
        body, name=name, grid=(S // ts,), in_specs=[row, row, vec, row], out_specs=[row, vec],
        out_shape=[jax.ShapeDtypeStruct((S, D), F32), jax.ShapeDtypeStruct((1, D), F32)],
        compiler_params=_cp(("arbitrary",)),
    )(dh, x, w, dres)


def loss_head(x, w, target, *, name):
    S, D = x.shape
    ts = _tile(S, TILES["row"])

    def body(x_ref, w_ref, t_ref, loss_ref, dx_ref, dw_ref):
        xf = x_ref[...]
        r = lax.rsqrt(jnp.mean(xf * xf, axis=-1, keepdims=True) + EPS)
        xh = xf * r
        err = xh * w_ref[...] - t_ref[...]
        sq = jnp.sum(jnp.sum(err * err, axis=0, keepdims=True), axis=1, keepdims=True)
        dy = err * (1.0 / D)
        dxh = dy * w_ref[...]
        dx_ref[...] = r * (dxh - xh * jnp.mean(dxh * xh, axis=-1, keepdims=True))
        part = jnp.sum(dy * xh, axis=0, keepdims=True)

        @pl.when(pl.program_id(0) == 0)
        def _():
            dw_ref[...] = part
            loss_ref[...] = jnp.broadcast_to(sq, (8, LANES))

        @pl.when(pl.program_id(0) > 0)
        def _():
            dw_ref[...] += part
            loss_ref[...] += jnp.broadcast_to(sq, (8, LANES))

    row = pl.BlockSpec((ts, D), lambda i: (i, 0))
    vec = pl.BlockSpec((1, D), lambda i: (0, 0))
    return pl.pallas_call(
        body, name=name, grid=(S // ts,), in_specs=[row, vec, row],
        out_specs=[pl.BlockSpec((8, LANES), lambda i: (0, 0)), row, vec],
        out_shape=[jax.ShapeDtypeStruct((8, LANES), F32), jax.ShapeDtypeStruct((S, D), F32),
                   jax.ShapeDtypeStruct((1, D), F32)],
        compiler_params=_cp(("arbitrary",)),
    )(x, w, target)


def _shift_scan(v, buf_ref, n, reverse=False):
    buf_ref[pl.ds(0, n), :] = jnp.zeros((n, LANES), F32)
    buf_ref[pl.ds(2 * n, n), :] = jnp.zeros((n, LANES), F32)
    s = 1
    while s < n:
        buf_ref[pl.ds(n, n), :] = v
        v = v + buf_ref[pl.ds(n + s if reverse else n - s, n), :]
        s *= 2
    return v


def small_fwd(u, bias, *, name):
    S = u.shape[0]
    ts = _tile(S, TILES["row"])
    cb = OFF["small"] // LANES

    def body(u_ref, b_ref, sp_ref, c_ref, buf_ref, carry_ref):
        x = u_ref[...] + b_ref[...]
        sp_ref[...] = _softplus(x)
        lf = jnp.minimum(x, 0.0) - jnp.log(1.0 + jnp.exp(-jnp.abs(x)))

        @pl.when(pl.program_id(0) == 0)
        def _():
            carry_ref[...] = jnp.zeros((8, LANES), F32)

        c = _shift_scan(lf, buf_ref, ts) + carry_ref[pl.ds(0, 1), :]
        c_ref[...] = c
        carry_ref[...] = jnp.broadcast_to(c_ref[pl.ds(ts - 1, 1), :], (8, LANES))

    blk = pl.BlockSpec((ts, LANES), lambda i: (i, 0))
    return pl.pallas_call(
        body, name=name, grid=(S // ts,),
        in_specs=[pl.BlockSpec((ts, LANES), lambda i: (i, cb)), pl.BlockSpec((1, LANES), lambda i: (0, 0))],
        out_specs=[blk, blk], out_shape=[jax.ShapeDtypeStruct((S, LANES), F32)] * 2,
        scratch_shapes=[pltpu.VMEM((3 * ts, LANES), F32), pltpu.VMEM((8, LANES), F32)],
        compiler_params=_cp(("arbitrary",)),
    )(u, bias)


def small_bwd(dc, dsp, u, bias, *, name):
    S = u.shape[0]
    ts = _tile(S, TILES["row"])
    cb = OFF["small"] // LANES
    nt = S // ts

    def body(dc_ref, dsp_ref, u_ref, b_ref, du_ref, db_ref, buf_ref, carry_ref):
        x = u_ref[...] + b_ref[...]

        @pl.when(pl.program_id(0) == 0)
        def _():
            carry_ref[...] = jnp.zeros((8, LANES), F32)

        dlf = _shift_scan(dc_ref[...], buf_ref, ts, reverse=True) + carry_ref[pl.ds(0, 1), :]
        buf_ref[pl.ds(0, ts), :] = dlf
        carry_ref[...] = jnp.broadcast_to(buf_ref[pl.ds(0, 1), :], (8, LANES))
        sg = _sig(x)
        dx = dlf * (1.0 - sg) + dsp_ref[...] * sg
        du_ref[...] = jnp.concatenate([dx, jnp.zeros((ts, SMALL_W - LANES), F32)], axis=1).astype(BF16)
        part = jnp.sum(dx, axis=0, keepdims=True)

        @pl.when(pl.program_id(0) == 0)
        def _():
            db_ref[...] = part

        @pl.when(pl.program_id(0) > 0)
        def _():
            db_ref[...] += part

    rev = pl.BlockSpec((ts, LANES), lambda i: (nt - 1 - i, 0))
    return pl.pallas_call(
        body, name=name, grid=(nt,),
        in_specs=[rev, rev, pl.BlockSpec((ts, LANES), lambda i: (nt - 1 - i, cb)), pl.BlockSpec((1, LANES), lambda i: (0, 0))],
        out_specs=[pl.BlockSpec((ts, SMALL_W), lambda i: (nt - 1 - i, 0)), pl.BlockSpec((1, LANES), lambda i: (0, 0))],
        out_shape=[jax.ShapeDtypeStruct((S, SMALL_W), BF16), jax.ShapeDtypeStruct((1, LANES), F32)],
        scratch_shapes=[pltpu.VMEM((3 * ts, LANES), F32), pltpu.VMEM((8, LANES), F32)],
        compiler_params=_cp(("arbitrary",)),
    )(dc, dsp, u, bias)


def _conv_in(mode, a, b):
    if mode == "plain":
        return a
    if mode == "mul":
        return a * b
    return a * _sig(b)


SUBLANES = 8
MANY_TAPS = 8


def _row_taps(src_ref, sh_ref, n_rows, many):
    if many:
        for b in range(1, SUBLANES):
            sh_ref[b, pl.ds(0, n_rows - SUBLANES), :] = src_ref[pl.ds(b, n_rows - SUBLANES), :]

    def tap(off, n):
        a, b = divmod(off, SUBLANES)
        if not many or b == 0:
            return src_ref[pl.ds(off, n), :]
        return sh_ref[b, pl.ds(SUBLANES * a, n), :]

    return tap


def conv_fwd(u, w, bias, *, mode, a_off, b_off, C, name):
    S = u.shape[0]
    Kc = w.shape[0]
    ts, cb, H = _tile(S, TILES["row"]), _tile(C, TILES["conv_c"]), CONV_HALO
    two = mode != "plain"
    rb = ts // H
    many = Kc >= MANY_TAPS

    def body(*refs):
        if two:
            a_ref, ap_ref, b_ref, bp_ref, w_ref, bias_ref, o_ref, x_ref = refs[:8]
            cur, prev = _conv_in(mode, a_ref[...], b_ref[...]), _conv_in(mode, ap_ref[...], bp_ref[...])
        else:
            a_ref, ap_ref, w_ref, bias_ref, o_ref, x_ref = refs[:6]
            cur, prev = a_ref[...], ap_ref[...]
        x_ref[pl.ds(0, H), :] = jnp.where(pl.program_id(1) == 0, 0.0, prev)
        x_ref[pl.ds(H, ts), :] = cur
        tap = _row_taps(x_ref, refs[-1], ts + H, many)
        acc = jnp.broadcast_to(bias_ref[...], (ts, cb))
        for j in range(Kc):
            acc = acc + w_ref[pl.ds(j, 1), :] * tap(H - (Kc - 1) + j, ts)
        o_ref[...] = acc

    def cur_spec(off):
        return pl.BlockSpec((ts, cb), lambda c, i: (i, c + off // cb))

    def prev_spec(off):
        return pl.BlockSpec((H, cb), lambda c, i: (jnp.maximum(i * rb - 1, 0), c + off // cb))

    in_specs, args = [cur_spec(a_off), prev_spec(a_off)], [u, u]
    if two:
        in_specs += [cur_spec(b_off), prev_spec(b_off)]
        args += [u, u]
    in_specs += [pl.BlockSpec((Kc, cb), lambda c, i: (0, c)), pl.BlockSpec((1, cb), lambda c, i: (0, c))]
    return pl.pallas_call(
        body, name=name, grid=(C // cb, S // ts), in_specs=in_specs,
        out_specs=pl.BlockSpec((ts, cb), lambda c, i: (i, c)), out_shape=jax.ShapeDtypeStruct((S, C), F32),
        scratch_shapes=[pltpu.VMEM((ts + H, cb), F32), pltpu.VMEM((SUBLANES if many else 1, ts + H, cb), F32)],
        compiler_params=_cp(("parallel", "arbitrary")),
    )(*args, w, bias)


def conv_bwd(dy, u, w, *, mode, a_off, b_off, C, name):
    S = u.shape[0]
    Kc = w.shape[0]
    ts, cb, H = _tile(S, TILES["row"]), _tile(C, TILES["conv_c"]), CONV_HALO
    two = mode != "plain"
    rb, nt = ts // H, S // ts

    many = Kc >= MANY_TAPS

    def body(*refs):
        if two:
            dy_ref, dyn_ref, a_ref, ap_ref, b_ref, bp_ref, w_ref, da_ref, db_ref, dw_ref, dbias_ref, x_ref, g_ref = refs[:13]
            a, b = a_ref[...], b_ref[...]
            cur, prev = _conv_in(mode, a, b), _conv_in(mode, ap_ref[...], bp_ref[...])
        else:
            dy_ref, dyn_ref, a_ref, ap_ref, w_ref, da_ref, dw_ref, dbias_ref, x_ref, g_ref = refs[:10]
            cur, prev = a_ref[...], ap_ref[...]
        i = pl.program_id(1)
        x_ref[pl.ds(0, H), :] = jnp.where(i == 0, 0.0, prev)
        x_ref[pl.ds(H, ts), :] = cur
        g = dy_ref[...]
        g_ref[pl.ds(0, ts), :] = g
        g_ref[pl.ds(ts, H), :] = jnp.where(i == nt - 1, 0.0, dyn_ref[...])
        x_tap = _row_taps(x_ref, refs[-2], ts + H, many)
        g_tap = _row_taps(g_ref, refs[-1], ts + H, many)

        @pl.when(i == 0)
        def _():
            dw_ref[...] = jnp.zeros((Kc, cb), F32)
            dbias_ref[...] = jnp.zeros((1, cb), F32)

        dbias_ref[...] += jnp.sum(g, axis=0, keepdims=True)
        dx = jnp.zeros((ts, cb), F32)
        for j in range(Kc):
            dx = dx + w_ref[pl.ds(j, 1), :] * g_tap(Kc - 1 - j, ts)
            dw_ref[pl.ds(j, 1), :] += jnp.sum(g * x_tap(H - (Kc - 1) + j, ts), axis=0, keepdims=True)
        if mode == "plain":
            da_ref[...] = dx.astype(BF16)
        elif mode == "mul":
            da_ref[...] = (dx * b).astype(BF16)
            db_ref[...] = (dx * a).astype(BF16)
        else:
            sg = _sig(b)
            da_ref[...] = (dx * sg).astype(BF16)
            db_ref[...] = (dx * a * sg * (1.0 - sg)).astype(BF16)

    def cur_spec(off):
        return pl.BlockSpec((ts, cb), lambda c, i: (i, c + off // cb))

    def prev_spec(off):
        return pl.BlockSpec((H, cb), lambda c, i: (jnp.maximum(i * rb - 1, 0), c + off // cb))

    out_blk = pl.BlockSpec((ts, cb), lambda c, i: (i, c))
    in_specs = [out_blk, pl.BlockSpec((H, cb), lambda c, i: (jnp.minimum((i + 1) * rb, S // H - 1), c)),
                cur_spec(a_off), prev_spec(a_off)]
    args = [dy, dy, u, u]
    if two:
        in_specs += [cur_spec(b_off), prev_spec(b_off)]
        args += [u, u]
    in_specs.append(pl.BlockSpec((Kc, cb), lambda c, i: (0, c)))
    n_d = 2 if two else 1
    return pl.pallas_call(
        body, name=name, grid=(C // cb, nt), in_specs=in_specs,
        out_specs=[out_blk] * n_d + [pl.BlockSpec((Kc, cb), lambda c, i: (0, c)), pl.BlockSpec((1, cb), lambda c, i: (0, c))],
        out_shape=[jax.ShapeDtypeStruct((S, C), BF16)] * n_d + [jax.ShapeDtypeStruct((Kc, C), F32), jax.ShapeDtypeStruct((1, C), F32)],
        scratch_shapes=[pltpu.VMEM((ts + H, cb), F32), pltpu.VMEM((ts + H, cb), F32)]
        + [pltpu.VMEM((SUBLANES if many else 1, ts + H, cb), F32)] * 2,
        compiler_params=_cp(("parallel", "arbitrary")),
    )(*args, w)


def _usec(ts, name):
    return pl.BlockSpec((ts, BW), lambda i, o=OFF[name] // BW: (i, o))


def _acc_rows(ref, part):
    @pl.when(pl.program_id(0) == 0)
    def _():
        ref[...] = part

    @pl.when(pl.program_id(0) > 0)
    def _():
        ref[...] += part


def post_fwd(u, y_ssd, cv_c, cv_d, nw, lnw, lnb, *, name):
    S = u.shape[0]
    ts = _tile(S, TILES["post"])

    def body(z_ref, scb_ref, gc_ref, gd_ref, ys_ref, cc_ref, cd_ref, nw_ref, lw_ref, lb_ref, yb_ref, yc_ref, yd_ref):
        t = ys_ref[...] * _silu(z_ref[...])
        r = lax.rsqrt(jnp.mean(t * t, axis=-1, keepdims=True) + EPS)
        yb_ref[...] = (t * r * nw_ref[...]).astype(BF16)
        yc_ref[...] = (scb_ref[...] * cc_ref[...] * _silu(gc_ref[...])).astype(BF16)
        cf = cd_ref[...]
        mu = jnp.mean(cf, axis=-1, keepdims=True)
        xc = cf - mu
        rl = lax.rsqrt(jnp.mean(xc * xc, axis=-1, keepdims=True) + EPS)
        yln = xc * rl * lw_ref[...] + lb_ref[...]
        yd_ref[...] = (_silu(yln) * _silu(gd_ref[...])).astype(BF16)

    row = pl.BlockSpec((ts, BW), lambda i: (i, 0))
    vec = pl.BlockSpec((1, BW), lambda i: (0, 0))
    return pl.pallas_call(
        body, name=name, grid=(S // ts,),
        in_specs=[_usec(ts, "z"), _usec(ts, "scb"), _usec(ts, "gc"), _usec(ts, "gd"), row, row, row, vec, vec, vec],
        out_specs=[row] * 3, out_shape=[jax.ShapeDtypeStruct((S, BW), BF16)] * 3, compiler_params=_cp(("parallel",)),
    )(u, u, u, u, y_ssd, cv_c, cv_d, nw, lnw, lnb)


def post_bwd(dy_a, dy_b, dy_c, dy_d, u, o, y_ssd, cv_c, cv_d, nw, lnw, lnb, *, name):
    S = u.shape[0]
    ts = _tile(S, TILES["post"])

    def body(dya_ref, dyb_ref, dyc_ref, dyd_ref, ga_ref, z_ref, scb_ref, gc_ref, gd_ref, o_ref, ys_ref, cc_ref, cd_ref,
             nw_ref, lw_ref, lb_ref,
             do_ref, dl_ref, dga_ref, dz_ref, dscb_ref, dgc_ref, dgd_ref, dys_ref, dcc_ref, dcd_ref, dnw_ref, dlw_ref, dlb_ref):
        ga, ov, dya = ga_ref[...], o_ref[...], dya_ref[...]
        dob = (dya * _silu(ga)).astype(BF16)
        do_ref[...] = dob
        dga_ref[...] = (dya * ov * _dsilu(ga)).astype(BF16)
        prod = dob.astype(F32) * ov
        lane = lax.broadcasted_iota(jnp.int32, (ts, LANES), 1)
        delta = jnp.zeros((ts, LANES), F32)
        for h in range(FOX_HEADS):
            col = jnp.sum(prod[:, h * FOX_HD:(h + 1) * FOX_HD], axis=1, keepdims=True)
            delta = jnp.where(lane == h, col, delta)
        dl_ref[...] = delta
        ys, z, dyb = ys_ref[...], z_ref[...], dyb_ref[...]
        sz = _silu(z)
        t = ys * sz
        r = lax.rsqrt(jnp.mean(t * t, axis=-1, keepdims=True) + EPS)
        th = t * r
        dth = dyb * nw_ref[...]
        dt_ = r * (dth - th * jnp.mean(dth * th, axis=-1, keepdims=True))
        dys_ref[...] = dt_ * sz
        dz_ref[...] = (dt_ * ys * _dsilu(z)).astype(BF16)
        _acc_rows(dnw_ref, jnp.sum(dyb * th, axis=0, keepdims=True))
        scb, cc, gc, dyc = scb_ref[...], cc_ref[...], gc_ref[...], dyc_ref[...]
        sg = _silu(gc)
        dscb_ref[...] = (dyc * cc * sg).astype(BF16)
        dcc_ref[...] = dyc * scb * sg
        dgc_ref[...] = (dyc * scb * cc * _dsilu(gc)).astype(BF16)
        cf, gd, dyd = cd_ref[...], gd_ref[...], dyd_ref[...]
        mu = jnp.mean(cf, axis=-1, keepdims=True)
        xc = cf - mu
        rl = lax.rsqrt(jnp.mean(xc * xc, axis=-1, keepdims=True) + EPS)
        xh = xc * rl
        yln = xh * lw_ref[...] + lb_ref[...]
        dyln = dyd * _silu(gd) * _dsilu(yln)
        dgd_ref[...] = (dyd * _silu(yln) * _dsilu(gd)).astype(BF16)
        _acc_rows(dlw_ref, jnp.sum(dyln * xh, axis=0, keepdims=True))
        _acc_rows(dlb_ref, jnp.sum(dyln, axis=0, keepdims=True))
        dxh = dyln * lw_ref[...]
        dcd_ref[...] = rl * (dxh - jnp.mean(dxh, axis=-1, keepdims=True) - xh * jnp.mean(dxh * xh, axis=-1, keepdims=True))

    row = pl.BlockSpec((ts, BW), lambda i: (i, 0))
    vec = pl.BlockSpec((1, BW), lambda i: (0, 0))
    sd = jax.ShapeDtypeStruct
    return pl.pallas_call(
        body, name=name, grid=(S // ts,),
        in_specs=[row] * 4 + [_usec(ts, n) for n in ("ga", "z", "scb", "gc", "gd")] + [row] * 4 + [vec] * 3,
        out_specs=[row, pl.BlockSpec((ts, LANES), lambda i: (i, 0))] + [row] * 8 + [vec] * 3,
        out_shape=[sd((S, BW), BF16), sd((S, LANES), F32)] + [sd((S, BW), BF16)] * 5 + [sd((S, BW), F32)] * 3 + [sd((1, BW), F32)] * 3,
        compiler_params=_cp(("arbitrary",)),
    )(dy_a, dy_b, dy_c, dy_d, u, u, u, u, u, o, y_ssd, cv_c, cv_d, nw, lnw, lnb)


def merge_fwd(h, ys, wg, bg, wb, *, name):
    S, D = h.shape
    tm, tn = _tile(S, 1024), _tile(D, 512)
    nb = D // tn

    def body(h_ref, y0, y1, y2, y3, wg_ref, bg_ref, wb_ref, m_ref, g_ref, p_ref, acc_ref):
        i = pl.program_id(2)
        g = jnp.dot(h_ref[...], wg_ref[...], preferred_element_type=F32) + bg_ref[...]
        gate = _sig(g)
        for b, y_ref in enumerate((y0, y1, y2, y3)):
            @pl.when(i == b)
            def _(y_ref=y_ref):
                p = jnp.dot(y_ref[...], wb_ref[...], preferred_element_type=F32)
                g_ref[...] = gate.astype(BF16)
                p_ref[...] = p.astype(BF16)
                if b == 0:
                    acc_ref[...] = gate * p
                else:
                    acc_ref[...] += gate * p

        @pl.when(i == N_BRANCH - 1)
        def _():
            m_ref[...] = acc_ref[...].astype(BF16)

    yspec = pl.BlockSpec((tm, BW), lambda m, n, i: (m, 0))
    return pl.pallas_call(
        body, name=name, grid=(S // tm, nb, N_BRANCH),
        in_specs=[pl.BlockSpec((tm, D), lambda m, n, i: (m, 0)), yspec, yspec, yspec, yspec,
                  pl.BlockSpec((None, D, tn), lambda m, n, i: (i, 0, n)),
                  pl.BlockSpec((None, 1, tn), lambda m, n, i: (i, 0, n)),
                  pl.BlockSpec((None, BW, tn), lambda m, n, i: (i, 0, n))],
        out_specs=[pl.BlockSpec((tm, tn), lambda m, n, i: (m, n)),
                   pl.BlockSpec((tm, tn), lambda m, n, i: (m, i * nb + n)),
                   pl.BlockSpec((tm, tn), lambda m, n, i: (m, i * nb + n))],
        out_shape=[jax.ShapeDtypeStruct((S, D), BF16), jax.ShapeDtypeStruct((S, N_BRANCH * D), BF16),
                   jax.ShapeDtypeStruct((S, N_BRANCH * D), BF16)],
        scratch_shapes=[pltpu.VMEM((tm, tn), F32)],
        compiler_params=_cp(("parallel", "parallel", "arbitrary")),
    )(h, *ys, wg, bg, wb)


def merge_bwd(dm, gates, proj, *, name):
    S, D = dm.shape
    ts, tn = _tile(S, TILES["row"]), _tile(D, 512)
    nb = D // tn

    def body(dm_ref, g_ref, p_ref, dp_ref, dg_ref, db_ref):
        i, b = pl.program_id(1), pl.program_id(2)
        d = dm_ref[...]
        g = g_ref[...].astype(F32)
        dp_ref[...] = (d * g).astype(BF16)
        dg = d * p_ref[...].astype(F32) * g * (1.0 - g)
        dg_ref[...] = dg.astype(BF16)
        part = jnp.sum(dg, axis=0, keepdims=True)

        @pl.when((i == 0) & (b == 0))
        def _():
            db_ref[...] = jnp.zeros((N_BRANCH, tn), F32)

        row = lax.broadcasted_iota(jnp.int32, (N_BRANCH, tn), 0)
        db_ref[...] += jnp.where(row == b, part, 0.0)

    wide = pl.BlockSpec((ts, tn), lambda n, i, b: (i, b * nb + n))
    return pl.pallas_call(
        body, name=name, grid=(nb, S // ts, N_BRANCH),
        in_specs=[pl.BlockSpec((ts, tn), lambda n, i, b: (i, n)), wide, wide],
        out_specs=[wide, wide, pl.BlockSpec((N_BRANCH, tn), lambda n, i, b: (0, n))],
        out_shape=[jax.ShapeDtypeStruct((S, N_BRANCH * D), BF16)] * 2 + [jax.ShapeDtypeStruct((N_BRANCH, D), F32)],
        compiler_params=_cp(("parallel", "arbitrary", "arbitrary")),
    )(dm, gates, proj)


_NT = (((1,), (1,)), ((), ()))
_TN = (((0,), (0,)), ((), ()))


def _tri_pairs(n, by_key):
    if by_key:
        pairs = [(i, j) for j in range(n) for i in range(j, n)]
    else:
        pairs = [(i, j) for i in range(n) for j in range(i + 1)]
    return (jnp.array([p[0] for p in pairs], jnp.int32), jnp.array([p[1] for p in pairs], jnp.int32))


def attn_fwd(u, c_row, *, name):
    S = u.shape[0]
    T = _tile(S, TILES["att"])
    n = S // T
    qo, ko, vo, go = (OFF[k] // FOX_HD for k in ("q", "k", "v", "ga"))
    scale = FOX_HD ** -0.5
    it, jt = _tri_pairs(n, by_key=False)

    def body(it_ref, jt_ref, q_ref, k_ref, v_ref, ck_ref, ga_ref, o_ref, lse_ref, ya_ref, m_ref, l_ref, acc_ref):
        i, j = it_ref[pl.program_id(1)], jt_ref[pl.program_id(1)]

        @pl.when(j == 0)
        def _():
            m_ref[...] = jnp.full((T, 1), NEG, F32)
            l_ref[...] = jnp.zeros((T, 1), F32)
            acc_ref[...] = jnp.zeros((T, FOX_HD), F32)

        def step(masked):
            qb = (q_ref[...] * (scale * LOG2E)).astype(BF16)
            s = lax.dot_general(qb, k_ref[...].astype(BF16), _NT, preferred_element_type=F32) - ck_ref[...]
            if masked:
                row = lax.broadcasted_iota(jnp.int32, (T, T), 0)
                col = lax.broadcasted_iota(jnp.int32, (T, T), 1)
                s = jnp.where(col <= row, s, NEG)
            m_old = m_ref[...]
            m_new = jnp.maximum(m_old, jnp.max(s, axis=1, keepdims=True))
            alpha = jnp.exp2(m_old - m_new)
            p = jnp.exp2(s - m_new)
            l_ref[...] = alpha * l_ref[...] + jnp.sum(p, axis=1, keepdims=True)
            p_hi = p.astype(BF16)
            p_lo = (p - p_hi.astype(F32)).astype(BF16)
            vb = v_ref[...].astype(BF16)
            pv = jnp.dot(p_hi, vb, preferred_element_type=F32) + jnp.dot(p_lo, vb, preferred_element_type=F32)
            acc_ref[...] = alpha * acc_ref[...] + pv
            m_ref[...] = m_new

        @pl.when(j < i)
        def _():
            step(False)

        @pl.when(j == i)
        def _():
            step(True)
            o = acc_ref[...] / l_ref[...]
            o_ref[...] = o
            lse_ref[...] = m_ref[...] + jnp.log(l_ref[...]) * LOG2E
            ya_ref[...] = (o * _silu(ga_ref[...])).astype(BF16)

    def qsec(off):
        return pl.BlockSpec((T, FOX_HD), lambda h, p, it, jt: (it[p], off + h))

    def ksec(off):
        return pl.BlockSpec((T, FOX_HD), lambda h, p, it, jt: (jt[p], off + h))

    out = pl.BlockSpec((T, FOX_HD), lambda h, p, it, jt: (it[p], h))
    colv = pl.BlockSpec((None, T, 1), lambda h, p, it, jt: (h, it[p], 0))
    return pl.pallas_call(
        body, name=name,
        grid_spec=pltpu.PrefetchScalarGridSpec(
            num_scalar_prefetch=2, grid=(FOX_HEADS, n * (n + 1) // 2),
            in_specs=[qsec(qo), ksec(ko), ksec(vo), pl.BlockSpec((None, 1, T), lambda h, p, it, jt: (h, 0, jt[p])), qsec(go)],
            out_specs=[out, colv, out],
            scratch_shapes=[pltpu.VMEM((T, 1), F32), pltpu.VMEM((T, 1), F32), pltpu.VMEM((T, FOX_HD), F32)]),
        out_shape=[jax.ShapeDtypeStruct((S, BW), F32), jax.ShapeDtypeStruct((FOX_HEADS, S, 1), F32),
                   jax.ShapeDtypeStruct((S, BW), BF16)],
        compiler_params=_cp(("parallel", "arbitrary")),
    )(it, jt, u, u, u, c_row, u)


def attn_bwd(u, do, c_col, lse_row, delta_row, *, name):
    S = u.shape[0]
    T = _tile(S, TILES["att"])
    n = S // T
    qo, ko, vo = (OFF[k] // FOX_HD for k in ("q", "k", "v"))
    scale = FOX_HD ** -0.5
    it, jt = _tri_pairs(n, by_key=True)

    def body(it_ref, jt_ref, q_ref, k_ref, v_ref, do_ref, ck_ref, lse_ref, dl_ref, dq_ref, dk_ref, dv_ref, dc_ref,
             dka_ref, dva_ref, dca_ref):
        i, j = it_ref[pl.program_id(1)], jt_ref[pl.program_id(1)]

        @pl.when(pl.program_id(1) == 0)
        def _():
            dq_ref[...] = jnp.zeros((S, FOX_HD), F32)

        @pl.when(i == j)
        def _():
            dka_ref[...] = jnp.zeros((T, FOX_HD), F32)
            dva_ref[...] = jnp.zeros((T, FOX_HD), F32)
            dca_ref[...] = jnp.zeros((T, 1), F32)

        def step(masked):
            qf, dob = q_ref[...], do_ref[...]
            qb = (qf * scale).astype(BF16)
            kf = k_ref[...]
            st = lax.dot_general(kf.astype(BF16), (qf * (scale * LOG2E)).astype(BF16), _NT, preferred_element_type=F32)
            pt = jnp.exp2(st - ck_ref[...] - lse_ref[...])
            if masked:
                kpos = lax.broadcasted_iota(jnp.int32, (T, T), 0)
                qpos = lax.broadcasted_iota(jnp.int32, (T, T), 1)
                pt = jnp.where(kpos <= qpos, pt, 0.0)
            dva_ref[...] += jnp.dot(pt.astype(BF16), dob, preferred_element_type=F32)
            dpt = lax.dot_general(v_ref[...].astype(BF16), dob, _NT, preferred_element_type=F32)
            dst = pt * (dpt - dl_ref[...])
            dca_ref[...] -= jnp.sum(dst, axis=1, keepdims=True)
            dsb = dst.astype(BF16)
            dka_ref[...] += jnp.dot(dsb, qb, preferred_element_type=F32)
            rows = pl.ds(pl.multiple_of(i * T, T), T)
            dq_ref[rows, :] += lax.dot_general(dsb, (kf * scale).astype(BF16), _TN, preferred_element_type=F32)

        @pl.when(i > j)
        def _():
            step(False)

        @pl.when(i == j)
        def _():
            step(True)

        @pl.when(i == n - 1)
        def _():
            dk_ref[...] = dka_ref[...].astype(BF16)
            dv_ref[...] = dva_ref[...].astype(BF16)
            dc_ref[...] = dca_ref[...]

    def qsec(off):
        return pl.BlockSpec((T, FOX_HD), lambda h, p, it, jt: (it[p], off + h))

    def ksec(off):
        return pl.BlockSpec((T, FOX_HD), lambda h, p, it, jt: (jt[p], off + h))

    qrow = pl.BlockSpec((None, 1, T), lambda h, p, it, jt: (h, 0, it[p]))
    kout = pl.BlockSpec((T, FOX_HD), lambda h, p, it, jt: (jt[p], h))
    kcol = pl.BlockSpec((None, T, 1), lambda h, p, it, jt: (h, jt[p], 0))
    return pl.pallas_call(
        body, name=name,
        grid_spec=pltpu.PrefetchScalarGridSpec(
            num_scalar_prefetch=2, grid=(FOX_HEADS, n * (n + 1) // 2),
            in_specs=[qsec(qo), ksec(ko), ksec(vo), pl.BlockSpec((T, FOX_HD), lambda h, p, it, jt: (it[p], h)),
                      kcol, qrow, qrow],
            out_specs=[pl.BlockSpec((S, FOX_HD), lambda h, p, it, jt: (0, h)), kout, kout, kcol],
            scratch_shapes=[pltpu.VMEM((T, FOX_HD), F32), pltpu.VMEM((T, FOX_HD), F32), pltpu.VMEM((T, 1), F32)]),
        out_shape=[jax.ShapeDtypeStruct((S, BW), F32), jax.ShapeDtypeStruct((S, BW), BF16),
                   jax.ShapeDtypeStruct((S, BW), BF16), jax.ShapeDtypeStruct((FOX_HEADS, S, 1), F32)],
        compiler_params=_cp(("parallel", "arbitrary")),
    )(it, jt, u, u, u, do, c_col, lse_row, delta_row)


N_PAIR = SSM_HEADS // 2
PAIRS_PER_GROUP = N_PAIR // SSM_G


def _sel_t():
    r = lax.broadcasted_iota(jnp.int32, (LANES, BW), 0)
    c = lax.broadcasted_iota(jnp.int32, (LANES, BW), 1)
    return (lax.shift_right_logical(c, SSM_P.bit_length() - 1) == r).astype(BF16)


def _sel():
    r = lax.broadcasted_iota(jnp.int32, (BW, LANES), 0)
    c = lax.broadcasted_iota(jnp.int32, (BW, LANES), 1)
    return (lax.shift_right_logical(r, SSM_P.bit_length() - 1) == c).astype(BF16)


def _dot3(x, m):
    hi = x.astype(BF16)
    r1 = x - hi.astype(F32)
    mid = r1.astype(BF16)
    lo = (r1 - mid.astype(F32)).astype(BF16)
    d = functools.partial(jnp.dot, preferred_element_type=F32)
    return d(hi, m) + d(mid, m) + d(lo, m)


def _ssd_common(x_ref, sp_ref, al_ref, buf_ref, big_ref, cst_ref, LC):
    pre = x_ref[...]
    sg = _sig(pre)
    act = pre * sg
    dt = sp_ref[...]
    a = -jnp.exp(al_ref[...])
    cs = _shift_scan(dt * a, buf_ref, LC)
    sel_t = _sel_t()
    dtl = _dot3(dt, sel_t)
    csl = _dot3(cs, sel_t)
    big_ref[...] = csl
    csl_last = big_ref[pl.ds(LC - 1, 1), :]
    cst_ref[...] = cs.T
    return pre, sg, act, dt, a, cs, dtl, csl, csl_last


def ssd_fwd(xbc, sp, alog, dskip_l, *, name):
    S = xbc.shape[0]
    LC = _tile(S, TILES["ssd"])
    nc = S // LC

    def body(x_ref, sp_ref, al_ref, dk_ref, y_ref, hs_ref, st_ref, buf_ref, big_ref, cst_ref):
        @pl.when(pl.program_id(0) == 0)
        def _():
            st_ref[...] = jnp.zeros((N_PAIR, SSM_N, LANES), F32)

        pre, sg, act, dt, a, cs, dtl, csl, csl_last = _ssd_common(x_ref, sp_ref, al_ref, buf_ref, big_ref, cst_ref, LC)
        xs, bm, cm = act[:, :BW], act[:, BW:BW + SSM_G * SSM_N], act[:, BW + SSM_G * SSM_N:]
        e_all = jnp.exp(csl)
        dec = jnp.exp(csl_last - csl)
        ad = jnp.exp(csl_last)
        xd = xs * dtl
        tril = lax.broadcasted_iota(jnp.int32, (LC, LC), 0) >= lax.broadcasted_iota(jnp.int32, (LC, LC), 1)
        lane = lax.broadcasted_iota(jnp.int32, (LC, LANES), 1)
        for g in range(SSM_G):
            bgt = bm[:, g * SSM_N:(g + 1) * SSM_N].T.astype(BF16)
            cgb = cm[:, g * SSM_N:(g + 1) * SSM_N].astype(BF16)
            cb = jnp.dot(cgb, bgt, preferred_element_type=F32)
            for q in range(PAIRS_PER_GROUP):
                pp = g * PAIRS_PER_GROUP + q
                ln = slice(pp * LANES, (pp + 1) * LANES)
                xp = xd[:, ln]
                xpb = xp.astype(BF16)
                yh = []
                for hh in range(2):
                    row_b = jnp.broadcast_to(cst_ref[pl.ds(2 * pp + hh, 1), :], (LC, LC))
                    lmat = jnp.exp(jnp.where(tril, row_b.T - row_b, NEG))
                    yh.append(jnp.dot((cb * lmat).astype(BF16), xpb, preferred_element_type=F32))
                hin = st_ref[pp]
                hs_ref[pp] = hin
                yoff = jnp.dot(cgb, hin.astype(BF16), preferred_element_type=F32) * e_all[:, ln]
                y_ref[:, ln] = jnp.where(lane < SSM_P, yh[0], yh[1]) + yoff + xs[:, ln] * dk_ref[:, ln]
                st_ref[pp] = hin * ad[:, ln] + jnp.dot(bgt, (xp * dec[:, ln]).astype(BF16), preferred_element_type=F32)

    return pl.pallas_call(
        body, name=name, grid=(nc,),
        in_specs=[pl.BlockSpec((LC, SSM_CONV_DIM), lambda c: (c, 0)), pl.BlockSpec((LC, LANES), lambda c: (c, 0)),
                  pl.BlockSpec((1, LANES), lambda c: (0, 0)), pl.BlockSpec((1, BW), lambda c: (0, 0))],
        out_specs=[pl.BlockSpec((LC, BW), lambda c: (c, 0)), pl.BlockSpec((None, N_PAIR, SSM_N, LANES), lambda c: (c, 0, 0, 0))],
        out_shape=[jax.ShapeDtypeStruct((S, BW), F32), jax.ShapeDtypeStruct((nc, N_PAIR, SSM_N, LANES), F32)],
        scratch_shapes=[pltpu.VMEM((N_PAIR, SSM_N, LANES), F32), pltpu.VMEM((3 * LC, LANES), F32),
                        pltpu.VMEM((LC, BW), F32), pltpu.VMEM((LANES, LC), F32)],
        compiler_params=_cp(("arbitrary",)),
    )(xbc, sp, alog, dskip_l)


def ssd_bwd(dy, xbc, sp, alog, dskip_l, hs, *, name):
    S = xbc.shape[0]
    LC = _tile(S, TILES["ssd"])
    nc = S // LC
    GN = SSM_G * SSM_N

    def body(dy_ref, x_ref, sp_ref, al_ref, dk_ref, hs_ref, dx_ref, ddt_ref, da_ref, dd_ref,
             dh_ref, buf_ref, big_ref, cst_ref, gcs_ref, dxd_ref):
        @pl.when(pl.program_id(0) == 0)
        def _():
            dh_ref[...] = jnp.zeros((N_PAIR, SSM_N, LANES), F32)
            da_ref[...] = jnp.zeros((1, LANES), F32)
            dd_ref[...] = jnp.zeros((1, BW), F32)

        pre, sg, act, dt, a, cs, dtl, csl, csl_last = _ssd_common(x_ref, sp_ref, al_ref, buf_ref, big_ref, cst_ref, LC)
        dact = sg * (1.0 + pre * (1.0 - sg))
        xs, bm, cm = act[:, :BW], act[:, BW:BW + GN], act[:, BW + GN:]
        e_all = jnp.exp(csl)
        dec = jnp.exp(csl_last - csl)
        ad = jnp.exp(csl_last)
        xd = xs * dtl
        d_y = dy_ref[...]
        ri = lax.broadcasted_iota(jnp.int32, (LC, LC), 0)
        ci = lax.broadcasted_iota(jnp.int32, (LC, LC), 1)
        tril, triu = ri >= ci, ci >= ri
        lane = lax.broadcasted_iota(jnp.int32, (LC, LANES), 1)
        rowi = lax.broadcasted_iota(jnp.int32, (LC, LANES), 0)
        dot = functools.partial(jnp.dot, preferred_element_type=F32)
        dot_nt = functools.partial(lax.dot_general, dimension_numbers=_NT, preferred_element_type=F32)
        for g in range(SSM_G):
            gs = slice(g * SSM_N, (g + 1) * SSM_N)
            bg, cg = bm[:, gs], cm[:, gs]
            bgb, cgb = bg.astype(BF16), cg.astype(BF16)
            bgt, cgt = bg.T.astype(BF16), cg.T.astype(BF16)
            cb, cbt = dot(cgb, bgt), dot(bgb, cgt)
            dcb = jnp.zeros((LC, LC), F32)
            dcbt = jnp.zeros((LC, LC), F32)
            dcg = jnp.zeros((LC, SSM_N), F32)
            dbg = jnp.zeros((LC, SSM_N), F32)
            for q in range(PAIRS_PER_GROUP):
                pp = g * PAIRS_PER_GROUP + q
                ln = slice(pp * LANES, (pp + 1) * LANES)
                xp, dyp, ep, decp, adp = xd[:, ln], d_y[:, ln], e_all[:, ln], dec[:, ln], ad[:, ln]
                xpb, dypb = xp.astype(BF16), dyp.astype(BF16)
                hin, dho = hs_ref[pp], dh_ref[pp]
                hb, dhob = hin.astype(BF16), dho.astype(BF16)
                yoff = dot(cgb, hb) * ep
                dgb = (dyp * ep).astype(BF16)
                dh_ref[pp] = dho * adp + dot(cgt, dgb)
                dcg = dcg + dot_nt(dgb, hb)
                zf = xp * decp
                d_z = dot(bgb, dhob)
                dbg = dbg + dot_nt(zf.astype(BF16), dhob)
                dzz = d_z * zf
                last = jnp.sum(dzz, axis=0, keepdims=True) + jnp.sum(dho * hin, axis=0, keepdims=True) * adp
                gcs = dyp * yoff - dzz + jnp.where(rowi == LC - 1, last, 0.0)
                dxd = d_z * decp
                for hh in range(2):
                    row_b = jnp.broadcast_to(cst_ref[pl.ds(2 * pp + hh, 1), :], (LC, LC))
                    col_b = row_b.T
                    lmat = jnp.exp(jnp.where(tril, col_b - row_b, NEG))
                    lmat_t = jnp.exp(jnp.where(triu, row_b - col_b, NEG))
                    hm = (lane < SSM_P) if hh == 0 else (lane >= SSM_P)
                    dml = dot_nt(jnp.where(hm, dyp, 0.0).astype(BF16), xpb) * lmat
                    dmtl = dot_nt(jnp.where(hm, xp, 0.0).astype(BF16), dypb) * lmat_t
                    dcb = dcb + dml
                    dcbt = dcbt + dmtl
                    contrib = jnp.sum(dml * cb, axis=1, keepdims=True) - jnp.sum(dmtl * cbt, axis=1, keepdims=True)
                    gcs = gcs + jnp.where(lane == hh * SSM_P, contrib, 0.0)
                    dxd = dxd + jnp.where(hm, dot((cbt * lmat_t).astype(BF16), dypb), 0.0)
                gcs_ref[:, ln] = gcs
                dxd_ref[:, ln] = dxd
            dcg = dcg + dot(dcb.astype(BF16), bgb)
            dbg = dbg + dot(dcbt.astype(BF16), cgb)
            dx_ref[:, BW + g * SSM_N:BW + (g + 1) * SSM_N] = dbg * dact[:, BW + g * SSM_N:BW + (g + 1) * SSM_N]
            dx_ref[:, BW + GN + g * SSM_N:BW + GN + (g + 1) * SSM_N] = dcg * dact[:, BW + GN + g * SSM_N:BW + GN + (g + 1) * SSM_N]
        d_xd = dxd_ref[...]
        dx_ref[:, :BW] = (d_y * dk_ref[...] + d_xd * dtl) * dact[:, :BW]
        sel = _sel()
        dda = _shift_scan(_dot3(gcs_ref[...], sel), buf_ref, LC, reverse=True)
        ddt_ref[...] = _dot3(d_xd * xs, sel) + dda * a
        da_ref[...] += jnp.sum(dda * dt, axis=0, keepdims=True)
        dd_ref[...] += jnp.sum(d_y * xs, axis=0, keepdims=True)

    rev = lambda c: (nc - 1 - c, 0)
    return pl.pallas_call(
        body, name=name, grid=(nc,),
        in_specs=[pl.BlockSpec((LC, BW), rev), pl.BlockSpec((LC, SSM_CONV_DIM), rev), pl.BlockSpec((LC, LANES), rev),
                  pl.BlockSpec((1, LANES), lambda c: (0, 0)), pl.BlockSpec((1, BW), lambda c: (0, 0)),
                  pl.BlockSpec((None, N_PAIR, SSM_N, LANES), lambda c: (nc - 1 - c, 0, 0, 0))],
        out_specs=[pl.BlockSpec((LC, SSM_CONV_DIM), rev), pl.BlockSpec((LC, LANES), rev),
                   pl.BlockSpec((1, LANES), lambda c: (0, 0)), pl.BlockSpec((1, BW), lambda c: (0, 0))],
        out_shape=[jax.ShapeDtypeStruct((S, SSM_CONV_DIM), F32), jax.ShapeDtypeStruct((S, LANES), F32),
                   jax.ShapeDtypeStruct((1, LANES), F32), jax.ShapeDtypeStruct((1, BW), F32)],
        scratch_shapes=[pltpu.VMEM((N_PAIR, SSM_N, LANES), F32), pltpu.VMEM((3 * LC, LANES), F32),
                        pltpu.VMEM((LC, BW), F32), pltpu.VMEM((LANES, LC), F32),
                        pltpu.VMEM((LC, BW), F32), pltpu.VMEM((LC, BW), F32)],
        compiler_params=_cp(("arbitrary",)),
    )(dy, xbc, sp, alog, dskip_l, hs)


def addn(arrs, *, out_dtype, name):
    R, C = arrs[0].shape
    tr = _tile(R, max(8, (1 << 20) // C))

    def body(*refs):
        acc = refs[0][...].astype(F32)
        for r in refs[1:-1]:
            acc = acc + r[...].astype(F32)
        refs[-1][...] = acc.astype(out_dtype)

    blk = pl.BlockSpec((tr, C), lambda i: (i, 0))
    return pl.pallas_call(
        body, name=name, grid=(R // tr,), in_specs=[blk] * len(arrs), out_specs=blk,
        out_shape=jax.ShapeDtypeStruct((R, C), out_dtype), compiler_params=_cp(("parallel",)),
    )(*arrs)


def adamw(w, g, m, v, *, name):
    R, C = w.shape
    tr = _tile(R, max(8, (1 << 18) // C))
    c1, c2 = 1.0 / (1.0 - ADAM_B1 ** ADAM_STEP), 1.0 / (1.0 - ADAM_B2 ** ADAM_STEP)

    def body(w_ref, g_ref, m_ref, v_ref, d_ref, nm_ref, nv_ref):
        gv = g_ref[...]
        nm = ADAM_B1 * m_ref[...] + (1.0 - ADAM_B1) * gv
        nv = ADAM_B2 * v_ref[...] + (1.0 - ADAM_B2) * (gv * gv)
        nm_ref[...] = nm
        nv_ref[...] = nv
        d_ref[...] = -ADAM_LR * ((nm * c1) / (jnp.sqrt(nv * c2) + ADAM_EPS) + ADAM_WD * w_ref[...])

    blk = pl.BlockSpec((tr, C), lambda i: (i, 0))
    return pl.pallas_call(
        body, name=name, grid=(R // tr,), in_specs=[blk] * 4, out_specs=[blk] * 3,
        out_shape=[jax.ShapeDtypeStruct((R, C), F32)] * 3, compiler_params=_cp(("parallel",)),
    )(w, g, m, v)


_ANY = pl.BlockSpec(memory_space=pl.ANY)


def _place():
    return lax.axis_index("x"), lax.axis_index("y"), lax.axis_index("c")


def _rcopy(src, dst, sems_s, sems_r, k, to):
    return pltpu.make_async_remote_copy(src_ref=src, dst_ref=dst, send_sem=sems_s.at[k], recv_sem=sems_r.at[k],
                                        device_id=to, device_id_type=MESH)


def gather_chips(buf, *, name):
    _, _, R, C = buf.shape

    def body(in_ref, out_ref, ss, rs):
        del in_ref
        x, y, c = _place()
        chips = [(1 - x, y), (x, 1 - y), (1 - x, 1 - y)]
        me = 2 * x + y
        first = [_rcopy(out_ref.at[me, c], out_ref.at[me, c], ss, rs, j, (cx, cy, c)) for j, (cx, cy) in enumerate(chips)]
        for cp in first:
            cp.start()
        passed = []
        for j, (cx, cy) in enumerate(chips):
            blk = out_ref.at[2 * cx + cy, c]
            _rcopy(blk, blk, ss, rs, j, (x, y, c)).wait_recv()
            cp = _rcopy(blk, blk, ss, rs, 3 + j, (x, y, 1 - c))
            cp.start()
            passed.append(cp)
        for j, (cx, cy) in enumerate(chips):
            blk = out_ref.at[2 * cx + cy, 1 - c]
            _rcopy(blk, blk, ss, rs, 3 + j, (x, y, c)).wait_recv()
        for cp in first + passed:
            cp.wait_send()

    return pl.pallas_call(
        body, name=name, in_specs=[_ANY], out_specs=_ANY, out_shape=jax.ShapeDtypeStruct(buf.shape, buf.dtype),
        input_output_aliases={0: 0},
        scratch_shapes=[pltpu.SemaphoreType.DMA((6,)), pltpu.SemaphoreType.DMA((6,))],
    )(buf)


def gather_all(block, *, name):
    R, C = block.shape
    flips = [(fx, fy, fc) for fx in (0, 1) for fy in (0, 1) for fc in (0, 1)][1:]

    def body(in_ref, out_ref, ss, rs, ls):
        x, y, c = _place()
        me = 4 * x + 2 * y + c
        mine = pltpu.make_async_copy(in_ref, out_ref.at[me], ls)
        mine.start()
        sends = []
        for k, (fx, fy, fc) in enumerate(flips):
            px, py, pc = x ^ fx, y ^ fy, c ^ fc
            cp = _rcopy(in_ref, out_ref.at[me], ss, rs, k, (px, py, pc))
            cp.start()
            sends.append(cp)
        for k, (fx, fy, fc) in enumerate(flips):
            blk = out_ref.at[4 * (x ^ fx) + 2 * (y ^ fy) + (c ^ fc)]
            _rcopy(blk, blk, ss, rs, k, (x, y, c)).wait_recv()
        for cp in sends:
            cp.wait_send()
        mine.wait()

    return pl.pallas_call(
        body, name=name, in_specs=[_ANY], out_specs=_ANY, out_shape=jax.ShapeDtypeStruct((8, R, C), block.dtype),
        scratch_shapes=[pltpu.SemaphoreType.DMA((7,)), pltpu.SemaphoreType.DMA((7,)), pltpu.SemaphoreType.DMA],
    )(block)


def swap_partials(g_all, *, name):
    N, _, R, C = g_all.shape

    def body(in_ref, out_ref, ss, rs):
        x, y, c = _place()
        cps = [_rcopy(in_ref.at[k, 1 - c], out_ref.at[k], ss, rs, k, (x, y, 1 - c)) for k in range(N)]
        for cp in cps:
            cp.start()
        for cp in cps:
            cp.wait_recv()
        for cp in cps:
            cp.wait_send()

    return pl.pallas_call(
        body, name=name, in_specs=[_ANY], out_specs=_ANY, out_shape=jax.ShapeDtypeStruct((N, R, C), g_all.dtype),
        scratch_shapes=[pltpu.SemaphoreType.DMA((N,)), pltpu.SemaphoreType.DMA((N,))],
    )(g_all)


def share_halves(buf, *, name):
    def body(in_ref, out_ref, ss, rs):
        del in_ref
        x, y, c = _place()
        cp = _rcopy(out_ref.at[c], out_ref.at[c], ss, rs, 0, (x, y, 1 - c))
        cp.start()
        _rcopy(out_ref.at[1 - c], out_ref.at[1 - c], ss, rs, 0, (x, y, c)).wait_recv()
        cp.wait_send()

    return pl.pallas_call(
        body, name=name, in_specs=[_ANY], out_specs=_ANY, out_shape=jax.ShapeDtypeStruct(buf.shape, buf.dtype),
        input_output_aliases={0: 0},
        scratch_shapes=[pltpu.SemaphoreType.DMA((1,)), pltpu.SemaphoreType.DMA((1,))],
    )(buf)


def add_sibling(g_all, sib, c_arr, *, name):
    N, _, R, C = g_all.shape
    tr = _tile(R, max(8, (1 << 20) // C))

    def body(c_ref, a_ref, b_ref, o_ref):
        o_ref[...] = (a_ref[...].astype(F32) + b_ref[...].astype(F32)).astype(BF16)

    blk = pl.BlockSpec((None, tr, C), lambda k, r, c_ref: (k, r, 0))
    return pl.pallas_call(
        body, name=name,
        grid_spec=pltpu.PrefetchScalarGridSpec(
            num_scalar_prefetch=1, grid=(N, R // tr),
            in_specs=[pl.BlockSpec((None, None, tr, C), lambda k, r, c_ref: (k, c_ref[0], r, 0)), blk], out_specs=blk),
        out_shape=jax.ShapeDtypeStruct((N, R, C), BF16), compiler_params=_cp(("parallel", "parallel")),
    )(c_arr, g_all, sib)


def add_chips(part, rcv, place_arr, *, name):
    _, R, C = part.shape
    tr = _tile(R, max(8, (1 << 20) // C))

    def body(place_ref, p_ref, r0, r1, r2, o_ref):
        acc = p_ref[...].astype(F32)
        for r in (r0, r1, r2):
            acc = acc + r[...].astype(F32)
        o_ref[...] = acc

    def slot(j):
        return pl.BlockSpec((None, tr, C), lambda r, place_ref, j=j: (j, r, 0))

    return pl.pallas_call(
        body, name=name,
        grid_spec=pltpu.PrefetchScalarGridSpec(
            num_scalar_prefetch=1, grid=(R // tr,),
            in_specs=[pl.BlockSpec((None, tr, C), lambda r, place_ref: (place_ref[0], r, 0)), slot(0), slot(1), slot(2)],
            out_specs=pl.BlockSpec((None, tr, C), lambda r, place_ref: (place_ref[1], r, 0))),
        out_shape=jax.ShapeDtypeStruct((2, R, C), F32), compiler_params=_cp(("parallel",)),
    )(place_arr, part, rcv, rcv, rcv)


WEIGHTS = ("norm_w", "w_in", "fg_bias", "ssm_conv_w", "ssm_conv_b", "dt_bias", "a_log", "d_skip", "ssm_norm_w", "sc_conv_w",
           "sc_conv_b", "cf_conv_w", "cf_conv_b", "cf_ln_w", "cf_ln_b", "w_gate", "b_gate", "w_branch", "w_out", "final_norm_w")
BIG = ("w_in", "w_gate", "w_branch", "w_out")
SMALL = tuple(n for n in WEIGHTS if n not in BIG)
SMALL_SHARDED = ("ssm_conv_w", "sc_conv_w", "cf_conv_w", "b_gate")
N_CHIP = 4
PACK_C = D_MODEL
PACK_ROWS = 1024


def _pack(arrs, cols, row_mult):
    return _pack_groups([arrs], cols, row_mult)[0]


def _pack_groups(groups, cols, row_mult):
    n = sum(math.prod(a.shape) for a in groups[0])
    rows = -(-n // cols)
    rows = -(-rows // row_mult) * row_mult
    parts = []
    for arrs in groups:
        parts += [a.reshape(-1) for a in arrs]
        if rows * cols > n:
            parts.append(jnp.zeros((rows * cols - n,), arrs[0].dtype))
    return jnp.concatenate(parts).reshape(len(groups), rows, cols)


PIECE_ROWS = 16


def _pack_rows(groups, cols, row_mult):
    as_list = lambda a: list(a) if isinstance(a, (list, tuple)) else [a]
    n_rows = lambda a: sum(math.prod(b.shape) // cols for b in as_list(a))
    rows = [sum(-(-n_rows(a) // PIECE_ROWS) * PIECE_ROWS for a in arrs) for arrs in groups]
    total = -(-max(rows) // row_mult) * row_mult
    dtype = as_list(groups[0][0])[0].dtype
    parts = []
    for arrs, r in zip(groups, rows):
        for a in arrs:
            parts += [b.reshape(-1, cols) for b in as_list(a)]
            pad = -n_rows(a) % PIECE_ROWS
            if pad:
                parts.append(jnp.zeros((pad, cols), dtype))
        if total > r:
            parts.append(jnp.zeros((total - r, cols), dtype))
    return jnp.concatenate(parts, axis=0).reshape(len(groups), total, cols)


def _unpack_rows(packed, shapes):
    cols = packed.shape[-1]
    out, o = [], 0
    for s in shapes:
        r = math.prod(s) // cols
        out.append(packed[o:o + r].reshape(s))
        o += -(-r // PIECE_ROWS) * PIECE_ROWS
    return out


def _unpack(packed, shapes):
    flat = packed.reshape(-1)
    out, o = [], 0
    for s in shapes:
        n = math.prod(s)
        out.append(flat[o:o + n].reshape(s))
        o += n
    return out


def _orig_cols():
    cols = []
    for n, s in zip(ORIG_NAMES, ORIG_SIZES):
        if n == "dt":
            cols.append((OFF["small"] + DT_LANE, s))
        elif n == "f":
            cols.append((OFF["small"] + F_LANE, s))
        else:
            cols.append((OFF[n], s))
    return cols


def _rows_to_padded(per_chip):
    q_in = N_IN // N_CHIP
    orig_off = dict(zip(ORIG_NAMES, [sum(ORIG_SIZES[:i]) for i in range(len(ORIG_SIZES))]))
    size = dict(zip(ORIG_NAMES, ORIG_SIZES))
    parts = []
    for n in PAD_ORDER + ("dt", "f"):
        lo, hi = orig_off[n], orig_off[n] + size[n]
        for k in range(N_CHIP):
            a, b = max(lo, k * q_in), min(hi, (k + 1) * q_in)
            if a < b:
                parts.append(per_chip[k][a - k * q_in:b - k * q_in])
    parts.append(jnp.zeros((SMALL_W - size["dt"] - size["f"], per_chip[0].shape[1]), per_chip[0].dtype))
    return jnp.concatenate(parts, axis=0)


def _owner_windows(k):
    q_in = N_IN // N_CHIP
    lo, hi = k * q_in, (k + 1) * q_in
    out, o = [], 0
    for start, s in _orig_cols():
        a, b = max(lo, o), min(hi, o + s)
        if a < b:
            p0, p1 = start + a - o, start + b - o
            w0, w1 = p0 // PIECE_ROWS * PIECE_ROWS, -(-p1 // PIECE_ROWS) * PIECE_ROWS
            out.append((w0, w1, p0 - w0, p1 - p0))
        o += s
    return out


def _owner_rows(windows_packed, k):
    parts, o = [], 0
    for w0, w1, lead, rows in _owner_windows(k):
        parts.append(windows_packed[o + lead:o + lead + rows])
        o += w1 - w0
    return jnp.concatenate(parts, axis=0)


IN_WINDOW_ROWS = max(sum(w1 - w0 for w0, w1, _, _ in _owner_windows(k)) for k in range(4))


def _lanes_row(parts, width=LANES):
    v = jnp.concatenate([p.reshape(-1) for p in parts])
    return jnp.pad(v, (0, width - v.shape[0])).reshape(1, width)


def kernel(x, norm_w, w_in, fg_bias, ssm_conv_w, ssm_conv_b, dt_bias, a_log, d_skip, ssm_norm_w, sc_conv_w, sc_conv_b, cf_conv_w, cf_conv_b, cf_ln_w, cf_ln_b, w_gate, b_gate, w_branch, w_out, final_norm_w, loss_target, m_norm_w, m_w_in, m_fg_bias, m_ssm_conv_w, m_ssm_conv_b, m_dt_bias, m_a_log, m_d_skip, m_ssm_norm_w, m_sc_conv_w, m_sc_conv_b, m_cf_conv_w, m_cf_conv_b, m_cf_ln_w, m_cf_ln_b, m_w_gate, m_b_gate, m_w_branch, m_w_out, m_final_norm_w, v_norm_w, v_w_in, v_fg_bias, v_ssm_conv_w, v_ssm_conv_b, v_dt_bias, v_a_log, v_d_skip, v_ssm_norm_w, v_sc_conv_w, v_sc_conv_b, v_cf_conv_w, v_cf_conv_b, v_cf_ln_w, v_cf_ln_b, v_w_gate, v_b_gate, v_w_branch, v_w_out, v_final_norm_w):
    wts = dict(norm_w=norm_w, w_in=w_in, fg_bias=fg_bias, ssm_conv_w=ssm_conv_w, ssm_conv_b=ssm_conv_b, dt_bias=dt_bias,
               a_log=a_log, d_skip=d_skip, ssm_norm_w=ssm_norm_w, sc_conv_w=sc_conv_w, sc_conv_b=sc_conv_b,
               cf_conv_w=cf_conv_w, cf_conv_b=cf_conv_b, cf_ln_w=cf_ln_w, cf_ln_b=cf_ln_b, w_gate=w_gate, b_gate=b_gate,
               w_branch=w_branch, w_out=w_out, final_norm_w=final_norm_w)
    mom = dict(norm_w=m_norm_w, w_in=m_w_in, fg_bias=m_fg_bias, ssm_conv_w=m_ssm_conv_w, ssm_conv_b=m_ssm_conv_b,
               dt_bias=m_dt_bias, a_log=m_a_log, d_skip=m_d_skip, ssm_norm_w=m_ssm_norm_w, sc_conv_w=m_sc_conv_w,
               sc_conv_b=m_sc_conv_b, cf_conv_w=m_cf_conv_w, cf_conv_b=m_cf_conv_b, cf_ln_w=m_cf_ln_w, cf_ln_b=m_cf_ln_b,
               w_gate=m_w_gate, b_gate=m_b_gate, w_branch=m_w_branch, w_out=m_w_out, final_norm_w=m_final_norm_w)
    vel = dict(norm_w=v_norm_w, w_in=v_w_in, fg_bias=v_fg_bias, ssm_conv_w=v_ssm_conv_w, ssm_conv_b=v_ssm_conv_b,
               dt_bias=v_dt_bias, a_log=v_a_log, d_skip=v_d_skip, ssm_norm_w=v_ssm_norm_w, sc_conv_w=v_sc_conv_w,
               sc_conv_b=v_sc_conv_b, cf_conv_w=v_cf_conv_w, cf_conv_b=v_cf_conv_b, cf_ln_w=v_cf_ln_w, cf_ln_b=v_cf_ln_b,
               w_gate=v_w_gate, b_gate=v_b_gate, w_branch=v_w_branch, w_out=v_w_out, final_norm_w=v_final_norm_w)
    L = norm_w.shape[0]
    S, D = x.shape[1], x.shape[2]
    assert D == D_MODEL and x.shape[0] == 1
    xi, yi, ci = _place()
    chip = 2 * xi + yi

    sh_shapes = [wts[n].shape for n in SMALL_SHARDED]
    got = gather_all(_pack([wts[n] for n in SMALL_SHARDED], LANES, 8), name="gather_small_w")
    per_chip = [_unpack(got[2 * k], sh_shapes) for k in range(N_CHIP)]
    full_small = {n: jnp.concatenate([per_chip[k][i] for k in range(N_CHIP)], axis=-1) for i, n in enumerate(SMALL_SHARDED)}

    q_in, q_d = N_IN // N_CHIP, D // N_CHIP
    sent_shapes = [(q_in, D), (N_BRANCH, q_d, D), (N_BRANCH, BW, q_d), (q_d, D)]
    def own_shard(l):
        packed = _pack_rows([[w_in[l].T.astype(BF16), w_gate[l].astype(BF16), w_branch[l].astype(BF16),
                              w_out[l].astype(BF16)]], PACK_C, PACK_ROWS)[0]
        half_rows = packed.shape[0] // 2
        return lax.dynamic_update_slice(lax.empty((N_CHIP, 2, half_rows, PACK_C), BF16),
                                        packed.reshape(1, 2, half_rows, PACK_C), (chip, 0, 0, 0))

    def whole_weights(got):
        pc = [_unpack_rows(got[k].reshape(-1, PACK_C), sent_shapes) for k in range(N_CHIP)]
        return dict(wpt=_rows_to_padded([pc[k][0] for k in range(N_CHIP)]),
                    wg=jnp.concatenate([pc[k][1] for k in range(N_CHIP)], axis=1),
                    wb=jnp.concatenate([pc[k][2] for k in range(N_CHIP)], axis=2),
                    wo=jnp.concatenate([pc[k][3] for k in range(N_CHIP)], axis=0))

    lw = [whole_weights(gather_chips(own_shard(0), name="gather_w"))]
    saved = []
    xl = x[0]
    for l in range(L):
        w = lw[l]
        h = rms_fwd(xl, norm_w[l][None], name="rms_fwd")
        if l + 1 < L:
            u, nxt = mm(h, w["wpt"], tb=True, comm=("gather_ici", own_shard(l + 1)), name="mm_in_gather")
        else:
            u = mm(h, w["wpt"], tb=True, name="mm_in")
        bias_small = _lanes_row([dt_bias[l], fg_bias[l]])
        sp, csum = small_fwd(u, bias_small, name="small_fwd")
        c8 = csum[:, F_LANE:F_LANE + FOX_HEADS].T * LOG2E
        c_col, c_row = c8[:, :, None], c8[:, None, :]
        o, lse, y_a = attn_fwd(u, c_row, name="attn_fwd")
        xbc = conv_fwd(u, full_small["ssm_conv_w"][l], ssm_conv_b[l][None], mode="plain", a_off=OFF["xbc"], b_off=0,
                       C=SSM_CONV_DIM, name="conv_ssm_fwd")
        alog_row = _lanes_row([a_log[l]])
        dskip_l = jnp.repeat(d_skip[l], SSM_P)[None]
        y_ssd, hs = ssd_fwd(xbc, sp, alog_row, dskip_l, name="ssd_fwd")
        cv_c = conv_fwd(u, full_small["sc_conv_w"][l], sc_conv_b[l][None], mode="mul", a_off=OFF["scc"], b_off=OFF["scx"],
                        C=BW, name="conv_sc_fwd")
        cv_d = conv_fwd(u, full_small["cf_conv_w"][l], cf_conv_b[l][None], mode="glu", a_off=OFF["glu"], b_off=OFF["glu"] + BW,
                        C=BW, name="conv_cf_fwd")
        y_b, y_c, y_d = post_fwd(u, y_ssd, cv_c, cv_d, ssm_norm_w[l][None], cf_ln_w[l][None], cf_ln_b[l][None], name="post_fwd")
        merged, gates, proj = merge_fwd(h, (y_a, y_b, y_c, y_d), w["wg"], full_small["b_gate"][l][:, None, :], w["wb"],
                                        name="merge_fwd")
        if l + 1 < L:
            x_next, nxt = mm(merged, w["wo"], add=xl, comm=("gather_d2d", nxt), name="mm_out_gather")
            lw.append(whole_weights(nxt))
        else:
            x_next = mm(merged, w["wo"], add=xl, name="mm_out")
        saved.append(dict(x=xl, h=h, u=u, bias_small=bias_small, sp=sp, c_col=c_col, c_row=c_row, o=o, lse=lse, xbc=xbc,
                          alog_row=alog_row, dskip_l=dskip_l, hs=hs, y_ssd=y_ssd, cv_c=cv_c, cv_d=cv_d,
                          ys=(y_a, y_b, y_c, y_d), merged=merged, gates=gates, proj=proj))
        xl = x_next

    sq, dx, d_final = loss_head(xl, final_norm_w[None], loss_target[0], name="loss_head")
    loss = lax.psum(sq[0, 0] * (0.5 / D), ("x", "y", "c"))

    small_g = {n: [None] * L for n in SMALL if n != "final_norm_w"}
    big_g = {n: [None] * L for n in BIG}

    def finish_exchange(l, part, rcv):
        full = share_halves(add_chips(part, rcv, jnp.stack([chip, ci]).astype(jnp.int32), name="add_chips"), name="share_halves")
        g_wg, g_wb, g_wo, g_win = _unpack_rows(full.reshape(-1, PACK_C),
                                               [(N_BRANCH, q_d, D), (N_BRANCH, q_d, BW), (q_d, D), (IN_WINDOW_ROWS, D)])
        g_in = lax.switch(chip, [functools.partial(_owner_rows, k=k) for k in range(N_CHIP)], g_win)
        big_g["w_in"][l] = g_in.T
        big_g["w_gate"][l] = g_wg
        big_g["w_branch"][l] = jnp.transpose(g_wb, (0, 2, 1))
        big_g["w_out"][l] = g_wo

    for l in reversed(range(L)):
        w, sv = lw[l], saved[l]
        u, h = sv["u"], sv["h"]
        dm = mm(dx, w["wo"], tb=True, name="mm_dmerged")
        d_wo = mm(sv["merged"], dx, ta=True, out_dtype=BF16, name="mm_dwo")
        dp, dg, dbg = merge_bwd(dm, sv["gates"], sv["proj"], name="merge_bwd")
        dys = [mm(dp, w["wb"][i], tb=True, K=D, a_koff=i * D, name="mm_dy") for i in range(N_BRANCH)]
        d_wbt = [mm(dp, sv["ys"][i], ta=True, M=D, a_moff=i * D, out_dtype=BF16, name="mm_dwb")
                 for i in range(N_BRANCH)]
        d_wg = mm(h, dg, ta=True, out_dtype=BF16, n_groups=N_BRANCH, name="mm_dwg")
        (do, delta, dga, dz, dscb, dgc, dgd, dy_ssd, dcv_c, dcv_d, dnw, dlnw, dlnb) = post_bwd(
            dys[0], dys[1], dys[2], dys[3], u, sv["o"], sv["y_ssd"], sv["cv_c"], sv["cv_d"],
            ssm_norm_w[l][None], cf_ln_w[l][None], cf_ln_b[l][None], name="post_bwd")
        delta_row = delta[:, :FOX_HEADS].T[:, None, :]
        lse_row = jnp.transpose(sv["lse"], (0, 2, 1))
        dq, dk, dv, dc_col = attn_bwd(u, do, sv["c_col"], lse_row, delta_row, name="attn_bwd")
        dscc, dscx, d_scw, d_scb = conv_bwd(dcv_c, u, full_small["sc_conv_w"][l], mode="mul", a_off=OFF["scc"],
                                            b_off=OFF["scx"], C=BW, name="conv_sc_bwd")
        dglua, dglug, d_cfw, d_cfb = conv_bwd(dcv_d, u, full_small["cf_conv_w"][l], mode="glu", a_off=OFF["glu"],
                                              b_off=OFF["glu"] + BW, C=BW, name="conv_cf_bwd")
        dxbc_pre, ddt, d_a, d_dl = ssd_bwd(dy_ssd, sv["xbc"], sv["sp"], sv["alog_row"], sv["dskip_l"], sv["hs"], name="ssd_bwd")
        dxbc, d_ssmw, d_ssmb = conv_bwd(dxbc_pre, u, full_small["ssm_conv_w"][l], mode="plain", a_off=OFF["xbc"], b_off=0,
                                        C=SSM_CONV_DIM, name="conv_ssm_bwd")
        dc_full = jnp.pad(dc_col[:, :, 0].T, ((0, 0), (F_LANE, LANES - F_LANE - FOX_HEADS)))
        du_small, dbias_small = small_bwd(dc_full, ddt, u, sv["bias_small"], name="small_bwd")
        by_name = dict(q=dq.astype(BF16), k=dk, v=dv, ga=dga, z=dz, scb=dscb, scc=dscc, scx=dscx, gc=dgc, gd=dgd, xbc=dxbc)
        du = jnp.concatenate([jnp.concatenate([dglua, dglug], axis=1) if n == "glu" else by_name[n] for n in PAD_ORDER]
                             + [du_small], axis=1)
        d_wpt = mm(du, h, ta=True, out_dtype=BF16, name="mm_dwp")

        dest = [[d_wg[:, k * q_d:(k + 1) * q_d]] + [t[k * q_d:(k + 1) * q_d] for t in d_wbt] + [d_wo[k * q_d:(k + 1) * q_d]]
                + [[d_wpt[w0:w1] for w0, w1, _, _ in _owner_windows(k)]] for k in range(N_CHIP)]
        g_all = _pack_rows(dest, PACK_C, PACK_ROWS)
        g_all = g_all.reshape(N_CHIP, 2, g_all.shape[1] // 2, PACK_C)
        sib = swap_partials(g_all, name="swap_partials")
        part = add_sibling(g_all, sib, ci.reshape(1).astype(jnp.int32), name="add_sibling")
        dh_gate = mm(dg, w["wg"], tb=True, name="mm_dh_gate")
        dh, rcv = mm(du, w["wpt"], add=dh_gate, comm=("scatter", part), name="mm_dh_in_scatter")
        finish_exchange(l, part, rcv)
        dx, d_nw = rms_bwd(dh, sv["x"], norm_w[l][None], dx, name="rms_bwd")

        a_neg = -jnp.exp(a_log[l])
        sg = dict(norm_w=d_nw[0], fg_bias=dbias_small[0, F_LANE:F_LANE + FOX_HEADS], ssm_conv_w=d_ssmw, ssm_conv_b=d_ssmb[0],
                  dt_bias=dbias_small[0, DT_LANE:DT_LANE + SSM_HEADS], a_log=d_a[0, :SSM_HEADS] * a_neg,
                  d_skip=d_dl.reshape(SSM_HEADS, SSM_P).sum(-1), ssm_norm_w=dnw[0], sc_conv_w=d_scw, sc_conv_b=d_scb[0],
                  cf_conv_w=d_cfw, cf_conv_b=d_cfb[0], cf_ln_w=dlnw[0], cf_ln_b=dlnb[0], b_gate=dbg.reshape(N_BRANCH, D))
        for n in sg:
            small_g[n][l] = sg[n]


    names = [n for n in SMALL if n != "final_norm_w"]
    stacked = [jnp.stack(small_g[n]) for n in names] + [d_final[0]]
    shapes = [a.shape for a in stacked]
    got = gather_all(_pack(stacked, LANES, 8), name="gather_small_g")
    tot = addn([got[d] for d in range(8)], out_dtype=F32, name="add_small_g")
    grads = dict(zip(names + ["final_norm_w"], _unpack(tot, shapes)))
    for n in SMALL_SHARDED:
        sz = wts[n].shape[-1]
        grads[n] = lax.dynamic_slice_in_dim(grads[n], chip * sz, sz, axis=grads[n].ndim - 1)
    for n in BIG:
        grads[n] = jnp.stack(big_g[n])

    delta, new_m, new_v = {}, {}, {}
    for n in BIG:
        two_d = lambda a: a.reshape(-1, a.shape[-1])
        d, nm, nv = adamw(two_d(wts[n]), two_d(grads[n]), two_d(mom[n]), two_d(vel[n]), name="adamw_" + n)
        delta[n], new_m[n], new_v[n] = (t.reshape(wts[n].shape) for t in (d, nm, nv))
    small_shapes = [wts[n].shape for n in SMALL]
    pk = lambda src: _pack([src[n] for n in SMALL], LANES, 8)
    d, nm, nv = adamw(pk(wts), pk(grads), pk(mom), pk(vel), name="adamw_small")
    for tgt, src in ((delta, d), (new_m, nm), (new_v, nv)):
        for n, a in zip(SMALL, _unpack(src, small_shapes)):
            tgt[n] = a

    return (loss, dx[None], *[grads[n] for n in WEIGHTS], *[delta[n] for n in WEIGHTS],
            *[new_m[n] for n in WEIGHTS], *[new_v[n] for n in WEIGHTS])
```

```python
import functools
import math

import jax
import jax.numpy as jnp
from jax import lax
from jax.experimental import pallas as pl
from jax.experimental.pallas import tpu as pltpu

F32, BF16 = jnp.float32, jnp.bfloat16
MESH = pl.DeviceIdType.MESH

D_MODEL = 2048
BW = D_MODEL // 2
FOX_HEADS, FOX_HD = 8, 128
SSM_HEADS, SSM_P, SSM_N, SSM_G = 16, 64, 128, 2
SSM_CONV_DIM = BW + 2 * SSM_G * SSM_N
N_BRANCH = 4
EPS = 1e-6
NEG = -1e30
LOG2E = 1.4426950408889634
ATT_KV_CHUNKS = 4
ORIG_SIZES = (BW, BW, BW, FOX_HEADS, BW, BW, SSM_CONV_DIM, SSM_HEADS, BW, BW, BW, BW, 2 * BW, BW)
ORIG_NAMES = ("q", "k", "v", "f", "ga", "z", "xbc", "dt", "scb", "scc", "scx", "gc", "glu", "gd")
N_IN = sum(ORIG_SIZES)
PAD_ORDER = ("q", "k", "v", "ga", "z", "scb", "scc", "scx", "gc", "glu", "gd", "xbc")
SMALL_W = 512
DT_LANE, F_LANE = 0, SSM_HEADS
OFF = {}
_o = 0
for _n in PAD_ORDER:
    OFF[_n] = _o
    _o += ORIG_SIZES[ORIG_NAMES.index(_n)]
OFF["small"] = _o
NP = _o + SMALL_W
LANES = 128
CONV_HALO = 32

VMEM_LIMIT = 56 * 1024 * 1024
MM_VMEM_BUDGET = 40 * 1024 * 1024

ADAM_LR, ADAM_B1, ADAM_B2, ADAM_EPS, ADAM_WD, ADAM_STEP = 0.001, 0.9, 0.999, 1e-08, 0.01, 10

TILES = dict(row=512, post=256, att=1024, ssd=256, mm_m=1024, mm_n=1024, conv_c=256)


def _cp(sem=None):
    return pltpu.CompilerParams(dimension_semantics=sem, vmem_limit_bytes=VMEM_LIMIT)


def _tile(n, pref):
    t = 1 << (min(n, pref).bit_length() - 1)
    while n % t:
        t //= 2
    return t


def _sig(x):
    return 1.0 / (1.0 + jnp.exp(-x))


def _silu(x):
    return x * _sig(x)


def _dsilu(x):
    s = _sig(x)
    return s * (1.0 + x * (1.0 - s))


def _softplus(x):
    return jnp.maximum(x, 0.0) + jnp.log(1.0 + jnp.exp(-jnp.abs(x)))


def _comm_copies(kind, src_ref, dst_ref, ss, rs):
    x, y, c = _place()
    chips = [(1 - x, y), (x, 1 - y), (1 - x, 1 - y)]
    me = 2 * x + y
    sends, recvs = [], []
    for j, (cx, cy) in enumerate(chips):
        if kind == "gather_ici":
            mine, theirs = dst_ref.at[me, c], dst_ref.at[2 * cx + cy, c]
            sends.append(_rcopy(mine, mine, ss, rs, j, (cx, cy, c)))
            recvs.append(_rcopy(theirs, theirs, ss, rs, j, (x, y, c)))
        elif kind == "gather_d2d":
            landed, theirs = dst_ref.at[2 * cx + cy, c], dst_ref.at[2 * cx + cy, 1 - c]
            sends.append(_rcopy(landed, landed, ss, rs, j, (x, y, 1 - c)))
            recvs.append(_rcopy(theirs, theirs, ss, rs, j, (x, y, c)))
        else:
            sends.append(_rcopy(src_ref.at[2 * cx + cy], dst_ref.at[j], ss, rs, j, (cx, cy, c)))
            recvs.append(_rcopy(src_ref.at[0], dst_ref.at[j], ss, rs, j, (x, y, c)))
    return sends, recvs


def mm(a, b, *, name, out_dtype=F32, add=None, ta=False, tb=False, M=None, K=None, N=None,
       a_koff=0, a_moff=0, b_koff=0, b_noff=0, comm=None, n_groups=1):
    b3 = b.ndim == 3
    assert not b3 or (tb and b_koff == 0 and b_noff == 0)
    M = M or (a.shape[1] if ta else a.shape[0])
    K = K or (a.shape[0] if ta else a.shape[1])
    N = N or (b.shape[-2] if tb else b.shape[1])
    tm, tn = _tile(M, TILES["mm_m"]), _tile(N, TILES["mm_n"])
    sa, sb, so = a.dtype.itemsize, b.dtype.itemsize, jnp.dtype(out_dtype).itemsize

    def need(tk):
        return 2 * tm * tk * sa + 2 * tk * tn * sb + 2 * tm * tn * so + (8 * tm * tn if add is not None else 0) + 4 * tm * tn

    tk = b.shape[2] if b3 else K
    while need(tk) > MM_VMEM_BUDGET and tk % 256 == 0:
        tk //= 2
    assert K % tk == 0 and a_koff % tk == 0 and b_koff % tk == 0 and b_noff % tn == 0 and a_moff % tm == 0, (name, K, tk, tn)
    nk = K // tk
    ako, amo, bko, bno = a_koff // tk, a_moff // tm, b_koff // tk, b_noff // tn
    dims = (((0 if ta else 1,), (1 if tb else 0,)), ((), ()))

    n_in = 2 + (add is not None) + (comm is not None)
    grid = (M // tm, N // tn, nk)

    def body(*refs):
        a_ref, b_ref = refs[:2]
        add_ref = refs[2] if add is not None else None
        o_ref = refs[n_in]
        acc_ref = refs[n_in + 1 + (comm is not None)] if nk > 1 else None
        if comm is not None:
            step = (pl.program_id(0) * grid[1] + pl.program_id(1)) * grid[2] + pl.program_id(2)
            sends, recvs = _comm_copies(comm[0], refs[n_in - 1], refs[n_in + 1], refs[-2], refs[-1])

            @pl.when(step == 0)
            def _():
                for cp in sends:
                    cp.start()

        prod = lax.dot_general(a_ref[...].astype(BF16), b_ref[...].astype(BF16), dims, preferred_element_type=F32)

        def finish(acc):
            if add_ref is not None:
                acc = acc + add_ref[...]
            o_ref[...] = acc.astype(out_dtype)

        if nk == 1:
            finish(prod)
        else:
            k = pl.program_id(2)

            @pl.when(k == 0)
            def _():
                acc_ref[...] = prod

            @pl.when(k > 0)
            def _():
                acc_ref[...] += prod

            @pl.when(k == nk - 1)
            def _():
                finish(acc_ref[...])

        if comm is not None:
            @pl.when(step == grid[0] * grid[1] * grid[2] - 1)
            def _():
                for cp in recvs:
                    cp.wait_recv()
                for cp in sends:
                    cp.wait_send()

    if ta:
        a_spec = pl.BlockSpec((tk, tm), lambda i, j, k: (k + ako, i + amo))
    else:
        a_spec = pl.BlockSpec((tm, tk), lambda i, j, k: (i, k + ako))
    if b3:
        assert tk == b.shape[2], (name, tk)
        b_spec = pl.BlockSpec((None, tn, tk), lambda i, j, k: (k, j, 0))
    elif tb:
        b_spec = pl.BlockSpec((tn, tk), lambda i, j, k: (j + bno, k + bko))
    else:
        b_spec = pl.BlockSpec((tk, tn), lambda i, j, k: (k + bko, j + bno))
    in_specs = [a_spec, b_spec]
    args = [a, b]
    if add is not None:
        in_specs.append(pl.BlockSpec((tm, tn), lambda i, j, k: (i, j)))
        args.append(add)
    if n_groups == 1:
        out_spec = pl.BlockSpec((tm, tn), lambda i, j, k: (i, j))
        out_shape = jax.ShapeDtypeStruct((M, N), out_dtype)
    else:
        nb = N // n_groups // tn
        sh = nb.bit_length() - 1
        assert nb == 1 << sh, (name, nb)
        out_spec = pl.BlockSpec((None, tm, tn), lambda i, j, k: (lax.shift_right_logical(j, sh), i, lax.bitwise_and(j, nb - 1)))
        out_shape = jax.ShapeDtypeStruct((n_groups, M, N // n_groups), out_dtype)
    scratch = [pltpu.VMEM((tm, tn), F32)] if nk > 1 else []
    if comm is None:
        return pl.pallas_call(
            body, name=name, grid=grid, in_specs=in_specs, out_specs=out_spec, out_shape=out_shape,
            scratch_shapes=scratch, compiler_params=_cp(("parallel", "parallel", "arbitrary")),
        )(*args)
    kind, buf = comm
    any_spec = pl.BlockSpec(memory_space=pl.ANY)
    if kind == "scatter":
        comm_shape, alias = jax.ShapeDtypeStruct((3,) + buf.shape[1:], buf.dtype), {}
    else:
        comm_shape, alias = jax.ShapeDtypeStruct(buf.shape, buf.dtype), {n_in - 1: 1}
    return pl.pallas_call(
        body, name=name, grid=grid, in_specs=in_specs + [any_spec], out_specs=[out_spec, any_spec],
        out_shape=[out_shape, comm_shape], input_output_aliases=alias,
        scratch_shapes=scratch + [pltpu.SemaphoreType.DMA((3,)), pltpu.SemaphoreType.DMA((3,))],
        compiler_params=_cp(("arbitrary", "arbitrary", "arbitrary")),
    )(*args, buf)


def rms_fwd(x, w, *, name):
    S, D = x.shape
    ts = _tile(S, TILES["row"])

    def body(x_ref, w_ref, o_ref):
        xf = x_ref[...]
        r = lax.rsqrt(jnp.mean(xf * xf, axis=-1, keepdims=True) + EPS)
        o_ref[...] = (xf * r * w_ref[...]).astype(BF16)

    return pl.pallas_call(
        body, name=name, grid=(S // ts,),
        in_specs=[pl.BlockSpec((ts, D), lambda i: (i, 0)), pl.BlockSpec((1, D), lambda i: (0, 0))],
        out_specs=pl.BlockSpec((ts, D), lambda i: (i, 0)),
        out_shape=jax.ShapeDtypeStruct((S, D), BF16), compiler_params=_cp(("parallel",)),
    )(x, w)


def rms_bwd(dh, x, w, dres, *, name):
    S, D = x.shape
    ts = _tile(S, TILES["row"])

    def body(dh_ref, x_ref, w_ref, dres_ref, dx_ref, dw_ref):
        xf = x_ref[...]
        r = lax.rsqrt(jnp.mean(xf * xf, axis=-1, keepdims=True) + EPS)
        xh = xf * r
        g = dh_ref[...]
        dxh = g * w_ref[...]
        dx_ref[...] = dres_ref[...] + r * (dxh - xh * jnp.mean(dxh * xh, axis=-1, keepdims=True))
        part = jnp.sum(g * xh, axis=0, keepdims=True)

        @pl.when(pl.program_id(0) == 0)
        def _():
            dw_ref[...] = part

        @pl.when(pl.program_id(0) > 0)
        def _():
            dw_ref[...] += part

    row = pl.BlockSpec((ts, D), lambda i: (i, 0))
    vec = pl.BlockSpec((1, D), lambda i: (0, 0))
    return pl.pallas_call(
        body, name=name, grid=(S // ts,), in_specs=[row, row, vec, row], out_specs=[row, vec],
        out_shape=[jax.ShapeDtypeStruct((S, D), F32), jax.ShapeDtypeStruct((1, D), F32)],
        compiler_params=_cp(("arbitrary",)),
    )(dh, x, w, dres)


def loss_head(x, w, target, *, name):
    S, D = x.shape
    ts = _tile(S, TILES["row"])

    def body(x_ref, w_ref, t_ref, loss_ref, dx_ref, dw_ref):
        xf = x_ref[...]
        r = lax.rsqrt(jnp.mean(xf * xf, axis=-1, keepdims=True) + EPS)
        xh = xf * r
        err = xh * w_ref[...] - t_ref[...]
        sq = jnp.sum(jnp.sum(err * err, axis=0, keepdims=True), axis=1, keepdims=True)
        dy = err * (1.0 / D)
        dxh = dy * w_ref[...]
        dx_ref[...] = r * (dxh - xh * jnp.mean(dxh * xh, axis=-1, keepdims=True))
        part = jnp.sum(dy * xh, axis=0, keepdims=True)

        @pl.when(pl.program_id(0) == 0)
        def _():
            dw_ref[...] = part
            loss_ref[...] = jnp.broadcast_to(sq, (8, LANES))

        @pl.when(pl.program_id(0) > 0)
        def _():
            dw_ref[...] += part
            loss_ref[...] += jnp.broadcast_to(sq, (8, LANES))

    row = pl.BlockSpec((ts, D), lambda i: (i, 0))
    vec = pl.BlockSpec((1, D), lambda i: (0, 0))
    return pl.pallas_call(
        body, name=name, grid=(S // ts,), in_specs=[row, vec, row],
        out_specs=[pl.BlockSpec((8, LANES), lambda i: (0, 0)), row, vec],
        out_shape=[jax.ShapeDtypeStruct((8, LANES), F32), jax.ShapeDtypeStruct((S, D), F32),
                   jax.ShapeDtypeStruct((1, D), F32)],
        compiler_params=_cp(("arbitrary",)),
    )(x, w, target)


def _shift_scan(v, buf_ref, n, reverse=False):
    buf_ref[pl.ds(0, n), :] = jnp.zeros((n, LANES), F32)
    buf_ref[pl.ds(2 * n, n), :] = jnp.zeros((n, LANES), F32)
    s = 1
    while s < n:
        buf_ref[pl.ds(n, n), :] = v
        v = v + buf_ref[pl.ds(n + s if reverse else n - s, n), :]
        s *= 2
    return v


def small_fwd(u, bias, *, name):
    S = u.shape[0]
    ts = _tile(S, TILES["row"])
    cb = OFF["small"] // LANES

    def body(u_ref, b_ref, sp_ref, c_ref, buf_ref, carry_ref):
        x = u_ref[...] + b_ref[...]
        sp_ref[...] = _softplus(x)
        lf = jnp.minimum(x, 0.0) - jnp.log(1.0 + jnp.exp(-jnp.abs(x)))

        @pl.when(pl.program_id(0) == 0)
        def _():
            carry_ref[...] = jnp.zeros((8, LANES), F32)

        c = _shift_scan(lf, buf_ref, ts) + carry_ref[pl.ds(0, 1), :]
        c_ref[...] = c
        carry_ref[...] = jnp.broadcast_to(c_ref[pl.ds(ts - 1, 1), :], (8, LANES))

    blk = pl.BlockSpec((ts, LANES), lambda i: (i, 0))
    return pl.pallas_call(
        body, name=name, grid=(S // ts,),
        in_specs=[pl.BlockSpec((ts, LANES), lambda i: (i, cb)), pl.BlockSpec((1, LANES), lambda i: (0, 0))],
        out_specs=[blk, blk], out_shape=[jax.ShapeDtypeStruct((S, LANES), F32)] * 2,
        scratch_shapes=[pltpu.VMEM((3 * ts, LANES), F32), pltpu.VMEM((8, LANES), F32)],
        compiler_params=_cp(("arbitrary",)),
    )(u, bias)


def small_bwd(dc, dsp, u, bias, *, name):
    S = u.shape[0]
    ts = _tile(S, TILES["row"])
    cb = OFF["small"] // LANES
    nt = S // ts

    def body(dc_ref, dsp_ref, u_ref, b_ref, du_ref, db_ref, buf_ref, carry_ref):
        x = u_ref[...] + b_ref[...]

        @pl.when(pl.program_id(0) == 0)
        def _():
            carry_ref[...] = jnp.zeros((8, LANES), F32)

        dlf = _shift_scan(dc_ref[...], buf_ref, ts, reverse=True) + carry_ref[pl.ds(0, 1), :]
        buf_ref[pl.ds(0, ts), :] = dlf
        carry_ref[...] = jnp.broadcast_to(buf_ref[pl.ds(0, 1), :], (8, LANES))
        sg = _sig(x)
        dx = dlf * (1.0 - sg) + dsp_ref[...] * sg
        du_ref[...] = jnp.concatenate([dx, jnp.zeros((ts, SMALL_W - LANES), F32)], axis=1).astype(BF16)
        part = jnp.sum(dx, axis=0, keepdims=True)

        @pl.when(pl.program_id(0) == 0)
        def _():
            db_ref[...] = part

        @pl.when(pl.program_id(0) > 0)
        def _():
            db_ref[...] += part

    rev = pl.BlockSpec((ts, LANES), lambda i: (nt - 1 - i, 0))
    return pl.pallas_call(
        body, name=name, grid=(nt,),
        in_specs=[rev, rev, pl.BlockSpec((ts, LANES), lambda i: (nt - 1 - i, cb)), pl.BlockSpec((1, LANES), lambda i: (0, 0))],
        out_specs=[pl.BlockSpec((ts, SMALL_W), lambda i: (nt - 1 - i, 0)), pl.BlockSpec((1, LANES), lambda i: (0, 0))],
        out_shape=[jax.ShapeDtypeStruct((S, SMALL_W), BF16), jax.ShapeDtypeStruct((1, LANES), F32)],
        scratch_shapes=[pltpu.VMEM((3 * ts, LANES), F32), pltpu.VMEM((8, LANES), F32)],
        compiler_params=_cp(("arbitrary",)),
    )(dc, dsp, u, bias)


def _conv_in(mode, a, b):
    if mode == "plain":
        return a
    if mode == "mul":
        return a * b
    return a * _sig(b)


SUBLANES = 8
MANY_TAPS = 8


def _row_taps(src_ref, sh_ref, n_rows, many):
    if many:
        for b in range(1, SUBLANES):
            sh_ref[b, pl.ds(0, n_rows - SUBLANES), :] = src_ref[pl.ds(b, n_rows - SUBLANES), :]

    def tap(off, n):
        a, b = divmod(off, SUBLANES)
        if not many or b == 0:
            return src_ref[pl.ds(off, n), :]
        return sh_ref[b, pl.ds(SUBLANES * a, n), :]

    return tap


def conv_fwd(u, w, bias, *, mode, a_off, b_off, C, name):
    S = u.shape[0]
    Kc = w.shape[0]
    ts, cb, H = _tile(S, TILES["row"]), _tile(C, TILES["conv_c"]), CONV_HALO
    two = mode != "plain"
    rb = ts // H
    many = Kc >= MANY_TAPS

    def body(*refs):
        if two:
            a_ref, ap_ref, b_ref, bp_ref, w_ref, bias_ref, o_ref, x_ref = refs[:8]
            cur, prev = _conv_in(mode, a_ref[...], b_ref[...]), _conv_in(mode, ap_ref[...], bp_ref[...])
        else:
            a_ref, ap_ref, w_ref, bias_ref, o_ref, x_ref = refs[:6]
            cur, prev = a_ref[...], ap_ref[...]
        x_ref[pl.ds(0, H), :] = jnp.where(pl.program_id(1) == 0, 0.0, prev)
        x_ref[pl.ds(H, ts), :] = cur
        tap = _row_taps(x_ref, refs[-1], ts + H, many)
        acc = jnp.broadcast_to(bias_ref[...], (ts, cb))
        for j in range(Kc):
            acc = acc + w_ref[pl.ds(j, 1), :] * tap(H - (Kc - 1) + j, ts)
        o_ref[...] = acc

    def cur_spec(off):
        return pl.BlockSpec((ts, cb), lambda c, i: (i, c + off // cb))

    def prev_spec(off):
        return pl.BlockSpec((H, cb), lambda c, i: (jnp.maximum(i * rb - 1, 0), c + off // cb))

    in_specs, args = [cur_spec(a_off), prev_spec(a_off)], [u, u]
    if two:
        in_specs += [cur_spec(b_off), prev_spec(b_off)]
        args += [u, u]
    in_specs += [pl.BlockSpec((Kc, cb), lambda c, i: (0, c)), pl.BlockSpec((1, cb), lambda c, i: (0, c))]
    return pl.pallas_call(
        body, name=name, grid=(C // cb, S // ts), in_specs=in_specs,
        out_specs=pl.BlockSpec((ts, cb), lambda c, i: (i, c)), out_shape=jax.ShapeDtypeStruct((S, C), F32),
        scratch_shapes=[pltpu.VMEM((ts + H, cb), F32), pltpu.VMEM((SUBLANES if many else 1, ts + H, cb), F32)],
        compiler_params=_cp(("parallel", "arbitrary")),
    )(*args, w, bias)


def conv_bwd(dy, u, w, *, mode, a_off, b_off, C, name):
    S = u.shape[0]
    Kc = w.shape[0]
    ts, cb, H = _tile(S, TILES["row"]), _tile(C, TILES["conv_c"]), CONV_HALO
    two = mode != "plain"
    rb, nt = ts // H, S // ts

    many = Kc >= MANY_TAPS

    def body(*refs):
        if two:
            dy_ref, dyn_ref, a_ref, ap_ref, b_ref, bp_ref, w_ref, da_ref, db_ref, dw_ref, dbias_ref, x_ref, g_ref = refs[:13]
            a, b = a_ref[...], b_ref[...]
            cur, prev = _conv_in(mode, a, b), _conv_in(mode, ap_ref[...], bp_ref[...])
        else:
            dy_ref, dyn_ref, a_ref, ap_ref, w_ref, da_ref, dw_ref, dbias_ref, x_ref, g_ref = refs[:10]
            cur, prev = a_ref[...], ap_ref[...]
        i = pl.program_id(1)
        x_ref[pl.ds(0, H), :] = jnp.where(i == 0, 0.0, prev)
        x_ref[pl.ds(H, ts), :] = cur
        g = dy_ref[...]
        g_ref[pl.ds(0, ts), :] = g
        g_ref[pl.ds(ts, H), :] = jnp.where(i == nt - 1, 0.0, dyn_ref[...])
        x_tap = _row_taps(x_ref, refs[-2], ts + H, many)
        g_tap = _row_taps(g_ref, refs[-1], ts + H, many)

        @pl.when(i == 0)
        def _():
            dw_ref[...] = jnp.zeros((Kc, cb), F32)
            dbias_ref[...] = jnp.zeros((1, cb), F32)

        dbias_ref[...] += jnp.sum(g, axis=0, keepdims=True)
        dx = jnp.zeros((ts, cb), F32)
        for j in range(Kc):
            dx = dx + w_ref[pl.ds(j, 1), :] * g_tap(Kc - 1 - j, ts)
            dw_ref[pl.ds(j, 1), :] += jnp.sum(g * x_tap(H - (Kc - 1) + j, ts), axis=0, keepdims=True)
        if mode == "plain":
            da_ref[...] = dx.astype(BF16)
        elif mode == "mul":
            da_ref[...] = (dx * b).astype(BF16)
            db_ref[...] = (dx * a).astype(BF16)
        else:
            sg = _sig(b)
            da_ref[...] = (dx * sg).astype(BF16)
            db_ref[...] = (dx * a * sg * (1.0 - sg)).astype(BF16)

    def cur_spec(off):
        return pl.BlockSpec((ts, cb), lambda c, i: (i, c + off // cb))

    def prev_spec(off):
        return pl.BlockSpec((H, cb), lambda c, i: (jnp.maximum(i * rb - 1, 0), c + off // cb))

    out_blk = pl.BlockSpec((ts, cb), lambda c, i: (i, c))
    in_specs = [out_blk, pl.BlockSpec((H, cb), lambda c, i: (jnp.minimum((i + 1) * rb, S // H - 1), c)),
                cur_spec(a_off), prev_spec(a_off)]
    args = [dy, dy, u, u]
    if two:
        in_specs += [cur_spec(b_off), prev_spec(b_off)]
        args += [u, u]
    in_specs.append(pl.BlockSpec((Kc, cb), lambda c, i: (0, c)))
    n_d = 2 if two else 1
    return pl.pallas_call(
        body, name=name, grid=(C // cb, nt), in_specs=in_specs,
        out_specs=[out_blk] * n_d + [pl.BlockSpec((Kc, cb), lambda c, i: (0, c)), pl.BlockSpec((1, cb), lambda c, i: (0, c))],
        out_shape=[jax.ShapeDtypeStruct((S, C), BF16)] * n_d + [jax.ShapeDtypeStruct((Kc, C), F32), jax.ShapeDtypeStruct((1, C), F32)],
        scratch_shapes=[pltpu.VMEM((ts + H, cb), F32), pltpu.VMEM((ts + H, cb), F32)]
        + [pltpu.VMEM((SUBLANES if many else 1, ts + H, cb), F32)] * 2,
        compiler_params=_cp(("parallel", "arbitrary")),
    )(*args, w)


def _usec(ts, name):
    return pl.BlockSpec((ts, BW), lambda i, o=OFF[name] // BW: (i, o))


def _acc_rows(ref, part):
    @pl.when(pl.program_id(0) == 0)
    def _():
        ref[...] = part

    @pl.when(pl.program_id(0) > 0)
    def _():
        ref[...] += part


def post_fwd(u, y_ssd, cv_c, cv_d, nw, lnw, lnb, *, name):
    S = u.shape[0]
    ts = _tile(S, TILES["post"])

    def body(z_ref, scb_ref, gc_ref, gd_ref, ys_ref, cc_ref, cd_ref, nw_ref, lw_ref, lb_ref, yb_ref, yc_ref, yd_ref):
        t = ys_ref[...] * _silu(z_ref[...])
        r = lax.rsqrt(jnp.mean(t * t, axis=-1, keepdims=True) + EPS)
        yb_ref[...] = (t * r * nw_ref[...]).astype(BF16)
        yc_ref[...] = (scb_ref[...] * cc_ref[...] * _silu(gc_ref[...])).astype(BF16)
        cf = cd_ref[...]
        mu = jnp.mean(cf, axis=-1, keepdims=True)
        xc = cf - mu
        rl = lax.rsqrt(jnp.mean(xc * xc, axis=-1, keepdims=True) + EPS)
        yln = xc * rl * lw_ref[...] + lb_ref[...]
        yd_ref[...] = (_silu(yln) * _silu(gd_ref[...])).astype(BF16)

    row = pl.BlockSpec((ts, BW), lambda i: (i, 0))
    vec = pl.BlockSpec((1, BW), lambda i: (0, 0))
    return pl.pallas_call(
        body, name=name, grid=(S // ts,),
        in_specs=[_usec(ts, "z"), _usec(ts, "scb"), _usec(ts, "gc"), _usec(ts, "gd"), row, row, row, vec, vec, vec],
        out_specs=[row] * 3, out_shape=[jax.ShapeDtypeStruct((S, BW), BF16)] * 3, compiler_params=_cp(("parallel",)),
    )(u, u, u, u, y_ssd, cv_c, cv_d, nw, lnw, lnb)


def post_bwd(dy_a, dy_b, dy_c, dy_d, u, o, y_ssd, cv_c, cv_d, nw, lnw, lnb, *, name):
    S = u.shape[0]
    ts = _tile(S, TILES["post"])

    def body(dya_ref, dyb_ref, dyc_ref, dyd_ref, ga_ref, z_ref, scb_ref, gc_ref, gd_ref, o_ref, ys_ref, cc_ref, cd_ref,
             nw_ref, lw_ref, lb_ref,
             do_ref, dl_ref, dga_ref, dz_ref, dscb_ref, dgc_ref, dgd_ref, dys_ref, dcc_ref, dcd_ref, dnw_ref, dlw_ref, dlb_ref):
        ga, ov, dya = ga_ref[...], o_ref[...], dya_ref[...]
        dob = (dya * _silu(ga)).astype(BF16)
        do_ref[...] = dob
        dga_ref[...] = (dya * ov * _dsilu(ga)).astype(BF16)
        prod = dob.astype(F32) * ov
        lane = lax.broadcasted_iota(jnp.int32, (ts, LANES), 1)
        delta = jnp.zeros((ts, LANES), F32)
        for h in range(FOX_HEADS):
            col = jnp.sum(prod[:, h * FOX_HD:(h + 1) * FOX_HD], axis=1, keepdims=True)
            delta = jnp.where(lane == h, col, delta)
        dl_ref[...] = delta
        ys, z, dyb = ys_ref[...], z_ref[...], dyb_ref[...]
        sz = _silu(z)
        t = ys * sz
        r = lax.rsqrt(jnp.mean(t * t, axis=-1, keepdims=True) + EPS)
        th = t * r
        dth = dyb * nw_ref[...]
        dt_ = r * (dth - th * jnp.mean(dth * th, axis=-1, keepdims=True))
        dys_ref[...] = dt_ * sz
        dz_ref[...] = (dt_ * ys * _dsilu(z)).astype(BF16)
        _acc_rows(dnw_ref, jnp.sum(dyb * th, axis=0, keepdims=True))
        scb, cc, gc, dyc = scb_ref[...], cc_ref[...], gc_ref[...], dyc_ref[...]
        sg = _silu(gc)
        dscb_ref[...] = (dyc * cc * sg).astype(BF16)
        dcc_ref[...] = dyc * scb * sg
        dgc_ref[...] = (dyc * scb * cc * _dsilu(gc)).astype(BF16)
        cf, gd, dyd = cd_ref[...], gd_ref[...], dyd_ref[...]
        mu = jnp.mean(cf, axis=-1, keepdims=True)
        xc = cf - mu
        rl = lax.rsqrt(jnp.mean(xc * xc, axis=-1, keepdims=True) + EPS)
        xh = xc * rl
        yln = xh * lw_ref[...] + lb_ref[...]
        dyln = dyd * _silu(gd) * _dsilu(yln)
        dgd_ref[...] = (dyd * _silu(yln) * _dsilu(gd)).astype(BF16)
        _acc_rows(dlw_ref, jnp.sum(dyln * xh, axis=0, keepdims=True))
        _acc_rows(dlb_ref, jnp.sum(dyln, axis=0, keepdims=True))
        dxh = dyln * lw_ref[...]
        dcd_ref[...] = rl * (dxh - jnp.mean(dxh, axis=-1, keepdims=True) - xh * jnp.mean(dxh * xh, axis=-1, keepdims=True))

    row = pl.BlockSpec((ts, BW), lambda i: (i, 0))
    vec = pl.BlockSpec((1, BW), lambda i: (0, 0))
    sd = jax.ShapeDtypeStruct
    return pl.pallas_call(
        body, name=name, grid=(S // ts,),
        in_specs=[row] * 4 + [_usec(ts, n) for n in ("ga", "z", "scb", "gc", "gd")] + [row] * 4 + [vec] * 3,
        out_specs=[row, pl.BlockSpec((ts, LANES), lambda i: (i, 0))] + [row] * 8 + [vec] * 3,
        out_shape=[sd((S, BW), BF16), sd((S, LANES), F32)] + [sd((S, BW), BF16)] * 5 + [sd((S, BW), F32)] * 3 + [sd((1, BW), F32)] * 3,
        compiler_params=_cp(("arbitrary",)),
    )(dy_a, dy_b, dy_c, dy_d, u, u, u, u, u, o, y_ssd, cv_c, cv_d, nw, lnw, lnb)


def merge_fwd(h, ys, wg, bg, wb, *, name):
    S, D = h.shape
    tm, tn = _tile(S, 1024), _tile(D, 512)
    nb = D // tn

    def body(h_ref, y0, y1, y2, y3, wg_ref, bg_ref, wb_ref, m_ref, g_ref, p_ref, acc_ref):
        i = pl.program_id(2)
        g = jnp.dot(h_ref[...], wg_ref[...], preferred_element_type=F32) + bg_ref[...]
        gate = _sig(g)
        for b, y_ref in enumerate((y0, y1, y2, y3)):
            @pl.when(i == b)
            def _(y_ref=y_ref):
                p = jnp.dot(y_ref[...], wb_ref[...], preferred_element_type=F32)
                g_ref[...] = gate.astype(BF16)
                p_ref[...] = p.astype(BF16)
                if b == 0:
                    acc_ref[...] = gate * p
                else:
                    acc_ref[...] += gate * p

        @pl.when(i == N_BRANCH - 1)
        def _():
            m_ref[...] = acc_ref[...].astype(BF16)

    yspec = pl.BlockSpec((tm, BW), lambda m, n, i: (m, 0))
    return pl.pallas_call(
        body, name=name, grid=(S // tm, nb, N_BRANCH),
        in_specs=[pl.BlockSpec((tm, D), lambda m, n, i: (m, 0)), yspec, yspec, yspec, yspec,
                  pl.BlockSpec((None, D, tn), lambda m, n, i: (i, 0, n)),
                  pl.BlockSpec((None, 1, tn), lambda m, n, i: (i, 0, n)),
                  pl.BlockSpec((None, BW, tn), lambda m, n, i: (i, 0, n))],
        out_specs=[pl.BlockSpec((tm, tn), lambda m, n, i: (m, n)),
                   pl.BlockSpec((tm, tn), lambda m, n, i: (m, i * nb + n)),
                   pl.BlockSpec((tm, tn), lambda m, n, i: (m, i * nb + n))],
        out_shape=[jax.ShapeDtypeStruct((S, D), BF16), jax.ShapeDtypeStruct((S, N_BRANCH * D), BF16),
                   jax.ShapeDtypeStruct((S, N_BRANCH * D), BF16)],
        scratch_shapes=[pltpu.VMEM((tm, tn), F32)],
        compiler_params=_cp(("parallel", "parallel", "arbitrary")),
    )(h, *ys, wg, bg, wb)


def merge_bwd(dm, gates, proj, *, name):
    S, D = dm.shape
    ts, tn = _tile(S, TILES["row"]), _tile(D, 512)
    nb = D // tn

    def body(dm_ref, g_ref, p_ref, dp_ref, dg_ref, db_ref):
        i, b = pl.program_id(1), pl.program_id(2)
        d = dm_ref[...]
        g = g_ref[...].astype(F32)
        dp_ref[...] = (d * g).astype(BF16)
        dg = d * p_ref[...].astype(F32) * g * (1.0 - g)
        dg_ref[...] = dg.astype(BF16)
        part = jnp.sum(dg, axis=0, keepdims=True)

        @pl.when((i == 0) & (b == 0))
        def _():
            db_ref[...] = jnp.zeros((N_BRANCH, tn), F32)

        row = lax.broadcasted_iota(jnp.int32, (N_BRANCH, tn), 0)
        db_ref[...] += jnp.where(row == b, part, 0.0)

    wide = pl.BlockSpec((ts, tn), lambda n, i, b: (i, b * nb + n))
    return pl.pallas_call(
        body, name=name, grid=(nb, S // ts, N_BRANCH),
        in_specs=[pl.BlockSpec((ts, tn), lambda n, i, b: (i, n)), wide, wide],
        out_specs=[wide, wide, pl.BlockSpec((N_BRANCH, tn), lambda n, i, b: (0, n))],
        out_shape=[jax.ShapeDtypeStruct((S, N_BRANCH * D), BF16)] * 2 + [jax.ShapeDtypeStruct((N_BRANCH, D), F32)],
        compiler_params=_cp(("parallel", "arbitrary", "arbitrary")),
    )(dm, gates, proj)


_NT = (((1,), (1,)), ((), ()))
_TN = (((0,), (0,)), ((), ()))


def _tri_pairs(n, by_key):
    if by_key:
        pairs = [(i, j) for j in range(n) for i in range(j, n)]
    else:
        pairs = [(i, j) for i in range(n) for j in range(i + 1)]
    return (jnp.array([p[0] for p in pairs], jnp.int32), jnp.array([p[1] for p in pairs], jnp.int32))


def attn_fwd(u, c_row, *, name):
    S = u.shape[0]
    T = _tile(S, TILES["att"])
    n = S // T
    qo, ko, vo, go = (OFF[k] // FOX_HD for k in ("q", "k", "v", "ga"))
    scale = FOX_HD ** -0.5
    it, jt = _tri_pairs(n, by_key=False)

    def body(it_ref, jt_ref, q_ref, k_ref, v_ref, ck_ref, ga_ref, o_ref, lse_ref, ya_ref, m_ref, l_ref, acc_ref):
        i, j = it_ref[pl.program_id(1)], jt_ref[pl.program_id(1)]

        @pl.when(j == 0)
        def _():
            m_ref[...] = jnp.full((T, 1), NEG, F32)
            l_ref[...] = jnp.zeros((T, 1), F32)
            acc_ref[...] = jnp.zeros((T, FOX_HD), F32)

        def step(masked):
            for kc in range(ATT_KV_CHUNKS if T >= 512 else 1):
                chunk(masked, kc * (T // (ATT_KV_CHUNKS if T >= 512 else 1)), T // (ATT_KV_CHUNKS if T >= 512 else 1))

        def chunk(masked, k0, W):
            qb = (q_ref[...] * (scale * LOG2E)).astype(BF16)
            s = lax.dot_general(qb, k_ref[pl.ds(k0, W), :].astype(BF16), _NT, preferred_element_type=F32) - ck_ref[:, pl.ds(k0, W)]
            if masked:
                row = lax.broadcasted_iota(jnp.int32, (T, W), 0)
                col = lax.broadcasted_iota(jnp.int32, (T, W), 1) + k0
                s = jnp.where(col <= row, s, NEG)
            m_old = m_ref[...]
            m_new = jnp.maximum(m_old, jnp.max(s, axis=1, keepdims=True))
            alpha = jnp.exp2(m_old - m_new)
            p = jnp.exp2(s - m_new)
            l_ref[...] = alpha * l_ref[...] + jnp.sum(p, axis=1, keepdims=True)
            p_hi = p.astype(BF16)
            p_lo = (p - p_hi.astype(F32)).astype(BF16)
            vb = v_ref[pl.ds(k0, W), :].astype(BF16)
            pv = jnp.dot(p_hi, vb, preferred_element_type=F32) + jnp.dot(p_lo, vb, preferred_element_type=F32)
            acc_ref[...] = alpha * acc_ref[...] + pv
            m_ref[...] = m_new

        @pl.when(j < i)
        def _():
            step(False)

        @pl.when(j == i)
        def _():
            step(True)
            o = acc_ref[...] / l_ref[...]
            o_ref[...] = o
            lse_ref[...] = m_ref[...] + jnp.log(l_ref[...]) * LOG2E
            ya_ref[...] = (o * _silu(ga_ref[...])).astype(BF16)

    def qsec(off):
        return pl.BlockSpec((T, FOX_HD), lambda h, p, it, jt: (it[p], off + h))

    def ksec(off):
        return pl.BlockSpec((T, FOX_HD), lambda h, p, it, jt: (jt[p], off + h))

    out = pl.BlockSpec((T, FOX_HD), lambda h, p, it, jt: (it[p], h))
    colv = pl.BlockSpec((None, T, 1), lambda h, p, it, jt: (h, it[p], 0))
    return pl.pallas_call(
        body, name=name,
        grid_spec=pltpu.PrefetchScalarGridSpec(
            num_scalar_prefetch=2, grid=(FOX_HEADS, n * (n + 1) // 2),
            in_specs=[qsec(qo), ksec(ko), ksec(vo), pl.BlockSpec((None, 1, T), lambda h, p, it, jt: (h, 0, jt[p])), qsec(go)],
            out_specs=[out, colv, out],
            scratch_shapes=[pltpu.VMEM((T, 1), F32), pltpu.VMEM((T, 1), F32), pltpu.VMEM((T, FOX_HD), F32)]),
        out_shape=[jax.ShapeDtypeStruct((S, BW), F32), jax.ShapeDtypeStruct((FOX_HEADS, S, 1), F32),
                   jax.ShapeDtypeStruct((S, BW), BF16)],
        compiler_params=_cp(("parallel", "arbitrary")),
    )(it, jt, u, u, u, c_row, u)


def attn_bwd(u, do, c_col, lse_row, delta_row, *, name):
    S = u.shape[0]
    T = _tile(S, TILES["att"])
    n = S // T
    qo, ko, vo = (OFF[k] // FOX_HD for k in ("q", "k", "v"))
    scale = FOX_HD ** -0.5
    it, jt = _tri_pairs(n, by_key=True)

    def body(it_ref, jt_ref, q_ref, k_ref, v_ref, do_ref, ck_ref, lse_ref, dl_ref, dq_ref, dk_ref, dv_ref, dc_ref,
             dka_ref, dva_ref, dca_ref):
        i, j = it_ref[pl.program_id(1)], jt_ref[pl.program_id(1)]

        @pl.when(pl.program_id(1) == 0)
        def _():
            dq_ref[...] = jnp.zeros((S, FOX_HD), F32)

        @pl.when(i == j)
        def _():
            dka_ref[...] = jnp.zeros((T, FOX_HD), F32)
            dva_ref[...] = jnp.zeros((T, FOX_HD), F32)
            dca_ref[...] = jnp.zeros((T, 1), F32)

        def step(masked):
            qf, dob = q_ref[...], do_ref[...]
            qb = (qf * scale).astype(BF16)
            kf = k_ref[...]
            st = lax.dot_general(kf.astype(BF16), (qf * (scale * LOG2E)).astype(BF16), _NT, preferred_element_type=F32)
            pt = jnp.exp2(st - ck_ref[...] - lse_ref[...])
            if masked:
                kpos = lax.broadcasted_iota(jnp.int32, (T, T), 0)
                qpos = lax.broadcasted_iota(jnp.int32, (T, T), 1)
                pt = jnp.where(kpos <= qpos, pt, 0.0)
            dva_ref[...] += jnp.dot(pt.astype(BF16), dob, preferred_element_type=F32)
            dpt = lax.dot_general(v_ref[...].astype(BF16), dob, _NT, preferred_element_type=F32)
            dst = pt * (dpt - dl_ref[...])
            dca_ref[...] -= jnp.sum(dst, axis=1, keepdims=True)
            dsb = dst.astype(BF16)
            dka_ref[...] += jnp.dot(dsb, qb, preferred_element_type=F32)
            rows = pl.ds(pl.multiple_of(i * T, T), T)
            dq_ref[rows, :] += lax.dot_general(dsb, (kf * scale).astype(BF16), _TN, preferred_element_type=F32)

        @pl.when(i > j)
        def _():
            step(False)

        @pl.when(i == j)
        def _():
            step(True)

        @pl.when(i == n - 1)
        def _():
            dk_ref[...] = dka_ref[...].astype(BF16)
            dv_ref[...] = dva_ref[...].astype(BF16)
            dc_ref[...] = dca_ref[...]

    def qsec(off):
        return pl.BlockSpec((T, FOX_HD), lambda h, p, it, jt: (it[p], off + h))

    def ksec(off):
        return pl.BlockSpec((T, FOX_HD), lambda h, p, it, jt: (jt[p], off + h))

    qrow = pl.BlockSpec((None, 1, T), lambda h, p, it, jt: (h, 0, it[p]))
    kout = pl.BlockSpec((T, FOX_HD), lambda h, p, it, jt: (jt[p], h))
    kcol = pl.BlockSpec((None, T, 1), lambda h, p, it, jt: (h, jt[p], 0))
    return pl.pallas_call(
        body, name=name,
        grid_spec=pltpu.PrefetchScalarGridSpec(
            num_scalar_prefetch=2, grid=(FOX_HEADS, n * (n + 1) // 2),
            in_specs=[qsec(qo), ksec(ko), ksec(vo), pl.BlockSpec((T, FOX_HD), lambda h, p, it, jt: (it[p], h)),
                      kcol, qrow, qrow],
            out_specs=[pl.BlockSpec((S, FOX_HD), lambda h, p, it, jt: (0, h)), kout, kout, kcol],
            scratch_shapes=[pltpu.VMEM((T, FOX_HD), F32), pltpu.VMEM((T, FOX_HD), F32), pltpu.VMEM((T, 1), F32)]),
        out_shape=[jax.ShapeDtypeStruct((S, BW), F32), jax.ShapeDtypeStruct((S, BW), BF16),
                   jax.ShapeDtypeStruct((S, BW), BF16), jax.ShapeDtypeStruct((FOX_HEADS, S, 1), F32)],
        compiler_params=_cp(("parallel", "arbitrary")),
    )(it, jt, u, u, u, do, c_col, lse_row, delta_row)


N_PAIR = SSM_HEADS // 2
PAIRS_PER_GROUP = N_PAIR // SSM_G


def _sel_t():
    r = lax.broadcasted_iota(jnp.int32, (LANES, BW), 0)
    c = lax.broadcasted_iota(jnp.int32, (LANES, BW), 1)
    return (lax.shift_right_logical(c, SSM_P.bit_length() - 1) == r).astype(BF16)


def _sel():
    r = lax.broadcasted_iota(jnp.int32, (BW, LANES), 0)
    c = lax.broadcasted_iota(jnp.int32, (BW, LANES), 1)
    return (lax.shift_right_logical(r, SSM_P.bit_length() - 1) == c).astype(BF16)


def _dot3(x, m):
    hi = x.astype(BF16)
    r1 = x - hi.astype(F32)
    mid = r1.astype(BF16)
    lo = (r1 - mid.astype(F32)).astype(BF16)
    d = functools.partial(jnp.dot, preferred_element_type=F32)
    return d(hi, m) + d(mid, m) + d(lo, m)


def _ssd_common(x_ref, sp_ref, al_ref, buf_ref, big_ref, cst_ref, LC):
    pre = x_ref[...]
    sg = _sig(pre)
    act = pre * sg
    dt = sp_ref[...]
    a = -jnp.exp(al_ref[...])
    cs = _shift_scan(dt * a, buf_ref, LC)
    sel_t = _sel_t()
    dtl = _dot3(dt, sel_t)
    csl = _dot3(cs, sel_t)
    big_ref[...] = csl
    csl_last = big_ref[pl.ds(LC - 1, 1), :]
    cst_ref[...] = cs.T
    return pre, sg, act, dt, a, cs, dtl, csl, csl_last


def ssd_fwd(xbc, sp, alog, dskip_l, *, name):
    S = xbc.shape[0]
    LC = _tile(S, TILES["ssd"])
    nc = S // LC

    def body(x_ref, sp_ref, al_ref, dk_ref, y_ref, hs_ref, st_ref, buf_ref, big_ref, cst_ref):
        @pl.when(pl.program_id(0) == 0)
        def _():
            st_ref[...] = jnp.zeros((N_PAIR, SSM_N, LANES), F32)

        pre, sg, act, dt, a, cs, dtl, csl, csl_last = _ssd_common(x_ref, sp_ref, al_ref, buf_ref, big_ref, cst_ref, LC)
        xs, bm, cm = act[:, :BW], act[:, BW:BW + SSM_G * SSM_N], act[:, BW + SSM_G * SSM_N:]
        e_all = jnp.exp(csl)
        dec = jnp.exp(csl_last - csl)
        ad = jnp.exp(csl_last)
        xd = xs * dtl
        tril = lax.broadcasted_iota(jnp.int32, (LC, LC), 0) >= lax.broadcasted_iota(jnp.int32, (LC, LC), 1)
        lane = lax.broadcasted_iota(jnp.int32, (LC, LANES), 1)
        for g in range(SSM_G):
            bgt = bm[:, g * SSM_N:(g + 1) * SSM_N].T.astype(BF16)
            cgb = cm[:, g * SSM_N:(g + 1) * SSM_N].astype(BF16)
            cb = jnp.dot(cgb, bgt, preferred_element_type=F32)
            for q in range(PAIRS_PER_GROUP):
                pp = g * PAIRS_PER_GROUP + q
                ln = slice(pp * LANES, (pp + 1) * LANES)
                xp = xd[:, ln]
                xpb = xp.astype(BF16)
                yh = []
                for hh in range(2):
                    row_b = jnp.broadcast_to(cst_ref[pl.ds(2 * pp + hh, 1), :], (LC, LC))
                    lmat = jnp.exp(jnp.where(tril, row_b.T - row_b, NEG))
                    yh.append(jnp.dot((cb * lmat).astype(BF16), xpb, preferred_element_type=F32))
                hin = st_ref[pp]
                hs_ref[pp] = hin
                yoff = jnp.dot(cgb, hin.astype(BF16), preferred_element_type=F32) * e_all[:, ln]
                y_ref[:, ln] = jnp.where(lane < SSM_P, yh[0], yh[1]) + yoff + xs[:, ln] * dk_ref[:, ln]
                st_ref[pp] = hin * ad[:, ln] + jnp.dot(bgt, (xp * dec[:, ln]).astype(BF16), preferred_element_type=F32)

    return pl.pallas_call(
        body, name=name, grid=(nc,),
        in_specs=[pl.BlockSpec((LC, SSM_CONV_DIM), lambda c: (c, 0)), pl.BlockSpec((LC, LANES), lambda c: (c, 0)),
                  pl.BlockSpec((1, LANES), lambda c: (0, 0)), pl.BlockSpec((1, BW), lambda c: (0, 0))],
        out_specs=[pl.BlockSpec((LC, BW), lambda c: (c, 0)), pl.BlockSpec((None, N_PAIR, SSM_N, LANES), lambda c: (c, 0, 0, 0))],
        out_shape=[jax.ShapeDtypeStruct((S, BW), F32), jax.ShapeDtypeStruct((nc, N_PAIR, SSM_N, LANES), F32)],
        scratch_shapes=[pltpu.VMEM((N_PAIR, SSM_N, LANES), F32), pltpu.VMEM((3 * LC, LANES), F32),
                        pltpu.VMEM((LC, BW), F32), pltpu.VMEM((LANES, LC), F32)],
        compiler_params=_cp(("arbitrary",)),
    )(xbc, sp, alog, dskip_l)


def ssd_bwd(dy, xbc, sp, alog, dskip_l, hs, *, name):
    S = xbc.shape[0]
    LC = _tile(S, TILES["ssd"])
    nc = S // LC
    GN = SSM_G * SSM_N

    def body(dy_ref, x_ref, sp_ref, al_ref, dk_ref, hs_ref, dx_ref, ddt_ref, da_ref, dd_ref,
             dh_ref, buf_ref, big_ref, cst_ref, gcs_ref, dxd_ref):
        @pl.when(pl.program_id(0) == 0)
        def _():
            dh_ref[...] = jnp.zeros((N_PAIR, SSM_N, LANES), F32)
            da_ref[...] = jnp.zeros((1, LANES), F32)
            dd_ref[...] = jnp.zeros((1, BW), F32)

        pre, sg, act, dt, a, cs, dtl, csl, csl_last = _ssd_common(x_ref, sp_ref, al_ref, buf_ref, big_ref, cst_ref, LC)
        dact = sg * (1.0 + pre * (1.0 - sg))
        xs, bm, cm = act[:, :BW], act[:, BW:BW + GN], act[:, BW + GN:]
        e_all = jnp.exp(csl)
        dec = jnp.exp(csl_last - csl)
        ad = jnp.exp(csl_last)
        xd = xs * dtl
        d_y = dy_ref[...]
        ri = lax.broadcasted_iota(jnp.int32, (LC, LC), 0)
        ci = lax.broadcasted_iota(jnp.int32, (LC, LC), 1)
        tril, triu = ri >= ci, ci >= ri
        lane = lax.broadcasted_iota(jnp.int32, (LC, LANES), 1)
        rowi = lax.broadcasted_iota(jnp.int32, (LC, LANES), 0)
        dot = functools.partial(jnp.dot, preferred_element_type=F32)
        dot_nt = functools.partial(lax.dot_general, dimension_numbers=_NT, preferred_element_type=F32)
        for g in range(SSM_G):
            gs = slice(g * SSM_N, (g + 1) * SSM_N)
            bg, cg = bm[:, gs], cm[:, gs]
            bgb, cgb = bg.astype(BF16), cg.astype(BF16)
            bgt, cgt = bg.T.astype(BF16), cg.T.astype(BF16)
            cb, cbt = dot(cgb, bgt), dot(bgb, cgt)
            dcb = jnp.zeros((LC, LC), F32)
            dcbt = jnp.zeros((LC, LC), F32)
            dcg = jnp.zeros((LC, SSM_N), F32)
            dbg = jnp.zeros((LC, SSM_N), F32)
            for q in range(PAIRS_PER_GROUP):
                pp = g * PAIRS_PER_GROUP + q
                ln = slice(pp * LANES, (pp + 1) * LANES)
                xp, dyp, ep, decp, adp = xd[:, ln], d_y[:, ln], e_all[:, ln], dec[:, ln], ad[:, ln]
                xpb, dypb = xp.astype(BF16), dyp.astype(BF16)
                hin, dho = hs_ref[pp], dh_ref[pp]
                hb, dhob = hin.astype(BF16), dho.astype(BF16)
                yoff = dot(cgb, hb) * ep
                dgb = (dyp * ep).astype(BF16)
                dh_ref[pp] = dho * adp + dot(cgt, dgb)
                dcg = dcg + dot_nt(dgb, hb)
                zf = xp * decp
                d_z = dot(bgb, dhob)
                dbg = dbg + dot_nt(zf.astype(BF16), dhob)
                dzz = d_z * zf
                last = jnp.sum(dzz, axis=0, keepdims=True) + jnp.sum(dho * hin, axis=0, keepdims=True) * adp
                gcs = dyp * yoff - dzz + jnp.where(rowi == LC - 1, last, 0.0)
                dxd = d_z * decp
                for hh in range(2):
                    row_b = jnp.broadcast_to(cst_ref[pl.ds(2 * pp + hh, 1), :], (LC, LC))
                    col_b = row_b.T
                    lmat = jnp.exp(jnp.where(tril, col_b - row_b, NEG))
                    lmat_t = jnp.exp(jnp.where(triu, row_b - col_b, NEG))
                    hm = (lane < SSM_P) if hh == 0 else (lane >= SSM_P)
                    dml = dot_nt(jnp.where(hm, dyp, 0.0).astype(BF16), xpb) * lmat
                    dmtl = dot_nt(jnp.where(hm, xp, 0.0).astype(BF16), dypb) * lmat_t
                    dcb = dcb + dml
                    dcbt = dcbt + dmtl
                    contrib = jnp.sum(dml * cb, axis=1, keepdims=True) - jnp.sum(dmtl * cbt, axis=1, keepdims=True)
                    gcs = gcs + jnp.where(lane == hh * SSM_P, contrib, 0.0)
                    dxd = dxd + jnp.where(hm, dot((cbt * lmat_t).astype(BF16), dypb), 0.0)
                gcs_ref[:, ln] = gcs
                dxd_ref[:, ln] = dxd
            dcg = dcg + dot(dcb.astype(BF16), bgb)
            dbg = dbg + dot(dcbt.astype(BF16), cgb)
            dx_ref[:, BW + g * SSM_N:BW + (g + 1) * SSM_N] = dbg * dact[:, BW + g * SSM_N:BW + (g + 1) * SSM_N]
            dx_ref[:, BW + GN + g * SSM_N:BW + GN + (g + 1) * SSM_N] = dcg * dact[:, BW + GN + g * SSM_N:BW + GN + (g + 1) * SSM_N]
        d_xd = dxd_ref[...]
        dx_ref[:, :BW] = (d_y * dk_ref[...] + d_xd * dtl) * dact[:, :BW]
        sel = _sel()
        dda = _shift_scan(_dot3(gcs_ref[...], sel), buf_ref, LC, reverse=True)
        ddt_ref[...] = _dot3(d_xd * xs, sel) + dda * a
        da_ref[...] += jnp.sum(dda * dt, axis=0, keepdims=True)
        dd_ref[...] += jnp.sum(d_y * xs, axis=0, keepdims=True)

    rev = lambda c: (nc - 1 - c, 0)
    return pl.pallas_call(
        body, name=name, grid=(nc,),
        in_specs=[pl.BlockSpec((LC, BW), rev), pl.BlockSpec((LC, SSM_CONV_DIM), rev), pl.BlockSpec((LC, LANES), rev),
                  pl.BlockSpec((1, LANES), lambda c: (0, 0)), pl.BlockSpec((1, BW), lambda c: (0, 0)),
                  pl.BlockSpec((None, N_PAIR, SSM_N, LANES), lambda c: (nc - 1 - c, 0, 0, 0))],
        out_specs=[pl.BlockSpec((LC, SSM_CONV_DIM), rev), pl.BlockSpec((LC, LANES), rev),
                   pl.BlockSpec((1, LANES), lambda c: (0, 0)), pl.BlockSpec((1, BW), lambda c: (0, 0))],
        out_shape=[jax.ShapeDtypeStruct((S, SSM_CONV_DIM), F32), jax.ShapeDtypeStruct((S, LANES), F32),
                   jax.ShapeDtypeStruct((1, LANES), F32), jax.ShapeDtypeStruct((1, BW), F32)],
        scratch_shapes=[pltpu.VMEM((N_PAIR, SSM_N, LANES), F32), pltpu.VMEM((3 * LC, LANES), F32),
                        pltpu.VMEM((LC, BW), F32), pltpu.VMEM((LANES, LC), F32),
                        pltpu.VMEM((LC, BW), F32), pltpu.VMEM((LC, BW), F32)],
        compiler_params=_cp(("arbitrary",)),
    )(dy, xbc, sp, alog, dskip_l, hs)


def addn(arrs, *, out_dtype, name):
    R, C = arrs[0].shape
    tr = _tile(R, max(8, (1 << 20) // C))

    def body(*refs):
        acc = refs[0][...].astype(F32)
        for r in refs[1:-1]:
            acc = acc + r[...].astype(F32)
        refs[-1][...] = acc.astype(out_dtype)

    blk = pl.BlockSpec((tr, C), lambda i: (i, 0))
    return pl.pallas_call(
        body, name=name, grid=(R // tr,), in_specs=[blk] * len(arrs), out_specs=blk,
        out_shape=jax.ShapeDtypeStruct((R, C), out_dtype), compiler_params=_cp(("parallel",)),
    )(*arrs)


def adamw(w, g, m, v, *, name):
    R, C = w.shape
    tr = _tile(R, max(8, (1 << 18) // C))
    c1, c2 = 1.0 / (1.0 - ADAM_B1 ** ADAM_STEP), 1.0 / (1.0 - ADAM_B2 ** ADAM_STEP)

    def body(w_ref, g_ref, m_ref, v_ref, d_ref, nm_ref, nv_ref):
        gv = g_ref[...]
        nm = ADAM_B1 * m_ref[...] + (1.0 - ADAM_B1) * gv
        nv = ADAM_B2 * v_ref[...] + (1.0 - ADAM_B2) * (gv * gv)
        nm_ref[...] = nm
        nv_ref[...] = nv
        d_ref[...] = -ADAM_LR * ((nm * c1) / (jnp.sqrt(nv * c2) + ADAM_EPS) + ADAM_WD * w_ref[...])

    blk = pl.BlockSpec((tr, C), lambda i: (i, 0))
    return pl.pallas_call(
        body, name=name, grid=(R // tr,), in_specs=[blk] * 4, out_specs=[blk] * 3,
        out_shape=[jax.ShapeDtypeStruct((R, C), F32)] * 3, compiler_params=_cp(("parallel",)),
    )(w, g, m, v)


_ANY = pl.BlockSpec(memory_space=pl.ANY)


def _place():
    return lax.axis_index("x"), lax.axis_index("y"), lax.axis_index("c")


def _rcopy(src, dst, sems_s, sems_r, k, to):
    return pltpu.make_async_remote_copy(src_ref=src, dst_ref=dst, send_sem=sems_s.at[k], recv_sem=sems_r.at[k],
                                        device_id=to, device_id_type=MESH)


def gather_chips(buf, *, name):
    _, _, R, C = buf.shape

    def body(in_ref, out_ref, ss, rs):
        del in_ref
        x, y, c = _place()
        chips = [(1 - x, y), (x, 1 - y), (1 - x, 1 - y)]
        me = 2 * x + y
        first = [_rcopy(out_ref.at[me, c], out_ref.at[me, c], ss, rs, j, (cx, cy, c)) for j, (cx, cy) in enumerate(chips)]
        for cp in first:
            cp.start()
        passed = []
        for j, (cx, cy) in enumerate(chips):
            blk = out_ref.at[2 * cx + cy, c]
            _rcopy(blk, blk, ss, rs, j, (x, y, c)).wait_recv()
            cp = _rcopy(blk, blk, ss, rs, 3 + j, (x, y, 1 - c))
            cp.start()
            passed.append(cp)
        for j, (cx, cy) in enumerate(chips):
            blk = out_ref.at[2 * cx + cy, 1 - c]
            _rcopy(blk, blk, ss, rs, 3 + j, (x, y, c)).wait_recv()
        for cp in first + passed:
            cp.wait_send()

    return pl.pallas_call(
        body, name=name, in_specs=[_ANY], out_specs=_ANY, out_shape=jax.ShapeDtypeStruct(buf.shape, buf.dtype),
        input_output_aliases={0: 0},
        scratch_shapes=[pltpu.SemaphoreType.DMA((6,)), pltpu.SemaphoreType.DMA((6,))],
    )(buf)


def gather_all(block, *, name):
    R, C = block.shape
    flips = [(fx, fy, fc) for fx in (0, 1) for fy in (0, 1) for fc in (0, 1)][1:]

    def body(in_ref, out_ref, ss, rs, ls):
        x, y, c = _place()
        me = 4 * x + 2 * y + c
        mine = pltpu.make_async_copy(in_ref, out_ref.at[me], ls)
        mine.start()
        sends = []
        for k, (fx, fy, fc) in enumerate(flips):
            px, py, pc = x ^ fx, y ^ fy, c ^ fc
            cp = _rcopy(in_ref, out_ref.at[me], ss, rs, k, (px, py, pc))
            cp.start()
            sends.append(cp)
        for k, (fx, fy, fc) in enumerate(flips):
            blk = out_ref.at[4 * (x ^ fx) + 2 * (y ^ fy) + (c ^ fc)]
            _rcopy(blk, blk, ss, rs, k, (x, y, c)).wait_recv()
        for cp in sends:
            cp.wait_send()
        mine.wait()

    return pl.pallas_call(
        body, name=name, in_specs=[_ANY], out_specs=_ANY, out_shape=jax.ShapeDtypeStruct((8, R, C), block.dtype),
        scratch_shapes=[pltpu.SemaphoreType.DMA((7,)), pltpu.SemaphoreType.DMA((7,)), pltpu.SemaphoreType.DMA],
    )(block)


def swap_partials(g_all, *, name):
    N, _, R, C = g_all.shape

    def body(in_ref, out_ref, ss, rs):
        x, y, c = _place()
        cps = [_rcopy(in_ref.at[k, 1 - c], out_ref.at[k], ss, rs, k, (x, y, 1 - c)) for k in range(N)]
        for cp in cps:
            cp.start()
        for cp in cps:
            cp.wait_recv()
        for cp in cps:
            cp.wait_send()

    return pl.pallas_call(
        body, name=name, in_specs=[_ANY], out_specs=_ANY, out_shape=jax.ShapeDtypeStruct((N, R, C), g_all.dtype),
        scratch_shapes=[pltpu.SemaphoreType.DMA((N,)), pltpu.SemaphoreType.DMA((N,))],
    )(g_all)


def share_halves(buf, *, name):
    def body(in_ref, out_ref, ss, rs):
        del in_ref
        x, y, c = _place()
        cp = _rcopy(out_ref.at[c], out_ref.at[c], ss, rs, 0, (x, y, 1 - c))
        cp.start()
        _rcopy(out_ref.at[1 - c], out_ref.at[1 - c], ss, rs, 0, (x, y, c)).wait_recv()
        cp.wait_send()

    return pl.pallas_call(
        body, name=name, in_specs=[_ANY], out_specs=_ANY, out_shape=jax.ShapeDtypeStruct(buf.shape, buf.dtype),
        input_output_aliases={0: 0},
        scratch_shapes=[pltpu.SemaphoreType.DMA((1,)), pltpu.SemaphoreType.DMA((1,))],
    )(buf)


def add_sibling(g_all, sib, c_arr, *, name):
    N, _, R, C = g_all.shape
    tr = _tile(R, max(8, (1 << 20) // C))

    def body(c_ref, a_ref, b_ref, o_ref):
        o_ref[...] = (a_ref[...].astype(F32) + b_ref[...].astype(F32)).astype(BF16)

    blk = pl.BlockSpec((None, tr, C), lambda k, r, c_ref: (k, r, 0))
    return pl.pallas_call(
        body, name=name,
        grid_spec=pltpu.PrefetchScalarGridSpec(
            num_scalar_prefetch=1, grid=(N, R // tr),
            in_specs=[pl.BlockSpec((None, None, tr, C), lambda k, r, c_ref: (k, c_ref[0], r, 0)), blk], out_specs=blk),
        out_shape=jax.ShapeDtypeStruct((N, R, C), BF16), compiler_params=_cp(("parallel", "parallel")),
    )(c_arr, g_all, sib)


def add_chips(part, rcv, place_arr, *, name):
    _, R, C = part.shape
    tr = _tile(R, max(8, (1 << 20) // C))

    def body(place_ref, p_ref, r0, r1, r2, o_ref):
        acc = p_ref[...].astype(F32)
        for r in (r0, r1, r2):
            acc = acc + r[...].astype(F32)
        o_ref[...] = acc

    def slot(j):
        return pl.BlockSpec((None, tr, C), lambda r, place_ref, j=j: (j, r, 0))

    return pl.pallas_call(
        body, name=name,
        grid_spec=pltpu.PrefetchScalarGridSpec(
            num_scalar_prefetch=1, grid=(R // tr,),
            in_specs=[pl.BlockSpec((None, tr, C), lambda r, place_ref: (place_ref[0], r, 0)), slot(0), slot(1), slot(2)],
            out_specs=pl.BlockSpec((None, tr, C), lambda r, place_ref: (place_ref[1], r, 0))),
        out_shape=jax.ShapeDtypeStruct((2, R, C), F32), compiler_params=_cp(("parallel",)),
    )(place_arr, part, rcv, rcv, rcv)


WEIGHTS = ("norm_w", "w_in", "fg_bias", "ssm_conv_w", "ssm_conv_b", "dt_bias", "a_log", "d_skip", "ssm_norm_w", "sc_conv_w",
           "sc_conv_b", "cf_conv_w", "cf_conv_b", "cf_ln_w", "cf_ln_b", "w_gate", "b_gate", "w_branch", "w_out", "final_norm_w")
BIG = ("w_in", "w_gate", "w_branch", "w_out")
SMALL = tuple(n for n in WEIGHTS if n not in BIG)
SMALL_SHARDED = ("ssm_conv_w", "sc_conv_w", "cf_conv_w", "b_gate")
N_CHIP = 4
PACK_C = D_MODEL
PACK_ROWS = 1024


def _pack(arrs, cols, row_mult):
    return _pack_groups([arrs], cols, row_mult)[0]


def _pack_groups(groups, cols, row_mult):
    n = sum(math.prod(a.shape) for a in groups[0])
    rows = -(-n // cols)
    rows = -(-rows // row_mult) * row_mult
    parts = []
    for arrs in groups:
        parts += [a.reshape(-1) for a in arrs]
        if rows * cols > n:
            parts.append(jnp.zeros((rows * cols - n,), arrs[0].dtype))
    return jnp.concatenate(parts).reshape(len(groups), rows, cols)


PIECE_ROWS = 16


def _pack_rows(groups, cols, row_mult):
    as_list = lambda a: list(a) if isinstance(a, (list, tuple)) else [a]
    n_rows = lambda a: sum(math.prod(b.shape) // cols for b in as_list(a))
    rows = [sum(-(-n_rows(a) // PIECE_ROWS) * PIECE_ROWS for a in arrs) for arrs in groups]
    total = -(-max(rows) // row_mult) * row_mult
    dtype = as_list(groups[0][0])[0].dtype
    parts = []
    for arrs, r in zip(groups, rows):
        for a in arrs:
            parts += [b.reshape(-1, cols) for b in as_list(a)]
            pad = -n_rows(a) % PIECE_ROWS
            if pad:
                parts.append(jnp.zeros((pad, cols), dtype))
        if total > r:
            parts.append(jnp.zeros((total - r, cols), dtype))
    return jnp.concatenate(parts, axis=0).reshape(len(groups), total, cols)


def _unpack_rows(packed, shapes):
    cols = packed.shape[-1]
    out, o = [], 0
    for s in shapes:
        r = math.prod(s) // cols
        out.append(packed[o:o + r].reshape(s))
        o += -(-r // PIECE_ROWS) * PIECE_ROWS
    return out


def _unpack(packed, shapes):
    flat = packed.reshape(-1)
    out, o = [], 0
    for s in shapes:
        n = math.prod(s)
        out.append(flat[o:o + n].reshape(s))
        o += n
    return out


def _orig_cols():
    cols = []
    for n, s in zip(ORIG_NAMES, ORIG_SIZES):
        if n == "dt":
            cols.append((OFF["small"] + DT_LANE, s))
        elif n == "f":
            cols.append((OFF["small"] + F_LANE, s))
        else:
            cols.append((OFF[n], s))
    return cols


def _rows_to_padded(per_chip):
    q_in = N_IN // N_CHIP
    orig_off = dict(zip(ORIG_NAMES, [sum(ORIG_SIZES[:i]) for i in range(len(ORIG_SIZES))]))
    size = dict(zip(ORIG_NAMES, ORIG_SIZES))
    parts = []
    for n in PAD_ORDER + ("dt", "f"):
        lo, hi = orig_off[n], orig_off[n] + size[n]
        for k in range(N_CHIP):
            a, b = max(lo, k * q_in), min(hi, (k + 1) * q_in)
            if a < b:
                parts.append(per_chip[k][a - k * q_in:b - k * q_in])
    parts.append(jnp.zeros((SMALL_W - size["dt"] - size["f"], per_chip[0].shape[1]), per_chip[0].dtype))
    return jnp.concatenate(parts, axis=0)


def _owner_windows(k):
    q_in = N_IN // N_CHIP
    lo, hi = k * q_in, (k + 1) * q_in
    out, o = [], 0
    for start, s in _orig_cols():
        a, b = max(lo, o), min(hi, o + s)
        if a < b:
            p0, p1 = start + a - o, start + b - o
            w0, w1 = p0 // PIECE_ROWS * PIECE_ROWS, -(-p1 // PIECE_ROWS) * PIECE_ROWS
            out.append((w0, w1, p0 - w0, p1 - p0))
        o += s
    return out


def _owner_rows(windows_packed, k):
    parts, o = [], 0
    for w0, w1, lead, rows in _owner_windows(k):
        parts.append(windows_packed[o + lead:o + lead + rows])
        o += w1 - w0
    return jnp.concatenate(parts, axis=0)


IN_WINDOW_ROWS = max(sum(w1 - w0 for w0, w1, _, _ in _owner_windows(k)) for k in range(4))


def _lanes_row(parts, width=LANES):
    v = jnp.concatenate([p.reshape(-1) for p in parts])
    return jnp.pad(v, (0, width - v.shape[0])).reshape(1, width)


def kernel(x, norm_w, w_in, fg_bias, ssm_conv_w, ssm_conv_b, dt_bias, a_log, d_skip, ssm_norm_w, sc_conv_w, sc_conv_b, cf_conv_w, cf_conv_b, cf_ln_w, cf_ln_b, w_gate, b_gate, w_branch, w_out, final_norm_w, loss_target, m_norm_w, m_w_in, m_fg_bias, m_ssm_conv_w, m_ssm_conv_b, m_dt_bias, m_a_log, m_d_skip, m_ssm_norm_w, m_sc_conv_w, m_sc_conv_b, m_cf_conv_w, m_cf_conv_b, m_cf_ln_w, m_cf_ln_b, m_w_gate, m_b_gate, m_w_branch, m_w_out, m_final_norm_w, v_norm_w, v_w_in, v_fg_bias, v_ssm_conv_w, v_ssm_conv_b, v_dt_bias, v_a_log, v_d_skip, v_ssm_norm_w, v_sc_conv_w, v_sc_conv_b, v_cf_conv_w, v_cf_conv_b, v_cf_ln_w, v_cf_ln_b, v_w_gate, v_b_gate, v_w_branch, v_w_out, v_final_norm_w):
    wts = dict(norm_w=norm_w, w_in=w_in, fg_bias=fg_bias, ssm_conv_w=ssm_conv_w, ssm_conv_b=ssm_conv_b, dt_bias=dt_bias,
               a_log=a_log, d_skip=d_skip, ssm_norm_w=ssm_norm_w, sc_conv_w=sc_conv_w, sc_conv_b=sc_conv_b,
               cf_conv_w=cf_conv_w, cf_conv_b=cf_conv_b, cf_ln_w=cf_ln_w, cf_ln_b=cf_ln_b, w_gate=w_gate, b_gate=b_gate,
               w_branch=w_branch, w_out=w_out, final_norm_w=final_norm_w)
    mom = dict(norm_w=m_norm_w, w_in=m_w_in, fg_bias=m_fg_bias, ssm_conv_w=m_ssm_conv_w, ssm_conv_b=m_ssm_conv_b,
               dt_bias=m_dt_bias, a_log=m_a_log, d_skip=m_d_skip, ssm_norm_w=m_ssm_norm_w, sc_conv_w=m_sc_conv_w,
               sc_conv_b=m_sc_conv_b, cf_conv_w=m_cf_conv_w, cf_conv_b=m_cf_conv_b, cf_ln_w=m_cf_ln_w, cf_ln_b=m_cf_ln_b,
               w_gate=m_w_gate, b_gate=m_b_gate, w_branch=m_w_branch, w_out=m_w_out, final_norm_w=m_final_norm_w)
    vel = dict(norm_w=v_norm_w, w_in=v_w_in, fg_bias=v_fg_bias, ssm_conv_w=v_ssm_conv_w, ssm_conv_b=v_ssm_conv_b,
               dt_bias=v_dt_bias, a_log=v_a_log, d_skip=v_d_skip, ssm_norm_w=v_ssm_norm_w, sc_conv_w=v_sc_conv_w,
               sc_conv_b=v_sc_conv_b, cf_conv_w=v_cf_conv_w, cf_conv_b=v_cf_conv_b, cf_ln_w=v_cf_ln_w, cf_ln_b=v_cf_ln_b,
               w_gate=v_w_gate, b_gate=v_b_gate, w_branch=v_w_branch, w_out=v_w_out, final_norm_w=v_final_norm_w)
    L = norm_w.shape[0]
    S, D = x.shape[1], x.shape[2]
    assert D == D_MODEL and x.shape[0] == 1
    xi, yi, ci = _place()
    chip = 2 * xi + yi

    sh_shapes = [wts[n].shape for n in SMALL_SHARDED]
    got = gather_all(_pack([wts[n] for n in SMALL_SHARDED], LANES, 8), name="gather_small_w")
    per_chip = [_unpack(got[2 * k], sh_shapes) for k in range(N_CHIP)]
    full_small = {n: jnp.concatenate([per_chip[k][i] for k in range(N_CHIP)], axis=-1) for i, n in enumerate(SMALL_SHARDED)}

    q_in, q_d = N_IN // N_CHIP, D // N_CHIP
    sent_shapes = [(q_in, D), (N_BRANCH, q_d, D), (N_BRANCH, BW, q_d), (q_d, D)]
    def own_shard(l):
        packed = _pack_rows([[w_in[l].T.astype(BF16), w_gate[l].astype(BF16), w_branch[l].astype(BF16),
                              w_out[l].astype(BF16)]], PACK_C, PACK_ROWS)[0]
        half_rows = packed.shape[0] // 2
        return lax.dynamic_update_slice(lax.empty((N_CHIP, 2, half_rows, PACK_C), BF16),
                                        packed.reshape(1, 2, half_rows, PACK_C), (chip, 0, 0, 0))

    def whole_weights(got):
        pc = [_unpack_rows(got[k].reshape(-1, PACK_C), sent_shapes) for k in range(N_CHIP)]
        return dict(wpt=_rows_to_padded([pc[k][0] for k in range(N_CHIP)]),
                    wg=jnp.concatenate([pc[k][1] for k in range(N_CHIP)], axis=1),
                    wb=jnp.concatenate([pc[k][2] for k in range(N_CHIP)], axis=2),
                    wo=jnp.concatenate([pc[k][3] for k in range(N_CHIP)], axis=0))

    lw = [whole_weights(gather_chips(own_shard(0), name="gather_w"))]
    saved = []
    xl = x[0]
    for l in range(L):
        w = lw[l]
        h = rms_fwd(xl, norm_w[l][None], name="rms_fwd")
        if l + 1 < L:
            u, nxt = mm(h, w["wpt"], tb=True, comm=("gather_ici", own_shard(l + 1)), name="mm_in_gather")
        else:
            u = mm(h, w["wpt"], tb=True, name="mm_in")
        bias_small = _lanes_row([dt_bias[l], fg_bias[l]])
        sp, csum = small_fwd(u, bias_small, name="small_fwd")
        c8 = csum[:, F_LANE:F_LANE + FOX_HEADS].T * LOG2E
        c_col, c_row = c8[:, :, None], c8[:, None, :]
        o, lse, y_a = attn_fwd(u, c_row, name="attn_fwd")
        xbc = conv_fwd(u, full_small["ssm_conv_w"][l], ssm_conv_b[l][None], mode="plain", a_off=OFF["xbc"], b_off=0,
                       C=SSM_CONV_DIM, name="conv_ssm_fwd")
        alog_row = _lanes_row([a_log[l]])
        dskip_l = jnp.repeat(d_skip[l], SSM_P)[None]
        y_ssd, hs = ssd_fwd(xbc, sp, alog_row, dskip_l, name="ssd_fwd")
        cv_c = conv_fwd(u, full_small["sc_conv_w"][l], sc_conv_b[l][None], mode="mul", a_off=OFF["scc"], b_off=OFF["scx"],
                        C=BW, name="conv_sc_fwd")
        cv_d = conv_fwd(u, full_small["cf_conv_w"][l], cf_conv_b[l][None], mode="glu", a_off=OFF["glu"], b_off=OFF["glu"] + BW,
                        C=BW, name="conv_cf_fwd")
        y_b, y_c, y_d = post_fwd(u, y_ssd, cv_c, cv_d, ssm_norm_w[l][None], cf_ln_w[l][None], cf_ln_b[l][None], name="post_fwd")
        merged, gates, proj = merge_fwd(h, (y_a, y_b, y_c, y_d), w["wg"], full_small["b_gate"][l][:, None, :], w["wb"],
                                        name="merge_fwd")
        if l + 1 < L:
            x_next, nxt = mm(merged, w["wo"], add=xl, comm=("gather_d2d", nxt), name="mm_out_gather")
            lw.append(whole_weights(nxt))
        else:
            x_next = mm(merged, w["wo"], add=xl, name="mm_out")
        saved.append(dict(x=xl, h=h, u=u, bias_small=bias_small, sp=sp, c_col=c_col, c_row=c_row, o=o, lse=lse, xbc=xbc,
                          alog_row=alog_row, dskip_l=dskip_l, hs=hs, y_ssd=y_ssd, cv_c=cv_c, cv_d=cv_d,
                          ys=(y_a, y_b, y_c, y_d), merged=merged, gates=gates, proj=proj))
        xl = x_next

    sq, dx, d_final = loss_head(xl, final_norm_w[None], loss_target[0], name="loss_head")
    loss = lax.psum(sq[0, 0] * (0.5 / D), ("x", "y", "c"))

    small_g = {n: [None] * L for n in SMALL if n != "final_norm_w"}
    big_g = {n: [None] * L for n in BIG}

    def finish_exchange(l, part, rcv):
        full = share_halves(add_chips(part, rcv, jnp.stack([chip, ci]).astype(jnp.int32), name="add_chips"), name="share_halves")
        g_wg, g_wb, g_wo, g_win = _unpack_rows(full.reshape(-1, PACK_C),
                                               [(N_BRANCH, q_d, D), (N_BRANCH, q_d, BW), (q_d, D), (IN_WINDOW_ROWS, D)])
        g_in = lax.switch(chip, [functools.partial(_owner_rows, k=k) for k in range(N_CHIP)], g_win)
        big_g["w_in"][l] = g_in.T
        big_g["w_gate"][l] = g_wg
        big_g["w_branch"][l] = jnp.transpose(g_wb, (0, 2, 1))
        big_g["w_out"][l] = g_wo

    for l in reversed(range(L)):
        w, sv = lw[l], saved[l]
        u, h = sv["u"], sv["h"]
        dm = mm(dx, w["wo"], tb=True, name="mm_dmerged")
        d_wo = mm(sv["merged"], dx, ta=True, out_dtype=BF16, name="mm_dwo")
        dp, dg, dbg = merge_bwd(dm, sv["gates"], sv["proj"], name="merge_bwd")
        dys = [mm(dp, w["wb"][i], tb=True, K=D, a_koff=i * D, name="mm_dy") for i in range(N_BRANCH)]
        d_wbt = [mm(dp, sv["ys"][i], ta=True, M=D, a_moff=i * D, out_dtype=BF16, name="mm_dwb")
                 for i in range(N_BRANCH)]
        d_wg = mm(h, dg, ta=True, out_dtype=BF16, n_groups=N_BRANCH, name="mm_dwg")
        (do, delta, dga, dz, dscb, dgc, dgd, dy_ssd, dcv_c, dcv_d, dnw, dlnw, dlnb) = post_bwd(
            dys[0], dys[1], dys[2], dys[3], u, sv["o"], sv["y_ssd"], sv["cv_c"], sv["cv_d"],
            ssm_norm_w[l][None], cf_ln_w[l][None], cf_ln_b[l][None], name="post_bwd")
        delta_row = delta[:, :FOX_HEADS].T[:, None, :]
        lse_row = jnp.transpose(sv["lse"], (0, 2, 1))
        dq, dk, dv, dc_col = attn_bwd(u, do, sv["c_col"], lse_row, delta_row, name="attn_bwd")
        dscc, dscx, d_scw, d_scb = conv_bwd(dcv_c, u, full_small["sc_conv_w"][l], mode="mul", a_off=OFF["scc"],
                                            b_off=OFF["scx"], C=BW, name="conv_sc_bwd")
        dglua, dglug, d_cfw, d_cfb = conv_bwd(dcv_d, u, full_small["cf_conv_w"][l], mode="glu", a_off=OFF["glu"],
                                              b_off=OFF["glu"] + BW, C=BW, name="conv_cf_bwd")
        dxbc_pre, ddt, d_a, d_dl = ssd_bwd(dy_ssd, sv["xbc"], sv["sp"], sv["alog_row"], sv["dskip_l"], sv["hs"], name="ssd_bwd")
        dxbc, d_ssmw, d_ssmb = conv_bwd(dxbc_pre, u, full_small["ssm_conv_w"][l], mode="plain", a_off=OFF["xbc"], b_off=0,
                                        C=SSM_CONV_DIM, name="conv_ssm_bwd")
        dc_full = jnp.pad(dc_col[:, :, 0].T, ((0, 0), (F_LANE, LANES - F_LANE - FOX_HEADS)))
        du_small, dbias_small = small_bwd(dc_full, ddt, u, sv["bias_small"], name="small_bwd")
        by_name = dict(q=dq.astype(BF16), k=dk, v=dv, ga=dga, z=dz, scb=dscb, scc=dscc, scx=dscx, gc=dgc, gd=dgd, xbc=dxbc)
        du = jnp.concatenate([jnp.concatenate([dglua, dglug], axis=1) if n == "glu" else by_name[n] for n in PAD_ORDER]
                             + [du_small], axis=1)
        d_wpt = mm(du, h, ta=True, out_dtype=BF16, name="mm_dwp")

        dest = [[d_wg[:, k * q_d:(k + 1) * q_d]] + [t[k * q_d:(k + 1) * q_d] for t in d_wbt] + [d_wo[k * q_d:(k + 1) * q_d]]
                + [[d_wpt[w0:w1] for w0, w1, _, _ in _owner_windows(k)]] for k in range(N_CHIP)]
        g_all = _pack_rows(dest, PACK_C, PACK_ROWS)
        g_all = g_all.reshape(N_CHIP, 2, g_all.shape[1] // 2, PACK_C)
        sib = swap_partials(g_all, name="swap_partials")
        part = add_sibling(g_all, sib, ci.reshape(1).astype(jnp.int32), name="add_sibling")
        dh_gate = mm(dg, w["wg"], tb=True, name="mm_dh_gate")
        dh, rcv = mm(du, w["wpt"], add=dh_gate, comm=("scatter", part), name="mm_dh_in_scatter")
        finish_exchange(l, part, rcv)
        dx, d_nw = rms_bwd(dh, sv["x"], norm_w[l][None], dx, name="rms_bwd")

        a_neg = -jnp.exp(a_log[l])
        sg = dict(norm_w=d_nw[0], fg_bias=dbias_small[0, F_LANE:F_LANE + FOX_HEADS], ssm_conv_w=d_ssmw, ssm_conv_b=d_ssmb[0],
                  dt_bias=dbias_small[0, DT_LANE:DT_LANE + SSM_HEADS], a_log=d_a[0, :SSM_HEADS] * a_neg,
                  d_skip=d_dl.reshape(SSM_HEADS, SSM_P).sum(-1), ssm_norm_w=dnw[0], sc_conv_w=d_scw, sc_conv_b=d_scb[0],
                  cf_conv_w=d_cfw, cf_conv_b=d_cfb[0], cf_ln_w=dlnw[0], cf_ln_b=dlnb[0], b_gate=dbg.reshape(N_BRANCH, D))
        for n in sg:
            small_g[n][l] = sg[n]


    names = [n for n in SMALL if n != "final_norm_w"]
    stacked = [jnp.stack(small_g[n]) for n in names] + [d_final[0]]
    shapes = [a.shape for a in stacked]
    got = gather_all(_pack(stacked, LANES, 8), name="gather_small_g")
    tot = addn([got[d] for d in range(8)], out_dtype=F32, name="add_small_g")
    grads = dict(zip(names + ["final_norm_w"], _unpack(tot, shapes)))
    for n in SMALL_SHARDED:
        sz = wts[n].shape[-1]
        grads[n] = lax.dynamic_slice_in_dim(grads[n], chip * sz, sz, axis=grads[n].ndim - 1)
    for n in BIG:
        grads[n] = jnp.stack(big_g[n])

    delta, new_m, new_v = {}, {}, {}
    for n in BIG:
        two_d = lambda a: a.reshape(-1, a.shape[-1])
        d, nm, nv = adamw(two_d(wts[n]), two_d(grads[n]), two_d(mom[n]), two_d(vel[n]), name="adamw_" + n)
        delta[n], new_m[n], new_v[n] = (t.reshape(wts[n].shape) for t in (d, nm, nv))
    small_shapes = [wts[n].shape for n in SMALL]
    pk = lambda src: _pack([src[n] for n in SMALL], LANES, 8)
    d, nm, nv = adamw(pk(wts), pk(grads), pk(mom), pk(vel), name="adamw_small")
    for tgt, src in ((delta, d), (new_m, nm), (new_v, nv)):
        for n, a in zip(SMALL, _unpack(src, small_shapes)):
            tgt[n] = a

    return (loss, dx[None], *[grads[n] for n in WEIGHTS], *[delta[n] for n in WEIGHTS],
            *[new_m[n] for n in WEIGHTS], *[new_v[n] for n in WEIGHTS])
```
